```python
import math
import jax, jax.numpy as jnp
from jax import lax
import numpy as np

D_MODEL = 1024
BATCH = 32
SEQ = 2048
DEPTH = 4

GRID_W = 64
CTX_LEN = 256
EPS = 1e-6
ROPE_BASE = 10000.0
ROT_DIM = 64
Q_BLOCK = 128
CHUNK = 64
N_BRANCH = 4
BRANCH_WIDTH = 512

MLA_HEADS = 4
MLA_NOPE = 128
MLA_ROPE = ROT_DIM
MLA_V = BRANCH_WIDTH // MLA_HEADS
MLA_Q_LORA = 384
MLA_KV_LORA = 256
GLA_HEADS = 4
GLA_DK = 64
GLA_DV = BRANCH_WIDTH // GLA_HEADS
GLA_GATE_RANK = 16
GLA_TAU = 16.0
ML_HEADS = 4
ML_DQK = 64
ML_DV = BRANCH_WIDTH // ML_HEADS
ML_CONV = 3
ML_F_BIAS = 3.0
DF_HEADS = 4
DF_DQK = ROT_DIM
DF_DV = BRANCH_WIDTH // DF_HEADS

IN_SPLITS = (
    ('mla_cq', MLA_Q_LORA), ('mla_ckv', MLA_KV_LORA), ('mla_kr', MLA_ROPE), ('mla_z', BRANCH_WIDTH),
    ('gla_q', GLA_HEADS * GLA_DK), ('gla_k', GLA_HEADS * GLA_DK), ('gla_v', BRANCH_WIDTH),
    ('gla_af', GLA_GATE_RANK), ('gla_ab', GLA_GATE_RANK), ('gla_z', BRANCH_WIDTH),
    ('ml_x', BRANCH_WIDTH), ('ml_v', BRANCH_WIDTH), ('ml_o', BRANCH_WIDTH), ('ml_if', 4 * ML_HEADS),
    ('ml_z', BRANCH_WIDTH),
    ('df_q', DF_HEADS * 2 * DF_DQK), ('df_k', DF_HEADS * 2 * DF_DQK), ('df_v', BRANCH_WIDTH),
    ('df_z', BRANCH_WIDTH),
    ('merge', N_BRANCH * D_MODEL),
)
D_IN = sum(w for _, w in IN_SPLITS)

kernel_name = 'hybrid_mla_gla_mlstm_diffattn_dit_block'

F32 = jnp.float32


def rmsnorm(x, g):
    xf = x.astype(F32)
    y = xf * lax.rsqrt(jnp.mean(xf * xf, axis=-1, keepdims=True) + EPS)
    return (y * g.astype(F32)).astype(x.dtype)


def split_cols(proj):
    out, off = {}, 0
    for name, w in IN_SPLITS:
        out[name] = proj[..., off:off + w]
        off += w
    return out


def axial_rope_tables(rows, n_ctx, dtype):
    quarter = ROT_DIM // 4
    inv_freq = ROPE_BASE ** (-jnp.arange(quarter, dtype=F32) / quarter)
    row = jnp.repeat(jnp.arange(rows, dtype=F32), GRID_W)
    col = jnp.tile(jnp.arange(GRID_W, dtype=F32), rows)
    ar = row[:, None] * inv_freq
    ac = col[:, None] * inv_freq
    ang = jnp.concatenate([ar, ar, ac, ac], axis=-1)
    cos = jnp.concatenate([jnp.ones((n_ctx, ROT_DIM), F32), jnp.cos(ang)], axis=0)
    sin = jnp.concatenate([jnp.zeros((n_ctx, ROT_DIM), F32), jnp.sin(ang)], axis=0)
    return cos.astype(dtype), sin.astype(dtype)


def apply_rope(x, cos, sin):
    a1, a2, b1, b2 = jnp.split(x, 4, axis=-1)
    rot = jnp.concatenate([-a2, a1, -b2, b1], axis=-1)
    return x * cos + rot * sin


def flip_segments(a, n_ctx, axis):
    c_part = lax.slice_in_dim(a, 0, n_ctx, axis=axis)
    l_part = lax.slice_in_dim(a, n_ctx, a.shape[axis], axis=axis)
    return jnp.concatenate([jnp.flip(c_part, axis), jnp.flip(l_part, axis)], axis=axis)


def centred_dwconv(x, w, b):
    pad = w.shape[0] // 2
    y = lax.conv_general_dilated(x, w[:, None, :].astype(x.dtype), window_strides=(1,),
                                 padding=[(pad, pad)], dimension_numbers=('NWC', 'WIO', 'NWC'),
                                 feature_group_count=x.shape[-1])
    return y + b


def _chunks(a):
    B, H, T = a.shape[:3]
    a = a.reshape((B, H, T // CHUNK, CHUNK) + a.shape[3:])
    return jnp.moveaxis(a, 2, 0)


def _unchunks(a):
    a = jnp.moveaxis(a, 0, 2)
    return a.reshape(a.shape[:2] + (-1,) + a.shape[4:])


def mixed_softmax_attention(q, k, v, w, n_ctx, with_ctx):
    B, H, P, T, dq = q.shape
    scale = dq ** -0.5
    wf = w.astype(F32)

    def attend(qb, kb, vb):
        s = jnp.einsum('bhpqd,bhpkd->bhpqk', qb, kb).astype(F32) * scale
        pr = jax.nn.softmax(s, axis=-1)
        pw = jnp.einsum('bhpqk,hp->bhqk', pr, wf)
        return jnp.einsum('bhqk,bhkd->bhqd', pw.astype(vb.dtype), vb)

    n_lat = T - n_ctx
    nb = n_lat // Q_BLOCK
    ql = q[:, :, :, n_ctx:].reshape(B, H, P, nb, Q_BLOCK, dq)
    ql = jnp.moveaxis(ql, 3, 0)
    ol = lax.map(lambda qb: attend(qb, k, v), ql)
    ol = jnp.moveaxis(ol, 0, 2).reshape(B, H, n_lat, v.shape[-1])
    if not with_ctx:
        return ol
    oc = attend(q[:, :, :, :n_ctx], k[:, :, :, :n_ctx], v[:, :, :n_ctx])
    return jnp.concatenate([oc, ol], axis=2)


def gla_chunk_scan(q, k, v, log_a):
    B, H, T, dk = q.shape
    dv = v.shape[-1]
    mask = jnp.tril(jnp.ones((CHUNK, CHUNK), bool))[:, :, None]

    def step(S, inp):
        qc, kc, vc, lac = inp
        b = jnp.cumsum(lac, axis=-2)
        o_inter = jnp.einsum('bhld,bhde->bhle', qc * jnp.exp(b), S)
        diff = jnp.where(mask, b[:, :, :, None, :] - b[:, :, None, :, :], -jnp.inf)
        A = jnp.einsum('bhtd,bhsd,bhtsd->bhts', qc, kc, jnp.exp(diff))
        o_intra = jnp.einsum('bhts,bhse->bhte', A, vc)
        bl = b[:, :, -1:, :]
        S_new = jnp.exp(bl[:, :, 0, :])[..., None] * S + jnp.einsum('bhsd,bhse->bhde', kc * jnp.exp(bl - b), vc)
        return S_new, o_inter + o_intra

    S0 = jnp.zeros((B, H, dk, dv), F32)
    _, o = lax.scan(step, S0, (_chunks(q), _chunks(k), _chunks(v), _chunks(log_a)))
    return _unchunks(o)


def mlstm_chunk_scan(q, k, v, log_i, log_f):
    B, H, T, dk = q.shape
    dv = v.shape[-1]
    mask = jnp.tril(jnp.ones((CHUNK, CHUNK), bool))

    def step(carry, inp):
        C, nv, m = carry
        qc, kc, vc, lic, lfc = inp
        b = jnp.cumsum(lfc, axis=-1)
        log_inter = b + m[..., None]
        log_D = jnp.where(mask, b[..., :, None] - b[..., None, :] + lic[..., None, :], -jnp.inf)
        m_t = jnp.maximum(log_inter, jnp.max(log_D, axis=-1))
        inter_w = jnp.exp(log_inter - m_t)
        s = jnp.einsum('bhtd,bhsd->bhts', qc, kc) * jnp.exp(log_D - m_t[..., None])
        num = inter_w[..., None] * jnp.einsum('bhtd,bhde->bhte', qc, C) + jnp.einsum('bhts,bhse->bhte', s, vc)
        den = inter_w * jnp.einsum('bhtd,bhd->bht', qc, nv) + jnp.sum(s, axis=-1)
        h = num / jnp.maximum(jnp.abs(den), jnp.exp(-m_t))[..., None]
        bL = b[..., -1]
        log_upd = bL[..., None] - b + lic
        m_new = jnp.maximum(bL + m, jnp.max(log_upd, axis=-1))
        decay = jnp.exp(bL + m - m_new)
        wu = jnp.exp(log_upd - m_new[..., None])
        C_new = decay[..., None, None] * C + jnp.einsum('bhs,bhsd,bhse->bhde', wu, kc, vc)
        n_new = decay[..., None] * nv + jnp.einsum('bhs,bhsd->bhd', wu, kc)
        return (C_new, n_new, m_new), h

    init = (jnp.zeros((B, H, dk, dv), F32), jnp.zeros((B, H, dk), F32), jnp.zeros((B, H), F32))
    _, h = lax.scan(step, init, (_chunks(q), _chunks(k), _chunks(v), _chunks(log_i), _chunks(log_f)))
    return _unchunks(h)


def mla_branch(p, cq_g, ckv_g, wuq, wukv, q_g, k_g, cos, sin, n_ctx, t0):
    B, T, _ = p['mla_cq'].shape
    H = MLA_HEADS
    q = (rmsnorm(p['mla_cq'], cq_g) @ wuq).reshape(B, T, H, MLA_NOPE + MLA_ROPE)
    kv = (rmsnorm(p['mla_ckv'], ckv_g) @ wukv).reshape(B, T, H, MLA_NOPE + MLA_V)
    q_nope = rmsnorm(q[..., :MLA_NOPE], q_g[:MLA_NOPE])
    q_rope = apply_rope(rmsnorm(q[..., MLA_NOPE:], q_g[MLA_NOPE:]), cos[:, None], sin[:, None])
    k_nope = rmsnorm(kv[..., :MLA_NOPE], k_g[:MLA_NOPE])
    k_rope = apply_rope(rmsnorm(p['mla_kr'], k_g[MLA_NOPE:]), cos, sin)
    k_rope = jnp.broadcast_to(k_rope[:, :, None, :], (B, T, H, MLA_ROPE))
    qf = jnp.concatenate([q_nope, q_rope], axis=-1).transpose(0, 2, 1, 3)[:, :, None]
    kf = jnp.concatenate([k_nope, k_rope], axis=-1).transpose(0, 2, 1, 3)[:, :, None]
    vf = kv[..., MLA_NOPE:].transpose(0, 2, 1, 3)
    o = mixed_softmax_attention(qf, kf, vf, jnp.ones((H, 1), F32), n_ctx, t0 == 0)
    o = o.transpose(0, 2, 1, 3).reshape(B, -1, BRANCH_WIDTH)
    return o * jax.nn.silu(p['mla_z'][:, t0:])


def gla_branch(p, a_w, a_b, out_g, n_ctx, t0):
    B, T, _ = p['gla_q'].shape
    H = GLA_HEADS

    def heads(a, d):
        return a.reshape(B, T, H, d).transpose(0, 2, 1, 3).astype(F32)

    q = heads(p['gla_q'], GLA_DK) * GLA_DK ** -0.5
    k = heads(p['gla_k'], GLA_DK)
    v = heads(p['gla_v'], GLA_DV)
    la_f = heads(jax.nn.log_sigmoid((p['gla_af'] @ a_w[0] + a_b[0]).astype(F32)) / GLA_TAU, GLA_DK)
    la_b = heads(jax.nn.log_sigmoid((p['gla_ab'] @ a_w[1] + a_b[1]).astype(F32)) / GLA_TAU, GLA_DK)
    o_f = gla_chunk_scan(q, k, v, la_f)
    fl = lambda a: flip_segments(a, n_ctx, 2)
    o_b = fl(gla_chunk_scan(fl(q), fl(k), fl(v), fl(la_b)))
    o = (o_f + o_b).transpose(0, 2, 1, 3)
    o = rmsnorm(o, out_g.reshape(H, GLA_DV)).reshape(B, T, BRANCH_WIDTH).astype(p['gla_z'].dtype)
    return (o * jax.nn.silu(p['gla_z']))[:, t0:]


def mlstm_branch(p, conv_w, conv_b, wq, wk, gate_b, out_g, skip, n_ctx, t0):
    B, T, _ = p['ml_x'].shape
    H = ML_HEADS
    xm = p['ml_x']
    xc = jnp.concatenate([centred_dwconv(xm[:, :n_ctx], conv_w, conv_b),
                          centred_dwconv(xm[:, n_ctx:], conv_w, conv_b)], axis=1)
    xc = jax.nn.silu(xc)
    xh = xc.reshape(B, T, H, ML_DV)
    q = jnp.einsum('bthc,hcd->bhtd', xh, wq).astype(F32)
    k = jnp.einsum('bthc,hcd->bhtd', xh, wk).astype(F32) * ML_DQK ** -0.5
    v = p['ml_v'].reshape(B, T, H, ML_DV).transpose(0, 2, 1, 3).astype(F32)
    g = (p['ml_if'].reshape(B, T, 4, H) + gate_b).astype(F32).transpose(2, 0, 3, 1)
    h_f = mlstm_chunk_scan(q, k, v, g[0], jax.nn.log_sigmoid(g[1]))
    fl = lambda a, ax=2: flip_segments(a, n_ctx, ax)
    h_b = fl(mlstm_chunk_scan(fl(q), fl(k), fl(v), fl(g[2], 2), fl(jax.nn.log_sigmoid(g[3]), 2)))
    h = (h_f + h_b).transpose(0, 2, 1, 3)
    h = jax.nn.sigmoid(p['ml_o'].astype(F32)).reshape(B, T, H, ML_DV) * h
    h = rmsnorm(h, out_g.reshape(H, ML_DV)).reshape(B, T, BRANCH_WIDTH).astype(xm.dtype)
    y = (h + skip * xc) * jax.nn.silu(p['ml_z'])
    return y[:, t0:]


def diff_branch(p, qk_g, lam_p, out_g, lam_init, cos, sin, n_ctx, t0):
    B, T, _ = p['df_q'].shape
    H = DF_HEADS
    cs, sn = cos[:, None, None], sin[:, None, None]
    q = apply_rope(rmsnorm(p['df_q'].reshape(B, T, H, 2, DF_DQK), qk_g[0]), cs, sn)
    k = apply_rope(rmsnorm(p['df_k'].reshape(B, T, H, 2, DF_DQK), qk_g[1]), cs, sn)
    q = q.transpose(0, 2, 3, 1, 4)
    k = k.transpose(0, 2, 3, 1, 4)
    v = p['df_v'].reshape(B, T, H, DF_DV).transpose(0, 2, 1, 3)
    lp = lam_p.astype(F32)
    lam = jnp.exp(jnp.sum(lp[0] * lp[1])) - jnp.exp(jnp.sum(lp[2] * lp[3])) + lam_init
    w = jnp.broadcast_to(jnp.stack([jnp.ones_like(lam), -lam]), (H, 2))
    o = mixed_softmax_attention(q, k, v, w, n_ctx, t0 == 0)
    o = rmsnorm(o, out_g) * (1.0 - lam_init)
    o = o.transpose(0, 2, 1, 3).reshape(B, -1, BRANCH_WIDTH)
    return o * jax.nn.silu(p['df_z'][:, t0:])


def setup_inputs(seed: int = 0) -> dict:
    key = jax.random.key(seed)
    ks = iter(jax.random.split(key, 40))
    L, D = DEPTH, D_MODEL

    def nrm(shape, scale):
        return scale * jax.random.normal(next(ks), shape, F32)

    def gain(shape):
        return 1.0 + nrm(shape, 0.02)

    out = {}
    out['x'] = nrm((BATCH, SEQ, D), 1.0)
    out['c'] = nrm((BATCH, D), 1.0)
    out['ctx'] = nrm((BATCH, CTX_LEN, D), 1.0)
    out['c_ctx'] = nrm((D,), 1.0)
    out['ada_w'] = nrm((L, D, 3 * D), D ** -0.5)
    out['ada_b'] = nrm((L, 3 * D), 0.01)
    out['norm_g'] = gain((L, D))
    out['w_in'] = nrm((L, D, D_IN), D ** -0.5)
    out['mla_cq_g'] = gain((L, MLA_Q_LORA))
    out['mla_ckv_g'] = gain((L, MLA_KV_LORA))
    out['mla_wuq'] = nrm((L, MLA_Q_LORA, MLA_HEADS * (MLA_NOPE + MLA_ROPE)), MLA_Q_LORA ** -0.5)
    out['mla_wukv'] = nrm((L, MLA_KV_LORA, MLA_HEADS * (MLA_NOPE + MLA_V)), MLA_KV_LORA ** -0.5)
    out['mla_q_g'] = gain((L, MLA_NOPE + MLA_ROPE))
    out['mla_k_g'] = gain((L, MLA_NOPE + MLA_ROPE))
    out['gla_a_w'] = nrm((L, 2, GLA_GATE_RANK, GLA_HEADS * GLA_DK), GLA_GATE_RANK ** -0.5)
    out['gla_a_b'] = nrm((L, 2, GLA_HEADS * GLA_DK), 0.01)
    out['gla_out_g'] = gain((L, BRANCH_WIDTH))
    out['ml_conv_w'] = nrm((L, ML_CONV, BRANCH_WIDTH), ML_CONV ** -0.5)
    out['ml_conv_b'] = nrm((L, BRANCH_WIDTH), 0.01)
    out['ml_wq'] = nrm((L, ML_HEADS, ML_DV, ML_DQK), ML_DV ** -0.5)
    out['ml_wk'] = nrm((L, ML_HEADS, ML_DV, ML_DQK), ML_DV ** -0.5)
    out['ml_gate_b'] = jnp.array([0.0, ML_F_BIAS, 0.0, ML_F_BIAS], F32)[None, :, None] + nrm((L, 4, ML_HEADS), 0.1)
    out['ml_out_g'] = gain((L, BRANCH_WIDTH))
    out['ml_skip'] = gain((L, BRANCH_WIDTH))
    out['df_qk_g'] = gain((L, 2, DF_DQK))
    out['df_lambda'] = nrm((L, 4, DF_DQK), 0.1)
    out['df_out_g'] = gain((L, DF_DV))
    out['br_w'] = nrm((L, N_BRANCH, BRANCH_WIDTH, D), BRANCH_WIDTH ** -0.5)
    out['w_out'] = nrm((L, D, D), D ** -0.5)
    return out


def reference(x, c, ctx, c_ctx, ada_w, ada_b, norm_g, w_in, mla_cq_g, mla_ckv_g, mla_wuq, mla_wukv,
              mla_q_g, mla_k_g, gla_a_w, gla_a_b, gla_out_g, ml_conv_w, ml_conv_b, ml_wq, ml_wk,
              ml_gate_b, ml_out_g, ml_skip, df_qk_g, df_lambda, df_out_g, br_w, w_out):
    B, S, D = x.shape
    n_ctx = ctx.shape[1]
    ROWS = S // GRID_W
    cos, sin = axial_rope_tables(ROWS, n_ctx, x.dtype)
    s_c = jax.nn.silu(c)
    s_cc = jax.nn.silu(c_ctx)
    for l in range(DEPTH):
        last = l == DEPTH - 1
        t0 = n_ctx if last else 0
        shift, scale, gate = jnp.split(s_c @ ada_w[l] + ada_b[l], 3, axis=-1)
        shift_c, scale_c, gate_c = jnp.split(s_cc @ ada_w[l] + ada_b[l], 3, axis=-1)
        h = jnp.concatenate([rmsnorm(ctx, norm_g[l]) * (1 + scale_c) + shift_c,
                             rmsnorm(x, norm_g[l]) * (1 + scale[:, None]) + shift[:, None]], axis=1)
        p = split_cols(h @ w_in[l])
        lam_init = 0.8 - 0.6 * math.exp(-0.3 * l)
        ys = (
            mla_branch(p, mla_cq_g[l], mla_ckv_g[l], mla_wuq[l], mla_wukv[l], mla_q_g[l], mla_k_g[l],
                       cos, sin, n_ctx, t0),
            gla_branch(p, gla_a_w[l], gla_a_b[l], gla_out_g[l], n_ctx, t0),
            mlstm_branch(p, ml_conv_w[l], ml_conv_b[l], ml_wq[l], ml_wk[l], ml_gate_b[l], ml_out_g[l],
                         ml_skip[l], n_ctx, t0),
            diff_branch(p, df_qk_g[l], df_lambda[l], df_out_g[l], lam_init, cos, sin, n_ctx, t0),
        )
        gl = p['merge'][:, t0:]
        merged = jax.nn.sigmoid(gl[..., :D]) * (ys[0] @ br_w[l, 0])
        for i in range(1, N_BRANCH):
            merged = merged + jax.nn.sigmoid(gl[..., i * D:(i + 1) * D]) * (ys[i] @ br_w[l, i])
        out = merged @ w_out[l]
        x = x + gate[:, None] * out[:, n_ctx - t0:]
        if not last:
            ctx = ctx + gate_c * out[:, :n_ctx]
    return x
```

```python
import functools
import math

import jax
import jax.numpy as jnp
from jax import lax
from jax.experimental import pallas as pl
from jax.experimental.pallas import tpu as pltpu

F32 = jnp.float32
BF16 = jnp.bfloat16

D_MODEL = 1024
GRID_W = 64
EPS = 1e-6
ROPE_BASE = 10000.0
ROT_DIM = 64
CHUNK = 64
N_BRANCH = 4
BRANCH_WIDTH = 512
HEADS = 4
HEAD_V = BRANCH_WIDTH // HEADS
MLA_NOPE = 128
MLA_Q_LORA = 384
MLA_KV_LORA = 256
GLA_DK = 64
GLA_GATE_RANK = 16
GLA_TAU = 16.0
ML_DQK = 64
DF_DQK = 64

IN_SPLITS = (
    ('mla_cq', 384), ('mla_ckv', 256), ('mla_kr', 64), ('mla_z', 512),
    ('gla_q', 256), ('gla_k', 256), ('gla_v', 512), ('gla_af', 16), ('gla_ab', 16), ('gla_z', 512),
    ('ml_x', 512), ('ml_v', 512), ('ml_o', 512), ('ml_if', 16), ('ml_z', 512),
    ('df_q', 512), ('df_k', 512), ('df_v', 512), ('df_z', 512),
    ('merge', 4096),
)

P_MERGE = 0
P_DF_KV = 4096
P_DF_Q = 5120
P_DF_Z = 5632
P_ML = 6144
P_GLA_VZ = 8192
P_MLA_Z = 9216
P_GLA_QK = 9728
P_MLA_A = 10240
P_MLA_CKV = 10752
P_GLA_G = 11008
P_ML_IF = 11136
P_WIDTH = 11264

ROW_TILE = 256
IN_PROJ_COLS = 1024
VMEM_LIMIT = 56 * 1024 * 1024
EXP_CLAMP = 80.0


def _dot(a, b):
    return jnp.dot(a, b, preferred_element_type=F32)


def _dot_nt(a, b):
    return lax.dot_general(a, b, (((1,), (1,)), ((), ())), preferred_element_type=F32)


def _dot_tn(a, b):
    return lax.dot_general(a, b, (((0,), (0,)), ((), ())), preferred_element_type=F32)


def _dot_hi(a, b):
    return jnp.dot(a, b, preferred_element_type=F32, precision=lax.Precision.HIGHEST)


def _dot_nt_hi(a, b):
    return lax.dot_general(a, b, (((1,), (1,)), ((), ())), preferred_element_type=F32,
                           precision=lax.Precision.HIGHEST)


def _sigmoid(x):
    return 1.0 / (1.0 + jnp.exp(-x))


def _silu(x):
    return x * _sigmoid(x)


def _log_sigmoid(x):
    return jnp.minimum(x, 0.0) - jnp.log(1.0 + jnp.exp(-jnp.abs(x)))


def _rms(x, n):
    return x * lax.rsqrt(jnp.sum(x * x, axis=-1, keepdims=True) * (1.0 / n) + EPS)


def _rope(x, cos, sin_a, sin_b):
    return x * cos + pltpu.roll(x, 112, 1) * sin_a + pltpu.roll(x, 16, 1) * sin_b


def _params(*sem):
    return pltpu.CompilerParams(dimension_semantics=sem, vmem_limit_bytes=VMEM_LIMIT)


def _mod_kernel(c_ref, w_ref, b_ref, o_ref):
    s = _silu(c_ref[...])
    o_ref[0] = _dot(s.astype(BF16), w_ref[0].astype(BF16)) + b_ref[0]


def _modulation(cc, ada_w, ada_b):
    L, D, _ = ada_w.shape
    R = cc.shape[0]
    return pl.pallas_call(
        _mod_kernel,
        grid=(L, 3),
        in_specs=[pl.BlockSpec((R, D), lambda l, j: (0, 0)),
                  pl.BlockSpec((1, D, D), lambda l, j: (l, 0, j)),
                  pl.BlockSpec((1, 1, D), lambda l, j: (l, 0, j))],
        out_specs=pl.BlockSpec((1, R, D), lambda l, j: (l, 0, j)),
        out_shape=jax.ShapeDtypeStruct((L, R, 3 * D), F32),
        compiler_params=_params("arbitrary", "arbitrary"),
        name="modulation",
    )(cc, ada_w, ada_b.reshape(L, 1, 3 * D))


def _inproj_kernel(x_ref, mod_ref, g_ref, w_ref, o_ref, h_scr, *, n_ctx):
    T = x_ref.shape[1]

    @pl.when(pl.program_id(1) == 0)
    def _():
        g = g_ref[...]
        for r0 in range(0, T, ROW_TILE):
            k = 0 if r0 < n_ctx else 3
            y = _rms(x_ref[0, r0:r0 + ROW_TILE, :], D_MODEL) * g
            h = y * (1.0 + mod_ref[0, k + 1:k + 2, :]) + mod_ref[0, k:k + 1, :]
            h_scr[r0:r0 + ROW_TILE, :] = h.astype(BF16)

    for r0 in range(0, T, ROW_TILE):
        o_ref[0, r0:r0 + ROW_TILE, :] = _dot(h_scr[r0:r0 + ROW_TILE, :], w_ref[...]).astype(BF16)


def _in_projection(xs, mod, norm_g, w, n_ctx):
    B, T, D = xs.shape
    n_col = P_WIDTH // IN_PROJ_COLS
    return pl.pallas_call(
        functools.partial(_inproj_kernel, n_ctx=n_ctx),
        grid=(B, n_col),
        in_specs=[pl.BlockSpec((1, T, D), lambda b, j: (b, 0, 0)),
                  pl.BlockSpec((1, 8, D), lambda b, j: (b, 0, 0)),
                  pl.BlockSpec((1, D), lambda b, j: (0, 0)),
                  pl.BlockSpec((D, IN_PROJ_COLS), lambda b, j: (0, j))],
        out_specs=pl.BlockSpec((1, T, IN_PROJ_COLS), lambda b, j: (b, 0, j)),
        out_shape=jax.ShapeDtypeStruct((B, T, P_WIDTH), BF16),
        scratch_shapes=[pltpu.VMEM((T, D), BF16)],
        compiler_params=_params("arbitrary", "arbitrary"),
        name="in_projection",
    )(xs, mod, norm_g.reshape(1, D), w)


def _mla_kernel(pa_ref, pc_ref, z_ref, wuq_ref, wukv_ref, cqg_ref, ckvg_ref, qg_ref, kg_ref, rope_ref,
                o_ref, q_scr, k_scr, v_scr, *, n_ctx, tile0, with_ctx):
    T = pa_ref.shape[1]
    qi = pl.program_id(1)
    scale = (MLA_NOPE + ROT_DIM) ** -0.5

    @pl.when(qi == 0)
    def _():
        def prep(i, carry):
            r = pl.multiple_of(i * ROW_TILE, ROW_TILE)
            rows = pl.ds(r, ROW_TILE)
            cos, sa, sb = rope_ref[0, rows, :], rope_ref[1, rows, :], rope_ref[2, rows, :]
            pa = pa_ref[0, rows, :].astype(F32)
            cq = (_rms(pa[:, :MLA_Q_LORA], MLA_Q_LORA) * cqg_ref[...]).astype(BF16)
            q = _dot(cq, wuq_ref[...])
            kr = _rms(pa[:, MLA_Q_LORA:], ROT_DIM) * kg_ref[1:2, :]
            kr = _rope(kr, cos, sa, sb).astype(BF16)
            ckv = (_rms(pc_ref[0, rows, :].astype(F32), MLA_KV_LORA) * ckvg_ref[...]).astype(BF16)
            kv = _dot(ckv, wukv_ref[...])
            for h in range(HEADS):
                qn = _rms(q[:, 256 * h:256 * h + 128], MLA_NOPE) * qg_ref[0:1, :]
                qr = _rms(q[:, 256 * h + 128:256 * h + 256], ROT_DIM) * qg_ref[1:2, :]
                qr = _rope(qr, cos, sa, sb)
                q_scr[rows, 256 * h:256 * h + 128] = (qn * scale).astype(BF16)
                q_scr[rows, 256 * h + 128:256 * h + 256] = (qr * scale).astype(BF16)
                kn = _rms(kv[:, 128 * h:128 * h + 128], MLA_NOPE) * kg_ref[0:1, :]
                k_scr[rows, 256 * h:256 * h + 128] = kn.astype(BF16)
                k_scr[rows, 256 * h + 128:256 * h + 256] = kr
            v_scr[rows, :] = kv[:, 512:].astype(BF16)
            return carry

        lax.fori_loop(0, T // ROW_TILE, prep, 0)

    r = pl.multiple_of((qi + tile0) * ROW_TILE, ROW_TILE)
    rows = pl.ds(r, ROW_TILE)

    def attend(nk):
        for h in range(HEADS):
            s = _dot_nt(q_scr[rows, 256 * h:256 * h + 256], k_scr[0:nk, 256 * h:256 * h + 256])
            e = jnp.exp(s - jnp.max(s, axis=-1, keepdims=True))
            l = jnp.sum(e, axis=-1, keepdims=True)
            o = _dot(e.astype(BF16), v_scr[0:nk, 128 * h:128 * h + 128]) / l
            z = z_ref[0, :, 128 * h:128 * h + 128].astype(F32)
            o_ref[0, :, 128 * h:128 * h + 128] = (o * _silu(z)).astype(BF16)

    if with_ctx:
        pl.when(qi == 0)(lambda: attend(n_ctx))
        pl.when(qi != 0)(lambda: attend(T))
    else:
        attend(T)


def _mla(p, wuq, wukv, cqg, ckvg, qg, kg, rope, n_ctx, last):
    B, T, _ = p.shape
    tile0 = 1 if last else 0
    nq = T // ROW_TILE - tile0
    full = lambda shape: pl.BlockSpec(shape, lambda b, i: (0,) * len(shape))
    return pl.pallas_call(
        functools.partial(_mla_kernel, n_ctx=n_ctx, tile0=tile0, with_ctx=not last),
        grid=(B, nq),
        in_specs=[pl.BlockSpec((1, T, 512), lambda b, i: (b, 0, P_MLA_A // 512)),
                  pl.BlockSpec((1, T, 256), lambda b, i: (b, 0, P_MLA_CKV // 256)),
                  pl.BlockSpec((1, ROW_TILE, 512), lambda b, i: (b, i + tile0, P_MLA_Z // 512)),
                  full(wuq.shape), full(wukv.shape), full(cqg.shape), full(ckvg.shape),
                  full(qg.shape), full(kg.shape), full(rope.shape)],
        out_specs=pl.BlockSpec((1, ROW_TILE, 512), lambda b, i: (b, i + tile0, 0)),
        out_shape=jax.ShapeDtypeStruct((B, T, BRANCH_WIDTH), BF16),
        scratch_shapes=[pltpu.VMEM((T, 1024), BF16), pltpu.VMEM((T, 1024), BF16),
                        pltpu.VMEM((T, 512), BF16)],
        compiler_params=_params("arbitrary", "arbitrary"),
        name="mla",
    )(p, p, p, wuq, wukv, cqg, ckvg, qg, kg, rope)


def _rms_halves(x, lo_mask):
    x2 = x * x
    s_all = jnp.sum(x2, axis=-1, keepdims=True)
    s_lo = jnp.sum(jnp.where(lo_mask, x2, 0.0), axis=-1, keepdims=True)
    inv = jnp.where(lo_mask, lax.rsqrt(s_lo * (1.0 / 64) + EPS), lax.rsqrt((s_all - s_lo) * (1.0 / 64) + EPS))
    return x * inv


def _diff_kernel(kv_ref, q_ref, z_ref, g_ref, lam_ref, og_ref, rope_ref, o_ref, q0_scr, q1_scr, k_scr,
                 *, n_ctx, tile0, with_ctx, lam_init):
    T = kv_ref.shape[1]
    qi = pl.program_id(1)
    scale = DF_DQK ** -0.5
    lo_mask = lax.broadcasted_iota(jnp.int32, (1, 128), 1) < 64

    @pl.when(qi == 0)
    def _():
        def prep(i, carry):
            r = pl.multiple_of(i * ROW_TILE, ROW_TILE)
            rows = pl.ds(r, ROW_TILE)
            cos, sa, sb = rope_ref[0, rows, :], rope_ref[1, rows, :], rope_ref[2, rows, :]
            for h in range(HEADS):
                cols = slice(128 * h, 128 * h + 128)
                q = _rms_halves(q_ref[0, rows, cols].astype(F32), lo_mask) * g_ref[0:1, :]
                q = _rope(q, cos, sa, sb) * scale
                q0_scr[rows, cols] = jnp.where(lo_mask, q, 0.0).astype(BF16)
                q1_scr[rows, cols] = jnp.where(lo_mask, 0.0, q).astype(BF16)
                k = _rms_halves(kv_ref[0, rows, cols].astype(F32), lo_mask) * g_ref[1:2, :]
                k_scr[rows, cols] = _rope(k, cos, sa, sb).astype(BF16)
            return carry

        lax.fori_loop(0, T // ROW_TILE, prep, 0)

    lp = lam_ref[...]
    lam = (jnp.exp(jnp.sum(lp[0:1] * lp[1:2], axis=-1, keepdims=True))
           - jnp.exp(jnp.sum(lp[2:3] * lp[3:4], axis=-1, keepdims=True)) + lam_init)
    r = pl.multiple_of((qi + tile0) * ROW_TILE, ROW_TILE)
    rows = pl.ds(r, ROW_TILE)

    def attend(nk):
        for h in range(HEADS):
            cols = slice(128 * h, 128 * h + 128)
            k = k_scr[0:nk, cols]
            s0 = _dot_nt(q0_scr[rows, cols], k)
            s1 = _dot_nt(q1_scr[rows, cols], k)
            e0 = jnp.exp(s0 - jnp.max(s0, axis=-1, keepdims=True))
            e1 = jnp.exp(s1 - jnp.max(s1, axis=-1, keepdims=True))
            w0 = 1.0 / jnp.sum(e0, axis=-1, keepdims=True)
            w1 = lam / jnp.sum(e1, axis=-1, keepdims=True)
            pw = (e0 * w0 - e1 * w1).astype(BF16)
            o = _dot(pw, kv_ref[0, 0:nk, 512 + 128 * h:512 + 128 * h + 128])
            o = _rms(o, HEAD_V) * og_ref[...] * (1.0 - lam_init)
            z = z_ref[0, :, cols].astype(F32)
            o_ref[0, :, cols] = (o * _silu(z)).astype(BF16)

    if with_ctx:
        pl.when(qi == 0)(lambda: attend(n_ctx))
        pl.when(qi != 0)(lambda: attend(T))
    else:
        attend(T)


def _diff(p, qkg, lam_p, og, rope, n_ctx, last, lam_init):
    B, T, _ = p.shape
    tile0 = 1 if last else 0
    nq = T // ROW_TILE - tile0
    full = lambda shape: pl.BlockSpec(shape, lambda b, i: (0,) * len(shape))
    return pl.pallas_call(
        functools.partial(_diff_kernel, n_ctx=n_ctx, tile0=tile0, with_ctx=not last, lam_init=lam_init),
        grid=(B, nq),
        in_specs=[pl.BlockSpec((1, T, 1024), lambda b, i: (b, 0, P_DF_KV // 1024)),
                  pl.BlockSpec((1, T, 512), lambda b, i: (b, 0, P_DF_Q // 512)),
                  pl.BlockSpec((1, ROW_TILE, 512), lambda b, i: (b, i + tile0, P_DF_Z // 512)),
                  full(qkg.shape), full(lam_p.shape), full(og.shape), full(rope.shape)],
        out_specs=pl.BlockSpec((1, ROW_TILE, 512), lambda b, i: (b, i + tile0, 0)),
        out_shape=jax.ShapeDtypeStruct((B, T, BRANCH_WIDTH), BF16),
        scratch_shapes=[pltpu.VMEM((T, 512), BF16), pltpu.VMEM((T, 512), BF16), pltpu.VMEM((T, 512), BF16)],
        compiler_params=_params("arbitrary", "arbitrary"),
        name="diff_attn",
    )(p, p, p, qkg, lam_p, og, rope)


def _tile_order(i, n_tiles, rev):
    if not rev:
        return i
    return jnp.where(i == 0, 0, n_tiles - i)


def _chunk_tri(n, rev):
    r = lax.broadcasted_iota(jnp.int32, (n, n), 0)
    c = lax.broadcasted_iota(jnp.int32, (n, n), 1)
    same = (r // CHUNK) == (c // CHUNK)
    tri = (c >= r) if rev else (c <= r)
    return jnp.where(same & tri, 1.0, 0.0).astype(F32)


def _causal(rev):
    t = lax.broadcasted_iota(jnp.int32, (CHUNK, CHUNK), 0)
    s = lax.broadcasted_iota(jnp.int32, (CHUNK, CHUNK), 1)
    return (s >= t) if rev else (s <= t)


def _head_mask(width, group):
    lane = lax.broadcasted_iota(jnp.int32, (1, width), 1) // group
    return [lane == h for h in range(HEADS)]


def _gla_kernel(qk_ref, vz_ref, g_ref, aw_ref, ab_ref, og_ref, o_ref, of_scr, st_scr):
    T = qk_ref.shape[1]
    n_tiles = T // ROW_TILE
    n_chunk = ROW_TILE // CHUNK
    hm = _head_mask(HEADS * GLA_DK, GLA_DK)
    er = lax.broadcasted_iota(jnp.int32, (BRANCH_WIDTH, HEADS * GLA_DK), 0) // HEAD_V
    dc = lax.broadcasted_iota(jnp.int32, (BRANCH_WIDTH, HEADS * GLA_DK), 1) // GLA_DK
    block = er == dc

    def run(rev):
        tri = _chunk_tri(ROW_TILE, rev)
        causal = _causal(rev)
        st_scr[...] = jnp.zeros_like(st_scr)
        gcol = 256 if rev else 0

        def tile(i, carry):
            r = pl.multiple_of(_tile_order(i, n_tiles, rev) * ROW_TILE, ROW_TILE)
            rows = pl.ds(r, ROW_TILE)
            qk = qk_ref[0, rows, :].astype(F32)
            q = qk[:, :256] * (GLA_DK ** -0.5)
            k = qk[:, 256:]
            pre = _dot(g_ref[0, rows, :], aw_ref[:, gcol:gcol + 256]) + ab_ref[:, gcol:gcol + 256]
            la = _log_sigmoid(pre) * (1.0 / GLA_TAU)
            b = _dot_hi(tri, la)
            outs = []
            for cc in (range(n_chunk - 1, -1, -1) if rev else range(n_chunk)):
                cr = slice(CHUNK * cc, CHUNK * cc + CHUNK)
                bc, qc, kc = b[cr], q[cr], k[cr]
                b_ref_row = bc[CHUNK - 1:CHUNK] if rev else bc[0:1]
                b_end = bc[0:1] if rev else bc[CHUNK - 1:CHUNK]
                vc = vz_ref[0, pl.ds(r + CHUNK * cc, CHUNK), 0:BRANCH_WIDTH]
                st = st_scr[...]
                o = _dot_nt((qc * jnp.exp(bc)).astype(BF16), st.astype(BF16))
                q_in = qc * jnp.exp(bc - b_ref_row)
                k_in = (kc * jnp.exp(jnp.minimum(b_ref_row - bc, EXP_CLAMP))).astype(BF16)
                parts = []
                for h in range(HEADS):
                    a = _dot_nt(jnp.where(hm[h], q_in, 0.0).astype(BF16), k_in)
                    a = jnp.where(causal, a, 0.0).astype(BF16)
                    parts.append(_dot(a, vc[:, 128 * h:128 * h + 128]))
                o = o + jnp.concatenate(parts, axis=-1)
                k_up = (kc * jnp.exp(b_end - bc)).astype(BF16)
                upd = _dot_tn(vc, k_up)
                st_scr[...] = st * jnp.exp(b_end) + jnp.where(block, upd, 0.0)
                outs.append((cc, o))
            for cc, o in outs:
                orow = pl.ds(r + CHUNK * cc, CHUNK)
                if not rev:
                    of_scr[orow, :] = o
                else:
                    tot = of_scr[orow, :] + o
                    z = vz_ref[0, orow, BRANCH_WIDTH:].astype(F32)
                    for h in range(HEADS):
                        cols = slice(128 * h, 128 * h + 128)
                        y = _rms(tot[:, cols], HEAD_V) * og_ref[:, cols]
                        o_ref[0, orow, cols] = (y * _silu(z[:, cols])).astype(BF16)
            return carry

        lax.fori_loop(0, n_tiles, tile, 0)

    run(False)
    run(True)


def _gla(p, aw, ab, og):
    B, T, _ = p.shape
    full = lambda shape: pl.BlockSpec(shape, lambda b: (0,) * len(shape))
    return pl.pallas_call(
        _gla_kernel,
        grid=(B,),
        in_specs=[pl.BlockSpec((1, T, 512), lambda b: (b, 0, P_GLA_QK // 512)),
                  pl.BlockSpec((1, T, 1024), lambda b: (b, 0, P_GLA_VZ // 1024)),
                  pl.BlockSpec((1, T, 128), lambda b: (b, 0, P_GLA_G // 128)),
                  full(aw.shape), full(ab.shape), full(og.shape)],
        out_specs=pl.BlockSpec((1, T, 512), lambda b: (b, 0, 0)),
        out_shape=jax.ShapeDtypeStruct((B, T, BRANCH_WIDTH), BF16),
        scratch_shapes=[pltpu.VMEM((T, BRANCH_WIDTH), F32), pltpu.VMEM((BRANCH_WIDTH, HEADS * GLA_DK), F32)],
        compiler_params=_params("arbitrary"),
        name="gla_scan",
    )(p, p, p, aw, ab, og)


def _mlstm_kernel(m_ref, if_ref, cw_ref, cb_ref, wq_ref, wk_ref, gb_ref, og_ref, sk_ref, o_ref,
                  xc_scr, q_scr, k_scr, g_scr, hf_scr, ct_scr, n_scr, m_scr, *, n_ctx):
    T = m_ref.shape[1]
    n_tiles = T // ROW_TILE
    n_chunk = ROW_TILE // CHUNK
    ctx_tiles = n_ctx // ROW_TILE
    hm = _head_mask(HEADS * ML_DQK, ML_DQK)
    lane128 = lax.broadcasted_iota(jnp.int32, (1, 128), 1)
    is_forget = ((lane128 // HEADS) % 2) == 1
    sel_r = lax.broadcasted_iota(jnp.int32, (128, CHUNK), 0)
    row_in_tile = lax.broadcasted_iota(jnp.int32, (ROW_TILE, 1), 0)

    def sel_col(j):
        return jnp.where(sel_r == j, 1.0, 0.0).astype(F32)

    def sel_row(j):
        return jnp.where(lax.broadcasted_iota(jnp.int32, (CHUNK, 128), 1) == j, 1.0, 0.0).astype(F32)

    def prep(i, carry):
        r = pl.multiple_of(i * ROW_TILE, ROW_TILE)
        rows = pl.ds(r, ROW_TILE)
        x = m_ref[0, rows, 0:512].astype(F32)
        rp = pl.multiple_of(jnp.maximum(r - 16, 0), 16)
        rn = pl.multiple_of(jnp.minimum(r + ROW_TILE, T - 16), 16)
        prev_row = m_ref[0, pl.ds(rp, 16), 0:512].astype(F32)[15:16]
        next_row = m_ref[0, pl.ds(rn, 16), 0:512].astype(F32)[0:1]
        seg_start = (i == 0) | (i == ctx_tiles)
        seg_end = (i == ctx_tiles - 1) | (i == n_tiles - 1)
        prev_row = jnp.where(seg_start, 0.0, prev_row)
        next_row = jnp.where(seg_end, 0.0, next_row)
        xm = jnp.where(row_in_tile == 0, prev_row, pltpu.roll(x, 1, 0))
        xp = jnp.where(row_in_tile == ROW_TILE - 1, next_row, pltpu.roll(x, ROW_TILE - 1, 0))
        xc = _silu(cw_ref[0:1, :] * xm + cw_ref[1:2, :] * x + cw_ref[2:3, :] * xp + cb_ref[...])
        xc_scr[rows, :] = xc
        xb = xc.astype(BF16)
        q_scr[rows, :] = _dot(xb, wq_ref[...]).astype(BF16)
        k_scr[rows, :] = (_dot(xb, wk_ref[...]) * (ML_DQK ** -0.5)).astype(BF16)
        g = if_ref[0, rows, :].astype(F32) + gb_ref[...]
        g_scr[rows, :] = jnp.where(is_forget, _log_sigmoid(g), g)
        return carry

    lax.fori_loop(0, n_tiles, prep, 0)

    def run(rev):
        tri = _chunk_tri(ROW_TILE, rev)
        causal = _causal(rev)
        ct_scr[...] = jnp.zeros_like(ct_scr)
        n_scr[...] = jnp.zeros_like(n_scr)
        m_scr[...] = jnp.zeros_like(m_scr)
        goff = 8 if rev else 0
        last = 0 if rev else CHUNK - 1

        def tile(i, carry):
            r = pl.multiple_of(_tile_order(i, n_tiles, rev) * ROW_TILE, ROW_TILE)
            rows = pl.ds(r, ROW_TILE)
            g2 = g_scr[rows, :]
            cs = _dot_hi(tri, g2)
            outs = []
            for cc in (range(n_chunk - 1, -1, -1) if rev else range(n_chunk)):
                cr = slice(CHUNK * cc, CHUNK * cc + CHUNK)
                crow = pl.ds(r + CHUNK * cc, CHUNK)
                csc, g2c = cs[cr], g2[cr]
                qc = q_scr[crow, :]
                kc = k_scr[crow, :]
                vc = m_ref[0, crow, 512:1024]
                q_c_all = _dot_nt(qc, ct_scr[...].astype(BF16))
                qn_all = qc.astype(F32) * n_scr[...]
                parts = []
                for h in range(HEADS):
                    cols = slice(128 * h, 128 * h + 128)
                    jf, ji = goff + HEADS + h, goff + h
                    b_col = _dot_hi(csc, sel_col(jf))
                    i_col = _dot_hi(g2c, sel_col(ji))
                    b_row = _dot_nt_hi(sel_row(jf), csc)
                    i_row = _dot_nt_hi(sel_row(ji), g2c)
                    m_old = m_scr[0:1, h:h + 1]
                    log_inter = b_col + m_old
                    log_d = jnp.where(causal, b_col - b_row + i_row, -jnp.inf)
                    m_t = jnp.maximum(log_inter, jnp.max(log_d, axis=-1, keepdims=True))
                    inter_w = jnp.exp(log_inter - m_t)
                    qh = jnp.where(hm[h], qc, jnp.zeros_like(qc))
                    s = _dot_nt(qh, kc) * jnp.exp(log_d - m_t)
                    iw = inter_w[:, 0:1]
                    num = iw * q_c_all[:, cols] + _dot(s.astype(BF16), vc[:, cols])
                    qn = jnp.sum(jnp.where(hm[h], qn_all, 0.0), axis=-1, keepdims=True)
                    den = iw * qn + jnp.sum(s, axis=-1, keepdims=True)
                    parts.append(num / jnp.maximum(jnp.abs(den), jnp.exp(-m_t[:, 0:1])))
                    b_end = b_col[last:last + 1, 0:1]
                    log_upd_row = b_end - b_row + i_row
                    m_new = jnp.maximum(b_end + m_old,
                                        jnp.max(log_upd_row[0:1], axis=-1, keepdims=True))
                    decay = jnp.exp(b_end + m_old - m_new)
                    wu = jnp.exp(b_end - b_col + i_col - m_new)
                    kw = jnp.where(hm[h], kc.astype(F32) * wu[:, 0:1], 0.0)
                    ct_scr[cols, :] = decay * ct_scr[cols, :] + _dot_tn(vc[:, cols], kw.astype(BF16))
                    n_old = n_scr[...]
                    n_scr[...] = jnp.where(hm[h], decay * n_old + jnp.sum(kw, axis=0, keepdims=True), n_old)
                    m_scr[0:1, h:h + 1] = m_new
                outs.append((cc, jnp.concatenate(parts, axis=-1)))
            for cc, hcur in outs:
                orow = pl.ds(r + CHUNK * cc, CHUNK)
                if not rev:
                    hf_scr[orow, :] = hcur
                else:
                    tot = (hf_scr[orow, :] + hcur) * _sigmoid(m_ref[0, orow, 1024:1536].astype(F32))
                    z = m_ref[0, orow, 1536:2048].astype(F32)
                    xc = xc_scr[orow, :]
                    for h in range(HEADS):
                        cols = slice(128 * h, 128 * h + 128)
                        y = _rms(tot[:, cols], HEAD_V) * og_ref[:, cols]
                        y = (y + sk_ref[:, cols] * xc[:, cols]) * _silu(z[:, cols])
                        o_ref[0, orow, cols] = y.astype(BF16)
            return carry

        lax.fori_loop(0, n_tiles, tile, 0)

    run(False)
    run(True)


def _mlstm(p, cw, cb, wq, wk, gb, og, sk, n_ctx):
    B, T, _ = p.shape
    full = lambda shape: pl.BlockSpec(shape, lambda b: (0,) * len(shape))
    return pl.pallas_call(
        functools.partial(_mlstm_kernel, n_ctx=n_ctx),
        grid=(B,),
        in_specs=[pl.BlockSpec((1, T, 2048), lambda b: (b, 0, P_ML // 2048)),
                  pl.BlockSpec((1, T, 128), lambda b: (b, 0, P_ML_IF // 128)),
                  full(cw.shape), full(cb.shape), full(wq.shape), full(wk.shape), full(gb.shape),
                  full(og.shape), full(sk.shape)],
        out_specs=pl.BlockSpec((1, T, 512), lambda b: (b, 0, 0)),
        out_shape=jax.ShapeDtypeStruct((B, T, BRANCH_WIDTH), BF16),
        scratch_shapes=[pltpu.VMEM((T, BRANCH_WIDTH), F32),
                        pltpu.VMEM((T, HEADS * ML_DQK), BF16),
                        pltpu.VMEM((T, HEADS * ML_DQK), BF16),
                        pltpu.VMEM((T, 128), F32),
                        pltpu.VMEM((T, BRANCH_WIDTH), F32),
                        pltpu.VMEM((BRANCH_WIDTH, HEADS * ML_DQK), F32),
                        pltpu.VMEM((1, HEADS * ML_DQK), F32),
                        pltpu.VMEM((8, 128), F32)],
        compiler_params=_params("arbitrary"),
        name="mlstm_scan",
    )(p, p, cw, cb, wq, wk, gb, og, sk)


def _merge_kernel(ya_ref, yb_ref, yc_ref, yd_ref, gl_ref, brw_ref, wo_ref, x_ref, gate_ref, o_ref):
    acc = None
    for i, y_ref in enumerate((ya_ref, yb_ref, yc_ref, yd_ref)):
        u = _dot(y_ref[0], brw_ref[i])
        gsig = _sigmoid(gl_ref[0, :, D_MODEL * i:D_MODEL * (i + 1)].astype(F32))
        acc = gsig * u if acc is None else acc + gsig * u
    out = _dot(acc.astype(BF16), wo_ref[...])
    o_ref[0] = x_ref[0] + gate_ref[0, 0] * out


def _merge(ys, p, brw, wo, xs, gates, n_ctx, last):
    B, T, D = xs.shape
    tile0 = n_ctx // ROW_TILE if last else 0
    nt = T // ROW_TILE - tile0
    ctx_tiles = n_ctx // ROW_TILE
    ymap = lambda b, t: (b, t + tile0, 0)
    return pl.pallas_call(
        _merge_kernel,
        grid=(B, nt),
        in_specs=[pl.BlockSpec((1, ROW_TILE, BRANCH_WIDTH), ymap)] * 4 + [
            pl.BlockSpec((1, ROW_TILE, N_BRANCH * D), lambda b, t: (b, t + tile0, P_MERGE // (N_BRANCH * D))),
            pl.BlockSpec(brw.shape, lambda b, t: (0, 0, 0)),
            pl.BlockSpec(wo.shape, lambda b, t: (0, 0)),
            pl.BlockSpec((1, ROW_TILE, D), ymap),
            pl.BlockSpec((1, 1, 1, D), lambda b, t: (b, jnp.where(t + tile0 < ctx_tiles, 0, 1), 0, 0))],
        out_specs=pl.BlockSpec((1, ROW_TILE, D), lambda b, t: (b, t, 0)),
        out_shape=jax.ShapeDtypeStruct((B, nt * ROW_TILE, D), F32),
        compiler_params=_params("arbitrary", "arbitrary"),
        name="merge",
    )(*ys, p, brw, wo, xs, gates)


def _layout_w_in(w_in):
    offs, off = {}, 0
    for name, w in IN_SPLITS:
        offs[name] = (off, w)
        off += w

    def col(name):
        o, w = offs[name]
        return w_in[..., o:o + w]

    def zeros(n):
        return jnp.zeros(w_in.shape[:-1] + (n,), w_in.dtype)

    parts = [col('merge'), col('df_k'), col('df_v'), col('df_q'), col('df_z'),
             col('ml_x'), col('ml_v'), col('ml_o'), col('ml_z'),
             col('gla_v'), col('gla_z'), col('mla_z'), col('gla_q'), col('gla_k'),
             col('mla_cq'), col('mla_kr'), zeros(64),
             col('mla_ckv'), col('gla_af'), col('gla_ab'), zeros(96), col('ml_if'), zeros(112)]
    out = jnp.concatenate(parts, axis=-1).astype(BF16)
    assert out.shape[-1] == P_WIDTH
    return out


def _rope_tables(rows, n_ctx):
    quarter = ROT_DIM // 4
    inv_freq = ROPE_BASE ** (-jnp.arange(quarter, dtype=F32) / quarter)
    row = jnp.repeat(jnp.arange(rows, dtype=F32), GRID_W)
    col = jnp.tile(jnp.arange(GRID_W, dtype=F32), rows)
    ar = row[:, None] * inv_freq
    ac = col[:, None] * inv_freq
    ang = jnp.concatenate([ar, ar, ac, ac], axis=-1)
    cos = jnp.concatenate([jnp.ones((n_ctx, ROT_DIM), F32), jnp.cos(ang)], axis=0)
    sin = jnp.concatenate([jnp.zeros((n_ctx, ROT_DIM), F32), jnp.sin(ang)], axis=0)
    even = ((jnp.arange(ROT_DIM) // quarter) % 2 == 0)[None, :]
    sin_a = jnp.where(even, -sin, 0.0)
    sin_b = jnp.where(even, 0.0, sin)
    zero = jnp.zeros_like(cos)
    both = jnp.stack([jnp.tile(cos, (1, 2)), jnp.tile(sin_a, (1, 2)), jnp.tile(sin_b, (1, 2))])
    half = jnp.stack([jnp.concatenate([cos, zero], -1), jnp.concatenate([sin_a, zero], -1),
                      jnp.concatenate([sin_b, zero], -1)])
    return half, both


def _pad_lanes(v, n):
    return jnp.concatenate([v, jnp.zeros(v.shape[:-1] + (n - v.shape[-1],), v.dtype)], axis=-1)


def kernel(x, c, ctx, c_ctx, ada_w, ada_b, norm_g, w_in, mla_cq_g, mla_ckv_g, mla_wuq, mla_wukv, mla_q_g,
           mla_k_g, gla_a_w, gla_a_b, gla_out_g, ml_conv_w, ml_conv_b, ml_wq, ml_wk, ml_gate_b, ml_out_g,
           ml_skip, df_qk_g, df_lambda, df_out_g, br_w, w_out):
    B, S, D = x.shape
    n_ctx = ctx.shape[1]
    L = ada_w.shape[0]
    assert D == D_MODEL and n_ctx == ROW_TILE and S % ROW_TILE == 0 and S % GRID_W == 0

    rope_half, rope_both = _rope_tables(S // GRID_W, n_ctx)
    w_in_p = _layout_w_in(w_in)

    n_rows = -(-(B + 1) // 8) * 8
    cc = jnp.concatenate([c, c_ctx[None], jnp.zeros((n_rows - B - 1, D), F32)], axis=0)
    mod_all = _modulation(cc, ada_w, ada_b)

    xs = jnp.concatenate([ctx, x], axis=1)
    for l in range(L):
        last = l == L - 1
        lam_init = 0.8 - 0.6 * math.exp(-0.3 * l)
        m3 = mod_all[l].reshape(n_rows, 3, D)
        lat, cx = m3[:B], jnp.broadcast_to(m3[B][None], (B, 3, D))
        mod = jnp.concatenate([cx, lat, jnp.zeros((B, 2, D), F32)], axis=1)
        gates = jnp.stack([cx[:, 2], lat[:, 2]], axis=1).reshape(B, 2, 1, D)

        p = _in_projection(xs, mod, norm_g[l], w_in_p[l], n_ctx)

        wq4 = mla_wuq[l].reshape(MLA_Q_LORA, HEADS, MLA_NOPE + ROT_DIM)
        wuq = _pad_lanes(wq4, 256).reshape(MLA_Q_LORA, HEADS * 256).astype(BF16)
        wkv4 = mla_wukv[l].reshape(MLA_KV_LORA, HEADS, MLA_NOPE + HEAD_V)
        wukv = jnp.concatenate([wkv4[..., :MLA_NOPE].reshape(MLA_KV_LORA, -1),
                                wkv4[..., MLA_NOPE:].reshape(MLA_KV_LORA, -1)], axis=-1).astype(BF16)
        qg = jnp.stack([mla_q_g[l, :MLA_NOPE], _pad_lanes(mla_q_g[l, MLA_NOPE:], 128)])
        kg = jnp.stack([mla_k_g[l, :MLA_NOPE], _pad_lanes(mla_k_g[l, MLA_NOPE:], 128)])
        y_mla = _mla(p, wuq, wukv, mla_cq_g[l][None], mla_ckv_g[l][None], qg, kg, rope_half, n_ctx, last)

        qkg = jnp.tile(df_qk_g[l], (1, 2))
        y_df = _diff(p, qkg, df_lambda[l], df_out_g[l][None], rope_both, n_ctx, last, lam_init)

        aw = jnp.zeros((128, 512), F32)
        aw = aw.at[0:16, 0:256].set(gla_a_w[l, 0]).at[16:32, 256:512].set(gla_a_w[l, 1]).astype(BF16)
        y_gla = _gla(p, aw, gla_a_b[l].reshape(1, 512), gla_out_g[l][None])

        wq_bd = jnp.zeros((BRANCH_WIDTH, HEADS * ML_DQK), F32)
        wk_bd = jnp.zeros((BRANCH_WIDTH, HEADS * ML_DQK), F32)
        for h in range(HEADS):
            wq_bd = wq_bd.at[128 * h:128 * h + 128, 64 * h:64 * h + 64].set(ml_wq[l, h])
            wk_bd = wk_bd.at[128 * h:128 * h + 128, 64 * h:64 * h + 64].set(ml_wk[l, h])
        gb = _pad_lanes(ml_gate_b[l].reshape(1, 16), 128)
        y_ml = _mlstm(p, ml_conv_w[l], ml_conv_b[l][None], wq_bd.astype(BF16), wk_bd.astype(BF16), gb,
                      ml_out_g[l][None], ml_skip[l][None], n_ctx)

        xs = _merge((y_mla, y_gla, y_ml, y_df), p, br_w[l].astype(BF16), w_out[l].astype(BF16), xs, gates,
                    n_ctx, last)
    return xs
```

```python
import functools
import math

import jax
import jax.numpy as jnp
from jax import lax
from jax.experimental import pallas as pl
from jax.experimental.pallas import tpu as pltpu

F32 = jnp.float32
BF16 = jnp.bfloat16

D_MODEL = 1024
GRID_W = 64
EPS = 1e-6
ROPE_BASE = 10000.0
ROT_DIM = 64
CHUNK = 64
N_BRANCH = 4
BRANCH_WIDTH = 512
HEADS = 4
HEAD_V = BRANCH_WIDTH // HEADS
MLA_NOPE = 128
MLA_Q_LORA = 384
MLA_KV_LORA = 256
GLA_DK = 64
GLA_GATE_RANK = 16
GLA_TAU = 16.0
ML_DQK = 64
DF_DQK = 64

IN_SPLITS = (
    ('mla_cq', 384), ('mla_ckv', 256), ('mla_kr', 64), ('mla_z', 512),
    ('gla_q', 256), ('gla_k', 256), ('gla_v', 512), ('gla_af', 16), ('gla_ab', 16), ('gla_z', 512),
    ('ml_x', 512), ('ml_v', 512), ('ml_o', 512), ('ml_if', 16), ('ml_z', 512),
    ('df_q', 512), ('df_k', 512), ('df_v', 512), ('df_z', 512),
    ('merge', 4096),
)

P_MERGE = 0
P_DF_KV = 4096
P_DF_Q = 5120
P_DF_Z = 5632
P_ML = 6144
P_GLA_VZ = 8192
P_MLA_Z = 9216
P_GLA_QK = 9728
P_MLA_A = 10240
P_MLA_CKV = 10752
P_GLA_G = 11008
P_ML_IF = 11136
P_WIDTH = 11264

ROW_TILE = 256
IN_PROJ_COLS = 1024
MERGE_ROWS = 768
VMEM_LIMIT = 56 * 1024 * 1024
EXP_CLAMP = 80.0


def _dot(a, b):
    return jnp.dot(a, b, preferred_element_type=F32)


def _dot_nt(a, b):
    return lax.dot_general(a, b, (((1,), (1,)), ((), ())), preferred_element_type=F32)


def _dot_tn(a, b):
    return lax.dot_general(a, b, (((0,), (0,)), ((), ())), preferred_element_type=F32)


def _dot_hi(a, b):
    return jnp.dot(a, b, preferred_element_type=F32, precision=lax.Precision.HIGHEST)


def _dot_nt_hi(a, b):
    return lax.dot_general(a, b, (((1,), (1,)), ((), ())), preferred_element_type=F32,
                           precision=lax.Precision.HIGHEST)


def _sigmoid(x):
    return 1.0 / (1.0 + jnp.exp(-x))


def _silu(x):
    return x * _sigmoid(x)


def _log_sigmoid(x):
    return jnp.minimum(x, 0.0) - jnp.log(1.0 + jnp.exp(-jnp.abs(x)))


def _rms(x, n):
    return x * lax.rsqrt(jnp.sum(x * x, axis=-1, keepdims=True) * (1.0 / n) + EPS)


def _rope(x, cos, sin_a, sin_b):
    return x * cos + pltpu.roll(x, 112, 1) * sin_a + pltpu.roll(x, 16, 1) * sin_b


def _ones_col(rows):
    return jnp.where(lax.broadcasted_iota(jnp.int32, (rows, HEAD_V), 1) == 0, 1.0, 0.0).astype(BF16)


def _softmax_pv(s, v_ext):
    e = jnp.exp((s - jnp.max(s, axis=-1, keepdims=True)).astype(BF16))
    o = _dot(e, v_ext)
    return o[:, 0:HEAD_V] / o[:, HEAD_V:HEAD_V + 1]


def _params(*sem):
    return pltpu.CompilerParams(dimension_semantics=sem, vmem_limit_bytes=VMEM_LIMIT)


def _mod_kernel(c_ref, w_ref, b_ref, o_ref):
    s = _silu(c_ref[...])
    o_ref[0] = _dot(s.astype(BF16), w_ref[0].astype(BF16)) + b_ref[0]


def _modulation(cc, ada_w, ada_b):
    L, D, _ = ada_w.shape
    R = cc.shape[0]
    return pl.pallas_call(
        _mod_kernel,
        grid=(L, 3),
        in_specs=[pl.BlockSpec((R, D), lambda l, j: (0, 0)),
                  pl.BlockSpec((1, D, D), lambda l, j: (l, 0, j)),
                  pl.BlockSpec((1, 1, D), lambda l, j: (l, 0, j))],
        out_specs=pl.BlockSpec((1, R, D), lambda l, j: (l, 0, j)),
        out_shape=jax.ShapeDtypeStruct((L, R, 3 * D), F32),
        compiler_params=_params("arbitrary", "arbitrary"),
        name="modulation",
    )(cc, ada_w, ada_b.reshape(L, 1, 3 * D))


def _inproj_kernel(x_ref, mod_ref, g_ref, w_ref, o_ref, h_scr, *, n_ctx):
    T = x_ref.shape[1]

    @pl.when(pl.program_id(1) == 0)
    def _():
        g = g_ref[...]
        for r0 in range(0, T, ROW_TILE):
            k = 0 if r0 < n_ctx else 3
            y = _rms(x_ref[0, r0:r0 + ROW_TILE, :], D_MODEL) * g
            h = y * (1.0 + mod_ref[0, k + 1:k + 2, :]) + mod_ref[0, k:k + 1, :]
            h_scr[r0:r0 + ROW_TILE, :] = h.astype(BF16)

    for r0 in range(0, T, ROW_TILE):
        o_ref[0, r0:r0 + ROW_TILE, :] = _dot(h_scr[r0:r0 + ROW_TILE, :], w_ref[...]).astype(BF16)


def _in_projection(xs, mod, norm_g, w, n_ctx):
    B, T, D = xs.shape
    n_col = P_WIDTH // IN_PROJ_COLS
    return pl.pallas_call(
        functools.partial(_inproj_kernel, n_ctx=n_ctx),
        grid=(B, n_col),
        in_specs=[pl.BlockSpec((1, T, D), lambda b, j: (b, 0, 0)),
                  pl.BlockSpec((1, 8, D), lambda b, j: (b, 0, 0)),
                  pl.BlockSpec((1, D), lambda b, j: (0, 0)),
                  pl.BlockSpec((D, IN_PROJ_COLS), lambda b, j: (0, j))],
        out_specs=pl.BlockSpec((1, T, IN_PROJ_COLS), lambda b, j: (b, 0, j)),
        out_shape=jax.ShapeDtypeStruct((B, T, P_WIDTH), BF16),
        scratch_shapes=[pltpu.VMEM((T, D), BF16)],
        compiler_params=_params("arbitrary", "arbitrary"),
        name="in_projection",
    )(xs, mod, norm_g.reshape(1, D), w)


def _mla_kernel(pa_ref, pc_ref, z_ref, wuq_ref, wukv_ref, cqg_ref, ckvg_ref, qg_ref, kg_ref, rope_ref,
                o_ref, q_scr, k_scr, v_scr, *, n_ctx, tile0, with_ctx):
    T = pa_ref.shape[1]
    qi = pl.program_id(1)
    scale = (MLA_NOPE + ROT_DIM) ** -0.5

    @pl.when(qi == 0)
    def _():
        def prep(i, carry):
            r = pl.multiple_of(i * ROW_TILE, ROW_TILE)
            rows = pl.ds(r, ROW_TILE)
            cos, sa, sb = rope_ref[0, rows, :], rope_ref[1, rows, :], rope_ref[2, rows, :]
            pa = pa_ref[0, rows, :].astype(F32)
            cq = (_rms(pa[:, :MLA_Q_LORA], MLA_Q_LORA) * cqg_ref[...]).astype(BF16)
            q = _dot(cq, wuq_ref[...])
            kr = _rms(pa[:, MLA_Q_LORA:], ROT_DIM) * kg_ref[1:2, :]
            kr = _rope(kr, cos, sa, sb).astype(BF16)
            ckv = (_rms(pc_ref[0, rows, :].astype(F32), MLA_KV_LORA) * ckvg_ref[...]).astype(BF16)
            kv = _dot(ckv, wukv_ref[...])
            for h in range(HEADS):
                qn = _rms(q[:, 256 * h:256 * h + 128], MLA_NOPE) * qg_ref[0:1, :]
                qr = _rms(q[:, 256 * h + 128:256 * h + 256], ROT_DIM) * qg_ref[1:2, :]
                qr = _rope(qr, cos, sa, sb)
                q_scr[rows, 256 * h:256 * h + 128] = (qn * scale).astype(BF16)
                q_scr[rows, 256 * h + 128:256 * h + 256] = (qr * scale).astype(BF16)
                kn = _rms(kv[:, 128 * h:128 * h + 128], MLA_NOPE) * kg_ref[0:1, :]
                k_scr[rows, 256 * h:256 * h + 128] = kn.astype(BF16)
                k_scr[rows, 256 * h + 128:256 * h + 256] = kr
                v_scr[rows, 256 * h:256 * h + 128] = kv[:, 512 + 128 * h:512 + 128 * h + 128].astype(BF16)
                v_scr[rows, 256 * h + 128:256 * h + 256] = _ones_col(ROW_TILE)
            return carry

        lax.fori_loop(0, T // ROW_TILE, prep, 0)

    r = pl.multiple_of((qi + tile0) * ROW_TILE, ROW_TILE)
    rows = pl.ds(r, ROW_TILE)

    def attend(nk):
        for h in range(HEADS):
            s = _dot_nt(q_scr[rows, 256 * h:256 * h + 256], k_scr[0:nk, 256 * h:256 * h + 256])
            o = _softmax_pv(s, v_scr[0:nk, 256 * h:256 * h + 256])
            z = z_ref[0, :, 128 * h:128 * h + 128].astype(F32)
            o_ref[0, :, 128 * h:128 * h + 128] = (o * _silu(z)).astype(BF16)

    if with_ctx:
        pl.when(qi == 0)(lambda: attend(n_ctx))
        pl.when(qi != 0)(lambda: attend(T))
    else:
        attend(T)


def _mla(p, wuq, wukv, cqg, ckvg, qg, kg, rope, n_ctx, last):
    B, T, _ = p.shape
    tile0 = 1 if last else 0
    nq = T // ROW_TILE - tile0
    full = lambda shape: pl.BlockSpec(shape, lambda b, i: (0,) * len(shape))
    return pl.pallas_call(
        functools.partial(_mla_kernel, n_ctx=n_ctx, tile0=tile0, with_ctx=not last),
        grid=(B, nq),
        in_specs=[pl.BlockSpec((1, T, 512), lambda b, i: (b, 0, P_MLA_A // 512)),
                  pl.BlockSpec((1, T, 256), lambda b, i: (b, 0, P_MLA_CKV // 256)),
                  pl.BlockSpec((1, ROW_TILE, 512), lambda b, i: (b, i + tile0, P_MLA_Z // 512)),
                  full(wuq.shape), full(wukv.shape), full(cqg.shape), full(ckvg.shape),
                  full(qg.shape), full(kg.shape), full(rope.shape)],
        out_specs=pl.BlockSpec((1, ROW_TILE, 512), lambda b, i: (b, i + tile0, 0)),
        out_shape=jax.ShapeDtypeStruct((B, T, BRANCH_WIDTH), BF16),
        scratch_shapes=[pltpu.VMEM((T, 1024), BF16), pltpu.VMEM((T, 1024), BF16),
                        pltpu.VMEM((T, 1024), BF16)],
        compiler_params=_params("arbitrary", "arbitrary"),
        name="mla",
    )(p, p, p, wuq, wukv, cqg, ckvg, qg, kg, rope)


def _rms_halves(x, lo_mask):
    x2 = x * x
    s_all = jnp.sum(x2, axis=-1, keepdims=True)
    s_lo = jnp.sum(jnp.where(lo_mask, x2, 0.0), axis=-1, keepdims=True)
    inv = jnp.where(lo_mask, lax.rsqrt(s_lo * (1.0 / 64) + EPS), lax.rsqrt((s_all - s_lo) * (1.0 / 64) + EPS))
    return x * inv


def _diff_kernel(kv_ref, q_ref, z_ref, g_ref, lam_ref, og_ref, rope_ref, o_ref, q0_scr, q1_scr, k_scr, v_scr,
                 *, n_ctx, tile0, with_ctx, lam_init):
    T = kv_ref.shape[1]
    qi = pl.program_id(1)
    scale = DF_DQK ** -0.5
    lo_mask = lax.broadcasted_iota(jnp.int32, (1, 128), 1) < 64

    @pl.when(qi == 0)
    def _():
        def prep(i, carry):
            r = pl.multiple_of(i * ROW_TILE, ROW_TILE)
            rows = pl.ds(r, ROW_TILE)
            cos, sa, sb = rope_ref[0, rows, :], rope_ref[1, rows, :], rope_ref[2, rows, :]
            for h in range(HEADS):
                cols = slice(128 * h, 128 * h + 128)
                q = _rms_halves(q_ref[0, rows, cols].astype(F32), lo_mask) * g_ref[0:1, :]
                q = _rope(q, cos, sa, sb) * scale
                q0_scr[rows, cols] = jnp.where(lo_mask, q, 0.0).astype(BF16)
                q1_scr[rows, cols] = jnp.where(lo_mask, 0.0, q).astype(BF16)
                k = _rms_halves(kv_ref[0, rows, cols].astype(F32), lo_mask) * g_ref[1:2, :]
                k_scr[rows, cols] = _rope(k, cos, sa, sb).astype(BF16)
                v_scr[rows, 256 * h:256 * h + 128] = kv_ref[0, rows, 512 + 128 * h:512 + 128 * h + 128]
                v_scr[rows, 256 * h + 128:256 * h + 256] = _ones_col(ROW_TILE)
            return carry

        lax.fori_loop(0, T // ROW_TILE, prep, 0)

    lp = lam_ref[...]
    lam = (jnp.exp(jnp.sum(lp[0:1] * lp[1:2], axis=-1, keepdims=True))
           - jnp.exp(jnp.sum(lp[2:3] * lp[3:4], axis=-1, keepdims=True)) + lam_init)
    r = pl.multiple_of((qi + tile0) * ROW_TILE, ROW_TILE)
    rows = pl.ds(r, ROW_TILE)

    def attend(nk):
        for h in range(HEADS):
            cols = slice(128 * h, 128 * h + 128)
            k = k_scr[0:nk, cols]
            v_ext = v_scr[0:nk, 256 * h:256 * h + 256]
            o = (_softmax_pv(_dot_nt(q0_scr[rows, cols], k), v_ext)
                 - lam * _softmax_pv(_dot_nt(q1_scr[rows, cols], k), v_ext))
            o = _rms(o, HEAD_V) * og_ref[...] * (1.0 - lam_init)
            z = z_ref[0, :, cols].astype(F32)
            o_ref[0, :, cols] = (o * _silu(z)).astype(BF16)

    if with_ctx:
        pl.when(qi == 0)(lambda: attend(n_ctx))
        pl.when(qi != 0)(lambda: attend(T))
    else:
        attend(T)


def _diff(p, qkg, lam_p, og, rope, n_ctx, last, lam_init):
    B, T, _ = p.shape
    tile0 = 1 if last else 0
    nq = T // ROW_TILE - tile0
    full = lambda shape: pl.BlockSpec(shape, lambda b, i: (0,) * len(shape))
    return pl.pallas_call(
        functools.partial(_diff_kernel, n_ctx=n_ctx, tile0=tile0, with_ctx=not last, lam_init=lam_init),
        grid=(B, nq),
        in_specs=[pl.BlockSpec((1, T, 1024), lambda b, i: (b, 0, P_DF_KV // 1024)),
                  pl.BlockSpec((1, T, 512), lambda b, i: (b, 0, P_DF_Q // 512)),
                  pl.BlockSpec((1, ROW_TILE, 512), lambda b, i: (b, i + tile0, P_DF_Z // 512)),
                  full(qkg.shape), full(lam_p.shape), full(og.shape), full(rope.shape)],
        out_specs=pl.BlockSpec((1, ROW_TILE, 512), lambda b, i: (b, i + tile0, 0)),
        out_shape=jax.ShapeDtypeStruct((B, T, BRANCH_WIDTH), BF16),
        scratch_shapes=[pltpu.VMEM((T, 512), BF16), pltpu.VMEM((T, 512), BF16), pltpu.VMEM((T, 512), BF16),
                        pltpu.VMEM((T, 1024), BF16)],
        compiler_params=_params("arbitrary", "arbitrary"),
        name="diff_attn",
    )(p, p, p, qkg, lam_p, og, rope)


def _tile_order(i, n_tiles, rev):
    if not rev:
        return i
    return jnp.where(i == 0, 0, n_tiles - i)


def _chunk_tri(n, rev):
    r = lax.broadcasted_iota(jnp.int32, (n, n), 0)
    c = lax.broadcasted_iota(jnp.int32, (n, n), 1)
    same = (r // CHUNK) == (c // CHUNK)
    tri = (c >= r) if rev else (c <= r)
    return jnp.where(same & tri, 1.0, 0.0).astype(F32)


def _causal(rev):
    t = lax.broadcasted_iota(jnp.int32, (CHUNK, CHUNK), 0)
    s = lax.broadcasted_iota(jnp.int32, (CHUNK, CHUNK), 1)
    return (s >= t) if rev else (s <= t)


def _head_mask(width, group):
    lane = lax.broadcasted_iota(jnp.int32, (1, width), 1) // group
    return [lane == h for h in range(HEADS)]


def _gla_kernel(qk_ref, vz_ref, g_ref, aw_ref, ab_ref, og_ref, o_ref, of_scr, ob_scr, st_scr):
    T = qk_ref.shape[1]
    n_tiles = T // ROW_TILE
    n_chunk = ROW_TILE // CHUNK
    hm = _head_mask(HEADS * GLA_DK, GLA_DK)
    er = lax.broadcasted_iota(jnp.int32, (BRANCH_WIDTH, HEADS * GLA_DK), 0) // HEAD_V
    dc = lax.broadcasted_iota(jnp.int32, (BRANCH_WIDTH, HEADS * GLA_DK), 1) // GLA_DK
    block = er == dc
    tris = (_chunk_tri(ROW_TILE, False), _chunk_tri(ROW_TILE, True))
    st_scr[...] = jnp.zeros_like(st_scr)

    def load_tile(t, rev):
        rows = pl.ds(pl.multiple_of(t * ROW_TILE, ROW_TILE), ROW_TILE)
        gcol = 256 if rev else 0
        qk = qk_ref[0, rows, :].astype(F32)
        pre = _dot(g_ref[0, rows, :], aw_ref[:, gcol:gcol + 256]) + ab_ref[:, gcol:gcol + 256]
        la = _log_sigmoid(pre) * (1.0 / GLA_TAU)
        return qk[:, :256] * (GLA_DK ** -0.5), qk[:, 256:], _dot_hi(tris[rev], la)

    def chunk_step(t, cc, rev, q, k, b):
        cr = slice(CHUNK * cc, CHUNK * cc + CHUNK)
        crow = pl.ds(pl.multiple_of(t * ROW_TILE + CHUNK * cc, CHUNK), CHUNK)
        d = 1 if rev else 0
        causal = _causal(rev)
        bc, qc, kc = b[cr], q[cr], k[cr]
        b_ref_row = bc[CHUNK - 1:CHUNK] if rev else bc[0:1]
        b_end = bc[0:1] if rev else bc[CHUNK - 1:CHUNK]
        vc = vz_ref[0, crow, 0:BRANCH_WIDTH]
        st = st_scr[d]
        o = _dot_nt((qc * jnp.exp(bc)).astype(BF16), st.astype(BF16))
        q_in = qc * jnp.exp(bc - b_ref_row)
        k_in = (kc * jnp.exp(jnp.minimum(b_ref_row - bc, EXP_CLAMP))).astype(BF16)
        parts = []
        for h in range(HEADS):
            a = _dot_nt(jnp.where(hm[h], q_in, 0.0).astype(BF16), k_in)
            a = jnp.where(causal, a, 0.0).astype(BF16)
            parts.append(_dot(a, vc[:, 128 * h:128 * h + 128]))
        k_up = (kc * jnp.exp(b_end - bc)).astype(BF16)
        upd = _dot_tn(vc, k_up)
        st_scr[d] = st * jnp.exp(b_end) + jnp.where(block, upd, 0.0)
        (ob_scr if rev else of_scr)[crow, :] = o + jnp.concatenate(parts, axis=-1)

    def tile(i, carry):
        t_f, t_b = i, _tile_order(i, n_tiles, True)
        qf, kf, bf = load_tile(t_f, False)
        qb, kb, bb = load_tile(t_b, True)
        for cc in range(n_chunk):
            chunk_step(t_f, cc, False, qf, kf, bf)
            chunk_step(t_b, n_chunk - 1 - cc, True, qb, kb, bb)
        return carry

    lax.fori_loop(0, n_tiles, tile, 0)

    def finish(i, carry):
        rows = pl.ds(pl.multiple_of(i * ROW_TILE, ROW_TILE), ROW_TILE)
        tot = of_scr[rows, :] + ob_scr[rows, :]
        z = vz_ref[0, rows, BRANCH_WIDTH:].astype(F32)
        for h in range(HEADS):
            cols = slice(128 * h, 128 * h + 128)
            y = _rms(tot[:, cols], HEAD_V) * og_ref[:, cols]
            o_ref[0, rows, cols] = (y * _silu(z[:, cols])).astype(BF16)
        return carry

    lax.fori_loop(0, n_tiles, finish, 0)


def _gla(p, aw, ab, og):
    B, T, _ = p.shape
    full = lambda shape: pl.BlockSpec(shape, lambda b: (0,) * len(shape))
    return pl.pallas_call(
        _gla_kernel,
        grid=(B,),
        in_specs=[pl.BlockSpec((1, T, 512), lambda b: (b, 0, P_GLA_QK // 512)),
                  pl.BlockSpec((1, T, 1024), lambda b: (b, 0, P_GLA_VZ // 1024)),
                  pl.BlockSpec((1, T, 128), lambda b: (b, 0, P_GLA_G // 128)),
                  full(aw.shape), full(ab.shape), full(og.shape)],
        out_specs=pl.BlockSpec((1, T, 512), lambda b: (b, 0, 0)),
        out_shape=jax.ShapeDtypeStruct((B, T, BRANCH_WIDTH), BF16),
        scratch_shapes=[pltpu.VMEM((T, BRANCH_WIDTH), F32), pltpu.VMEM((T, BRANCH_WIDTH), F32),
                        pltpu.VMEM((2, BRANCH_WIDTH, HEADS * GLA_DK), F32)],
        compiler_params=_params("arbitrary"),
        name="gla_scan",
    )(p, p, p, aw, ab, og)


def _mlstm_kernel(m_ref, if_ref, cw_ref, cb_ref, wq_ref, wkt_ref, gb_ref, og_ref, sk_ref, o_ref,
                  xc_scr, q_scr, kt_scr, at_scr, bt_scr, hf_scr, hb_scr, c_scr, *, n_ctx):
    T = m_ref.shape[1]
    n_tiles = T // ROW_TILE
    n_chunk = ROW_TILE // CHUNK
    ctx_tiles = n_ctx // ROW_TILE
    hm = _head_mask(HEADS * ML_DQK, ML_DQK)
    lane128 = lax.broadcasted_iota(jnp.int32, (1, 128), 1)
    is_forget = ((lane128 // HEADS) % 2) == 1
    row_in_tile = lax.broadcasted_iota(jnp.int32, (ROW_TILE, 1), 0)
    tri_f = _chunk_tri(ROW_TILE, False)
    tri_b = _chunk_tri(ROW_TILE, True)

    def prep(i, carry):
        r = pl.multiple_of(i * ROW_TILE, ROW_TILE)
        rows = pl.ds(r, ROW_TILE)
        x = m_ref[0, rows, 0:512].astype(F32)
        rp = pl.multiple_of(jnp.maximum(r - 16, 0), 16)
        rn = pl.multiple_of(jnp.minimum(r + ROW_TILE, T - 16), 16)
        prev_row = m_ref[0, pl.ds(rp, 16), 0:512].astype(F32)[15:16]
        next_row = m_ref[0, pl.ds(rn, 16), 0:512].astype(F32)[0:1]
        seg_start = (i == 0) | (i == ctx_tiles)
        seg_end = (i == ctx_tiles - 1) | (i == n_tiles - 1)
        prev_row = jnp.where(seg_start, 0.0, prev_row)
        next_row = jnp.where(seg_end, 0.0, next_row)
        xm = jnp.where(row_in_tile == 0, prev_row, pltpu.roll(x, 1, 0))
        xp = jnp.where(row_in_tile == ROW_TILE - 1, next_row, pltpu.roll(x, ROW_TILE - 1, 0))
        xc = _silu(cw_ref[0:1, :] * xm + cw_ref[1:2, :] * x + cw_ref[2:3, :] * xp + cb_ref[...])
        xc_scr[rows, :] = xc
        xb = xc.astype(BF16)
        q_scr[rows, :] = _dot(xb, wq_ref[...]).astype(BF16)
        g = if_ref[0, rows, :].astype(F32) + gb_ref[...]
        g2 = jnp.where(is_forget, _log_sigmoid(g), g)
        cs = jnp.where(lane128 < 2 * HEADS, _dot_hi(tri_f, g2), _dot_hi(tri_b, g2))
        b = pltpu.roll(cs, 128 - HEADS, 1)
        a = g2 - b
        for cc in range(n_chunk):
            cr = slice(CHUNK * cc, CHUNK * cc + CHUNK)
            c = i * n_chunk + cc
            kt_scr[c] = (_dot_nt(wkt_ref[...], xb[cr]) * (ML_DQK ** -0.5)).astype(BF16)
            at_scr[c] = a[cr].T[0:16, :]
            bt_scr[c] = b[cr].T[0:16, :]
        return carry

    lax.fori_loop(0, n_tiles, prep, 0)

    c_scr[...] = jnp.zeros_like(c_scr)
    eye = _causal(False) & _causal(True)
    ones_col = jnp.where(lax.broadcasted_iota(jnp.int32, (CHUNK, HEAD_V), 1) == 0, 1.0, 0.0).astype(BF16)

    def chunk_step(c, rev, m_in):
        crow = pl.ds(pl.multiple_of(c * CHUNK, CHUNK), CHUNK)
        d = 1 if rev else 0
        causal = _causal(rev)
        last = 0 if rev else CHUNK - 1
        qc = q_scr[crow, :]
        kt = kt_scr[c]
        vc = m_ref[0, crow, 512:1024]
        at = at_scr[c]
        bt = bt_scr[c]
        parts, m_out = [], []
        for h in range(HEADS):
            j = 2 * HEADS * d + h
            hr = slice(ML_DQK * h, ML_DQK * h + ML_DQK)
            a_row, b_row, m_old = at[j:j + 1, :], bt[j:j + 1, :], m_in[h]
            pm = jnp.where(causal, a_row, -jnp.inf)
            m_run = jnp.maximum(m_old, jnp.max(pm, axis=-1, keepdims=True))
            qh = jnp.where(hm[h], qc, jnp.zeros_like(qc))
            s = _dot(qh, kt) * jnp.exp(pm - m_run)
            v_ext = jnp.concatenate([vc[:, 128 * h:128 * h + 128], ones_col], axis=-1)
            num = jnp.exp(m_old - m_run) * _dot(qh, c_scr[d].astype(BF16)) + _dot(s.astype(BF16), v_ext)
            b_col = jnp.sum(jnp.where(eye, b_row, 0.0), axis=-1, keepdims=True)
            den = num[:, HEAD_V:HEAD_V + 1]
            parts.append(num[:, 0:HEAD_V] / jnp.maximum(jnp.abs(den), jnp.exp(-(b_col + m_run))))
            m_last = m_run[last:last + 1]
            ktw = (kt[hr, :].astype(F32) * jnp.exp(a_row - m_last)).astype(BF16)
            c_scr[d, hr, :] = jnp.exp(m_old - m_last) * c_scr[d, hr, :] + _dot(ktw, v_ext)
            m_out.append(b_row[:, last:last + 1] + m_last)
        (hb_scr if rev else hf_scr)[crow, :] = jnp.concatenate(parts, axis=-1)
        return m_out

    def tile(i, carry):
        m_f, m_b = list(carry[:HEADS]), list(carry[HEADS:])
        c_f = i * n_chunk
        c_b = _tile_order(i, n_tiles, True) * n_chunk + (n_chunk - 1)
        for cc in range(n_chunk):
            m_f = chunk_step(c_f + cc, False, m_f)
            m_b = chunk_step(c_b - cc, True, m_b)
        return tuple(m_f) + tuple(m_b)

    lax.fori_loop(0, n_tiles, tile, tuple(jnp.zeros((1, 1), F32) for _ in range(2 * HEADS)))

    def finish(i, carry):
        rows = pl.ds(pl.multiple_of(i * ROW_TILE, ROW_TILE), ROW_TILE)
        tot = (hf_scr[rows, :] + hb_scr[rows, :]) * _sigmoid(m_ref[0, rows, 1024:1536].astype(F32))
        z = m_ref[0, rows, 1536:2048].astype(F32)
        xc = xc_scr[rows, :]
        for h in range(HEADS):
            cols = slice(128 * h, 128 * h + 128)
            y = _rms(tot[:, cols], HEAD_V) * og_ref[:, cols]
            y = (y + sk_ref[:, cols] * xc[:, cols]) * _silu(z[:, cols])
            o_ref[0, rows, cols] = y.astype(BF16)
        return carry

    lax.fori_loop(0, n_tiles, finish, 0)


def _mlstm(p, cw, cb, wq, wkt, gb, og, sk, n_ctx):
    B, T, _ = p.shape
    full = lambda shape: pl.BlockSpec(shape, lambda b: (0,) * len(shape))
    return pl.pallas_call(
        functools.partial(_mlstm_kernel, n_ctx=n_ctx),
        grid=(B,),
        in_specs=[pl.BlockSpec((1, T, 2048), lambda b: (b, 0, P_ML // 2048)),
                  pl.BlockSpec((1, T, 128), lambda b: (b, 0, P_ML_IF // 128)),
                  full(cw.shape), full(cb.shape), full(wq.shape), full(wkt.shape), full(gb.shape),
                  full(og.shape), full(sk.shape)],
        out_specs=pl.BlockSpec((1, T, 512), lambda b: (b, 0, 0)),
        out_shape=jax.ShapeDtypeStruct((B, T, BRANCH_WIDTH), BF16),
        scratch_shapes=[pltpu.VMEM((T, BRANCH_WIDTH), F32),
                        pltpu.VMEM((T, HEADS * ML_DQK), BF16),
                        pltpu.VMEM((T // CHUNK, HEADS * ML_DQK, CHUNK), BF16),
                        pltpu.VMEM((T // CHUNK, 16, CHUNK), F32),
                        pltpu.VMEM((T // CHUNK, 16, CHUNK), F32),
                        pltpu.VMEM((T, BRANCH_WIDTH), F32),
                        pltpu.VMEM((T, BRANCH_WIDTH), F32),
                        pltpu.VMEM((2, HEADS * ML_DQK, 2 * HEAD_V), F32)],
        compiler_params=_params("arbitrary"),
        name="mlstm_scan",
    )(p, p, cw, cb, wq, wkt, gb, og, sk)


def _merge_kernel(ya_ref, yb_ref, yc_ref, yd_ref, gl_ref, brw_ref, wo_ref, x_ref, gate_ref, o_ref,
                  *, row0, ctx_rows):
    tm = x_ref.shape[1]
    first = pl.program_id(1) * tm + row0
    for r0 in range(0, tm, ROW_TILE):
        rs = slice(r0, r0 + ROW_TILE)
        acc = None
        for i, y_ref in enumerate((ya_ref, yb_ref, yc_ref, yd_ref)):
            u = _dot(y_ref[0, rs, :], brw_ref[i])
            gsig = _sigmoid(gl_ref[0, rs, D_MODEL * i:D_MODEL * (i + 1)].astype(F32))
            acc = gsig * u if acc is None else acc + gsig * u
        out = _dot(acc.astype(BF16), wo_ref[...])
        gate = jnp.where(first + r0 < ctx_rows, gate_ref[0, 0:1, :], gate_ref[0, 1:2, :])
        o_ref[0, rs, :] = x_ref[0, rs, :] + gate * out


def _merge(ys, p, brw, wo, xs, gates, n_ctx, last):
    B, T, D = xs.shape
    tm = ROW_TILE if last else MERGE_ROWS
    row0 = n_ctx if last else 0
    tile0 = row0 // tm
    nt = (T - row0) // tm
    assert (T - row0) % tm == 0 and row0 % tm == 0 and n_ctx % ROW_TILE == 0
    ymap = lambda b, t: (b, t + tile0, 0)
    return pl.pallas_call(
        functools.partial(_merge_kernel, row0=row0, ctx_rows=n_ctx),
        grid=(B, nt),
        in_specs=[pl.BlockSpec((1, tm, BRANCH_WIDTH), ymap)] * 4 + [
            pl.BlockSpec((1, tm, N_BRANCH * D), lambda b, t: (b, t + tile0, P_MERGE // (N_BRANCH * D))),
            pl.BlockSpec(brw.shape, lambda b, t: (0, 0, 0)),
            pl.BlockSpec(wo.shape, lambda b, t: (0, 0)),
            pl.BlockSpec((1, tm, D), ymap),
            pl.BlockSpec((1, 2, D), lambda b, t: (b, 0, 0))],
        out_specs=pl.BlockSpec((1, tm, D), lambda b, t: (b, t, 0)),
        out_shape=jax.ShapeDtypeStruct((B, nt * tm, D), F32),
        compiler_params=_params("arbitrary", "arbitrary"),
        name="merge",
    )(*ys, p, brw, wo, xs, gates)


def _layout_w_in(w_in):
    offs, off = {}, 0
    for name, w in IN_SPLITS:
        offs[name] = (off, w)
        off += w

    def col(name):
        o, w = offs[name]
        return w_in[..., o:o + w]

    def zeros(n):
        return jnp.zeros(w_in.shape[:-1] + (n,), w_in.dtype)

    parts = [col('merge'), col('df_k'), col('df_v'), col('df_q'), col('df_z'),
             col('ml_x'), col('ml_v'), col('ml_o'), col('ml_z'),
             col('gla_v'), col('gla_z'), col('mla_z'), col('gla_q'), col('gla_k'),
             col('mla_cq'), col('mla_kr'), zeros(64),
             col('mla_ckv'), col('gla_af'), col('gla_ab'), zeros(96), col('ml_if'), zeros(112)]
    out = jnp.concatenate(parts, axis=-1).astype(BF16)
    assert out.shape[-1] == P_WIDTH
    return out


def _rope_tables(rows, n_ctx):
    quarter = ROT_DIM // 4
    inv_freq = ROPE_BASE ** (-jnp.arange(quarter, dtype=F32) / quarter)
    row = jnp.repeat(jnp.arange(rows, dtype=F32), GRID_W)
    col = jnp.tile(jnp.arange(GRID_W, dtype=F32), rows)
    ar = row[:, None] * inv_freq
    ac = col[:, None] * inv_freq
    ang = jnp.concatenate([ar, ar, ac, ac], axis=-1)
    cos = jnp.concatenate([jnp.ones((n_ctx, ROT_DIM), F32), jnp.cos(ang)], axis=0)
    sin = jnp.concatenate([jnp.zeros((n_ctx, ROT_DIM), F32), jnp.sin(ang)], axis=0)
    even = ((jnp.arange(ROT_DIM) // quarter) % 2 == 0)[None, :]
    sin_a = jnp.where(even, -sin, 0.0)
    sin_b = jnp.where(even, 0.0, sin)
    zero = jnp.zeros_like(cos)
    both = jnp.stack([jnp.tile(cos, (1, 2)), jnp.tile(sin_a, (1, 2)), jnp.tile(sin_b, (1, 2))])
    half = jnp.stack([jnp.concatenate([cos, zero], -1), jnp.concatenate([sin_a, zero], -1),
                      jnp.concatenate([sin_b, zero], -1)])
    return half, both


def _pad_lanes(v, n):
    return jnp.concatenate([v, jnp.zeros(v.shape[:-1] + (n - v.shape[-1],), v.dtype)], axis=-1)


def kernel(x, c, ctx, c_ctx, ada_w, ada_b, norm_g, w_in, mla_cq_g, mla_ckv_g, mla_wuq, mla_wukv, mla_q_g,
           mla_k_g, gla_a_w, gla_a_b, gla_out_g, ml_conv_w, ml_conv_b, ml_wq, ml_wk, ml_gate_b, ml_out_g,
           ml_skip, df_qk_g, df_lambda, df_out_g, br_w, w_out):
    B, S, D = x.shape
    n_ctx = ctx.shape[1]
    L = ada_w.shape[0]
    assert D == D_MODEL and n_ctx == ROW_TILE and S % ROW_TILE == 0 and S % GRID_W == 0

    rope_half, rope_both = _rope_tables(S // GRID_W, n_ctx)
    w_in_p = _layout_w_in(w_in)

    n_rows = -(-(B + 1) // 8) * 8
    cc = jnp.concatenate([c, c_ctx[None], jnp.zeros((n_rows - B - 1, D), F32)], axis=0)
    mod_all = _modulation(cc, ada_w, ada_b)

    xs = jnp.concatenate([ctx, x], axis=1)
    for l in range(L):
        last = l == L - 1
        lam_init = 0.8 - 0.6 * math.exp(-0.3 * l)
        m3 = mod_all[l].reshape(n_rows, 3, D)
        lat, cx = m3[:B], jnp.broadcast_to(m3[B][None], (B, 3, D))
        mod = jnp.concatenate([cx, lat, jnp.zeros((B, 2, D), F32)], axis=1)
        gates = jnp.stack([cx[:, 2], lat[:, 2]], axis=1)

        p = _in_projection(xs, mod, norm_g[l], w_in_p[l], n_ctx)

        wq4 = mla_wuq[l].reshape(MLA_Q_LORA, HEADS, MLA_NOPE + ROT_DIM)
        wuq = _pad_lanes(wq4, 256).reshape(MLA_Q_LORA, HEADS * 256).astype(BF16)
        wkv4 = mla_wukv[l].reshape(MLA_KV_LORA, HEADS, MLA_NOPE + HEAD_V)
        wukv = jnp.concatenate([wkv4[..., :MLA_NOPE].reshape(MLA_KV_LORA, -1),
                                wkv4[..., MLA_NOPE:].reshape(MLA_KV_LORA, -1)], axis=-1).astype(BF16)
        qg = jnp.stack([mla_q_g[l, :MLA_NOPE], _pad_lanes(mla_q_g[l, MLA_NOPE:], 128)])
        kg = jnp.stack([mla_k_g[l, :MLA_NOPE], _pad_lanes(mla_k_g[l, MLA_NOPE:], 128)])
        y_mla = _mla(p, wuq, wukv, mla_cq_g[l][None], mla_ckv_g[l][None], qg, kg, rope_half, n_ctx, last)

        qkg = jnp.tile(df_qk_g[l], (1, 2))
        y_df = _diff(p, qkg, df_lambda[l], df_out_g[l][None], rope_both, n_ctx, last, lam_init)

        aw = jnp.zeros((128, 512), F32)
        aw = aw.at[0:16, 0:256].set(gla_a_w[l, 0]).at[16:32, 256:512].set(gla_a_w[l, 1]).astype(BF16)
        y_gla = _gla(p, aw, gla_a_b[l].reshape(1, 512), gla_out_g[l][None])

        wq_bd = jnp.zeros((BRANCH_WIDTH, HEADS * ML_DQK), F32)
        wk_bd = jnp.zeros((BRANCH_WIDTH, HEADS * ML_DQK), F32)
        for h in range(HEADS):
            wq_bd = wq_bd.at[128 * h:128 * h + 128, 64 * h:64 * h + 64].set(ml_wq[l, h])
            wk_bd = wk_bd.at[128 * h:128 * h + 128, 64 * h:64 * h + 64].set(ml_wk[l, h])
        gb = _pad_lanes(ml_gate_b[l].reshape(1, 16), 128)
        y_ml = _mlstm(p, ml_conv_w[l], ml_conv_b[l][None], wq_bd.astype(BF16), wk_bd.T.astype(BF16), gb,
                      ml_out_g[l][None], ml_skip[l][None], n_ctx)

        xs = _merge((y_mla, y_gla, y_ml, y_df), p, br_w[l].astype(BF16), w_out[l].astype(BF16), xs, gates,
                    n_ctx, last)
    return xs
```

```python
import functools
import math

import jax
import jax.numpy as jnp
from jax import lax
from jax.experimental import pallas as pl
from jax.experimental.pallas import tpu as pltpu

F32 = jnp.float32
BF16 = jnp.bfloat16

D_MODEL = 1024
GRID_W = 64
EPS = 1e-6
ROPE_BASE = 10000.0
ROT_DIM = 64
CHUNK = 64
N_BRANCH = 4
BRANCH_WIDTH = 512
HEADS = 4
HEAD_V = BRANCH_WIDTH // HEADS
MLA_NOPE = 128
MLA_Q_LORA = 384
MLA_KV_LORA = 256
GLA_DK = 64
GLA_GATE_RANK = 16
GLA_TAU = 16.0
ML_DQK = 64
DF_DQK = 64

IN_SPLITS = (
    ('mla_cq', 384), ('mla_ckv', 256), ('mla_kr', 64), ('mla_z', 512),
    ('gla_q', 256), ('gla_k', 256), ('gla_v', 512), ('gla_af', 16), ('gla_ab', 16), ('gla_z', 512),
    ('ml_x', 512), ('ml_v', 512), ('ml_o', 512), ('ml_if', 16), ('ml_z', 512),
    ('df_q', 512), ('df_k', 512), ('df_v', 512), ('df_z', 512),
    ('merge', 4096),
)

P_MERGE = 0
P_DF_KV = 4096
P_DF_Q = 5120
P_DF_Z = 5632
P_ML = 6144
P_GLA_VZ = 8192
P_MLA_Z = 9216
P_GLA_QK = 9728
P_MLA_A = 10240
P_MLA_CKV = 10752
P_GLA_G = 11008
P_ML_IF = 11136
P_WIDTH = 11264

ROW_TILE = 256
IN_PROJ_COLS = 1024
MERGE_ROWS = 768
ML_CHUNK = 256
VMEM_LIMIT = 56 * 1024 * 1024
EXP_CLAMP = 80.0


def _dot(a, b):
    return jnp.dot(a, b, preferred_element_type=F32)


def _dot_nt(a, b):
    return lax.dot_general(a, b, (((1,), (1,)), ((), ())), preferred_element_type=F32)


def _dot_tn(a, b):
    return lax.dot_general(a, b, (((0,), (0,)), ((), ())), preferred_element_type=F32)


def _dot_hi(a, b):
    return jnp.dot(a, b, preferred_element_type=F32, precision=lax.Precision.HIGHEST)


def _dot_nt_hi(a, b):
    return lax.dot_general(a, b, (((1,), (1,)), ((), ())), preferred_element_type=F32,
                           precision=lax.Precision.HIGHEST)


def _sigmoid(x):
    return 1.0 / (1.0 + jnp.exp(-x))


def _silu(x):
    return x * _sigmoid(x)


def _log_sigmoid(x):
    return jnp.minimum(x, 0.0) - jnp.log(1.0 + jnp.exp(-jnp.abs(x)))


def _rms(x, n):
    return x * lax.rsqrt(jnp.sum(x * x, axis=-1, keepdims=True) * (1.0 / n) + EPS)


def _split(x):
    hi = x.astype(BF16)
    return hi, (x - hi.astype(F32)).astype(BF16)


def _group_sum(x, group):
    k = x.shape[-1]
    row = lax.broadcasted_iota(jnp.int32, (k, 128), 0)
    col = lax.broadcasted_iota(jnp.int32, (k, 128), 1)
    sel = jnp.ones((k, 128), BF16) if group is None else jnp.where(row // group == col // group, 1.0, 0.0).astype(BF16)
    hi, lo = _split(x)
    return _dot(hi, sel) + _dot(lo, sel)


def _rms_mxu(x, n, group=None):
    inv = lax.rsqrt(_group_sum(x * x, group) * (1.0 / n) + EPS)
    return x * (inv if x.shape[-1] == 128 else jnp.tile(inv, (1, x.shape[-1] // 128)))


def _rot_matrix():
    src = lax.broadcasted_iota(jnp.int32, (128, 128), 0)
    dst = lax.broadcasted_iota(jnp.int32, (128, 128), 1)
    even = (dst // 16) % 2 == 0
    return jnp.where(even & (src == dst + 16), -1.0, jnp.where(~even & (src == dst - 16), 1.0, 0.0)).astype(BF16)


def _rope_mxu(x, cos, sin, rot):
    hi, lo = _split(x)
    return x * cos + (_dot(hi, rot) + _dot(lo, rot)) * sin


def _ones_col(rows):
    return jnp.where(lax.broadcasted_iota(jnp.int32, (rows, HEAD_V), 1) == 0, 1.0, 0.0).astype(BF16)


def _softmax_pv(s, v_ext):
    e = jnp.exp((s - jnp.max(s, axis=-1, keepdims=True)).astype(BF16))
    o = _dot(e, v_ext)
    return o[:, 0:HEAD_V] / o[:, HEAD_V:HEAD_V + 1]


def _params(*sem):
    return pltpu.CompilerParams(dimension_semantics=sem, vmem_limit_bytes=VMEM_LIMIT)


def _mod_kernel(c_ref, w_ref, b_ref, o_ref):
    s = _silu(c_ref[...])
    o_ref[0] = _dot(s.astype(BF16), w_ref[0].astype(BF16)) + b_ref[0]


def _modulation(cc, ada_w, ada_b):
    L, D, _ = ada_w.shape
    R = cc.shape[0]
    return pl.pallas_call(
        _mod_kernel,
        grid=(L, 3),
        in_specs=[pl.BlockSpec((R, D), lambda l, j: (0, 0)),
                  pl.BlockSpec((1, D, D), lambda l, j: (l, 0, j)),
                  pl.BlockSpec((1, 1, D), lambda l, j: (l, 0, j))],
        out_specs=pl.BlockSpec((1, R, D), lambda l, j: (l, 0, j)),
        out_shape=jax.ShapeDtypeStruct((L, R, 3 * D), F32),
        compiler_params=_params("arbitrary", "arbitrary"),
        name="modulation",
    )(cc, ada_w, ada_b.reshape(L, 1, 3 * D))


def _inproj_kernel(x_ref, mod_ref, g_ref, w_ref, o_ref, h_scr, *, n_ctx):
    T = x_ref.shape[1]

    @pl.when(pl.program_id(1) == 0)
    def _():
        g = g_ref[...]
        for r0 in range(0, T, ROW_TILE):
            k = 0 if r0 < n_ctx else 3
            y = _rms(x_ref[0, r0:r0 + ROW_TILE, :], D_MODEL) * g
            h = y * (1.0 + mod_ref[0, k + 1:k + 2, :]) + mod_ref[0, k:k + 1, :]
            h_scr[r0:r0 + ROW_TILE, :] = h.astype(BF16)

    for r0 in range(0, T, ROW_TILE):
        o_ref[0, r0:r0 + ROW_TILE, :] = _dot(h_scr[r0:r0 + ROW_TILE, :], w_ref[...]).astype(BF16)


def _in_projection(xs, mod, norm_g, w, n_ctx):
    B, T, D = xs.shape
    n_col = P_WIDTH // IN_PROJ_COLS
    return pl.pallas_call(
        functools.partial(_inproj_kernel, n_ctx=n_ctx),
        grid=(B, n_col),
        in_specs=[pl.BlockSpec((1, T, D), lambda b, j: (b, 0, 0)),
                  pl.BlockSpec((1, 8, D), lambda b, j: (b, 0, 0)),
                  pl.BlockSpec((1, D), lambda b, j: (0, 0)),
                  pl.BlockSpec((D, IN_PROJ_COLS), lambda b, j: (0, j))],
        out_specs=pl.BlockSpec((1, T, IN_PROJ_COLS), lambda b, j: (b, 0, j)),
        out_shape=jax.ShapeDtypeStruct((B, T, P_WIDTH), BF16),
        scratch_shapes=[pltpu.VMEM((T, D), BF16)],
        compiler_params=_params("arbitrary", "arbitrary"),
        name="in_projection",
    )(xs, mod, norm_g.reshape(1, D), w)


def _mla_kernel(pa_ref, pc_ref, z_ref, wuq_ref, wukv_ref, cqg_ref, ckvg_ref, qg_ref, kg_ref, rope_ref,
                o_ref, q_scr, k_scr, v_scr, *, n_ctx, tile0, with_ctx):
    T = pa_ref.shape[1]
    qi = pl.program_id(1)
    scale = (MLA_NOPE + ROT_DIM) ** -0.5

    @pl.when(qi == 0)
    def _():
        rot = _rot_matrix()

        def prep(i, carry):
            r = pl.multiple_of(i * ROW_TILE, ROW_TILE)
            rows = pl.ds(r, ROW_TILE)
            cos, sin = rope_ref[0, rows, :], rope_ref[1, rows, :]
            pa = pa_ref[0, rows, :].astype(F32)
            cq = (_rms(pa[:, :MLA_Q_LORA], MLA_Q_LORA) * cqg_ref[...]).astype(BF16)
            q = _dot(cq, wuq_ref[...])
            kr = _rms(pa[:, MLA_Q_LORA:], ROT_DIM) * kg_ref[1:2, :]
            kr = _rope_mxu(kr, cos, sin, rot).astype(BF16)
            ckv = (_rms(pc_ref[0, rows, :].astype(F32), MLA_KV_LORA) * ckvg_ref[...]).astype(BF16)
            kv = _dot(ckv, wukv_ref[...])
            for h in range(HEADS):
                qn = _rms(q[:, 256 * h:256 * h + 128], MLA_NOPE) * qg_ref[0:1, :]
                qr = _rms(q[:, 256 * h + 128:256 * h + 256], ROT_DIM) * qg_ref[1:2, :]
                qr = _rope_mxu(qr, cos, sin, rot)
                q_scr[rows, 256 * h:256 * h + 128] = (qn * scale).astype(BF16)
                q_scr[rows, 256 * h + 128:256 * h + 256] = (qr * scale).astype(BF16)
                kn = _rms(kv[:, 128 * h:128 * h + 128], MLA_NOPE) * kg_ref[0:1, :]
                k_scr[rows, 256 * h:256 * h + 128] = kn.astype(BF16)
                k_scr[rows, 256 * h + 128:256 * h + 256] = kr
                v_scr[rows, 256 * h:256 * h + 128] = kv[:, 512 + 128 * h:512 + 128 * h + 128].astype(BF16)
                v_scr[rows, 256 * h + 128:256 * h + 256] = _ones_col(ROW_TILE)
            return carry

        lax.fori_loop(0, T // ROW_TILE, prep, 0)

    r = pl.multiple_of((qi + tile0) * ROW_TILE, ROW_TILE)
    rows = pl.ds(r, ROW_TILE)

    def attend(nk):
        for h in range(HEADS):
            s = _dot_nt(q_scr[rows, 256 * h:256 * h + 256], k_scr[0:nk, 256 * h:256 * h + 256])
            o = _softmax_pv(s, v_scr[0:nk, 256 * h:256 * h + 256])
            z = z_ref[0, :, 128 * h:128 * h + 128].astype(F32)
            o_ref[0, :, 128 * h:128 * h + 128] = (o * _silu(z)).astype(BF16)

    if with_ctx:
        pl.when(qi == 0)(lambda: attend(n_ctx))
        pl.when(qi != 0)(lambda: attend(T))
    else:
        attend(T)


def _mla(p, wuq, wukv, cqg, ckvg, qg, kg, rope, n_ctx, last):
    B, T, _ = p.shape
    tile0 = 1 if last else 0
    nq = T // ROW_TILE - tile0
    full = lambda shape: pl.BlockSpec(shape, lambda b, i: (0,) * len(shape))
    return pl.pallas_call(
        functools.partial(_mla_kernel, n_ctx=n_ctx, tile0=tile0, with_ctx=not last),
        grid=(B, nq),
        in_specs=[pl.BlockSpec((1, T, 512), lambda b, i: (b, 0, P_MLA_A // 512)),
                  pl.BlockSpec((1, T, 256), lambda b, i: (b, 0, P_MLA_CKV // 256)),
                  pl.BlockSpec((1, ROW_TILE, 512), lambda b, i: (b, i + tile0, P_MLA_Z // 512)),
                  full(wuq.shape), full(wukv.shape), full(cqg.shape), full(ckvg.shape),
                  full(qg.shape), full(kg.shape), full(rope.shape)],
        out_specs=pl.BlockSpec((1, ROW_TILE, 512), lambda b, i: (b, i + tile0, 0)),
        out_shape=jax.ShapeDtypeStruct((B, T, BRANCH_WIDTH), BF16),
        scratch_shapes=[pltpu.VMEM((T, 1024), BF16), pltpu.VMEM((T, 1024), BF16),
                        pltpu.VMEM((T, 1024), BF16)],
        compiler_params=_params("arbitrary", "arbitrary"),
        name="mla",
    )(p, p, p, wuq, wukv, cqg, ckvg, qg, kg, rope)


def _diff_kernel(kv_ref, q_ref, z_ref, g_ref, lam_ref, og_ref, rope_ref, o_ref, q0_scr, q1_scr, k_scr, v_scr,
                 *, n_ctx, tile0, with_ctx, lam_init):
    T = kv_ref.shape[1]
    qi = pl.program_id(1)
    scale = DF_DQK ** -0.5
    lo_mask = lax.broadcasted_iota(jnp.int32, (1, 128), 1) < 64

    @pl.when(qi == 0)
    def _():
        rot = _rot_matrix()

        def prep(i, carry):
            r = pl.multiple_of(i * ROW_TILE, ROW_TILE)
            rows = pl.ds(r, ROW_TILE)
            cos, sin = rope_ref[0, rows, :], rope_ref[1, rows, :]
            for h in range(HEADS):
                cols = slice(128 * h, 128 * h + 128)
                q = _rms_mxu(q_ref[0, rows, cols].astype(F32), DF_DQK, DF_DQK) * g_ref[0:1, :]
                q = _rope_mxu(q, cos, sin, rot) * scale
                q0_scr[rows, cols] = jnp.where(lo_mask, q, 0.0).astype(BF16)
                q1_scr[rows, cols] = jnp.where(lo_mask, 0.0, q).astype(BF16)
                k = _rms_mxu(kv_ref[0, rows, cols].astype(F32), DF_DQK, DF_DQK) * g_ref[1:2, :]
                k_scr[rows, cols] = _rope_mxu(k, cos, sin, rot).astype(BF16)
                v_scr[rows, 256 * h:256 * h + 128] = kv_ref[0, rows, 512 + 128 * h:512 + 128 * h + 128]
                v_scr[rows, 256 * h + 128:256 * h + 256] = _ones_col(ROW_TILE)
            return carry

        lax.fori_loop(0, T // ROW_TILE, prep, 0)

    lp = lam_ref[...]
    lam = (jnp.exp(jnp.sum(lp[0:1] * lp[1:2], axis=-1, keepdims=True))
           - jnp.exp(jnp.sum(lp[2:3] * lp[3:4], axis=-1, keepdims=True)) + lam_init)
    r = pl.multiple_of((qi + tile0) * ROW_TILE, ROW_TILE)
    rows = pl.ds(r, ROW_TILE)

    def attend(nk):
        for h in range(HEADS):
            cols = slice(128 * h, 128 * h + 128)
            k = k_scr[0:nk, cols]
            v_ext = v_scr[0:nk, 256 * h:256 * h + 256]
            o = (_softmax_pv(_dot_nt(q0_scr[rows, cols], k), v_ext)
                 - lam * _softmax_pv(_dot_nt(q1_scr[rows, cols], k), v_ext))
            o = _rms(o, HEAD_V) * og_ref[...] * (1.0 - lam_init)
            z = z_ref[0, :, cols].astype(F32)
            o_ref[0, :, cols] = (o * _silu(z)).astype(BF16)

    if with_ctx:
        pl.when(qi == 0)(lambda: attend(n_ctx))
        pl.when(qi != 0)(lambda: attend(T))
    else:
        attend(T)


def _diff(p, qkg, lam_p, og, rope, n_ctx, last, lam_init):
    B, T, _ = p.shape
    tile0 = 1 if last else 0
    nq = T // ROW_TILE - tile0
    full = lambda shape: pl.BlockSpec(shape, lambda b, i: (0,) * len(shape))
    return pl.pallas_call(
        functools.partial(_diff_kernel, n_ctx=n_ctx, tile0=tile0, with_ctx=not last, lam_init=lam_init),
        grid=(B, nq),
        in_specs=[pl.BlockSpec((1, T, 1024), lambda b, i: (b, 0, P_DF_KV // 1024)),
                  pl.BlockSpec((1, T, 512), lambda b, i: (b, 0, P_DF_Q // 512)),
                  pl.BlockSpec((1, ROW_TILE, 512), lambda b, i: (b, i + tile0, P_DF_Z // 512)),
                  full(qkg.shape), full(lam_p.shape), full(og.shape), full(rope.shape)],
        out_specs=pl.BlockSpec((1, ROW_TILE, 512), lambda b, i: (b, i + tile0, 0)),
        out_shape=jax.ShapeDtypeStruct((B, T, BRANCH_WIDTH), BF16),
        scratch_shapes=[pltpu.VMEM((T, 512), BF16), pltpu.VMEM((T, 512), BF16), pltpu.VMEM((T, 512), BF16),
                        pltpu.VMEM((T, 1024), BF16)],
        compiler_params=_params("arbitrary", "arbitrary"),
        name="diff_attn",
    )(p, p, p, qkg, lam_p, og, rope)


def _tile_order(i, n_tiles, rev):
    if not rev:
        return i
    return jnp.where(i == 0, 0, n_tiles - i)


def _chunk_tri(n, rev, chunk=CHUNK):
    r = lax.broadcasted_iota(jnp.int32, (n, n), 0)
    c = lax.broadcasted_iota(jnp.int32, (n, n), 1)
    same = (r // chunk) == (c // chunk)
    tri = (c >= r) if rev else (c <= r)
    return jnp.where(same & tri, 1.0, 0.0).astype(F32)


def _causal(rev, n=CHUNK):
    t = lax.broadcasted_iota(jnp.int32, (n, n), 0)
    s = lax.broadcasted_iota(jnp.int32, (n, n), 1)
    return (s >= t) if rev else (s <= t)


def _head_mask(width, group):
    lane = lax.broadcasted_iota(jnp.int32, (1, width), 1) // group
    return [lane == h for h in range(HEADS)]


def _gla_kernel(qk_ref, vz_ref, g_ref, aw_ref, ab_ref, og_ref, o_ref, of_scr, ob_scr, st_scr):
    T = qk_ref.shape[1]
    n_tiles = T // ROW_TILE
    n_chunk = ROW_TILE // CHUNK
    hm = _head_mask(HEADS * GLA_DK, GLA_DK)
    er = lax.broadcasted_iota(jnp.int32, (BRANCH_WIDTH, HEADS * GLA_DK), 0) // HEAD_V
    dc = lax.broadcasted_iota(jnp.int32, (BRANCH_WIDTH, HEADS * GLA_DK), 1) // GLA_DK
    block = er == dc
    tris = (_chunk_tri(ROW_TILE, False), _chunk_tri(ROW_TILE, True))
    st_scr[...] = jnp.zeros_like(st_scr)

    def load_tile(t, rev):
        rows = pl.ds(pl.multiple_of(t * ROW_TILE, ROW_TILE), ROW_TILE)
        gcol = 256 if rev else 0
        qk = qk_ref[0, rows, :].astype(F32)
        pre = _dot(g_ref[0, rows, :], aw_ref[:, gcol:gcol + 256]) + ab_ref[:, gcol:gcol + 256]
        la = _log_sigmoid(pre) * (1.0 / GLA_TAU)
        return qk[:, :256] * (GLA_DK ** -0.5), qk[:, 256:], _dot_hi(tris[rev], la)

    def tile(i, carry):
        tiles = (i, _tile_order(i, n_tiles, True))
        steps = []
        for cc in range(n_chunk):
            steps += [(0, cc), (1, n_chunk - 1 - cc)]
        data = [load_tile(tiles[d], bool(d)) for d in (0, 1)]
        pre = {}
        for d, cc in steps:
            q, k, b = data[d]
            cr = slice(CHUNK * cc, CHUNK * cc + CHUNK)
            bc, qc, kc = b[cr], q[cr], k[cr]
            b_first = bc[CHUNK - 1:CHUNK] if d else bc[0:1]
            b_end = bc[0:1] if d else bc[CHUNK - 1:CHUNK]
            crow = pl.ds(pl.multiple_of(tiles[d] * ROW_TILE + CHUNK * cc, CHUNK), CHUNK)
            q_in = qc * jnp.exp(bc - b_first)
            pre[d, cc] = dict(
                crow=crow, vc=vz_ref[0, crow, 0:BRANCH_WIDTH], decay=jnp.exp(b_end),
                qe=(qc * jnp.exp(bc)).astype(BF16),
                q_in=[jnp.where(hm[h], q_in, 0.0).astype(BF16) for h in range(HEADS)],
                k_in=(kc * jnp.exp(jnp.minimum(b_first - bc, EXP_CLAMP))).astype(BF16),
                k_up=(kc * jnp.exp(b_end - bc)).astype(BF16))
        a_raw = {(d, cc, h): _dot_nt(pre[d, cc]['q_in'][h], pre[d, cc]['k_in'])
                 for d, cc in steps for h in range(HEADS)}
        upd = {(d, cc): _dot_tn(pre[d, cc]['vc'], pre[d, cc]['k_up']) for d, cc in steps}
        a_msk = {key: jnp.where(_causal(bool(key[0])), v, 0.0).astype(BF16) for key, v in a_raw.items()}
        intra = {(d, cc): jnp.concatenate(
            [_dot(a_msk[d, cc, h], pre[d, cc]['vc'][:, 128 * h:128 * h + 128]) for h in range(HEADS)], axis=-1)
            for d, cc in steps}
        st = [st_scr[0], st_scr[1]]
        for d, cc in steps:
            o = _dot_nt(pre[d, cc]['qe'], st[d].astype(BF16))
            (ob_scr if d else of_scr)[pre[d, cc]['crow'], :] = o + intra[d, cc]
            st[d] = st[d] * pre[d, cc]['decay'] + jnp.where(block, upd[d, cc], 0.0)
        st_scr[0] = st[0]
        st_scr[1] = st[1]
        return carry

    lax.fori_loop(0, n_tiles, tile, 0)

    def finish(i, carry):
        rows = pl.ds(pl.multiple_of(i * ROW_TILE, ROW_TILE), ROW_TILE)
        tot = of_scr[rows, :] + ob_scr[rows, :]
        z = vz_ref[0, rows, BRANCH_WIDTH:].astype(F32)
        for h in range(HEADS):
            cols = slice(128 * h, 128 * h + 128)
            y = _rms(tot[:, cols], HEAD_V) * og_ref[:, cols]
            o_ref[0, rows, cols] = (y * _silu(z[:, cols])).astype(BF16)
        return carry

    lax.fori_loop(0, n_tiles, finish, 0)


def _gla(p, aw, ab, og):
    B, T, _ = p.shape
    full = lambda shape: pl.BlockSpec(shape, lambda b: (0,) * len(shape))
    return pl.pallas_call(
        _gla_kernel,
        grid=(B,),
        in_specs=[pl.BlockSpec((1, T, 512), lambda b: (b, 0, P_GLA_QK // 512)),
                  pl.BlockSpec((1, T, 1024), lambda b: (b, 0, P_GLA_VZ // 1024)),
                  pl.BlockSpec((1, T, 128), lambda b: (b, 0, P_GLA_G // 128)),
                  full(aw.shape), full(ab.shape), full(og.shape)],
        out_specs=pl.BlockSpec((1, T, 512), lambda b: (b, 0, 0)),
        out_shape=jax.ShapeDtypeStruct((B, T, BRANCH_WIDTH), BF16),
        scratch_shapes=[pltpu.VMEM((T, BRANCH_WIDTH), F32), pltpu.VMEM((T, BRANCH_WIDTH), F32),
                        pltpu.VMEM((2, BRANCH_WIDTH, HEADS * GLA_DK), F32)],
        compiler_params=_params("arbitrary"),
        name="gla_scan",
    )(p, p, p, aw, ab, og)


def _mlstm_kernel(m_ref, if_ref, cw_ref, cb_ref, wq_ref, wkt_ref, gb_ref, og_ref, sk_ref, o_ref,
                  xc_scr, q_scr, kt_scr, at_scr, bt_scr, hf_scr, hb_scr, c_scr, *, n_ctx):
    T = m_ref.shape[1]
    n_tiles = T // ROW_TILE
    n_chunk = ROW_TILE // ML_CHUNK
    ctx_tiles = n_ctx // ROW_TILE
    hm = _head_mask(HEADS * ML_DQK, ML_DQK)
    lane128 = lax.broadcasted_iota(jnp.int32, (1, 128), 1)
    is_forget = ((lane128 // HEADS) % 2) == 1
    row_in_tile = lax.broadcasted_iota(jnp.int32, (ROW_TILE, 1), 0)
    tri_f = _chunk_tri(ROW_TILE, False, ML_CHUNK)
    tri_b = _chunk_tri(ROW_TILE, True, ML_CHUNK)

    def prep(i, carry):
        r = pl.multiple_of(i * ROW_TILE, ROW_TILE)
        rows = pl.ds(r, ROW_TILE)
        x = m_ref[0, rows, 0:512].astype(F32)
        rp = pl.multiple_of(jnp.maximum(r - 16, 0), 16)
        rn = pl.multiple_of(jnp.minimum(r + ROW_TILE, T - 16), 16)
        prev_row = m_ref[0, pl.ds(rp, 16), 0:512].astype(F32)[15:16]
        next_row = m_ref[0, pl.ds(rn, 16), 0:512].astype(F32)[0:1]
        seg_start = (i == 0) | (i == ctx_tiles)
        seg_end = (i == ctx_tiles - 1) | (i == n_tiles - 1)
        prev_row = jnp.where(seg_start, 0.0, prev_row)
        next_row = jnp.where(seg_end, 0.0, next_row)
        xm = jnp.where(row_in_tile == 0, prev_row, pltpu.roll(x, 1, 0))
        xp = jnp.where(row_in_tile == ROW_TILE - 1, next_row, pltpu.roll(x, ROW_TILE - 1, 0))
        xc = _silu(cw_ref[0:1, :] * xm + cw_ref[1:2, :] * x + cw_ref[2:3, :] * xp + cb_ref[...])
        xc_scr[rows, :] = xc
        xb = xc.astype(BF16)
        q_scr[rows, :] = _dot(xb, wq_ref[...]).astype(BF16)
        g = if_ref[0, rows, :].astype(F32) + gb_ref[...]
        g2 = jnp.where(is_forget, _log_sigmoid(g), g)
        cs = jnp.where(lane128 < 2 * HEADS, _dot_hi(tri_f, g2), _dot_hi(tri_b, g2))
        b = pltpu.roll(cs, 128 - HEADS, 1)
        a = g2 - b
        for cc in range(n_chunk):
            cr = slice(ML_CHUNK * cc, ML_CHUNK * cc + ML_CHUNK)
            c = i * n_chunk + cc
            kt_scr[c] = (_dot_nt(wkt_ref[...], xb[cr]) * (ML_DQK ** -0.5)).astype(BF16)
            at_scr[c] = a[cr].T[0:16, :]
            bt_scr[c] = b[cr].T[0:16, :]
        return carry

    lax.fori_loop(0, n_tiles, prep, 0)

    c_scr[...] = jnp.zeros_like(c_scr)
    eye = _causal(False, ML_CHUNK) & _causal(True, ML_CHUNK)
    ones_col = _ones_col(ML_CHUNK)

    def chunk_pair(c_f, c_b, m_in):
        units = [(d, h) for d in (0, 1) for h in range(HEADS)]
        cs = (c_f, c_b)
        crow = [pl.ds(pl.multiple_of(c * ML_CHUNK, ML_CHUNK), ML_CHUNK) for c in cs]
        qc = [q_scr[crow[d], :] for d in (0, 1)]
        kt = [kt_scr[cs[d]] for d in (0, 1)]
        vc = [m_ref[0, crow[d], 512:1024] for d in (0, 1)]
        at = [at_scr[cs[d]] for d in (0, 1)]
        bt = [bt_scr[cs[d]] for d in (0, 1)]
        c_bf = [c_scr[d].astype(BF16) for d in (0, 1)]
        causal = [_causal(False, ML_CHUNK), _causal(True, ML_CHUNK)]
        last = [ML_CHUNK - 1, 0]
        qh = {(d, h): jnp.where(hm[h], qc[d], jnp.zeros_like(qc[d])) for d, h in units}
        v_ext = {(d, h): jnp.concatenate([vc[d][:, 128 * h:128 * h + 128], ones_col], axis=-1)
                 for d, h in units}
        s_raw = {u: _dot(qh[u], kt[u[0]]) for u in units}
        q_c = {u: _dot(qh[u], c_bf[u[0]]) for u in units}
        s_w, m_run, m_last, ktw = {}, {}, {}, {}
        for d, h in units:
            j = 2 * HEADS * d + h
            a_row, m_old = at[d][j:j + 1, :], m_in[HEADS * d + h]
            pm = jnp.where(causal[d], a_row, -jnp.inf)
            m_run[d, h] = jnp.maximum(m_old, jnp.max(pm, axis=-1, keepdims=True))
            s_w[d, h] = (s_raw[d, h] * jnp.exp(pm - m_run[d, h])).astype(BF16)
            m_last[d, h] = m_run[d, h][last[d]:last[d] + 1]
            k_h = kt[d][ML_DQK * h:ML_DQK * h + ML_DQK, :].astype(F32)
            ktw[d, h] = (k_h * jnp.exp(a_row - m_last[d, h])).astype(BF16)
        s_v = {u: _dot(s_w[u], v_ext[u]) for u in units}
        upd = {u: _dot(ktw[u], v_ext[u]) for u in units}
        m_out = []
        for d in (0, 1):
            parts = []
            for h in range(HEADS):
                j = 2 * HEADS * d + h
                hr = slice(ML_DQK * h, ML_DQK * h + ML_DQK)
                b_row, m_old = bt[d][j:j + 1, :], m_in[HEADS * d + h]
                num = jnp.exp(m_old - m_run[d, h]) * q_c[d, h] + s_v[d, h]
                b_col = jnp.sum(jnp.where(eye, b_row, 0.0), axis=-1, keepdims=True)
                den = num[:, HEAD_V:HEAD_V + 1]
                parts.append(num[:, 0:HEAD_V] / jnp.maximum(jnp.abs(den), jnp.exp(-(b_col + m_run[d, h]))))
                c_scr[d, hr, :] = jnp.exp(m_old - m_last[d, h]) * c_scr[d, hr, :] + upd[d, h]
                m_out.append(b_row[:, last[d]:last[d] + 1] + m_last[d, h])
            (hb_scr if d else hf_scr)[crow[d], :] = jnp.concatenate(parts, axis=-1)
        return m_out

    def tile(i, carry):
        m = list(carry)
        c_f = i * n_chunk
        c_b = _tile_order(i, n_tiles, True) * n_chunk + (n_chunk - 1)
        for cc in range(n_chunk):
            m = chunk_pair(c_f + cc, c_b - cc, m)
        return tuple(m)

    lax.fori_loop(0, n_tiles, tile, tuple(jnp.zeros((1, 1), F32) for _ in range(2 * HEADS)))

    def finish(i, carry):
        rows = pl.ds(pl.multiple_of(i * ROW_TILE, ROW_TILE), ROW_TILE)
        tot = (hf_scr[rows, :] + hb_scr[rows, :]) * _sigmoid(m_ref[0, rows, 1024:1536].astype(F32))
        z = m_ref[0, rows, 1536:2048].astype(F32)
        xc = xc_scr[rows, :]
        for h in range(HEADS):
            cols = slice(128 * h, 128 * h + 128)
            y = _rms(tot[:, cols], HEAD_V) * og_ref[:, cols]
            y = (y + sk_ref[:, cols] * xc[:, cols]) * _silu(z[:, cols])
            o_ref[0, rows, cols] = y.astype(BF16)
        return carry

    lax.fori_loop(0, n_tiles, finish, 0)


def _mlstm(p, cw, cb, wq, wkt, gb, og, sk, n_ctx):
    B, T, _ = p.shape
    full = lambda shape: pl.BlockSpec(shape, lambda b: (0,) * len(shape))
    return pl.pallas_call(
        functools.partial(_mlstm_kernel, n_ctx=n_ctx),
        grid=(B,),
        in_specs=[pl.BlockSpec((1, T, 2048), lambda b: (b, 0, P_ML // 2048)),
                  pl.BlockSpec((1, T, 128), lambda b: (b, 0, P_ML_IF // 128)),
                  full(cw.shape), full(cb.shape), full(wq.shape), full(wkt.shape), full(gb.shape),
                  full(og.shape), full(sk.shape)],
        out_specs=pl.BlockSpec((1, T, 512), lambda b: (b, 0, 0)),
        out_shape=jax.ShapeDtypeStruct((B, T, BRANCH_WIDTH), BF16),
        scratch_shapes=[pltpu.VMEM((T, BRANCH_WIDTH), F32),
                        pltpu.VMEM((T, HEADS * ML_DQK), BF16),
                        pltpu.VMEM((T // ML_CHUNK, HEADS * ML_DQK, ML_CHUNK), BF16),
                        pltpu.VMEM((T // ML_CHUNK, 16, ML_CHUNK), F32),
                        pltpu.VMEM((T // ML_CHUNK, 16, ML_CHUNK), F32),
                        pltpu.VMEM((T, BRANCH_WIDTH), F32),
                        pltpu.VMEM((T, BRANCH_WIDTH), F32),
                        pltpu.VMEM((2, HEADS * ML_DQK, 2 * HEAD_V), F32)],
        compiler_params=_params("arbitrary"),
        name="mlstm_scan",
    )(p, p, cw, cb, wq, wkt, gb, og, sk)


def _merge_kernel(ya_ref, yb_ref, yc_ref, yd_ref, gl_ref, brw_ref, wo_ref, x_ref, gate_ref, o_ref,
                  *, row0, ctx_rows):
    tm = x_ref.shape[1]
    first = pl.program_id(1) * tm + row0
    for r0 in range(0, tm, ROW_TILE):
        rs = slice(r0, r0 + ROW_TILE)
        acc = None
        for i, y_ref in enumerate((ya_ref, yb_ref, yc_ref, yd_ref)):
            u = _dot(y_ref[0, rs, :], brw_ref[i])
            gsig = _sigmoid(gl_ref[0, rs, D_MODEL * i:D_MODEL * (i + 1)].astype(F32))
            acc = gsig * u if acc is None else acc + gsig * u
        out = _dot(acc.astype(BF16), wo_ref[...])
        gate = jnp.where(first + r0 < ctx_rows, gate_ref[0, 0:1, :], gate_ref[0, 1:2, :])
        o_ref[0, rs, :] = x_ref[0, rs, :] + gate * out


def _merge(ys, p, brw, wo, xs, gates, n_ctx, last):
    B, T, D = xs.shape
    tm = ROW_TILE if last else MERGE_ROWS
    row0 = n_ctx if last else 0
    tile0 = row0 // tm
    nt = (T - row0) // tm
    assert (T - row0) % tm == 0 and row0 % tm == 0 and n_ctx % ROW_TILE == 0
    ymap = lambda b, t: (b, t + tile0, 0)
    return pl.pallas_call(
        functools.partial(_merge_kernel, row0=row0, ctx_rows=n_ctx),
        grid=(B, nt),
        in_specs=[pl.BlockSpec((1, tm, BRANCH_WIDTH), ymap)] * 4 + [
            pl.BlockSpec((1, tm, N_BRANCH * D), lambda b, t: (b, t + tile0, P_MERGE // (N_BRANCH * D))),
            pl.BlockSpec(brw.shape, lambda b, t: (0, 0, 0)),
            pl.BlockSpec(wo.shape, lambda b, t: (0, 0)),
            pl.BlockSpec((1, tm, D), ymap),
            pl.BlockSpec((1, 2, D), lambda b, t: (b, 0, 0))],
        out_specs=pl.BlockSpec((1, tm, D), lambda b, t: (b, t, 0)),
        out_shape=jax.ShapeDtypeStruct((B, nt * tm, D), F32),
        compiler_params=_params("arbitrary", "arbitrary"),
        name="merge",
    )(*ys, p, brw, wo, xs, gates)


def _layout_w_in(w_in):
    offs, off = {}, 0
    for name, w in IN_SPLITS:
        offs[name] = (off, w)
        off += w

    def col(name):
        o, w = offs[name]
        return w_in[..., o:o + w]

    def zeros(n):
        return jnp.zeros(w_in.shape[:-1] + (n,), w_in.dtype)

    parts = [col('merge'), col('df_k'), col('df_v'), col('df_q'), col('df_z'),
             col('ml_x'), col('ml_v'), col('ml_o'), col('ml_z'),
             col('gla_v'), col('gla_z'), col('mla_z'), col('gla_q'), col('gla_k'),
             col('mla_cq'), col('mla_kr'), zeros(64),
             col('mla_ckv'), col('gla_af'), col('gla_ab'), zeros(96), col('ml_if'), zeros(112)]
    out = jnp.concatenate(parts, axis=-1).astype(BF16)
    assert out.shape[-1] == P_WIDTH
    return out


def _rope_tables(rows, n_ctx):
    quarter = ROT_DIM // 4
    inv_freq = ROPE_BASE ** (-jnp.arange(quarter, dtype=F32) / quarter)
    row = jnp.repeat(jnp.arange(rows, dtype=F32), GRID_W)
    col = jnp.tile(jnp.arange(GRID_W, dtype=F32), rows)
    ar = row[:, None] * inv_freq
    ac = col[:, None] * inv_freq
    ang = jnp.concatenate([ar, ar, ac, ac], axis=-1)
    cos = jnp.concatenate([jnp.ones((n_ctx, ROT_DIM), F32), jnp.cos(ang)], axis=0)
    sin = jnp.concatenate([jnp.zeros((n_ctx, ROT_DIM), F32), jnp.sin(ang)], axis=0)
    zero = jnp.zeros_like(cos)
    both = jnp.stack([jnp.tile(cos, (1, 2)), jnp.tile(sin, (1, 2))])
    half = jnp.stack([jnp.concatenate([cos, zero], -1), jnp.concatenate([sin, zero], -1)])
    return half, both


def _pad_lanes(v, n):
    return jnp.concatenate([v, jnp.zeros(v.shape[:-1] + (n - v.shape[-1],), v.dtype)], axis=-1)


def kernel(x, c, ctx, c_ctx, ada_w, ada_b, norm_g, w_in, mla_cq_g, mla_ckv_g, mla_wuq, mla_wukv, mla_q_g,
           mla_k_g, gla_a_w, gla_a_b, gla_out_g, ml_conv_w, ml_conv_b, ml_wq, ml_wk, ml_gate_b, ml_out_g,
           ml_skip, df_qk_g, df_lambda, df_out_g, br_w, w_out):
    B, S, D = x.shape
    n_ctx = ctx.shape[1]
    L = ada_w.shape[0]
    assert D == D_MODEL and n_ctx == ROW_TILE and S % ROW_TILE == 0 and S % GRID_W == 0

    rope_half, rope_both = _rope_tables(S // GRID_W, n_ctx)
    w_in_p = _layout_w_in(w_in)

    n_rows = -(-(B + 1) // 8) * 8
    cc = jnp.concatenate([c, c_ctx[None], jnp.zeros((n_rows - B - 1, D), F32)], axis=0)
    mod_all = _modulation(cc, ada_w, ada_b)

    xs = jnp.concatenate([ctx, x], axis=1)
    for l in range(L):
        last = l == L - 1
        lam_init = 0.8 - 0.6 * math.exp(-0.3 * l)
        m3 = mod_all[l].reshape(n_rows, 3, D)
        lat, cx = m3[:B], jnp.broadcast_to(m3[B][None], (B, 3, D))
        mod = jnp.concatenate([cx, lat, jnp.zeros((B, 2, D), F32)], axis=1)
        gates = jnp.stack([cx[:, 2], lat[:, 2]], axis=1)

        p = _in_projection(xs, mod, norm_g[l], w_in_p[l], n_ctx)

        wq4 = mla_wuq[l].reshape(MLA_Q_LORA, HEADS, MLA_NOPE + ROT_DIM)
        wuq = _pad_lanes(wq4, 256).reshape(MLA_Q_LORA, HEADS * 256).astype(BF16)
        wkv4 = mla_wukv[l].reshape(MLA_KV_LORA, HEADS, MLA_NOPE + HEAD_V)
        wukv = jnp.concatenate([wkv4[..., :MLA_NOPE].reshape(MLA_KV_LORA, -1),
                                wkv4[..., MLA_NOPE:].reshape(MLA_KV_LORA, -1)], axis=-1).astype(BF16)
        qg = jnp.stack([mla_q_g[l, :MLA_NOPE], _pad_lanes(mla_q_g[l, MLA_NOPE:], 128)])
        kg = jnp.stack([mla_k_g[l, :MLA_NOPE], _pad_lanes(mla_k_g[l, MLA_NOPE:], 128)])
        y_mla = _mla(p, wuq, wukv, mla_cq_g[l][None], mla_ckv_g[l][None], qg, kg, rope_half, n_ctx, last)

        qkg = jnp.tile(df_qk_g[l], (1, 2))
        y_df = _diff(p, qkg, df_lambda[l], df_out_g[l][None], rope_both, n_ctx, last, lam_init)

        aw = jnp.zeros((128, 512), F32)
        aw = aw.at[0:16, 0:256].set(gla_a_w[l, 0]).at[16:32, 256:512].set(gla_a_w[l, 1]).astype(BF16)
        y_gla = _gla(p, aw, gla_a_b[l].reshape(1, 512), gla_out_g[l][None])

        wq_bd = jnp.zeros((BRANCH_WIDTH, HEADS * ML_DQK), F32)
        wk_bd = jnp.zeros((BRANCH_WIDTH, HEADS * ML_DQK), F32)
        for h in range(HEADS):
            wq_bd = wq_bd.at[128 * h:128 * h + 128, 64 * h:64 * h + 64].set(ml_wq[l, h])
            wk_bd = wk_bd.at[128 * h:128 * h + 128, 64 * h:64 * h + 64].set(ml_wk[l, h])
        gb = _pad_lanes(ml_gate_b[l].reshape(1, 16), 128)
        y_ml = _mlstm(p, ml_conv_w[l], ml_conv_b[l][None], wq_bd.astype(BF16), wk_bd.T.astype(BF16), gb,
                      ml_out_g[l][None], ml_skip[l][None], n_ctx)

        xs = _merge((y_mla, y_gla, y_ml, y_df), p, br_w[l].astype(BF16), w_out[l].astype(BF16), xs, gates,
                    n_ctx, last)
    return xs
```

```python
import functools
import math

import jax
import jax.numpy as jnp
from jax import lax
from jax.experimental import pallas as pl
from jax.experimental.pallas import tpu as pltpu

F32 = jnp.float32
BF16 = jnp.bfloat16

D_MODEL = 1024
GRID_W = 64
EPS = 1e-6
ROPE_BASE = 10000.0
ROT_DIM = 64
CHUNK = 64
N_BRANCH = 4
BRANCH_WIDTH = 512
HEADS = 4
HEAD_V = BRANCH_WIDTH // HEADS
MLA_NOPE = 128
MLA_Q_LORA = 384
MLA_KV_LORA = 256
GLA_DK = 64
GLA_GATE_RANK = 16
GLA_TAU = 16.0
ML_DQK = 64
DF_DQK = 64

IN_SPLITS = (
    ('mla_cq', 384), ('mla_ckv', 256), ('mla_kr', 64), ('mla_z', 512),
    ('gla_q', 256), ('gla_k', 256), ('gla_v', 512), ('gla_af', 16), ('gla_ab', 16), ('gla_z', 512),
    ('ml_x', 512), ('ml_v', 512), ('ml_o', 512), ('ml_if', 16), ('ml_z', 512),
    ('df_q', 512), ('df_k', 512), ('df_v', 512), ('df_z', 512),
    ('merge', 4096),
)

P_MERGE = 0
P_DF_KV = 4096
P_DF_Q = 5120
P_DF_Z = 5632
P_ML = 6144
P_GLA_VZ = 8192
P_MLA_Z = 9216
P_GLA_QK = 9728
P_MLA_A = 10240
P_MLA_CKV = 10752
P_GLA_G = 11008
P_ML_IF = 11136
P_WIDTH = 11264

ROW_TILE = 256
IN_PROJ_COLS = 1024
MERGE_ROWS = 768
ML_CHUNK = 256
ATT_ROWS = 512
VMEM_LIMIT = 56 * 1024 * 1024
EXP_CLAMP = 80.0


def _dot(a, b):
    return jnp.dot(a, b, preferred_element_type=F32)


def _dot_nt(a, b):
    return lax.dot_general(a, b, (((1,), (1,)), ((), ())), preferred_element_type=F32)


def _dot_tn(a, b):
    return lax.dot_general(a, b, (((0,), (0,)), ((), ())), preferred_element_type=F32)


def _sigmoid(x):
    return 1.0 / (1.0 + jnp.exp(-x))


def _silu(x):
    return x * _sigmoid(x)


def _log_sigmoid(x):
    return jnp.minimum(x, 0.0) - jnp.log(1.0 + jnp.exp(-jnp.abs(x)))


def _rms(x, n):
    return x * lax.rsqrt(jnp.sum(x * x, axis=-1, keepdims=True) * (1.0 / n) + EPS)


def _split(x):
    hi = x.astype(BF16)
    return hi, (x - hi.astype(F32)).astype(BF16)


def _group_sum(x, group):
    k = x.shape[-1]
    row = lax.broadcasted_iota(jnp.int32, (k, 128), 0)
    col = lax.broadcasted_iota(jnp.int32, (k, 128), 1)
    sel = jnp.ones((k, 128), BF16) if group is None else jnp.where(row // group == col // group, 1.0, 0.0).astype(BF16)
    hi, lo = _split(x)
    return _dot(hi, sel) + _dot(lo, sel)


def _rms_mxu(x, n, group=None):
    inv = lax.rsqrt(_group_sum(x * x, group) * (1.0 / n) + EPS)
    return x * (inv if x.shape[-1] == 128 else jnp.tile(inv, (1, x.shape[-1] // 128)))


def _rot_matrix():
    src = lax.broadcasted_iota(jnp.int32, (128, 128), 0)
    dst = lax.broadcasted_iota(jnp.int32, (128, 128), 1)
    even = (dst // 16) % 2 == 0
    return jnp.where(even & (src == dst + 16), -1.0, jnp.where(~even & (src == dst - 16), 1.0, 0.0)).astype(BF16)


def _rope_mxu(x, cos, sin, rot):
    hi, lo = _split(x)
    return x * cos + (_dot(hi, rot) + _dot(lo, rot)) * sin


def _ones_col(rows):
    return jnp.where(lax.broadcasted_iota(jnp.int32, (rows, HEAD_V), 1) == 0, 1.0, 0.0).astype(BF16)


def _softmax_pv(s, v_ext):
    e = jnp.exp((s - jnp.max(s, axis=-1, keepdims=True)).astype(BF16))
    o = _dot(e, v_ext)
    return o[:, 0:HEAD_V] / o[:, HEAD_V:HEAD_V + 1]


def _params(*sem):
    return pltpu.CompilerParams(dimension_semantics=sem, vmem_limit_bytes=VMEM_LIMIT)


def _mod_kernel(c_ref, w_ref, b_ref, o_ref):
    s = _silu(c_ref[...])
    o_ref[0] = _dot(s.astype(BF16), w_ref[0].astype(BF16)) + b_ref[0]


def _modulation(cc, ada_w, ada_b):
    L, D, _ = ada_w.shape
    R = cc.shape[0]
    return pl.pallas_call(
        _mod_kernel,
        grid=(L, 3),
        in_specs=[pl.BlockSpec((R, D), lambda l, j: (0, 0)),
                  pl.BlockSpec((1, D, D), lambda l, j: (l, 0, j)),
                  pl.BlockSpec((1, 1, D), lambda l, j: (l, 0, j))],
        out_specs=pl.BlockSpec((1, R, D), lambda l, j: (l, 0, j)),
        out_shape=jax.ShapeDtypeStruct((L, R, 3 * D), F32),
        compiler_params=_params("arbitrary", "arbitrary"),
        name="modulation",
    )(cc, ada_w, ada_b.reshape(L, 1, 3 * D))


def _inproj_kernel(x_ref, mod_ref, g_ref, w_ref, o_ref, h_scr, *, n_ctx):
    T = x_ref.shape[1]

    @pl.when(pl.program_id(1) == 0)
    def _():
        g = g_ref[...]
        for r0 in range(0, T, ROW_TILE):
            k = 0 if r0 < n_ctx else 3
            y = _rms(x_ref[0, r0:r0 + ROW_TILE, :], D_MODEL) * g
            h = y * (1.0 + mod_ref[0, k + 1:k + 2, :]) + mod_ref[0, k:k + 1, :]
            h_scr[r0:r0 + ROW_TILE, :] = h.astype(BF16)

    for r0 in range(0, T, ROW_TILE):
        o_ref[0, r0:r0 + ROW_TILE, :] = _dot(h_scr[r0:r0 + ROW_TILE, :], w_ref[...]).astype(BF16)


def _in_projection(xs, mod, norm_g, w, n_ctx):
    B, T, D = xs.shape
    n_col = P_WIDTH // IN_PROJ_COLS
    return pl.pallas_call(
        functools.partial(_inproj_kernel, n_ctx=n_ctx),
        grid=(B, n_col),
        in_specs=[pl.BlockSpec((1, T, D), lambda b, j: (b, 0, 0)),
                  pl.BlockSpec((1, 8, D), lambda b, j: (b, 0, 0)),
                  pl.BlockSpec((1, D), lambda b, j: (0, 0)),
                  pl.BlockSpec((D, IN_PROJ_COLS), lambda b, j: (0, j))],
        out_specs=pl.BlockSpec((1, T, IN_PROJ_COLS), lambda b, j: (b, 0, j)),
        out_shape=jax.ShapeDtypeStruct((B, T, P_WIDTH), BF16),
        scratch_shapes=[pltpu.VMEM((T, D), BF16)],
        compiler_params=_params("arbitrary", "arbitrary"),
        name="in_projection",
    )(xs, mod, norm_g.reshape(1, D), w)


def _attend_all(attend, o_ref, T, n_ctx, with_ctx):
    if with_ctx:
        attend(pl.ds(0, n_ctx), n_ctx)
    else:
        o_ref[0, 0:n_ctx, :] = jnp.zeros((n_ctx, o_ref.shape[2]), o_ref.dtype)
    rows_per = ATT_ROWS if (T - n_ctx) % ATT_ROWS == 0 else ROW_TILE

    def latent(i, carry):
        attend(pl.ds(pl.multiple_of(n_ctx + i * rows_per, ROW_TILE), rows_per), T)
        return carry

    lax.fori_loop(0, (T - n_ctx) // rows_per, latent, 0)


def _mla_kernel(pa_ref, pc_ref, z_ref, wuq_ref, wukv_ref, cqg_ref, ckvg_ref, qg_ref, kg_ref, rope_ref,
                o_ref, q_scr, k_scr, v_scr, *, n_ctx, with_ctx):
    T = pa_ref.shape[1]
    scale = (MLA_NOPE + ROT_DIM) ** -0.5
    rot = _rot_matrix()

    def prep(i, carry):
        r = pl.multiple_of(i * ROW_TILE, ROW_TILE)
        rows = pl.ds(r, ROW_TILE)
        cos, sin = rope_ref[0, rows, :], rope_ref[1, rows, :]
        pa = pa_ref[0, rows, :].astype(F32)
        cq = (_rms(pa[:, :MLA_Q_LORA], MLA_Q_LORA) * cqg_ref[...]).astype(BF16)
        q = _dot(cq, wuq_ref[...])
        kr = _rms(pa[:, MLA_Q_LORA:], ROT_DIM) * kg_ref[1:2, :]
        kr = _rope_mxu(kr, cos, sin, rot).astype(BF16)
        ckv = (_rms(pc_ref[0, rows, :].astype(F32), MLA_KV_LORA) * ckvg_ref[...]).astype(BF16)
        kv = _dot(ckv, wukv_ref[...])
        for h in range(HEADS):
            qn = _rms(q[:, 256 * h:256 * h + 128], MLA_NOPE) * qg_ref[0:1, :]
            qr = _rms(q[:, 256 * h + 128:256 * h + 256], ROT_DIM) * qg_ref[1:2, :]
            qr = _rope_mxu(qr, cos, sin, rot)
            q_scr[rows, 256 * h:256 * h + 128] = (qn * scale).astype(BF16)
            q_scr[rows, 256 * h + 128:256 * h + 256] = (qr * scale).astype(BF16)
            kn = _rms(kv[:, 128 * h:128 * h + 128], MLA_NOPE) * kg_ref[0:1, :]
            k_scr[rows, 256 * h:256 * h + 128] = kn.astype(BF16)
            k_scr[rows, 256 * h + 128:256 * h + 256] = kr
            v_scr[rows, 256 * h:256 * h + 128] = kv[:, 512 + 128 * h:512 + 128 * h + 128].astype(BF16)
            v_scr[rows, 256 * h + 128:256 * h + 256] = _ones_col(ROW_TILE)
        return carry

    lax.fori_loop(0, T // ROW_TILE, prep, 0)

    def attend(rows, nk):
        def scores(h):
            return _dot_nt(q_scr[rows, 256 * h:256 * h + 256], k_scr[0:nk, 256 * h:256 * h + 256])

        s_next = scores(0)
        for h in range(HEADS):
            s = s_next
            if h + 1 < HEADS:
                s_next = scores(h + 1)
            o = _softmax_pv(s, v_scr[0:nk, 256 * h:256 * h + 256])
            z = z_ref[0, rows, 128 * h:128 * h + 128].astype(F32)
            o_ref[0, rows, 128 * h:128 * h + 128] = (o * _silu(z)).astype(BF16)

    _attend_all(attend, o_ref, T, n_ctx, with_ctx)


def _mla(p, wuq, wukv, cqg, ckvg, qg, kg, rope, n_ctx, last):
    B, T, _ = p.shape
    full = lambda shape: pl.BlockSpec(shape, lambda b: (0,) * len(shape))
    return pl.pallas_call(
        functools.partial(_mla_kernel, n_ctx=n_ctx, with_ctx=not last),
        grid=(B,),
        in_specs=[pl.BlockSpec((1, T, 512), lambda b: (b, 0, P_MLA_A // 512)),
                  pl.BlockSpec((1, T, 256), lambda b: (b, 0, P_MLA_CKV // 256)),
                  pl.BlockSpec((1, T, 512), lambda b: (b, 0, P_MLA_Z // 512)),
                  full(wuq.shape), full(wukv.shape), full(cqg.shape), full(ckvg.shape),
                  full(qg.shape), full(kg.shape), full(rope.shape)],
        out_specs=pl.BlockSpec((1, T, 512), lambda b: (b, 0, 0)),
        out_shape=jax.ShapeDtypeStruct((B, T, BRANCH_WIDTH), BF16),
        scratch_shapes=[pltpu.VMEM((T, 1024), BF16), pltpu.VMEM((T, 1024), BF16),
                        pltpu.VMEM((T, 1024), BF16)],
        compiler_params=_params("arbitrary"),
        name="mla",
    )(p, p, p, wuq, wukv, cqg, ckvg, qg, kg, rope)


def _diff_kernel(kv_ref, q_ref, z_ref, g_ref, lam_ref, og_ref, rope_ref, o_ref, q0_scr, q1_scr, k_scr, v_scr,
                 *, n_ctx, with_ctx, lam_init):
    T = kv_ref.shape[1]
    scale = DF_DQK ** -0.5
    lo_mask = lax.broadcasted_iota(jnp.int32, (1, 128), 1) < 64
    rot = _rot_matrix()

    def prep(i, carry):
        r = pl.multiple_of(i * ROW_TILE, ROW_TILE)
        rows = pl.ds(r, ROW_TILE)
        cos, sin = rope_ref[0, rows, :], rope_ref[1, rows, :]
        for h in range(HEADS):
            cols = slice(128 * h, 128 * h + 128)
            q = _rms_mxu(q_ref[0, rows, cols].astype(F32), DF_DQK, DF_DQK) * g_ref[0:1, :]
            q = _rope_mxu(q, cos, sin, rot) * scale
            q0_scr[rows, cols] = jnp.where(lo_mask, q, 0.0).astype(BF16)
            q1_scr[rows, cols] = jnp.where(lo_mask, 0.0, q).astype(BF16)
            k = _rms_mxu(kv_ref[0, rows, cols].astype(F32), DF_DQK, DF_DQK) * g_ref[1:2, :]
            k_scr[rows, cols] = _rope_mxu(k, cos, sin, rot).astype(BF16)
            v_scr[rows, 256 * h:256 * h + 128] = kv_ref[0, rows, 512 + 128 * h:512 + 128 * h + 128]
            v_scr[rows, 256 * h + 128:256 * h + 256] = _ones_col(ROW_TILE)
        return carry

    lax.fori_loop(0, T // ROW_TILE, prep, 0)

    lp = lam_ref[...]
    lam = (jnp.exp(jnp.sum(lp[0:1] * lp[1:2], axis=-1, keepdims=True))
           - jnp.exp(jnp.sum(lp[2:3] * lp[3:4], axis=-1, keepdims=True)) + lam_init)
    def attend(rows, nk):
        def scores(u):
            cols = slice(128 * (u // 2), 128 * (u // 2) + 128)
            return _dot_nt((q1_scr if u % 2 else q0_scr)[rows, cols], k_scr[0:nk, cols])

        s_next = scores(0)
        for h in range(HEADS):
            cols = slice(128 * h, 128 * h + 128)
            v_ext = v_scr[0:nk, 256 * h:256 * h + 256]
            parts = []
            for mp in range(2):
                s = s_next
                if 2 * h + mp + 1 < 2 * HEADS:
                    s_next = scores(2 * h + mp + 1)
                parts.append(_softmax_pv(s, v_ext))
            o = parts[0] - lam * parts[1]
            o = _rms(o, HEAD_V) * og_ref[...] * (1.0 - lam_init)
            z = z_ref[0, rows, cols].astype(F32)
            o_ref[0, rows, cols] = (o * _silu(z)).astype(BF16)

    _attend_all(attend, o_ref, T, n_ctx, with_ctx)


def _diff(p, qkg, lam_p, og, rope, n_ctx, last, lam_init):
    B, T, _ = p.shape
    full = lambda shape: pl.BlockSpec(shape, lambda b: (0,) * len(shape))
    return pl.pallas_call(
        functools.partial(_diff_kernel, n_ctx=n_ctx, with_ctx=not last, lam_init=lam_init),
        grid=(B,),
        in_specs=[pl.BlockSpec((1, T, 1024), lambda b: (b, 0, P_DF_KV // 1024)),
                  pl.BlockSpec((1, T, 512), lambda b: (b, 0, P_DF_Q // 512)),
                  pl.BlockSpec((1, T, 512), lambda b: (b, 0, P_DF_Z // 512)),
                  full(qkg.shape), full(lam_p.shape), full(og.shape), full(rope.shape)],
        out_specs=pl.BlockSpec((1, T, 512), lambda b: (b, 0, 0)),
        out_shape=jax.ShapeDtypeStruct((B, T, BRANCH_WIDTH), BF16),
        scratch_shapes=[pltpu.VMEM((T, 512), BF16), pltpu.VMEM((T, 512), BF16), pltpu.VMEM((T, 512), BF16),
                        pltpu.VMEM((T, 1024), BF16)],
        compiler_params=_params("arbitrary"),
        name="diff_attn",
    )(p, p, p, qkg, lam_p, og, rope)


def _tile_order(i, n_tiles, rev):
    if not rev:
        return i
    return jnp.where(i == 0, 0, n_tiles - i)


def _chunk_tri(n, rev, chunk=CHUNK):
    r = lax.broadcasted_iota(jnp.int32, (n, n), 0)
    c = lax.broadcasted_iota(jnp.int32, (n, n), 1)
    same = (r // chunk) == (c // chunk)
    tri = (c >= r) if rev else (c <= r)
    return jnp.where(same & tri, 1.0, 0.0).astype(F32)


def _causal(rev, n=CHUNK):
    t = lax.broadcasted_iota(jnp.int32, (n, n), 0)
    s = lax.broadcasted_iota(jnp.int32, (n, n), 1)
    return (s >= t) if rev else (s <= t)


def _cummax_rows(x, rev):
    n = x.shape[0]
    row = lax.broadcasted_iota(jnp.int32, (n, 1), 0)
    s = 1
    while s < n:
        if rev:
            x = jnp.maximum(x, jnp.where(row < n - s, pltpu.roll(x, n - s, 0), -jnp.inf))
        else:
            x = jnp.maximum(x, jnp.where(row >= s, pltpu.roll(x, s, 0), -jnp.inf))
        s *= 2
    return x


def _head_mask(width, group):
    lane = lax.broadcasted_iota(jnp.int32, (1, width), 1) // group
    return [lane == h for h in range(HEADS)]


def _gla_kernel(qk_ref, vz_ref, g_ref, aw_ref, ab_ref, og_ref, o_ref, of_scr, ob_scr, st_scr):
    T = qk_ref.shape[1]
    n_tiles = T // ROW_TILE
    n_chunk = ROW_TILE // CHUNK
    hm = _head_mask(HEADS * GLA_DK, GLA_DK)
    er = lax.broadcasted_iota(jnp.int32, (BRANCH_WIDTH, HEADS * GLA_DK), 0) // HEAD_V
    dc = lax.broadcasted_iota(jnp.int32, (BRANCH_WIDTH, HEADS * GLA_DK), 1) // GLA_DK
    block = er == dc
    tris = (_chunk_tri(ROW_TILE, False), _chunk_tri(ROW_TILE, True))
    st_scr[...] = jnp.zeros_like(st_scr)

    def load_tile(t, rev):
        rows = pl.ds(pl.multiple_of(t * ROW_TILE, ROW_TILE), ROW_TILE)
        gcol = 256 if rev else 0
        qk = qk_ref[0, rows, :].astype(F32)
        pre = _dot(g_ref[0, rows, :], aw_ref[:, gcol:gcol + 256]) + ab_ref[:, gcol:gcol + 256]
        la = _log_sigmoid(pre) * (1.0 / GLA_TAU)
        hi, lo = _split(la)
        tri = tris[rev].astype(BF16)
        return qk[:, :256] * (GLA_DK ** -0.5), qk[:, 256:], _dot(tri, hi) + _dot(tri, lo)

    def tile(i, carry):
        tiles = (i, _tile_order(i, n_tiles, True))
        steps = []
        for cc in range(n_chunk):
            steps += [(0, cc), (1, n_chunk - 1 - cc)]
        data = [load_tile(tiles[d], bool(d)) for d in (0, 1)]
        pre = {}
        for d, cc in steps:
            q, k, b = data[d]
            cr = slice(CHUNK * cc, CHUNK * cc + CHUNK)
            bc, qc, kc = b[cr], q[cr], k[cr]
            b_first = bc[CHUNK - 1:CHUNK] if d else bc[0:1]
            b_end = bc[0:1] if d else bc[CHUNK - 1:CHUNK]
            crow = pl.ds(pl.multiple_of(tiles[d] * ROW_TILE + CHUNK * cc, CHUNK), CHUNK)
            q_in = qc * jnp.exp(bc - b_first)
            pre[d, cc] = dict(
                crow=crow, vc=vz_ref[0, crow, 0:BRANCH_WIDTH], decay=jnp.exp(b_end),
                qe=(qc * jnp.exp(bc)).astype(BF16),
                q_in=[jnp.where(hm[h], q_in, 0.0).astype(BF16) for h in range(HEADS)],
                k_in=(kc * jnp.exp(jnp.minimum(b_first - bc, EXP_CLAMP))).astype(BF16),
                k_up=(kc * jnp.exp(b_end - bc)).astype(BF16))
        a_raw = {(d, cc, h): _dot_nt(pre[d, cc]['q_in'][h], pre[d, cc]['k_in'])
                 for d, cc in steps for h in range(HEADS)}
        upd = {(d, cc): _dot_tn(pre[d, cc]['vc'], pre[d, cc]['k_up']) for d, cc in steps}
        a_msk = {key: jnp.where(_causal(bool(key[0])), v, 0.0).astype(BF16) for key, v in a_raw.items()}
        intra = {(d, cc): jnp.concatenate(
            [_dot(a_msk[d, cc, h], pre[d, cc]['vc'][:, 128 * h:128 * h + 128]) for h in range(HEADS)], axis=-1)
            for d, cc in steps}
        st = [st_scr[0], st_scr[1]]
        for d, cc in steps:
            o = _dot_nt(pre[d, cc]['qe'], st[d].astype(BF16))
            (ob_scr if d else of_scr)[pre[d, cc]['crow'], :] = o + intra[d, cc]
            st[d] = st[d] * pre[d, cc]['decay'] + jnp.where(block, upd[d, cc], 0.0)
        st_scr[0] = st[0]
        st_scr[1] = st[1]
        return carry

    lax.fori_loop(0, n_tiles, tile, 0)

    def finish(i, carry):
        rows = pl.ds(pl.multiple_of(i * ROW_TILE, ROW_TILE), ROW_TILE)
        tot = of_scr[rows, :] + ob_scr[rows, :]
        z = vz_ref[0, rows, BRANCH_WIDTH:].astype(F32)
        for h in range(HEADS):
            cols = slice(128 * h, 128 * h + 128)
            y = _rms(tot[:, cols], HEAD_V) * og_ref[:, cols]
            o_ref[0, rows, cols] = (y * _silu(z[:, cols])).astype(BF16)
        return carry

    lax.fori_loop(0, n_tiles, finish, 0)


def _gla(p, aw, ab, og):
    B, T, _ = p.shape
    full = lambda shape: pl.BlockSpec(shape, lambda b: (0,) * len(shape))
    return pl.pallas_call(
        _gla_kernel,
        grid=(B,),
        in_specs=[pl.BlockSpec((1, T, 512), lambda b: (b, 0, P_GLA_QK // 512)),
                  pl.BlockSpec((1, T, 1024), lambda b: (b, 0, P_GLA_VZ // 1024)),
                  pl.BlockSpec((1, T, 128), lambda b: (b, 0, P_GLA_G // 128)),
                  full(aw.shape), full(ab.shape), full(og.shape)],
        out_specs=pl.BlockSpec((1, T, 512), lambda b: (b, 0, 0)),
        out_shape=jax.ShapeDtypeStruct((B, T, BRANCH_WIDTH), BF16),
        scratch_shapes=[pltpu.VMEM((T, BRANCH_WIDTH), F32), pltpu.VMEM((T, BRANCH_WIDTH), F32),
                        pltpu.VMEM((2, BRANCH_WIDTH, HEADS * GLA_DK), F32)],
        compiler_params=_params("arbitrary"),
        name="gla_scan",
    )(p, p, p, aw, ab, og)


def _mlstm_kernel(m_ref, if_ref, cw_ref, cb_ref, wq_ref, wkt_ref, gb_ref, og_ref, sk_ref, o_ref,
                  xc_scr, q_scr, kt_scr, at_scr, bt_scr, cm_scr, b_scr, hf_scr, hb_scr, c_scr, *, n_ctx):
    T = m_ref.shape[1]
    n_tiles = T // ROW_TILE
    assert ML_CHUNK == ROW_TILE
    n_chunk = 1
    ctx_tiles = n_ctx // ROW_TILE
    hm = _head_mask(HEADS * ML_DQK, ML_DQK)
    lane128 = lax.broadcasted_iota(jnp.int32, (1, 128), 1)
    is_forget = ((lane128 // HEADS) % 2) == 1
    row_in_tile = lax.broadcasted_iota(jnp.int32, (ROW_TILE, 1), 0)
    tri_f = _chunk_tri(ROW_TILE, False, ML_CHUNK).astype(BF16)
    tri_b = _chunk_tri(ROW_TILE, True, ML_CHUNK).astype(BF16)

    def prep(i, carry):
        r = pl.multiple_of(i * ROW_TILE, ROW_TILE)
        rows = pl.ds(r, ROW_TILE)
        x = m_ref[0, rows, 0:512].astype(F32)
        rp = pl.multiple_of(jnp.maximum(r - 16, 0), 16)
        rn = pl.multiple_of(jnp.minimum(r + ROW_TILE, T - 16), 16)
        prev_row = m_ref[0, pl.ds(rp, 16), 0:512].astype(F32)[15:16]
        next_row = m_ref[0, pl.ds(rn, 16), 0:512].astype(F32)[0:1]
        seg_start = (i == 0) | (i == ctx_tiles)
        seg_end = (i == ctx_tiles - 1) | (i == n_tiles - 1)
        prev_row = jnp.where(seg_start, 0.0, prev_row)
        next_row = jnp.where(seg_end, 0.0, next_row)
        xm = jnp.where(row_in_tile == 0, prev_row, pltpu.roll(x, 1, 0))
        xp = jnp.where(row_in_tile == ROW_TILE - 1, next_row, pltpu.roll(x, ROW_TILE - 1, 0))
        xc = _silu(cw_ref[0:1, :] * xm + cw_ref[1:2, :] * x + cw_ref[2:3, :] * xp + cb_ref[...])
        xc_scr[rows, :] = xc
        xb = xc.astype(BF16)
        q_scr[rows, :] = _dot(xb, wq_ref[...]).astype(BF16)
        g = if_ref[0, rows, :].astype(F32) + gb_ref[...]
        g2 = jnp.where(is_forget, _log_sigmoid(g), g)
        g_hi, g_lo = _split(g2)
        cs = jnp.where(lane128 < 2 * HEADS, _dot(tri_f, g_hi) + _dot(tri_f, g_lo),
                       _dot(tri_b, g_hi) + _dot(tri_b, g_lo))
        b = pltpu.roll(cs, 128 - HEADS, 1)
        a = g2 - b
        cm_scr[rows, :] = jnp.where(lane128 < 2 * HEADS, _cummax_rows(a, False), _cummax_rows(a, True))
        b_scr[rows, :] = b
        kt_scr[i] = (_dot_nt(wkt_ref[...], xb) * (ML_DQK ** -0.5)).astype(BF16)
        at_scr[i] = a.T[0:16, :]
        bt_scr[i] = b.T[0:16, :]
        return carry

    lax.fori_loop(0, n_tiles, prep, 0)

    c_scr[...] = jnp.zeros_like(c_scr)
    ones_col = _ones_col(ML_CHUNK)
    sel_r = lax.broadcasted_iota(jnp.int32, (256, HEADS * 128), 0) % 128
    sel_c = lax.broadcasted_iota(jnp.int32, (256, HEADS * 128), 1) // 128
    sel = [jnp.where(sel_r == 2 * HEADS * d + sel_c, 1.0, 0.0).astype(BF16) for d in (0, 1)]

    def col_bcast(x, d):
        hi, lo = _split(x)
        return _dot(jnp.concatenate([hi, lo], axis=-1), sel[d])

    def chunk_pair(c_f, c_b, m_in):
        units = [(d, h) for d in (0, 1) for h in range(HEADS)]
        cs = (c_f, c_b)
        crow = [pl.ds(pl.multiple_of(c * ML_CHUNK, ML_CHUNK), ML_CHUNK) for c in cs]
        qc = [q_scr[crow[d], :] for d in (0, 1)]
        kt = [kt_scr[cs[d]] for d in (0, 1)]
        vc = [m_ref[0, crow[d], 512:1024] for d in (0, 1)]
        at = [at_scr[cs[d]] for d in (0, 1)]
        bt = [bt_scr[cs[d]] for d in (0, 1)]
        c_bf = [c_scr[d].astype(BF16) for d in (0, 1)]
        causal = [_causal(False, ML_CHUNK), _causal(True, ML_CHUNK)]
        last = [ML_CHUNK - 1, 0]
        qh = {(d, h): jnp.where(hm[h], qc[d], jnp.zeros_like(qc[d])) for d, h in units}
        v_ext = {(d, h): jnp.concatenate([vc[d][:, 128 * h:128 * h + 128], ones_col], axis=-1)
                 for d, h in units}
        s_raw = {u: _dot(qh[u], kt[u[0]]) for u in units}
        q_c = {u: _dot(qh[u], c_bf[u[0]]) for u in units}
        cm_col = [col_bcast(cm_scr[crow[d], :], d) for d in (0, 1)]
        b_col = [col_bcast(b_scr[crow[d], :], d) for d in (0, 1)]
        s_w, m_run, m_last, ktw = {}, {}, {}, {}
        for d, h in units:
            j = 2 * HEADS * d + h
            a_row, m_old = at[d][j:j + 1, :], m_in[HEADS * d + h]
            pm = jnp.where(causal[d], a_row, -jnp.inf)
            m_run[d, h] = jnp.maximum(m_old, cm_col[d][:, 128 * h:128 * h + 128])
            m_wide = jnp.concatenate([m_run[d, h]] * (ML_CHUNK // 128), axis=-1)
            s_w[d, h] = (s_raw[d, h] * jnp.exp(pm - m_wide)).astype(BF16)
            m_last[d, h] = m_run[d, h][last[d]:last[d] + 1, 0:1]
            k_h = kt[d][ML_DQK * h:ML_DQK * h + ML_DQK, :].astype(F32)
            ktw[d, h] = (k_h * jnp.exp(a_row - m_last[d, h])).astype(BF16)
        s_v = {u: _dot(s_w[u], v_ext[u]) for u in units}
        upd = {u: _dot(ktw[u], v_ext[u]) for u in units}
        m_out = []
        for d in (0, 1):
            parts = []
            for h in range(HEADS):
                j = 2 * HEADS * d + h
                hr = slice(ML_DQK * h, ML_DQK * h + ML_DQK)
                b_row, m_old = bt[d][j:j + 1, :], m_in[HEADS * d + h]
                carry_w = jnp.exp(m_old - m_run[d, h])
                num = jnp.concatenate([carry_w, carry_w], axis=-1) * q_c[d, h] + s_v[d, h]
                den = num[:, HEAD_V:HEAD_V + 1]
                floor = jnp.exp(-(b_col[d][:, 128 * h:128 * h + 128] + m_run[d, h]))
                parts.append(num[:, 0:HEAD_V] / jnp.maximum(jnp.abs(den), floor))
                c_scr[d, hr, :] = jnp.exp(m_old - m_last[d, h]) * c_scr[d, hr, :] + upd[d, h]
                m_out.append(b_row[:, last[d]:last[d] + 1] + m_last[d, h])
            (hb_scr if d else hf_scr)[crow[d], :] = jnp.concatenate(parts, axis=-1)
        return m_out

    def tile(i, carry):
        m = list(carry)
        c_f = i * n_chunk
        c_b = _tile_order(i, n_tiles, True) * n_chunk + (n_chunk - 1)
        for cc in range(n_chunk):
            m = chunk_pair(c_f + cc, c_b - cc, m)
        return tuple(m)

    lax.fori_loop(0, n_tiles, tile, tuple(jnp.zeros((1, 1), F32) for _ in range(2 * HEADS)))

    def finish(i, carry):
        rows = pl.ds(pl.multiple_of(i * ROW_TILE, ROW_TILE), ROW_TILE)
        tot = (hf_scr[rows, :] + hb_scr[rows, :]) * _sigmoid(m_ref[0, rows, 1024:1536].astype(F32))
        z = m_ref[0, rows, 1536:2048].astype(F32)
        xc = xc_scr[rows, :]
        for h in range(HEADS):
            cols = slice(128 * h, 128 * h + 128)
            y = _rms(tot[:, cols], HEAD_V) * og_ref[:, cols]
            y = (y + sk_ref[:, cols] * xc[:, cols]) * _silu(z[:, cols])
            o_ref[0, rows, cols] = y.astype(BF16)
        return carry

    lax.fori_loop(0, n_tiles, finish, 0)


def _mlstm(p, cw, cb, wq, wkt, gb, og, sk, n_ctx):
    B, T, _ = p.shape
    full = lambda shape: pl.BlockSpec(shape, lambda b: (0,) * len(shape))
    return pl.pallas_call(
        functools.partial(_mlstm_kernel, n_ctx=n_ctx),
        grid=(B,),
        in_specs=[pl.BlockSpec((1, T, 2048), lambda b: (b, 0, P_ML // 2048)),
                  pl.BlockSpec((1, T, 128), lambda b: (b, 0, P_ML_IF // 128)),
                  full(cw.shape), full(cb.shape), full(wq.shape), full(wkt.shape), full(gb.shape),
                  full(og.shape), full(sk.shape)],
        out_specs=pl.BlockSpec((1, T, 512), lambda b: (b, 0, 0)),
        out_shape=jax.ShapeDtypeStruct((B, T, BRANCH_WIDTH), BF16),
        scratch_shapes=[pltpu.VMEM((T, BRANCH_WIDTH), F32),
                        pltpu.VMEM((T, HEADS * ML_DQK), BF16),
                        pltpu.VMEM((T // ML_CHUNK, HEADS * ML_DQK, ML_CHUNK), BF16),
                        pltpu.VMEM((T // ML_CHUNK, 16, ML_CHUNK), F32),
                        pltpu.VMEM((T // ML_CHUNK, 16, ML_CHUNK), F32),
                        pltpu.VMEM((T, 128), F32),
                        pltpu.VMEM((T, 128), F32),
                        pltpu.VMEM((T, BRANCH_WIDTH), F32),
                        pltpu.VMEM((T, BRANCH_WIDTH), F32),
                        pltpu.VMEM((2, HEADS * ML_DQK, 2 * HEAD_V), F32)],
        compiler_params=_params("arbitrary"),
        name="mlstm_scan",
    )(p, p, cw, cb, wq, wkt, gb, og, sk)


def _merge_kernel(ya_ref, yb_ref, yc_ref, yd_ref, gl_ref, brw_ref, wo_ref, x_ref, gate_ref, o_ref,
                  *, row0, ctx_rows):
    tm = x_ref.shape[1]
    first = pl.program_id(1) * tm + row0
    for r0 in range(0, tm, ROW_TILE):
        rs = slice(r0, r0 + ROW_TILE)
        acc = None
        for i, y_ref in enumerate((ya_ref, yb_ref, yc_ref, yd_ref)):
            u = _dot(y_ref[0, rs, :], brw_ref[i])
            gsig = _sigmoid(gl_ref[0, rs, D_MODEL * i:D_MODEL * (i + 1)].astype(F32))
            acc = gsig * u if acc is None else acc + gsig * u
        out = _dot(acc.astype(BF16), wo_ref[...])
        gate = jnp.where(first + r0 < ctx_rows, gate_ref[0, 0:1, :], gate_ref[0, 1:2, :])
        o_ref[0, rs, :] = x_ref[0, rs, :] + gate * out


def _merge(ys, p, brw, wo, xs, gates, n_ctx, last):
    B, T, D = xs.shape
    tm = ROW_TILE if last else MERGE_ROWS
    row0 = n_ctx if last else 0
    tile0 = row0 // tm
    nt = (T - row0) // tm
    assert (T - row0) % tm == 0 and row0 % tm == 0 and n_ctx % ROW_TILE == 0
    ymap = lambda b, t: (b, t + tile0, 0)
    return pl.pallas_call(
        functools.partial(_merge_kernel, row0=row0, ctx_rows=n_ctx),
        grid=(B, nt),
        in_specs=[pl.BlockSpec((1, tm, BRANCH_WIDTH), ymap)] * 4 + [
            pl.BlockSpec((1, tm, N_BRANCH * D), lambda b, t: (b, t + tile0, P_MERGE // (N_BRANCH * D))),
            pl.BlockSpec(brw.shape, lambda b, t: (0, 0, 0)),
            pl.BlockSpec(wo.shape, lambda b, t: (0, 0)),
            pl.BlockSpec((1, tm, D), ymap),
            pl.BlockSpec((1, 2, D), lambda b, t: (b, 0, 0))],
        out_specs=pl.BlockSpec((1, tm, D), lambda b, t: (b, t, 0)),
        out_shape=jax.ShapeDtypeStruct((B, nt * tm, D), F32),
        compiler_params=_params("arbitrary", "arbitrary"),
        name="merge",
    )(*ys, p, brw, wo, xs, gates)


def _layout_w_in(w_in):
    offs, off = {}, 0
    for name, w in IN_SPLITS:
        offs[name] = (off, w)
        off += w

    def col(name):
        o, w = offs[name]
        return w_in[..., o:o + w]

    def zeros(n):
        return jnp.zeros(w_in.shape[:-1] + (n,), w_in.dtype)

    parts = [col('merge'), col('df_k'), col('df_v'), col('df_q'), col('df_z'),
             col('ml_x'), col('ml_v'), col('ml_o'), col('ml_z'),
             col('gla_v'), col('gla_z'), col('mla_z'), col('gla_q'), col('gla_k'),
             col('mla_cq'), col('mla_kr'), zeros(64),
             col('mla_ckv'), col('gla_af'), col('gla_ab'), zeros(96), col('ml_if'), zeros(112)]
    out = jnp.concatenate(parts, axis=-1).astype(BF16)
    assert out.shape[-1] == P_WIDTH
    return out


def _rope_tables(rows, n_ctx):
    quarter = ROT_DIM // 4
    inv_freq = ROPE_BASE ** (-jnp.arange(quarter, dtype=F32) / quarter)
    row = jnp.repeat(jnp.arange(rows, dtype=F32), GRID_W)
    col = jnp.tile(jnp.arange(GRID_W, dtype=F32), rows)
    ar = row[:, None] * inv_freq
    ac = col[:, None] * inv_freq
    ang = jnp.concatenate([ar, ar, ac, ac], axis=-1)
    cos = jnp.concatenate([jnp.ones((n_ctx, ROT_DIM), F32), jnp.cos(ang)], axis=0)
    sin = jnp.concatenate([jnp.zeros((n_ctx, ROT_DIM), F32), jnp.sin(ang)], axis=0)
    zero = jnp.zeros_like(cos)
    both = jnp.stack([jnp.tile(cos, (1, 2)), jnp.tile(sin, (1, 2))])
    half = jnp.stack([jnp.concatenate([cos, zero], -1), jnp.concatenate([sin, zero], -1)])
    return half, both


def _pad_lanes(v, n):
    return jnp.concatenate([v, jnp.zeros(v.shape[:-1] + (n - v.shape[-1],), v.dtype)], axis=-1)


def kernel(x, c, ctx, c_ctx, ada_w, ada_b, norm_g, w_in, mla_cq_g, mla_ckv_g, mla_wuq, mla_wukv, mla_q_g,
           mla_k_g, gla_a_w, gla_a_b, gla_out_g, ml_conv_w, ml_conv_b, ml_wq, ml_wk, ml_gate_b, ml_out_g,
           ml_skip, df_qk_g, df_lambda, df_out_g, br_w, w_out):
    B, S, D = x.shape
    n_ctx = ctx.shape[1]
    L = ada_w.shape[0]
    assert D == D_MODEL and n_ctx == ROW_TILE and S % ROW_TILE == 0 and S % GRID_W == 0

    rope_half, rope_both = _rope_tables(S // GRID_W, n_ctx)
    w_in_p = _layout_w_in(w_in)

    n_rows = -(-(B + 1) // 8) * 8
    cc = jnp.concatenate([c, c_ctx[None], jnp.zeros((n_rows - B - 1, D), F32)], axis=0)
    mod_all = _modulation(cc, ada_w, ada_b)

    xs = jnp.concatenate([ctx, x], axis=1)
    for l in range(L):
        last = l == L - 1
        lam_init = 0.8 - 0.6 * math.exp(-0.3 * l)
        m3 = mod_all[l].reshape(n_rows, 3, D)
        lat, cx = m3[:B], jnp.broadcast_to(m3[B][None], (B, 3, D))
        mod = jnp.concatenate([cx, lat, jnp.zeros((B, 2, D), F32)], axis=1)
        gates = jnp.stack([cx[:, 2], lat[:, 2]], axis=1)

        p = _in_projection(xs, mod, norm_g[l], w_in_p[l], n_ctx)

        wq4 = mla_wuq[l].reshape(MLA_Q_LORA, HEADS, MLA_NOPE + ROT_DIM)
        wuq = _pad_lanes(wq4, 256).reshape(MLA_Q_LORA, HEADS * 256).astype(BF16)
        wkv4 = mla_wukv[l].reshape(MLA_KV_LORA, HEADS, MLA_NOPE + HEAD_V)
        wukv = jnp.concatenate([wkv4[..., :MLA_NOPE].reshape(MLA_KV_LORA, -1),
                                wkv4[..., MLA_NOPE:].reshape(MLA_KV_LORA, -1)], axis=-1).astype(BF16)
        qg = jnp.stack([mla_q_g[l, :MLA_NOPE], _pad_lanes(mla_q_g[l, MLA_NOPE:], 128)])
        kg = jnp.stack([mla_k_g[l, :MLA_NOPE], _pad_lanes(mla_k_g[l, MLA_NOPE:], 128)])
        y_mla = _mla(p, wuq, wukv, mla_cq_g[l][None], mla_ckv_g[l][None], qg, kg, rope_half, n_ctx, last)

        qkg = jnp.tile(df_qk_g[l], (1, 2))
        y_df = _diff(p, qkg, df_lambda[l], df_out_g[l][None], rope_both, n_ctx, last, lam_init)

        aw = jnp.zeros((128, 512), F32)
        aw = aw.at[0:16, 0:256].set(gla_a_w[l, 0]).at[16:32, 256:512].set(gla_a_w[l, 1]).astype(BF16)
        y_gla = _gla(p, aw, gla_a_b[l].reshape(1, 512), gla_out_g[l][None])

        wq_bd = jnp.zeros((BRANCH_WIDTH, HEADS * ML_DQK), F32)
        wk_bd = jnp.zeros((BRANCH_WIDTH, HEADS * ML_DQK), F32)
        for h in range(HEADS):
            wq_bd = wq_bd.at[128 * h:128 * h + 128, 64 * h:64 * h + 64].set(ml_wq[l, h])
            wk_bd = wk_bd.at[128 * h:128 * h + 128, 64 * h:64 * h + 64].set(ml_wk[l, h])
        gb = _pad_lanes(ml_gate_b[l].reshape(1, 16), 128)
        y_ml = _mlstm(p, ml_conv_w[l], ml_conv_b[l][None], wq_bd.astype(BF16), wk_bd.T.astype(BF16), gb,
                      ml_out_g[l][None], ml_skip[l][None], n_ctx)

        xs = _merge((y_mla, y_gla, y_ml, y_df), p, br_w[l].astype(BF16), w_out[l].astype(BF16), xs, gates,
                    n_ctx, last)
    return xs
```

```python
import functools
import math

import jax
import jax.numpy as jnp
from jax import lax
from jax.experimental import pallas as pl
from jax.experimental.pallas import tpu as pltpu

F32 = jnp.float32
BF16 = jnp.bfloat16

D_MODEL = 1024
GRID_W = 64
EPS = 1e-6
ROPE_BASE = 10000.0
ROT_DIM = 64
CHUNK = 64
N_BRANCH = 4
BRANCH_WIDTH = 512
HEADS = 4
HEAD_V = BRANCH_WIDTH // HEADS
MLA_NOPE = 128
MLA_Q_LORA = 384
MLA_KV_LORA = 256
GLA_DK = 64
GLA_GATE_RANK = 16
GLA_TAU = 16.0
ML_DQK = 64
DF_DQK = 64

IN_SPLITS = (
    ('mla_cq', 384), ('mla_ckv', 256), ('mla_kr', 64), ('mla_z', 512),
    ('gla_q', 256), ('gla_k', 256), ('gla_v', 512), ('gla_af', 16), ('gla_ab', 16), ('gla_z', 512),
    ('ml_x', 512), ('ml_v', 512), ('ml_o', 512), ('ml_if', 16), ('ml_z', 512),
    ('df_q', 512), ('df_k', 512), ('df_v', 512), ('df_z', 512),
    ('merge', 4096),
)

P_MERGE = 0
P_DF_KV = 4096
P_DF_Q = 5120
P_DF_Z = 5632
P_ML = 6144
P_GLA_VZ = 8192
P_MLA_Z = 9216
P_GLA_QK = 9728
P_MLA_A = 10240
P_MLA_CKV = 10752
P_GLA_G = 11008
P_ML_IF = 11136
P_WIDTH = 11264

ROW_TILE = 256
IN_PROJ_COLS = 1024
MERGE_ROWS = 768
ML_CHUNK = 256
ATT_ROWS = 512
VMEM_LIMIT = 56 * 1024 * 1024
EXP_CLAMP = 80.0
GLA_SUB = 16


def _dot(a, b):
    return jnp.dot(a, b, preferred_element_type=F32)


def _dot_nt(a, b):
    return lax.dot_general(a, b, (((1,), (1,)), ((), ())), preferred_element_type=F32)


def _dot_tn(a, b):
    return lax.dot_general(a, b, (((0,), (0,)), ((), ())), preferred_element_type=F32)


def _sigmoid(x):
    return 1.0 / (1.0 + jnp.exp(-x))


def _silu(x):
    return x * _sigmoid(x)


def _log_sigmoid(x):
    return jnp.minimum(x, 0.0) - jnp.log(1.0 + jnp.exp(-jnp.abs(x)))


def _rms(x, n):
    return x * lax.rsqrt(jnp.sum(x * x, axis=-1, keepdims=True) * (1.0 / n) + EPS)


def _split(x):
    hi = x.astype(BF16)
    return hi, (x - hi.astype(F32)).astype(BF16)


def _group_sum(x, group):
    k = x.shape[-1]
    row = lax.broadcasted_iota(jnp.int32, (k, 128), 0)
    col = lax.broadcasted_iota(jnp.int32, (k, 128), 1)
    sel = jnp.ones((k, 128), BF16) if group is None else jnp.where(row // group == col // group, 1.0, 0.0).astype(BF16)
    hi, lo = _split(x)
    return _dot(hi, sel) + _dot(lo, sel)


def _rms_mxu(x, n, group=None):
    inv = lax.rsqrt(_group_sum(x * x, group) * (1.0 / n) + EPS)
    return x * (inv if x.shape[-1] == 128 else jnp.tile(inv, (1, x.shape[-1] // 128)))


def _rot_matrix():
    src = lax.broadcasted_iota(jnp.int32, (128, 128), 0)
    dst = lax.broadcasted_iota(jnp.int32, (128, 128), 1)
    even = (dst // 16) % 2 == 0
    return jnp.where(even & (src == dst + 16), -1.0, jnp.where(~even & (src == dst - 16), 1.0, 0.0)).astype(BF16)


def _rope_mxu(x, cos, sin, rot):
    hi, lo = _split(x)
    return x * cos + (_dot(hi, rot) + _dot(lo, rot)) * sin


def _ones_col(rows):
    return jnp.where(lax.broadcasted_iota(jnp.int32, (rows, HEAD_V), 1) == 0, 1.0, 0.0).astype(BF16)


def _softmax_pv(s, v_ext):
    e = jnp.exp((s - jnp.max(s, axis=-1, keepdims=True)).astype(BF16))
    o = _dot(e, v_ext)
    return o[:, 0:HEAD_V] / o[:, HEAD_V:HEAD_V + 1]


def _params(*sem):
    return pltpu.CompilerParams(dimension_semantics=sem, vmem_limit_bytes=VMEM_LIMIT)


def _mod_kernel(c_ref, w_ref, b_ref, o_ref):
    s = _silu(c_ref[...])
    o_ref[0] = _dot(s.astype(BF16), w_ref[0].astype(BF16)) + b_ref[0]


def _modulation(cc, ada_w, ada_b):
    L, D, _ = ada_w.shape
    R = cc.shape[0]
    return pl.pallas_call(
        _mod_kernel,
        grid=(L, 3),
        in_specs=[pl.BlockSpec((R, D), lambda l, j: (0, 0)),
                  pl.BlockSpec((1, D, D), lambda l, j: (l, 0, j)),
                  pl.BlockSpec((1, 1, D), lambda l, j: (l, 0, j))],
        out_specs=pl.BlockSpec((1, R, D), lambda l, j: (l, 0, j)),
        out_shape=jax.ShapeDtypeStruct((L, R, 3 * D), F32),
        compiler_params=_params("arbitrary", "arbitrary"),
        name="modulation",
    )(cc, ada_w, ada_b.reshape(L, 1, 3 * D))


def _inproj_kernel(x_ref, mod_ref, g_ref, w_ref, o_ref, h_scr, *, n_ctx):
    T = x_ref.shape[1]

    @pl.when(pl.program_id(1) == 0)
    def _():
        g = g_ref[...]
        for r0 in range(0, T, ROW_TILE):
            k = 0 if r0 < n_ctx else 3
            y = _rms(x_ref[0, r0:r0 + ROW_TILE, :], D_MODEL) * g
            h = y * (1.0 + mod_ref[0, k + 1:k + 2, :]) + mod_ref[0, k:k + 1, :]
            h_scr[r0:r0 + ROW_TILE, :] = h.astype(BF16)

    for r0 in range(0, T, ROW_TILE):
        o_ref[0, r0:r0 + ROW_TILE, :] = _dot(h_scr[r0:r0 + ROW_TILE, :], w_ref[...]).astype(BF16)


def _in_projection(xs, mod, norm_g, w, n_ctx):
    B, T, D = xs.shape
    n_col = P_WIDTH // IN_PROJ_COLS
    return pl.pallas_call(
        functools.partial(_inproj_kernel, n_ctx=n_ctx),
        grid=(B, n_col),
        in_specs=[pl.BlockSpec((1, T, D), lambda b, j: (b, 0, 0)),
                  pl.BlockSpec((1, 8, D), lambda b, j: (b, 0, 0)),
                  pl.BlockSpec((1, D), lambda b, j: (0, 0)),
                  pl.BlockSpec((D, IN_PROJ_COLS), lambda b, j: (0, j))],
        out_specs=pl.BlockSpec((1, T, IN_PROJ_COLS), lambda b, j: (b, 0, j)),
        out_shape=jax.ShapeDtypeStruct((B, T, P_WIDTH), BF16),
        scratch_shapes=[pltpu.VMEM((T, D), BF16)],
        compiler_params=_params("arbitrary", "arbitrary"),
        name="in_projection",
    )(xs, mod, norm_g.reshape(1, D), w)


def _attend_all(attend, o_ref, T, n_ctx, with_ctx):
    if with_ctx:
        attend(pl.ds(0, n_ctx), n_ctx)
    else:
        o_ref[0, 0:n_ctx, :] = jnp.zeros((n_ctx, o_ref.shape[2]), o_ref.dtype)
    rows_per = ATT_ROWS if (T - n_ctx) % ATT_ROWS == 0 else ROW_TILE

    def latent(i, carry):
        attend(pl.ds(pl.multiple_of(n_ctx + i * rows_per, ROW_TILE), rows_per), T)
        return carry

    lax.fori_loop(0, (T - n_ctx) // rows_per, latent, 0)


def _mla_kernel(pa_ref, pc_ref, z_ref, wuq_ref, wukv_ref, cqg_ref, ckvg_ref, qg_ref, kg_ref, rope_ref,
                o_ref, q_scr, k_scr, v_scr, *, n_ctx, with_ctx):
    T = pa_ref.shape[1]
    scale = (MLA_NOPE + ROT_DIM) ** -0.5
    rot = _rot_matrix()

    def prep(i, carry):
        r = pl.multiple_of(i * ROW_TILE, ROW_TILE)
        rows = pl.ds(r, ROW_TILE)
        cos, sin = rope_ref[0, rows, :], rope_ref[1, rows, :]
        pa = pa_ref[0, rows, :].astype(F32)
        cq = (_rms(pa[:, :MLA_Q_LORA], MLA_Q_LORA) * cqg_ref[...]).astype(BF16)
        q = _dot(cq, wuq_ref[...])
        kr = _rms(pa[:, MLA_Q_LORA:], ROT_DIM) * kg_ref[1:2, :]
        kr = _rope_mxu(kr, cos, sin, rot).astype(BF16)
        ckv = (_rms(pc_ref[0, rows, :].astype(F32), MLA_KV_LORA) * ckvg_ref[...]).astype(BF16)
        kv = _dot(ckv, wukv_ref[...])
        for h in range(HEADS):
            qn = _rms(q[:, 256 * h:256 * h + 128], MLA_NOPE) * qg_ref[0:1, :]
            qr = _rms(q[:, 256 * h + 128:256 * h + 256], ROT_DIM) * qg_ref[1:2, :]
            qr = _rope_mxu(qr, cos, sin, rot)
            q_scr[rows, 256 * h:256 * h + 128] = (qn * scale).astype(BF16)
            q_scr[rows, 256 * h + 128:256 * h + 256] = (qr * scale).astype(BF16)
            kn = _rms(kv[:, 128 * h:128 * h + 128], MLA_NOPE) * kg_ref[0:1, :]
            k_scr[rows, 256 * h:256 * h + 128] = kn.astype(BF16)
            k_scr[rows, 256 * h + 128:256 * h + 256] = kr
            v_scr[rows, 256 * h:256 * h + 128] = kv[:, 512 + 128 * h:512 + 128 * h + 128].astype(BF16)
            v_scr[rows, 256 * h + 128:256 * h + 256] = _ones_col(ROW_TILE)
        return carry

    lax.fori_loop(0, T // ROW_TILE, prep, 0)

    def attend(rows, nk):
        def scores(h):
            return _dot_nt(q_scr[rows, 256 * h:256 * h + 256], k_scr[0:nk, 256 * h:256 * h + 256])

        s_next = scores(0)
        for h in range(HEADS):
            s = s_next
            if h + 1 < HEADS:
                s_next = scores(h + 1)
            o = _softmax_pv(s, v_scr[0:nk, 256 * h:256 * h + 256])
            z = z_ref[0, rows, 128 * h:128 * h + 128].astype(F32)
            o_ref[0, rows, 128 * h:128 * h + 128] = (o * _silu(z)).astype(BF16)

    _attend_all(attend, o_ref, T, n_ctx, with_ctx)


def _mla(p, wuq, wukv, cqg, ckvg, qg, kg, rope, n_ctx, last):
    B, T, _ = p.shape
    full = lambda shape: pl.BlockSpec(shape, lambda b: (0,) * len(shape))
    return pl.pallas_call(
        functools.partial(_mla_kernel, n_ctx=n_ctx, with_ctx=not last),
        grid=(B,),
        in_specs=[pl.BlockSpec((1, T, 512), lambda b: (b, 0, P_MLA_A // 512)),
                  pl.BlockSpec((1, T, 256), lambda b: (b, 0, P_MLA_CKV // 256)),
                  pl.BlockSpec((1, T, 512), lambda b: (b, 0, P_MLA_Z // 512)),
                  full(wuq.shape), full(wukv.shape), full(cqg.shape), full(ckvg.shape),
                  full(qg.shape), full(kg.shape), full(rope.shape)],
        out_specs=pl.BlockSpec((1, T, 512), lambda b: (b, 0, 0)),
        out_shape=jax.ShapeDtypeStruct((B, T, BRANCH_WIDTH), BF16),
        scratch_shapes=[pltpu.VMEM((T, 1024), BF16), pltpu.VMEM((T, 1024), BF16),
                        pltpu.VMEM((T, 1024), BF16)],
        compiler_params=_params("arbitrary"),
        name="mla",
    )(p, p, p, wuq, wukv, cqg, ckvg, qg, kg, rope)


def _diff_kernel(kv_ref, q_ref, z_ref, g_ref, lam_ref, og_ref, rope_ref, o_ref, q0_scr, q1_scr, k_scr, v_scr,
                 *, n_ctx, with_ctx, lam_init):
    T = kv_ref.shape[1]
    scale = DF_DQK ** -0.5
    lo_mask = lax.broadcasted_iota(jnp.int32, (1, 128), 1) < 64
    rot = _rot_matrix()

    def prep(i, carry):
        r = pl.multiple_of(i * ROW_TILE, ROW_TILE)
        rows = pl.ds(r, ROW_TILE)
        cos, sin = rope_ref[0, rows, :], rope_ref[1, rows, :]
        for h in range(HEADS):
            cols = slice(128 * h, 128 * h + 128)
            q = _rms_mxu(q_ref[0, rows, cols].astype(F32), DF_DQK, DF_DQK) * g_ref[0:1, :]
            q = _rope_mxu(q, cos, sin, rot) * scale
            q0_scr[rows, cols] = jnp.where(lo_mask, q, 0.0).astype(BF16)
            q1_scr[rows, cols] = jnp.where(lo_mask, 0.0, q).astype(BF16)
            k = _rms_mxu(kv_ref[0, rows, cols].astype(F32), DF_DQK, DF_DQK) * g_ref[1:2, :]
            k_scr[rows, cols] = _rope_mxu(k, cos, sin, rot).astype(BF16)
            v_scr[rows, 256 * h:256 * h + 128] = kv_ref[0, rows, 512 + 128 * h:512 + 128 * h + 128]
            v_scr[rows, 256 * h + 128:256 * h + 256] = _ones_col(ROW_TILE)
        return carry

    lax.fori_loop(0, T // ROW_TILE, prep, 0)

    lp = lam_ref[...]
    lam = (jnp.exp(jnp.sum(lp[0:1] * lp[1:2], axis=-1, keepdims=True))
           - jnp.exp(jnp.sum(lp[2:3] * lp[3:4], axis=-1, keepdims=True)) + lam_init)
    def attend(rows, nk):
        def scores(u):
            cols = slice(128 * (u // 2), 128 * (u // 2) + 128)
            return _dot_nt((q1_scr if u % 2 else q0_scr)[rows, cols], k_scr[0:nk, cols])

        s_next = scores(0)
        for h in range(HEADS):
            cols = slice(128 * h, 128 * h + 128)
            v_ext = v_scr[0:nk, 256 * h:256 * h + 256]
            parts = []
            for mp in range(2):
                s = s_next
                if 2 * h + mp + 1 < 2 * HEADS:
                    s_next = scores(2 * h + mp + 1)
                parts.append(_softmax_pv(s, v_ext))
            o = parts[0] - lam * parts[1]
            o = _rms(o, HEAD_V) * og_ref[...] * (1.0 - lam_init)
            z = z_ref[0, rows, cols].astype(F32)
            o_ref[0, rows, cols] = (o * _silu(z)).astype(BF16)

    _attend_all(attend, o_ref, T, n_ctx, with_ctx)


def _diff(p, qkg, lam_p, og, rope, n_ctx, last, lam_init):
    B, T, _ = p.shape
    full = lambda shape: pl.BlockSpec(shape, lambda b: (0,) * len(shape))
    return pl.pallas_call(
        functools.partial(_diff_kernel, n_ctx=n_ctx, with_ctx=not last, lam_init=lam_init),
        grid=(B,),
        in_specs=[pl.BlockSpec((1, T, 1024), lambda b: (b, 0, P_DF_KV // 1024)),
                  pl.BlockSpec((1, T, 512), lambda b: (b, 0, P_DF_Q // 512)),
                  pl.BlockSpec((1, T, 512), lambda b: (b, 0, P_DF_Z // 512)),
                  full(qkg.shape), full(lam_p.shape), full(og.shape), full(rope.shape)],
        out_specs=pl.BlockSpec((1, T, 512), lambda b: (b, 0, 0)),
        out_shape=jax.ShapeDtypeStruct((B, T, BRANCH_WIDTH), BF16),
        scratch_shapes=[pltpu.VMEM((T, 512), BF16), pltpu.VMEM((T, 512), BF16), pltpu.VMEM((T, 512), BF16),
                        pltpu.VMEM((T, 1024), BF16)],
        compiler_params=_params("arbitrary"),
        name="diff_attn",
    )(p, p, p, qkg, lam_p, og, rope)


def _tile_order(i, n_tiles, rev):
    if not rev:
        return i
    return jnp.where(i == 0, 0, n_tiles - i)


def _chunk_tri(n, rev, chunk=CHUNK):
    r = lax.broadcasted_iota(jnp.int32, (n, n), 0)
    c = lax.broadcasted_iota(jnp.int32, (n, n), 1)
    same = (r // chunk) == (c // chunk)
    tri = (c >= r) if rev else (c <= r)
    return jnp.where(same & tri, 1.0, 0.0).astype(F32)


def _causal(rev, n=CHUNK):
    t = lax.broadcasted_iota(jnp.int32, (n, n), 0)
    s = lax.broadcasted_iota(jnp.int32, (n, n), 1)
    return (s >= t) if rev else (s <= t)


def _cummax_rows(x, rev):
    n = x.shape[0]
    row = lax.broadcasted_iota(jnp.int32, (n, 1), 0)
    s = 1
    while s < n:
        if rev:
            x = jnp.maximum(x, jnp.where(row < n - s, pltpu.roll(x, n - s, 0), -jnp.inf))
        else:
            x = jnp.maximum(x, jnp.where(row >= s, pltpu.roll(x, s, 0), -jnp.inf))
        s *= 2
    return x


def _head_mask(width, group):
    lane = lax.broadcasted_iota(jnp.int32, (1, width), 1) // group
    return [lane == h for h in range(HEADS)]


def _gla_kernel(qk_ref, vz_ref, g_ref, aw_ref, ab_ref, og_ref, o_ref, of_scr, ob_scr, st_scr):
    T = qk_ref.shape[1]
    n_tiles = T // ROW_TILE
    n_chunk = ROW_TILE // CHUNK
    hm = _head_mask(HEADS * GLA_DK, GLA_DK)
    er = lax.broadcasted_iota(jnp.int32, (BRANCH_WIDTH, HEADS * GLA_DK), 0) // HEAD_V
    dc = lax.broadcasted_iota(jnp.int32, (BRANCH_WIDTH, HEADS * GLA_DK), 1) // GLA_DK
    block = er == dc
    tris = (_chunk_tri(ROW_TILE, False), _chunk_tri(ROW_TILE, True))
    st_scr[...] = jnp.zeros_like(st_scr)

    def load_tile(t, rev):
        rows = pl.ds(pl.multiple_of(t * ROW_TILE, ROW_TILE), ROW_TILE)
        gcol = 256 if rev else 0
        qk = qk_ref[0, rows, :].astype(F32)
        pre = _dot(g_ref[0, rows, :], aw_ref[:, gcol:gcol + 256]) + ab_ref[:, gcol:gcol + 256]
        la = _log_sigmoid(pre) * (1.0 / GLA_TAU)
        hi, lo = _split(la)
        tri = tris[rev].astype(BF16)
        return qk[:, :256] * (GLA_DK ** -0.5), qk[:, 256:], _dot(tri, hi) + _dot(tri, lo)

    def tile(i, carry):
        tiles = (i, _tile_order(i, n_tiles, True))
        steps = []
        for cc in range(n_chunk):
            steps += [(0, cc), (1, n_chunk - 1 - cc)]
        data = [load_tile(tiles[d], bool(d)) for d in (0, 1)]
        pre = {}
        for d, cc in steps:
            q, k, b = data[d]
            cr = slice(CHUNK * cc, CHUNK * cc + CHUNK)
            bc, qc, kc = b[cr], q[cr], k[cr]
            b_end = bc[0:1] if d else bc[CHUNK - 1:CHUNK]
            crow = pl.ds(pl.multiple_of(tiles[d] * ROW_TILE + CHUNK * cc, CHUNK), CHUNK)
            q_sub, k_sub = [], []
            for i in range(CHUNK // GLA_SUB):
                sr = slice(GLA_SUB * i, GLA_SUB * i + GLA_SUB)
                first = GLA_SUB * i + (GLA_SUB - 1 if d else 0)
                beta = bc[first:first + 1]
                q_i = qc[sr] * jnp.exp(bc[sr] - beta)
                q_sub.append(jnp.concatenate([jnp.where(hm[h], q_i, 0.0) for h in range(HEADS)],
                                             axis=0).astype(BF16))
                k_sub.append((kc * jnp.exp(jnp.minimum(beta - bc, EXP_CLAMP))).astype(BF16))
            pre[d, cc] = dict(
                crow=crow, vc=vz_ref[0, crow, 0:BRANCH_WIDTH], decay=jnp.exp(b_end),
                qe=(qc * jnp.exp(bc)).astype(BF16), q_sub=q_sub, k_sub=k_sub,
                k_up=(kc * jnp.exp(b_end - bc)).astype(BF16))
        a_blk = {(d, cc, i): _dot_nt(pre[d, cc]['q_sub'][i], pre[d, cc]['k_sub'][i])
                 for d, cc in steps for i in range(CHUNK // GLA_SUB)}
        a_raw = {(d, cc, h): jnp.concatenate(
            [a_blk[d, cc, i][GLA_SUB * h:GLA_SUB * h + GLA_SUB] for i in range(CHUNK // GLA_SUB)], axis=0)
            for d, cc in steps for h in range(HEADS)}
        upd = {(d, cc): _dot_tn(pre[d, cc]['vc'], pre[d, cc]['k_up']) for d, cc in steps}
        a_msk = {key: jnp.where(_causal(bool(key[0])), v, 0.0).astype(BF16) for key, v in a_raw.items()}
        intra = {(d, cc): jnp.concatenate(
            [_dot(a_msk[d, cc, h], pre[d, cc]['vc'][:, 128 * h:128 * h + 128]) for h in range(HEADS)], axis=-1)
            for d, cc in steps}
        st = [st_scr[0], st_scr[1]]
        for d, cc in steps:
            o = _dot_nt(pre[d, cc]['qe'], st[d].astype(BF16))
            (ob_scr if d else of_scr)[pre[d, cc]['crow'], :] = o + intra[d, cc]
            st[d] = st[d] * pre[d, cc]['decay'] + jnp.where(block, upd[d, cc], 0.0)
        st_scr[0] = st[0]
        st_scr[1] = st[1]
        return carry

    lax.fori_loop(0, n_tiles, tile, 0)

    def finish(i, carry):
        rows = pl.ds(pl.multiple_of(i * ROW_TILE, ROW_TILE), ROW_TILE)
        tot = of_scr[rows, :] + ob_scr[rows, :]
        z = vz_ref[0, rows, BRANCH_WIDTH:].astype(F32)
        for h in range(HEADS):
            cols = slice(128 * h, 128 * h + 128)
            y = _rms(tot[:, cols], HEAD_V) * og_ref[:, cols]
            o_ref[0, rows, cols] = (y * _silu(z[:, cols])).astype(BF16)
        return carry

    lax.fori_loop(0, n_tiles, finish, 0)


def _gla(p, aw, ab, og):
    B, T, _ = p.shape
    full = lambda shape: pl.BlockSpec(shape, lambda b: (0,) * len(shape))
    return pl.pallas_call(
        _gla_kernel,
        grid=(B,),
        in_specs=[pl.BlockSpec((1, T, 512), lambda b: (b, 0, P_GLA_QK // 512)),
                  pl.BlockSpec((1, T, 1024), lambda b: (b, 0, P_GLA_VZ // 1024)),
                  pl.BlockSpec((1, T, 128), lambda b: (b, 0, P_GLA_G // 128)),
                  full(aw.shape), full(ab.shape), full(og.shape)],
        out_specs=pl.BlockSpec((1, T, 512), lambda b: (b, 0, 0)),
        out_shape=jax.ShapeDtypeStruct((B, T, BRANCH_WIDTH), BF16),
        scratch_shapes=[pltpu.VMEM((T, BRANCH_WIDTH), F32), pltpu.VMEM((T, BRANCH_WIDTH), F32),
                        pltpu.VMEM((2, BRANCH_WIDTH, HEADS * GLA_DK), F32)],
        compiler_params=_params("arbitrary"),
        name="gla_scan",
    )(p, p, p, aw, ab, og)


def _mlstm_kernel(m_ref, if_ref, cw_ref, cb_ref, wq_ref, wkt_ref, gb_ref, og_ref, sk_ref, o_ref,
                  xc_scr, q_scr, kt_scr, at_scr, bt_scr, cm_scr, b_scr, hf_scr, hb_scr, c_scr, *, n_ctx):
    T = m_ref.shape[1]
    n_tiles = T // ROW_TILE
    assert ML_CHUNK == ROW_TILE
    n_chunk = 1
    ctx_tiles = n_ctx // ROW_TILE
    hm = _head_mask(HEADS * ML_DQK, ML_DQK)
    lane128 = lax.broadcasted_iota(jnp.int32, (1, 128), 1)
    is_forget = ((lane128 // HEADS) % 2) == 1
    row_in_tile = lax.broadcasted_iota(jnp.int32, (ROW_TILE, 1), 0)
    tri_f = _chunk_tri(ROW_TILE, False, ML_CHUNK).astype(BF16)
    tri_b = _chunk_tri(ROW_TILE, True, ML_CHUNK).astype(BF16)

    def prep(i, carry):
        r = pl.multiple_of(i * ROW_TILE, ROW_TILE)
        rows = pl.ds(r, ROW_TILE)
        x = m_ref[0, rows, 0:512].astype(F32)
        rp = pl.multiple_of(jnp.maximum(r - 16, 0), 16)
        rn = pl.multiple_of(jnp.minimum(r + ROW_TILE, T - 16), 16)
        prev_row = m_ref[0, pl.ds(rp, 16), 0:512].astype(F32)[15:16]
        next_row = m_ref[0, pl.ds(rn, 16), 0:512].astype(F32)[0:1]
        seg_start = (i == 0) | (i == ctx_tiles)
        seg_end = (i == ctx_tiles - 1) | (i == n_tiles - 1)
        prev_row = jnp.where(seg_start, 0.0, prev_row)
        next_row = jnp.where(seg_end, 0.0, next_row)
        xm = jnp.where(row_in_tile == 0, prev_row, pltpu.roll(x, 1, 0))
        xp = jnp.where(row_in_tile == ROW_TILE - 1, next_row, pltpu.roll(x, ROW_TILE - 1, 0))
        xc = _silu(cw_ref[0:1, :] * xm + cw_ref[1:2, :] * x + cw_ref[2:3, :] * xp + cb_ref[...])
        xc_scr[rows, :] = xc
        xb = xc.astype(BF16)
        q_scr[rows, :] = _dot(xb, wq_ref[...]).astype(BF16)
        g = if_ref[0, rows, :].astype(F32) + gb_ref[...]
        g2 = jnp.where(is_forget, _log_sigmoid(g), g)
        g_hi, g_lo = _split(g2)
        cs = jnp.where(lane128 < 2 * HEADS, _dot(tri_f, g_hi) + _dot(tri_f, g_lo),
                       _dot(tri_b, g_hi) + _dot(tri_b, g_lo))
        b = pltpu.roll(cs, 128 - HEADS, 1)
        a = g2 - b
        cm_scr[rows, :] = jnp.where(lane128 < 2 * HEADS, _cummax_rows(a, False), _cummax_rows(a, True))
        b_scr[rows, :] = b
        kt_scr[i] = (_dot_nt(wkt_ref[...], xb) * (ML_DQK ** -0.5)).astype(BF16)
        at_scr[i] = a.T[0:16, :]
        bt_scr[i] = b.T[0:16, :]
        return carry

    lax.fori_loop(0, n_tiles, prep, 0)

    c_scr[...] = jnp.zeros_like(c_scr)
    ones_col = _ones_col(ML_CHUNK)
    sel_r = lax.broadcasted_iota(jnp.int32, (256, HEADS * 128), 0) % 128
    sel_c = lax.broadcasted_iota(jnp.int32, (256, HEADS * 128), 1) // 128
    sel = [jnp.where(sel_r == 2 * HEADS * d + sel_c, 1.0, 0.0).astype(BF16) for d in (0, 1)]

    def col_bcast(x, d):
        hi, lo = _split(x)
        return _dot(jnp.concatenate([hi, lo], axis=-1), sel[d])

    def chunk_pair(c_f, c_b, m_in):
        units = [(d, h) for d in (0, 1) for h in range(HEADS)]
        cs = (c_f, c_b)
        crow = [pl.ds(pl.multiple_of(c * ML_CHUNK, ML_CHUNK), ML_CHUNK) for c in cs]
        qc = [q_scr[crow[d], :] for d in (0, 1)]
        kt = [kt_scr[cs[d]] for d in (0, 1)]
        vc = [m_ref[0, crow[d], 512:1024] for d in (0, 1)]
        at = [at_scr[cs[d]] for d in (0, 1)]
        bt = [bt_scr[cs[d]] for d in (0, 1)]
        c_bf = [c_scr[d].astype(BF16) for d in (0, 1)]
        causal = [_causal(False, ML_CHUNK), _causal(True, ML_CHUNK)]
        last = [ML_CHUNK - 1, 0]
        qh = {(d, h): jnp.where(hm[h], qc[d], jnp.zeros_like(qc[d])) for d, h in units}
        v_ext = {(d, h): jnp.concatenate([vc[d][:, 128 * h:128 * h + 128], ones_col], axis=-1)
                 for d, h in units}
        s_raw = {u: _dot(qh[u], kt[u[0]]) for u in units}
        q_c = {u: _dot(qh[u], c_bf[u[0]]) for u in units}
        cm_col = [col_bcast(cm_scr[crow[d], :], d) for d in (0, 1)]
        b_col = [col_bcast(b_scr[crow[d], :], d) for d in (0, 1)]
        s_w, m_run, m_last, ktw = {}, {}, {}, {}
        for d, h in units:
            j = 2 * HEADS * d + h
            a_row, m_old = at[d][j:j + 1, :], m_in[HEADS * d + h]
            pm = jnp.where(causal[d], a_row, -jnp.inf)
            m_run[d, h] = jnp.maximum(m_old, cm_col[d][:, 128 * h:128 * h + 128])
            m_wide = jnp.concatenate([m_run[d, h]] * (ML_CHUNK // 128), axis=-1)
            s_w[d, h] = (s_raw[d, h] * jnp.exp(pm - m_wide)).astype(BF16)
            m_last[d, h] = m_run[d, h][last[d]:last[d] + 1, 0:1]
            k_h = kt[d][ML_DQK * h:ML_DQK * h + ML_DQK, :].astype(F32)
            ktw[d, h] = (k_h * jnp.exp(a_row - m_last[d, h])).astype(BF16)
        s_v = {u: _dot(s_w[u], v_ext[u]) for u in units}
        upd = {u: _dot(ktw[u], v_ext[u]) for u in units}
        m_out = []
        for d in (0, 1):
            parts = []
            for h in range(HEADS):
                j = 2 * HEADS * d + h
                hr = slice(ML_DQK * h, ML_DQK * h + ML_DQK)
                b_row, m_old = bt[d][j:j + 1, :], m_in[HEADS * d + h]
                carry_w = jnp.exp(m_old - m_run[d, h])
                num = jnp.concatenate([carry_w, carry_w], axis=-1) * q_c[d, h] + s_v[d, h]
                den = num[:, HEAD_V:HEAD_V + 1]
                floor = jnp.exp(-(b_col[d][:, 128 * h:128 * h + 128] + m_run[d, h]))
                parts.append(num[:, 0:HEAD_V] / jnp.maximum(jnp.abs(den), floor))
                c_scr[d, hr, :] = jnp.exp(m_old - m_last[d, h]) * c_scr[d, hr, :] + upd[d, h]
                m_out.append(b_row[:, last[d]:last[d] + 1] + m_last[d, h])
            (hb_scr if d else hf_scr)[crow[d], :] = jnp.concatenate(parts, axis=-1)
        return m_out

    def tile(i, carry):
        m = list(carry)
        c_f = i * n_chunk
        c_b = _tile_order(i, n_tiles, True) * n_chunk + (n_chunk - 1)
        for cc in range(n_chunk):
            m = chunk_pair(c_f + cc, c_b - cc, m)
        return tuple(m)

    lax.fori_loop(0, n_tiles, tile, tuple(jnp.zeros((1, 1), F32) for _ in range(2 * HEADS)))

    def finish(i, carry):
        rows = pl.ds(pl.multiple_of(i * ROW_TILE, ROW_TILE), ROW_TILE)
        tot = (hf_scr[rows, :] + hb_scr[rows, :]) * _sigmoid(m_ref[0, rows, 1024:1536].astype(F32))
        z = m_ref[0, rows, 1536:2048].astype(F32)
        xc = xc_scr[rows, :]
        for h in range(HEADS):
            cols = slice(128 * h, 128 * h + 128)
            y = _rms(tot[:, cols], HEAD_V) * og_ref[:, cols]
            y = (y + sk_ref[:, cols] * xc[:, cols]) * _silu(z[:, cols])
            o_ref[0, rows, cols] = y.astype(BF16)
        return carry

    lax.fori_loop(0, n_tiles, finish, 0)


def _mlstm(p, cw, cb, wq, wkt, gb, og, sk, n_ctx):
    B, T, _ = p.shape
    full = lambda shape: pl.BlockSpec(shape, lambda b: (0,) * len(shape))
    return pl.pallas_call(
        functools.partial(_mlstm_kernel, n_ctx=n_ctx),
        grid=(B,),
        in_specs=[pl.BlockSpec((1, T, 2048), lambda b: (b, 0, P_ML // 2048)),
                  pl.BlockSpec((1, T, 128), lambda b: (b, 0, P_ML_IF // 128)),
                  full(cw.shape), full(cb.shape), full(wq.shape), full(wkt.shape), full(gb.shape),
                  full(og.shape), full(sk.shape)],
        out_specs=pl.BlockSpec((1, T, 512), lambda b: (b, 0, 0)),
        out_shape=jax.ShapeDtypeStruct((B, T, BRANCH_WIDTH), BF16),
        scratch_shapes=[pltpu.VMEM((T, BRANCH_WIDTH), F32),
                        pltpu.VMEM((T, HEADS * ML_DQK), BF16),
                        pltpu.VMEM((T // ML_CHUNK, HEADS * ML_DQK, ML_CHUNK), BF16),
                        pltpu.VMEM((T // ML_CHUNK, 16, ML_CHUNK), F32),
                        pltpu.VMEM((T // ML_CHUNK, 16, ML_CHUNK), F32),
                        pltpu.VMEM((T, 128), F32),
                        pltpu.VMEM((T, 128), F32),
                        pltpu.VMEM((T, BRANCH_WIDTH), F32),
                        pltpu.VMEM((T, BRANCH_WIDTH), F32),
                        pltpu.VMEM((2, HEADS * ML_DQK, 2 * HEAD_V), F32)],
        compiler_params=_params("arbitrary"),
        name="mlstm_scan",
    )(p, p, cw, cb, wq, wkt, gb, og, sk)


def _merge_kernel(ya_ref, yb_ref, yc_ref, yd_ref, gl_ref, brw_ref, wo_ref, x_ref, gate_ref, o_ref,
                  *, row0, ctx_rows):
    tm = x_ref.shape[1]
    first = pl.program_id(1) * tm + row0
    acc = None
    for i, y_ref in enumerate((ya_ref, yb_ref, yc_ref, yd_ref)):
        u = _dot(y_ref[0], brw_ref[i])
        gsig = _sigmoid(gl_ref[0, :, D_MODEL * i:D_MODEL * (i + 1)].astype(F32))
        acc = gsig * u if acc is None else acc + gsig * u
    out = _dot(acc.astype(BF16), wo_ref[...])
    for r0 in range(0, tm, ROW_TILE):
        rs = slice(r0, r0 + ROW_TILE)
        gate = jnp.where(first + r0 < ctx_rows, gate_ref[0, 0:1, :], gate_ref[0, 1:2, :])
        o_ref[0, rs, :] = x_ref[0, rs, :] + gate * out[rs]


def _merge(ys, p, brw, wo, xs, gates, n_ctx, last):
    B, T, D = xs.shape
    tm = ROW_TILE if last else MERGE_ROWS
    row0 = n_ctx if last else 0
    tile0 = row0 // tm
    nt = (T - row0) // tm
    assert (T - row0) % tm == 0 and row0 % tm == 0 and n_ctx % ROW_TILE == 0
    ymap = lambda b, t: (b, t + tile0, 0)
    return pl.pallas_call(
        functools.partial(_merge_kernel, row0=row0, ctx_rows=n_ctx),
        grid=(B, nt),
        in_specs=[pl.BlockSpec((1, tm, BRANCH_WIDTH), ymap)] * 4 + [
            pl.BlockSpec((1, tm, N_BRANCH * D), lambda b, t: (b, t + tile0, P_MERGE // (N_BRANCH * D))),
            pl.BlockSpec(brw.shape, lambda b, t: (0, 0, 0)),
            pl.BlockSpec(wo.shape, lambda b, t: (0, 0)),
            pl.BlockSpec((1, tm, D), ymap),
            pl.BlockSpec((1, 2, D), lambda b, t: (b, 0, 0))],
        out_specs=pl.BlockSpec((1, tm, D), lambda b, t: (b, t, 0)),
        out_shape=jax.ShapeDtypeStruct((B, nt * tm, D), F32),
        compiler_params=_params("arbitrary", "arbitrary"),
        name="merge",
    )(*ys, p, brw, wo, xs, gates)


def _layout_w_in(w_in):
    offs, off = {}, 0
    for name, w in IN_SPLITS:
        offs[name] = (off, w)
        off += w

    def col(name):
        o, w = offs[name]
        return w_in[..., o:o + w]

    def zeros(n):
        return jnp.zeros(w_in.shape[:-1] + (n,), w_in.dtype)

    parts = [col('merge'), col('df_k'), col('df_v'), col('df_q'), col('df_z'),
             col('ml_x'), col('ml_v'), col('ml_o'), col('ml_z'),
             col('gla_v'), col('gla_z'), col('mla_z'), col('gla_q'), col('gla_k'),
             col('mla_cq'), col('mla_kr'), zeros(64),
             col('mla_ckv'), col('gla_af'), col('gla_ab'), zeros(96), col('ml_if'), zeros(112)]
    out = jnp.concatenate(parts, axis=-1).astype(BF16)
    assert out.shape[-1] == P_WIDTH
    return out


def _rope_tables(rows, n_ctx):
    quarter = ROT_DIM // 4
    inv_freq = ROPE_BASE ** (-jnp.arange(quarter, dtype=F32) / quarter)
    row = jnp.repeat(jnp.arange(rows, dtype=F32), GRID_W)
    col = jnp.tile(jnp.arange(GRID_W, dtype=F32), rows)
    ar = row[:, None] * inv_freq
    ac = col[:, None] * inv_freq
    ang = jnp.concatenate([ar, ar, ac, ac], axis=-1)
    cos = jnp.concatenate([jnp.ones((n_ctx, ROT_DIM), F32), jnp.cos(ang)], axis=0)
    sin = jnp.concatenate([jnp.zeros((n_ctx, ROT_DIM), F32), jnp.sin(ang)], axis=0)
    zero = jnp.zeros_like(cos)
    both = jnp.stack([jnp.tile(cos, (1, 2)), jnp.tile(sin, (1, 2))])
    half = jnp.stack([jnp.concatenate([cos, zero], -1), jnp.concatenate([sin, zero], -1)])
    return half, both


def _pad_lanes(v, n):
    return jnp.concatenate([v, jnp.zeros(v.shape[:-1] + (n - v.shape[-1],), v.dtype)], axis=-1)


def kernel(x, c, ctx, c_ctx, ada_w, ada_b, norm_g, w_in, mla_cq_g, mla_ckv_g, mla_wuq, mla_wukv, mla_q_g,
           mla_k_g, gla_a_w, gla_a_b, gla_out_g, ml_conv_w, ml_conv_b, ml_wq, ml_wk, ml_gate_b, ml_out_g,
           ml_skip, df_qk_g, df_lambda, df_out_g, br_w, w_out):
    B, S, D = x.shape
    n_ctx = ctx.shape[1]
    L = ada_w.shape[0]
    assert D == D_MODEL and n_ctx == ROW_TILE and S % ROW_TILE == 0 and S % GRID_W == 0

    rope_half, rope_both = _rope_tables(S // GRID_W, n_ctx)
    w_in_p = _layout_w_in(w_in)

    n_rows = -(-(B + 1) // 8) * 8
    cc = jnp.concatenate([c, c_ctx[None], jnp.zeros((n_rows - B - 1, D), F32)], axis=0)
    mod_all = _modulation(cc, ada_w, ada_b)

    xs = jnp.concatenate([ctx, x], axis=1)
    for l in range(L):
        last = l == L - 1
        lam_init = 0.8 - 0.6 * math.exp(-0.3 * l)
        m3 = mod_all[l].reshape(n_rows, 3, D)
        lat, cx = m3[:B], jnp.broadcast_to(m3[B][None], (B, 3, D))
        mod = jnp.concatenate([cx, lat, jnp.zeros((B, 2, D), F32)], axis=1)
        gates = jnp.stack([cx[:, 2], lat[:, 2]], axis=1)

        p = _in_projection(xs, mod, norm_g[l], w_in_p[l], n_ctx)

        wq4 = mla_wuq[l].reshape(MLA_Q_LORA, HEADS, MLA_NOPE + ROT_DIM)
        wuq = _pad_lanes(wq4, 256).reshape(MLA_Q_LORA, HEADS * 256).astype(BF16)
        wkv4 = mla_wukv[l].reshape(MLA_KV_LORA, HEADS, MLA_NOPE + HEAD_V)
        wukv = jnp.concatenate([wkv4[..., :MLA_NOPE].reshape(MLA_KV_LORA, -1),
                                wkv4[..., MLA_NOPE:].reshape(MLA_KV_LORA, -1)], axis=-1).astype(BF16)
        qg = jnp.stack([mla_q_g[l, :MLA_NOPE], _pad_lanes(mla_q_g[l, MLA_NOPE:], 128)])
        kg = jnp.stack([mla_k_g[l, :MLA_NOPE], _pad_lanes(mla_k_g[l, MLA_NOPE:], 128)])
        y_mla = _mla(p, wuq, wukv, mla_cq_g[l][None], mla_ckv_g[l][None], qg, kg, rope_half, n_ctx, last)

        qkg = jnp.tile(df_qk_g[l], (1, 2))
        y_df = _diff(p, qkg, df_lambda[l], df_out_g[l][None], rope_both, n_ctx, last, lam_init)

        aw = jnp.zeros((128, 512), F32)
        aw = aw.at[0:16, 0:256].set(gla_a_w[l, 0]).at[16:32, 256:512].set(gla_a_w[l, 1]).astype(BF16)
        y_gla = _gla(p, aw, gla_a_b[l].reshape(1, 512), gla_out_g[l][None])

        wq_bd = jnp.zeros((BRANCH_WIDTH, HEADS * ML_DQK), F32)
        wk_bd = jnp.zeros((BRANCH_WIDTH, HEADS * ML_DQK), F32)
        for h in range(HEADS):
            wq_bd = wq_bd.at[128 * h:128 * h + 128, 64 * h:64 * h + 64].set(ml_wq[l, h])
            wk_bd = wk_bd.at[128 * h:128 * h + 128, 64 * h:64 * h + 64].set(ml_wk[l, h])
        gb = _pad_lanes(ml_gate_b[l].reshape(1, 16), 128)
        y_ml = _mlstm(p, ml_conv_w[l], ml_conv_b[l][None], wq_bd.astype(BF16), wk_bd.T.astype(BF16), gb,
                      ml_out_g[l][None], ml_skip[l][None], n_ctx)

        xs = _merge((y_mla, y_gla, y_ml, y_df), p, br_w[l].astype(BF16), w_out[l].astype(BF16), xs, gates,
                    n_ctx, last)
    return xs
```

```python
import functools
import math

import jax
import jax.numpy as jnp
from jax import lax
from jax.experimental import pallas as pl
from jax.experimental.pallas import tpu as pltpu

F32 = jnp.float32
BF16 = jnp.bfloat16

D_MODEL = 1024
GRID_W = 64
EPS = 1e-6
ROPE_BASE = 10000.0
ROT_DIM = 64
CHUNK = 64
N_BRANCH = 4
BRANCH_WIDTH = 512
HEADS = 4
HEAD_V = BRANCH_WIDTH // HEADS
MLA_NOPE = 128
MLA_Q_LORA = 384
MLA_KV_LORA = 256
GLA_DK = 64
GLA_GATE_RANK = 16
GLA_TAU = 16.0
ML_DQK = 64
DF_DQK = 64

IN_SPLITS = (
    ('mla_cq', 384), ('mla_ckv', 256), ('mla_kr', 64), ('mla_z', 512),
    ('gla_q', 256), ('gla_k', 256), ('gla_v', 512), ('gla_af', 16), ('gla_ab', 16), ('gla_z', 512),
    ('ml_x', 512), ('ml_v', 512), ('ml_o', 512), ('ml_if', 16), ('ml_z', 512),
    ('df_q', 512), ('df_k', 512), ('df_v', 512), ('df_z', 512),
    ('merge', 4096),
)

P_MERGE = 0
P_DF_KV = 4096
P_DF_Q = 5120
P_DF_Z = 5632
P_ML = 6144
P_GLA_VZ = 8192
P_MLA_Z = 9216
P_GLA_QK = 9728
P_MLA_A = 10240
P_MLA_CKV = 10752
P_GLA_G = 11008
P_ML_IF = 11136
P_WIDTH = 11264

ROW_TILE = 256
IN_PROJ_COLS = 1024
MERGE_ROWS = 768
ML_CHUNK = 256
ATT_ROWS = 512
VMEM_LIMIT = 56 * 1024 * 1024
EXP_CLAMP = 80.0
GLA_SUB = 16


def _dot(a, b):
    return jnp.dot(a, b, preferred_element_type=F32)


def _dot_nt(a, b):
    return lax.dot_general(a, b, (((1,), (1,)), ((), ())), preferred_element_type=F32)


def _dot_tn(a, b):
    return lax.dot_general(a, b, (((0,), (0,)), ((), ())), preferred_element_type=F32)


def _sigmoid(x):
    return 1.0 / (1.0 + jnp.exp(-x))


def _silu(x):
    return x * _sigmoid(x)


def _log_sigmoid(x):
    return jnp.minimum(x, 0.0) - jnp.log(1.0 + jnp.exp(-jnp.abs(x)))


def _rms(x, n):
    return x * lax.rsqrt(jnp.sum(x * x, axis=-1, keepdims=True) * (1.0 / n) + EPS)


def _split(x):
    hi = x.astype(BF16)
    return hi, (x - hi.astype(F32)).astype(BF16)


def _group_sum(x, group):
    k = x.shape[-1]
    row = lax.broadcasted_iota(jnp.int32, (k, 128), 0)
    col = lax.broadcasted_iota(jnp.int32, (k, 128), 1)
    sel = jnp.ones((k, 128), BF16) if group is None else jnp.where(row // group == col // group, 1.0, 0.0).astype(BF16)
    hi, lo = _split(x)
    return _dot(hi, sel) + _dot(lo, sel)


def _rms_mxu(x, n, group=None):
    inv = lax.rsqrt(_group_sum(x * x, group) * (1.0 / n) + EPS)
    return x * (inv if x.shape[-1] == 128 else jnp.tile(inv, (1, x.shape[-1] // 128)))


def _rot_matrix():
    src = lax.broadcasted_iota(jnp.int32, (128, 128), 0)
    dst = lax.broadcasted_iota(jnp.int32, (128, 128), 1)
    even = (dst // 16) % 2 == 0
    return jnp.where(even & (src == dst + 16), -1.0, jnp.where(~even & (src == dst - 16), 1.0, 0.0)).astype(BF16)


def _rope_mxu(x, cos, sin, rot):
    hi, lo = _split(x)
    return x * cos + (_dot(hi, rot) + _dot(lo, rot)) * sin


def _ones_col(rows):
    return jnp.where(lax.broadcasted_iota(jnp.int32, (rows, HEAD_V), 1) == 0, 1.0, 0.0).astype(BF16)


def _softmax_pv(s, v_ext):
    e = jnp.exp((s - jnp.max(s, axis=-1, keepdims=True)).astype(BF16))
    o = _dot(e, v_ext)
    return o[:, 0:HEAD_V] / o[:, HEAD_V:HEAD_V + 1]


def _params(*sem):
    return pltpu.CompilerParams(dimension_semantics=sem, vmem_limit_bytes=VMEM_LIMIT)


def _mod_kernel(c_ref, w_ref, b_ref, o_ref):
    s = _silu(c_ref[...])
    o_ref[0] = _dot(s.astype(BF16), w_ref[0].astype(BF16)) + b_ref[0]


def _modulation(cc, ada_w, ada_b):
    L, D, _ = ada_w.shape
    R = cc.shape[0]
    return pl.pallas_call(
        _mod_kernel,
        grid=(L, 3),
        in_specs=[pl.BlockSpec((R, D), lambda l, j: (0, 0)),
                  pl.BlockSpec((1, D, D), lambda l, j: (l, 0, j)),
                  pl.BlockSpec((1, 1, D), lambda l, j: (l, 0, j))],
        out_specs=pl.BlockSpec((1, R, D), lambda l, j: (l, 0, j)),
        out_shape=jax.ShapeDtypeStruct((L, R, 3 * D), F32),
        compiler_params=_params("arbitrary", "arbitrary"),
        name="modulation",
    )(cc, ada_w, ada_b.reshape(L, 1, 3 * D))


def _inproj_kernel(x_ref, mod_ref, g_ref, w_ref, o_ref, h_scr, *, n_ctx):
    T = x_ref.shape[1]

    @pl.when(pl.program_id(1) == 0)
    def _():
        g = g_ref[...]
        for r0 in range(0, T, ROW_TILE):
            k = 0 if r0 < n_ctx else 3
            y = _rms(x_ref[0, r0:r0 + ROW_TILE, :], D_MODEL) * g
            h = y * (1.0 + mod_ref[0, k + 1:k + 2, :]) + mod_ref[0, k:k + 1, :]
            h_scr[r0:r0 + ROW_TILE, :] = h.astype(BF16)

    for r0 in range(0, T, ROW_TILE):
        o_ref[0, r0:r0 + ROW_TILE, :] = _dot(h_scr[r0:r0 + ROW_TILE, :], w_ref[...]).astype(BF16)


def _in_projection(xs, mod, norm_g, w, n_ctx):
    B, T, D = xs.shape
    n_col = P_WIDTH // IN_PROJ_COLS
    return pl.pallas_call(
        functools.partial(_inproj_kernel, n_ctx=n_ctx),
        grid=(B, n_col),
        in_specs=[pl.BlockSpec((1, T, D), lambda b, j: (b, 0, 0)),
                  pl.BlockSpec((1, 8, D), lambda b, j: (b, 0, 0)),
                  pl.BlockSpec((1, D), lambda b, j: (0, 0)),
                  pl.BlockSpec((D, IN_PROJ_COLS), lambda b, j: (0, j))],
        out_specs=pl.BlockSpec((1, T, IN_PROJ_COLS), lambda b, j: (b, 0, j)),
        out_shape=jax.ShapeDtypeStruct((B, T, P_WIDTH), BF16),
        scratch_shapes=[pltpu.VMEM((T, D), BF16)],
        compiler_params=_params("arbitrary", "arbitrary"),
        name="in_projection",
    )(xs, mod, norm_g.reshape(1, D), w)


def _attend_all(attend, o_ref, T, n_ctx, with_ctx):
    if with_ctx:
        attend(pl.ds(0, n_ctx), n_ctx)
    else:
        o_ref[0, 0:n_ctx, :] = jnp.zeros((n_ctx, o_ref.shape[2]), o_ref.dtype)
    rows_per = ATT_ROWS if (T - n_ctx) % ATT_ROWS == 0 else ROW_TILE

    def latent(i, carry):
        attend(pl.ds(pl.multiple_of(n_ctx + i * rows_per, ROW_TILE), rows_per), T)
        return carry

    lax.fori_loop(0, (T - n_ctx) // rows_per, latent, 0)


def _mla_kernel(pa_ref, pc_ref, z_ref, wuq_ref, wukv_ref, cqg_ref, ckvg_ref, qg_ref, kg_ref, rope_ref,
                o_ref, q_scr, k_scr, v_scr, *, n_ctx, with_ctx):
    T = pa_ref.shape[1]
    scale = (MLA_NOPE + ROT_DIM) ** -0.5
    rot = _rot_matrix()

    def prep(i, carry):
        r = pl.multiple_of(i * ROW_TILE, ROW_TILE)
        rows = pl.ds(r, ROW_TILE)
        cos, sin = rope_ref[0, rows, :], rope_ref[1, rows, :]
        pa = pa_ref[0, rows, :].astype(F32)
        cq = (_rms(pa[:, :MLA_Q_LORA], MLA_Q_LORA) * cqg_ref[...]).astype(BF16)
        q = _dot(cq, wuq_ref[...])
        kr = _rms(pa[:, MLA_Q_LORA:], ROT_DIM) * kg_ref[1:2, :]
        kr = _rope_mxu(kr, cos, sin, rot).astype(BF16)
        ckv = (_rms(pc_ref[0, rows, :].astype(F32), MLA_KV_LORA) * ckvg_ref[...]).astype(BF16)
        kv = _dot(ckv, wukv_ref[...])
        for h in range(HEADS):
            qn = _rms(q[:, 256 * h:256 * h + 128], MLA_NOPE) * qg_ref[0:1, :]
            qr = _rms(q[:, 256 * h + 128:256 * h + 256], ROT_DIM) * qg_ref[1:2, :]
            qr = _rope_mxu(qr, cos, sin, rot)
            q_scr[rows, 256 * h:256 * h + 128] = (qn * scale).astype(BF16)
            q_scr[rows, 256 * h + 128:256 * h + 256] = (qr * scale).astype(BF16)
            kn = _rms(kv[:, 128 * h:128 * h + 128], MLA_NOPE) * kg_ref[0:1, :]
            k_scr[rows, 256 * h:256 * h + 128] = kn.astype(BF16)
            k_scr[rows, 256 * h + 128:256 * h + 256] = kr
            v_scr[rows, 256 * h:256 * h + 128] = kv[:, 512 + 128 * h:512 + 128 * h + 128].astype(BF16)
            v_scr[rows, 256 * h + 128:256 * h + 256] = _ones_col(ROW_TILE)
        return carry

    lax.fori_loop(0, T // ROW_TILE, prep, 0)

    def attend(rows, nk):
        def scores(h):
            return _dot_nt(q_scr[rows, 256 * h:256 * h + 256], k_scr[0:nk, 256 * h:256 * h + 256])

        s_next = scores(0)
        for h in range(HEADS):
            s = s_next
            if h + 1 < HEADS:
                s_next = scores(h + 1)
            o = _softmax_pv(s, v_scr[0:nk, 256 * h:256 * h + 256])
            z = z_ref[0, rows, 128 * h:128 * h + 128].astype(F32)
            o_ref[0, rows, 128 * h:128 * h + 128] = (o * _silu(z)).astype(BF16)

    _attend_all(attend, o_ref, T, n_ctx, with_ctx)


def _mla(p, wuq, wukv, cqg, ckvg, qg, kg, rope, n_ctx, last):
    B, T, _ = p.shape
    full = lambda shape: pl.BlockSpec(shape, lambda b: (0,) * len(shape))
    return pl.pallas_call(
        functools.partial(_mla_kernel, n_ctx=n_ctx, with_ctx=not last),
        grid=(B,),
        in_specs=[pl.BlockSpec((1, T, 512), lambda b: (b, 0, P_MLA_A // 512)),
                  pl.BlockSpec((1, T, 256), lambda b: (b, 0, P_MLA_CKV // 256)),
                  pl.BlockSpec((1, T, 512), lambda b: (b, 0, P_MLA_Z // 512)),
                  full(wuq.shape), full(wukv.shape), full(cqg.shape), full(ckvg.shape),
                  full(qg.shape), full(kg.shape), full(rope.shape)],
        out_specs=pl.BlockSpec((1, T, 512), lambda b: (b, 0, 0)),
        out_shape=jax.ShapeDtypeStruct((B, T, BRANCH_WIDTH), BF16),
        scratch_shapes=[pltpu.VMEM((T, 1024), BF16), pltpu.VMEM((T, 1024), BF16),
                        pltpu.VMEM((T, 1024), BF16)],
        compiler_params=_params("arbitrary"),
        name="mla",
    )(p, p, p, wuq, wukv, cqg, ckvg, qg, kg, rope)


def _diff_kernel(kv_ref, q_ref, z_ref, g_ref, lam_ref, og_ref, rope_ref, o_ref, q0_scr, q1_scr, k_scr, v_scr,
                 *, n_ctx, with_ctx, lam_init):
    T = kv_ref.shape[1]
    scale = DF_DQK ** -0.5
    lo_mask = lax.broadcasted_iota(jnp.int32, (1, 128), 1) < 64
    rot = _rot_matrix()

    def prep(i, carry):
        r = pl.multiple_of(i * ROW_TILE, ROW_TILE)
        rows = pl.ds(r, ROW_TILE)
        cos, sin = rope_ref[0, rows, :], rope_ref[1, rows, :]
        for h in range(HEADS):
            cols = slice(128 * h, 128 * h + 128)
            q = _rms_mxu(q_ref[0, rows, cols].astype(F32), DF_DQK, DF_DQK) * g_ref[0:1, :]
            q = _rope_mxu(q, cos, sin, rot) * scale
            q0_scr[rows, cols] = jnp.where(lo_mask, q, 0.0).astype(BF16)
            q1_scr[rows, cols] = jnp.where(lo_mask, 0.0, q).astype(BF16)
            k = _rms_mxu(kv_ref[0, rows, cols].astype(F32), DF_DQK, DF_DQK) * g_ref[1:2, :]
            k_scr[rows, cols] = _rope_mxu(k, cos, sin, rot).astype(BF16)
            v_scr[rows, 256 * h:256 * h + 128] = kv_ref[0, rows, 512 + 128 * h:512 + 128 * h + 128]
            v_scr[rows, 256 * h + 128:256 * h + 256] = _ones_col(ROW_TILE)
        return carry

    lax.fori_loop(0, T // ROW_TILE, prep, 0)

    lp = lam_ref[...]
    lam = (jnp.exp(jnp.sum(lp[0:1] * lp[1:2], axis=-1, keepdims=True))
           - jnp.exp(jnp.sum(lp[2:3] * lp[3:4], axis=-1, keepdims=True)) + lam_init)
    def attend(rows, nk):
        def scores(u):
            cols = slice(128 * (u // 2), 128 * (u // 2) + 128)
            return _dot_nt((q1_scr if u % 2 else q0_scr)[rows, cols], k_scr[0:nk, cols])

        s_next = scores(0)
        for h in range(HEADS):
            cols = slice(128 * h, 128 * h + 128)
            v_ext = v_scr[0:nk, 256 * h:256 * h + 256]
            parts = []
            for mp in range(2):
                s = s_next
                if 2 * h + mp + 1 < 2 * HEADS:
                    s_next = scores(2 * h + mp + 1)
                parts.append(_softmax_pv(s, v_ext))
            o = parts[0] - lam * parts[1]
            o = _rms(o, HEAD_V) * og_ref[...] * (1.0 - lam_init)
            z = z_ref[0, rows, cols].astype(F32)
            o_ref[0, rows, cols] = (o * _silu(z)).astype(BF16)

    _attend_all(attend, o_ref, T, n_ctx, with_ctx)


def _diff(p, qkg, lam_p, og, rope, n_ctx, last, lam_init):
    B, T, _ = p.shape
    full = lambda shape: pl.BlockSpec(shape, lambda b: (0,) * len(shape))
    return pl.pallas_call(
        functools.partial(_diff_kernel, n_ctx=n_ctx, with_ctx=not last, lam_init=lam_init),
        grid=(B,),
        in_specs=[pl.BlockSpec((1, T, 1024), lambda b: (b, 0, P_DF_KV // 1024)),
                  pl.BlockSpec((1, T, 512), lambda b: (b, 0, P_DF_Q // 512)),
                  pl.BlockSpec((1, T, 512), lambda b: (b, 0, P_DF_Z // 512)),
                  full(qkg.shape), full(lam_p.shape), full(og.shape), full(rope.shape)],
        out_specs=pl.BlockSpec((1, T, 512), lambda b: (b, 0, 0)),
        out_shape=jax.ShapeDtypeStruct((B, T, BRANCH_WIDTH), BF16),
        scratch_shapes=[pltpu.VMEM((T, 512), BF16), pltpu.VMEM((T, 512), BF16), pltpu.VMEM((T, 512), BF16),
                        pltpu.VMEM((T, 1024), BF16)],
        compiler_params=_params("arbitrary"),
        name="diff_attn",
    )(p, p, p, qkg, lam_p, og, rope)


def _tile_order(i, n_tiles, rev):
    if not rev:
        return i
    return jnp.where(i == 0, 0, n_tiles - i)


def _chunk_tri(n, rev, chunk=CHUNK):
    r = lax.broadcasted_iota(jnp.int32, (n, n), 0)
    c = lax.broadcasted_iota(jnp.int32, (n, n), 1)
    same = (r // chunk) == (c // chunk)
    tri = (c >= r) if rev else (c <= r)
    return jnp.where(same & tri, 1.0, 0.0).astype(F32)


def _causal(rev, n=CHUNK):
    t = lax.broadcasted_iota(jnp.int32, (n, n), 0)
    s = lax.broadcasted_iota(jnp.int32, (n, n), 1)
    return (s >= t) if rev else (s <= t)


def _cummax_rows(x, rev):
    n = x.shape[0]
    row = lax.broadcasted_iota(jnp.int32, (n, 1), 0)
    s = 1
    while s < n:
        if rev:
            x = jnp.maximum(x, jnp.where(row < n - s, pltpu.roll(x, n - s, 0), -jnp.inf))
        else:
            x = jnp.maximum(x, jnp.where(row >= s, pltpu.roll(x, s, 0), -jnp.inf))
        s *= 2
    return x


def _head_mask(width, group):
    lane = lax.broadcasted_iota(jnp.int32, (1, width), 1) // group
    return [lane == h for h in range(HEADS)]


def _gla_kernel(qk_ref, vz_ref, g_ref, aw_ref, ab_ref, og_ref, o_ref, of_scr, ob_scr, st_scr):
    T = qk_ref.shape[1]
    n_tiles = T // ROW_TILE
    n_chunk = ROW_TILE // CHUNK
    hm = _head_mask(HEADS * GLA_DK, GLA_DK)
    tris = (_chunk_tri(ROW_TILE, False), _chunk_tri(ROW_TILE, True))
    st_scr[...] = jnp.zeros_like(st_scr)

    def load_tile(t, rev):
        rows = pl.ds(pl.multiple_of(t * ROW_TILE, ROW_TILE), ROW_TILE)
        gcol = 256 if rev else 0
        qk = qk_ref[0, rows, :].astype(F32)
        pre = _dot(g_ref[0, rows, :], aw_ref[:, gcol:gcol + 256]) + ab_ref[:, gcol:gcol + 256]
        la = _log_sigmoid(pre) * (1.0 / GLA_TAU)
        hi, lo = _split(la)
        tri = tris[rev].astype(BF16)
        return qk[:, :256] * (GLA_DK ** -0.5), qk[:, 256:], _dot(tri, hi) + _dot(tri, lo)

    def tile(i, carry):
        tiles = (i, _tile_order(i, n_tiles, True))
        steps = []
        for cc in range(n_chunk):
            steps += [(0, cc), (1, n_chunk - 1 - cc)]
        data = [load_tile(tiles[d], bool(d)) for d in (0, 1)]
        n_sub = CHUNK // GLA_SUB
        pre, a_blk, upd, a_msk, intra, o_inter = {}, {}, {}, {}, {}, {}
        st = [st_scr[0], st_scr[1]]

        def prepare(d, cc):
            q, k, b = data[d]
            cr = slice(CHUNK * cc, CHUNK * cc + CHUNK)
            bc, qc, kc = b[cr], q[cr], k[cr]
            b_end = bc[0:1] if d else bc[CHUNK - 1:CHUNK]
            crow = pl.ds(pl.multiple_of(tiles[d] * ROW_TILE + CHUNK * cc, CHUNK), CHUNK)
            q_sub, k_sub = [], []
            for i in range(CHUNK // GLA_SUB):
                sr = slice(GLA_SUB * i, GLA_SUB * i + GLA_SUB)
                first = GLA_SUB * i + (GLA_SUB - 1 if d else 0)
                beta = bc[first:first + 1]
                q_i = qc[sr] * jnp.exp(bc[sr] - beta)
                q_sub.append(jnp.concatenate([jnp.where(hm[h], q_i, 0.0) for h in range(HEADS)],
                                             axis=0).astype(BF16))
                k_sub.append((kc * jnp.exp(jnp.minimum(beta - bc, EXP_CLAMP))).astype(BF16))
            qe = qc * jnp.exp(bc)
            stacked = jnp.concatenate([kc * jnp.exp(b_end - bc), jnp.broadcast_to(jnp.exp(b_end), (8, 256)),
                                       jnp.zeros((CHUNK - 8, 256), F32)], axis=0).T
            pre[d, cc] = dict(
                crow=crow, vc=vz_ref[0, crow, 0:BRANCH_WIDTH], decay=stacked[:, CHUNK:CHUNK + 1],
                qe=[jnp.where(hm[h], qe, 0.0).astype(BF16) for h in range(HEADS)], q_sub=q_sub, k_sub=k_sub,
                k_up_t=stacked[:, 0:CHUNK].astype(BF16))

        def score_dots(d, cc):
            p = pre[d, cc]
            a_blk[d, cc] = [_dot_nt(p['q_sub'][i], p['k_sub'][i]) for i in range(n_sub)]
            upd[d, cc] = jnp.concatenate(
                [_dot(p['k_up_t'][GLA_DK * h:GLA_DK * h + GLA_DK], p['vc'][:, 128 * h:128 * h + 128])
                 for h in range(HEADS)], axis=0)

        def mask_scores(d, cc):
            a_msk[d, cc] = [
                jnp.where(_causal(bool(d)), jnp.concatenate(
                    [a_blk[d, cc][i][GLA_SUB * h:GLA_SUB * h + GLA_SUB] for i in range(n_sub)], axis=0),
                    0.0).astype(BF16) for h in range(HEADS)]

        def value_dots(d, cc):
            vc = pre[d, cc]['vc']
            intra[d, cc] = jnp.concatenate(
                [_dot(a_msk[d, cc][h], vc[:, 128 * h:128 * h + 128]) for h in range(HEADS)], axis=-1)

        def state_dot(d, cc):
            s_bf = st[d].astype(BF16)
            o_inter[d, cc] = jnp.concatenate([_dot(pre[d, cc]['qe'][h], s_bf) for h in range(HEADS)], axis=-1)

        def finish_step(d, cc):
            (ob_scr if d else of_scr)[pre[d, cc]['crow'], :] = o_inter[d, cc] + intra[d, cc]
            st[d] = st[d] * pre[d, cc]['decay'] + upd[d, cc]

        def at(t):
            return [steps[t]] if 0 <= t < len(steps) else []

        for t in range(len(steps) + 4):
            for g in at(t - 1):
                score_dots(*g)
            for g in at(t - 3):
                value_dots(*g)
            for g in at(t - 4):
                state_dot(*g)
            for g in at(t):
                prepare(*g)
            for g in at(t - 2):
                mask_scores(*g)
            for g in at(t - 4):
                finish_step(*g)
        st_scr[0] = st[0]
        st_scr[1] = st[1]
        return carry

    lax.fori_loop(0, n_tiles, tile, 0)

    def finish(i, carry):
        rows = pl.ds(pl.multiple_of(i * ROW_TILE, ROW_TILE), ROW_TILE)
        tot = of_scr[rows, :] + ob_scr[rows, :]
        z = vz_ref[0, rows, BRANCH_WIDTH:].astype(F32)
        for h in range(HEADS):
            cols = slice(128 * h, 128 * h + 128)
            y = _rms(tot[:, cols], HEAD_V) * og_ref[:, cols]
            o_ref[0, rows, cols] = (y * _silu(z[:, cols])).astype(BF16)
        return carry

    lax.fori_loop(0, n_tiles, finish, 0)


def _gla(p, aw, ab, og):
    B, T, _ = p.shape
    full = lambda shape: pl.BlockSpec(shape, lambda b: (0,) * len(shape))
    return pl.pallas_call(
        _gla_kernel,
        grid=(B,),
        in_specs=[pl.BlockSpec((1, T, 512), lambda b: (b, 0, P_GLA_QK // 512)),
                  pl.BlockSpec((1, T, 1024), lambda b: (b, 0, P_GLA_VZ // 1024)),
                  pl.BlockSpec((1, T, 128), lambda b: (b, 0, P_GLA_G // 128)),
                  full(aw.shape), full(ab.shape), full(og.shape)],
        out_specs=pl.BlockSpec((1, T, 512), lambda b: (b, 0, 0)),
        out_shape=jax.ShapeDtypeStruct((B, T, BRANCH_WIDTH), BF16),
        scratch_shapes=[pltpu.VMEM((T, BRANCH_WIDTH), F32), pltpu.VMEM((T, BRANCH_WIDTH), F32),
                        pltpu.VMEM((2, HEADS * GLA_DK, HEAD_V), F32)],
        compiler_params=_params("arbitrary"),
        name="gla_scan",
    )(p, p, p, aw, ab, og)


def _mlstm_kernel(m_ref, if_ref, cw_ref, cb_ref, wq_ref, wkt_ref, gb_ref, og_ref, sk_ref, o_ref,
                  xc_scr, q_scr, kt_scr, at_scr, bt_scr, cm_scr, b_scr, hf_scr, hb_scr, c_scr, *, n_ctx):
    T = m_ref.shape[1]
    n_tiles = T // ROW_TILE
    assert ML_CHUNK == ROW_TILE
    n_chunk = 1
    ctx_tiles = n_ctx // ROW_TILE
    hm = _head_mask(HEADS * ML_DQK, ML_DQK)
    lane128 = lax.broadcasted_iota(jnp.int32, (1, 128), 1)
    is_forget = ((lane128 // HEADS) % 2) == 1
    row_in_tile = lax.broadcasted_iota(jnp.int32, (ROW_TILE, 1), 0)
    tri_f = _chunk_tri(ROW_TILE, False, ML_CHUNK).astype(BF16)
    tri_b = _chunk_tri(ROW_TILE, True, ML_CHUNK).astype(BF16)

    def prep(i, carry):
        r = pl.multiple_of(i * ROW_TILE, ROW_TILE)
        rows = pl.ds(r, ROW_TILE)
        x = m_ref[0, rows, 0:512].astype(F32)
        rp = pl.multiple_of(jnp.maximum(r - 16, 0), 16)
        rn = pl.multiple_of(jnp.minimum(r + ROW_TILE, T - 16), 16)
        prev_row = m_ref[0, pl.ds(rp, 16), 0:512].astype(F32)[15:16]
        next_row = m_ref[0, pl.ds(rn, 16), 0:512].astype(F32)[0:1]
        seg_start = (i == 0) | (i == ctx_tiles)
        seg_end = (i == ctx_tiles - 1) | (i == n_tiles - 1)
        prev_row = jnp.where(seg_start, 0.0, prev_row)
        next_row = jnp.where(seg_end, 0.0, next_row)
        xm = jnp.where(row_in_tile == 0, prev_row, pltpu.roll(x, 1, 0))
        xp = jnp.where(row_in_tile == ROW_TILE - 1, next_row, pltpu.roll(x, ROW_TILE - 1, 0))
        xc = _silu(cw_ref[0:1, :] * xm + cw_ref[1:2, :] * x + cw_ref[2:3, :] * xp + cb_ref[...])
        xc_scr[rows, :] = xc
        xb = xc.astype(BF16)
        q_scr[rows, :] = _dot(xb, wq_ref[...]).astype(BF16)
        g = if_ref[0, rows, :].astype(F32) + gb_ref[...]
        g2 = jnp.where(is_forget, _log_sigmoid(g), g)
        g_hi, g_lo = _split(g2)
        cs = jnp.where(lane128 < 2 * HEADS, _dot(tri_f, g_hi) + _dot(tri_f, g_lo),
                       _dot(tri_b, g_hi) + _dot(tri_b, g_lo))
        b = pltpu.roll(cs, 128 - HEADS, 1)
        a = g2 - b
        cm_scr[rows, :] = jnp.where(lane128 < 2 * HEADS, _cummax_rows(a, False), _cummax_rows(a, True))
        b_scr[rows, :] = b
        kt_scr[i] = (_dot_nt(wkt_ref[...], xb) * (ML_DQK ** -0.5)).astype(BF16)
        at_scr[i] = a.T[0:16, :]
        bt_scr[i] = b.T[0:16, :]
        return carry

    lax.fori_loop(0, n_tiles, prep, 0)

    c_scr[...] = jnp.zeros_like(c_scr)
    ones_col = _ones_col(ML_CHUNK)
    sel_r = lax.broadcasted_iota(jnp.int32, (256, HEADS * 128), 0) % 128
    sel_c = lax.broadcasted_iota(jnp.int32, (256, HEADS * 128), 1) // 128
    sel = [jnp.where(sel_r == 2 * HEADS * d + sel_c, 1.0, 0.0).astype(BF16) for d in (0, 1)]

    def col_bcast(x, d):
        hi, lo = _split(x)
        return _dot(jnp.concatenate([hi, lo], axis=-1), sel[d])

    def chunk_pair(c_f, c_b, m_in):
        units = [(d, h) for d in (0, 1) for h in range(HEADS)]
        cs = (c_f, c_b)
        crow = [pl.ds(pl.multiple_of(c * ML_CHUNK, ML_CHUNK), ML_CHUNK) for c in cs]
        qc = [q_scr[crow[d], :] for d in (0, 1)]
        kt = [kt_scr[cs[d]] for d in (0, 1)]
        vc = [m_ref[0, crow[d], 512:1024] for d in (0, 1)]
        at = [at_scr[cs[d]] for d in (0, 1)]
        bt = [bt_scr[cs[d]] for d in (0, 1)]
        c_bf = [c_scr[d].astype(BF16) for d in (0, 1)]
        causal = [_causal(False, ML_CHUNK), _causal(True, ML_CHUNK)]
        last = [ML_CHUNK - 1, 0]
        qh = {(d, h): jnp.where(hm[h], qc[d], jnp.zeros_like(qc[d])) for d, h in units}
        v_ext = {(d, h): jnp.concatenate([vc[d][:, 128 * h:128 * h + 128], ones_col], axis=-1)
                 for d, h in units}
        s_raw = {u: _dot(qh[u], kt[u[0]]) for u in units}
        q_c = {u: _dot(qh[u], c_bf[u[0]]) for u in units}
        cm_col = [col_bcast(cm_scr[crow[d], :], d) for d in (0, 1)]
        b_col = [col_bcast(b_scr[crow[d], :], d) for d in (0, 1)]
        s_w, m_run, m_last, ktw = {}, {}, {}, {}
        for d, h in units:
            j = 2 * HEADS * d + h
            a_row, m_old = at[d][j:j + 1, :], m_in[HEADS * d + h]
            pm = jnp.where(causal[d], a_row, -jnp.inf)
            m_run[d, h] = jnp.maximum(m_old, cm_col[d][:, 128 * h:128 * h + 128])
            m_wide = jnp.concatenate([m_run[d, h]] * (ML_CHUNK // 128), axis=-1)
            s_w[d, h] = (s_raw[d, h] * jnp.exp(pm - m_wide)).astype(BF16)
            m_last[d, h] = m_run[d, h][last[d]:last[d] + 1, 0:1]
            k_h = kt[d][ML_DQK * h:ML_DQK * h + ML_DQK, :].astype(F32)
            ktw[d, h] = (k_h * jnp.exp(a_row - m_last[d, h])).astype(BF16)
        s_v = {u: _dot(s_w[u], v_ext[u]) for u in units}
        upd = {u: _dot(ktw[u], v_ext[u]) for u in units}
        m_out = []
        for d in (0, 1):
            parts = []
            for h in range(HEADS):
                j = 2 * HEADS * d + h
                hr = slice(ML_DQK * h, ML_DQK * h + ML_DQK)
                b_row, m_old = bt[d][j:j + 1, :], m_in[HEADS * d + h]
                carry_w = jnp.exp(m_old - m_run[d, h])
                num = jnp.concatenate([carry_w, carry_w], axis=-1) * q_c[d, h] + s_v[d, h]
                den = num[:, HEAD_V:HEAD_V + 1]
                floor = jnp.exp(-(b_col[d][:, 128 * h:128 * h + 128] + m_run[d, h]))
                parts.append(num[:, 0:HEAD_V] / jnp.maximum(jnp.abs(den), floor))
                c_scr[d, hr, :] = jnp.exp(m_old - m_last[d, h]) * c_scr[d, hr, :] + upd[d, h]
                m_out.append(b_row[:, last[d]:last[d] + 1] + m_last[d, h])
            (hb_scr if d else hf_scr)[crow[d], :] = jnp.concatenate(parts, axis=-1)
        return m_out

    def tile(i, carry):
        m = list(carry)
        c_f = i * n_chunk
        c_b = _tile_order(i, n_tiles, True) * n_chunk + (n_chunk - 1)
        for cc in range(n_chunk):
            m = chunk_pair(c_f + cc, c_b - cc, m)
        return tuple(m)

    lax.fori_loop(0, n_tiles, tile, tuple(jnp.zeros((1, 1), F32) for _ in range(2 * HEADS)))

    def finish(i, carry):
        rows = pl.ds(pl.multiple_of(i * ROW_TILE, ROW_TILE), ROW_TILE)
        tot = (hf_scr[rows, :] + hb_scr[rows, :]) * _sigmoid(m_ref[0, rows, 1024:1536].astype(F32))
        z = m_ref[0, rows, 1536:2048].astype(F32)
        xc = xc_scr[rows, :]
        for h in range(HEADS):
            cols = slice(128 * h, 128 * h + 128)
            y = _rms(tot[:, cols], HEAD_V) * og_ref[:, cols]
            y = (y + sk_ref[:, cols] * xc[:, cols]) * _silu(z[:, cols])
            o_ref[0, rows, cols] = y.astype(BF16)
        return carry

    lax.fori_loop(0, n_tiles, finish, 0)


def _mlstm(p, cw, cb, wq, wkt, gb, og, sk, n_ctx):
    B, T, _ = p.shape
    full = lambda shape: pl.BlockSpec(shape, lambda b: (0,) * len(shape))
    return pl.pallas_call(
        functools.partial(_mlstm_kernel, n_ctx=n_ctx),
        grid=(B,),
        in_specs=[pl.BlockSpec((1, T, 2048), lambda b: (b, 0, P_ML // 2048)),
                  pl.BlockSpec((1, T, 128), lambda b: (b, 0, P_ML_IF // 128)),
                  full(cw.shape), full(cb.shape), full(wq.shape), full(wkt.shape), full(gb.shape),
                  full(og.shape), full(sk.shape)],
        out_specs=pl.BlockSpec((1, T, 512), lambda b: (b, 0, 0)),
        out_shape=jax.ShapeDtypeStruct((B, T, BRANCH_WIDTH), BF16),
        scratch_shapes=[pltpu.VMEM((T, BRANCH_WIDTH), F32),
                        pltpu.VMEM((T, HEADS * ML_DQK), BF16),
                        pltpu.VMEM((T // ML_CHUNK, HEADS * ML_DQK, ML_CHUNK), BF16),
                        pltpu.VMEM((T // ML_CHUNK, 16, ML_CHUNK), F32),
                        pltpu.VMEM((T // ML_CHUNK, 16, ML_CHUNK), F32),
                        pltpu.VMEM((T, 128), F32),
                        pltpu.VMEM((T, 128), F32),
                        pltpu.VMEM((T, BRANCH_WIDTH), F32),
                        pltpu.VMEM((T, BRANCH_WIDTH), F32),
                        pltpu.VMEM((2, HEADS * ML_DQK, 2 * HEAD_V), F32)],
        compiler_params=_params("arbitrary"),
        name="mlstm_scan",
    )(p, p, cw, cb, wq, wkt, gb, og, sk)


def _merge_kernel(ya_ref, yb_ref, yc_ref, yd_ref, gl_ref, brw_ref, wo_ref, x_ref, gate_ref, o_ref,
                  *, row0, ctx_rows):
    tm = x_ref.shape[1]
    first = pl.program_id(1) * tm + row0
    acc = None
    for i, y_ref in enumerate((ya_ref, yb_ref, yc_ref, yd_ref)):
        u = _dot(y_ref[0], brw_ref[i])
        gsig = _sigmoid(gl_ref[0, :, D_MODEL * i:D_MODEL * (i + 1)].astype(F32))
        acc = gsig * u if acc is None else acc + gsig * u
    out = _dot(acc.astype(BF16), wo_ref[...])
    for r0 in range(0, tm, ROW_TILE):
        rs = slice(r0, r0 + ROW_TILE)
        gate = jnp.where(first + r0 < ctx_rows, gate_ref[0, 0:1, :], gate_ref[0, 1:2, :])
        o_ref[0, rs, :] = x_ref[0, rs, :] + gate * out[rs]


def _merge(ys, p, brw, wo, xs, gates, n_ctx, last):
    B, T, D = xs.shape
    tm = ROW_TILE if last else MERGE_ROWS
    row0 = n_ctx if last else 0
    tile0 = row0 // tm
    nt = (T - row0) // tm
    assert (T - row0) % tm == 0 and row0 % tm == 0 and n_ctx % ROW_TILE == 0
    ymap = lambda b, t: (b, t + tile0, 0)
    return pl.pallas_call(
        functools.partial(_merge_kernel, row0=row0, ctx_rows=n_ctx),
        grid=(B, nt),
        in_specs=[pl.BlockSpec((1, tm, BRANCH_WIDTH), ymap)] * 4 + [
            pl.BlockSpec((1, tm, N_BRANCH * D), lambda b, t: (b, t + tile0, P_MERGE // (N_BRANCH * D))),
            pl.BlockSpec(brw.shape, lambda b, t: (0, 0, 0)),
            pl.BlockSpec(wo.shape, lambda b, t: (0, 0)),
            pl.BlockSpec((1, tm, D), ymap),
            pl.BlockSpec((1, 2, D), lambda b, t: (b, 0, 0))],
        out_specs=pl.BlockSpec((1, tm, D), lambda b, t: (b, t, 0)),
        out_shape=jax.ShapeDtypeStruct((B, nt * tm, D), F32),
        compiler_params=_params("arbitrary", "arbitrary"),
        name="merge",
    )(*ys, p, brw, wo, xs, gates)


def _layout_w_in(w_in):
    offs, off = {}, 0
    for name, w in IN_SPLITS:
        offs[name] = (off, w)
        off += w

    def col(name):
        o, w = offs[name]
        return w_in[..., o:o + w]

    def zeros(n):
        return jnp.zeros(w_in.shape[:-1] + (n,), w_in.dtype)

    parts = [col('merge'), col('df_k'), col('df_v'), col('df_q'), col('df_z'),
             col('ml_x'), col('ml_v'), col('ml_o'), col('ml_z'),
             col('gla_v'), col('gla_z'), col('mla_z'), col('gla_q'), col('gla_k'),
             col('mla_cq'), col('mla_kr'), zeros(64),
             col('mla_ckv'), col('gla_af'), col('gla_ab'), zeros(96), col('ml_if'), zeros(112)]
    out = jnp.concatenate(parts, axis=-1).astype(BF16)
    assert out.shape[-1] == P_WIDTH
    return out


def _rope_tables(rows, n_ctx):
    quarter = ROT_DIM // 4
    inv_freq = ROPE_BASE ** (-jnp.arange(quarter, dtype=F32) / quarter)
    row = jnp.repeat(jnp.arange(rows, dtype=F32), GRID_W)
    col = jnp.tile(jnp.arange(GRID_W, dtype=F32), rows)
    ar = row[:, None] * inv_freq
    ac = col[:, None] * inv_freq
    ang = jnp.concatenate([ar, ar, ac, ac], axis=-1)
    cos = jnp.concatenate([jnp.ones((n_ctx, ROT_DIM), F32), jnp.cos(ang)], axis=0)
    sin = jnp.concatenate([jnp.zeros((n_ctx, ROT_DIM), F32), jnp.sin(ang)], axis=0)
    zero = jnp.zeros_like(cos)
    both = jnp.stack([jnp.tile(cos, (1, 2)), jnp.tile(sin, (1, 2))])
    half = jnp.stack([jnp.concatenate([cos, zero], -1), jnp.concatenate([sin, zero], -1)])
    return half, both


def _pad_lanes(v, n):
    return jnp.concatenate([v, jnp.zeros(v.shape[:-1] + (n - v.shape[-1],), v.dtype)], axis=-1)


def kernel(x, c, ctx, c_ctx, ada_w, ada_b, norm_g, w_in, mla_cq_g, mla_ckv_g, mla_wuq, mla_wukv, mla_q_g,
           mla_k_g, gla_a_w, gla_a_b, gla_out_g, ml_conv_w, ml_conv_b, ml_wq, ml_wk, ml_gate_b, ml_out_g,
           ml_skip, df_qk_g, df_lambda, df_out_g, br_w, w_out):
    B, S, D = x.shape
    n_ctx = ctx.shape[1]
    L = ada_w.shape[0]
    assert D == D_MODEL and n_ctx == ROW_TILE and S % ROW_TILE == 0 and S % GRID_W == 0

    rope_half, rope_both = _rope_tables(S // GRID_W, n_ctx)
    w_in_p = _layout_w_in(w_in)

    n_rows = -(-(B + 1) // 8) * 8
    cc = jnp.concatenate([c, c_ctx[None], jnp.zeros((n_rows - B - 1, D), F32)], axis=0)
    mod_all = _modulation(cc, ada_w, ada_b)

    xs = jnp.concatenate([ctx, x], axis=1)
    for l in range(L):
        last = l == L - 1
        lam_init = 0.8 - 0.6 * math.exp(-0.3 * l)
        m3 = mod_all[l].reshape(n_rows, 3, D)
        lat, cx = m3[:B], jnp.broadcast_to(m3[B][None], (B, 3, D))
        mod = jnp.concatenate([cx, lat, jnp.zeros((B, 2, D), F32)], axis=1)
        gates = jnp.stack([cx[:, 2], lat[:, 2]], axis=1)

        p = _in_projection(xs, mod, norm_g[l], w_in_p[l], n_ctx)

        wq4 = mla_wuq[l].reshape(MLA_Q_LORA, HEADS, MLA_NOPE + ROT_DIM)
        wuq = _pad_lanes(wq4, 256).reshape(MLA_Q_LORA, HEADS * 256).astype(BF16)
        wkv4 = mla_wukv[l].reshape(MLA_KV_LORA, HEADS, MLA_NOPE + HEAD_V)
        wukv = jnp.concatenate([wkv4[..., :MLA_NOPE].reshape(MLA_KV_LORA, -1),
                                wkv4[..., MLA_NOPE:].reshape(MLA_KV_LORA, -1)], axis=-1).astype(BF16)
        qg = jnp.stack([mla_q_g[l, :MLA_NOPE], _pad_lanes(mla_q_g[l, MLA_NOPE:], 128)])
        kg = jnp.stack([mla_k_g[l, :MLA_NOPE], _pad_lanes(mla_k_g[l, MLA_NOPE:], 128)])
        y_mla = _mla(p, wuq, wukv, mla_cq_g[l][None], mla_ckv_g[l][None], qg, kg, rope_half, n_ctx, last)

        qkg = jnp.tile(df_qk_g[l], (1, 2))
        y_df = _diff(p, qkg, df_lambda[l], df_out_g[l][None], rope_both, n_ctx, last, lam_init)

        aw = jnp.zeros((128, 512), F32)
        aw = aw.at[0:16, 0:256].set(gla_a_w[l, 0]).at[16:32, 256:512].set(gla_a_w[l, 1]).astype(BF16)
        y_gla = _gla(p, aw, gla_a_b[l].reshape(1, 512), gla_out_g[l][None])

        wq_bd = jnp.zeros((BRANCH_WIDTH, HEADS * ML_DQK), F32)
        wk_bd = jnp.zeros((BRANCH_WIDTH, HEADS * ML_DQK), F32)
        for h in range(HEADS):
            wq_bd = wq_bd.at[128 * h:128 * h + 128, 64 * h:64 * h + 64].set(ml_wq[l, h])
            wk_bd = wk_bd.at[128 * h:128 * h + 128, 64 * h:64 * h + 64].set(ml_wk[l, h])
        gb = _pad_lanes(ml_gate_b[l].reshape(1, 16), 128)
        y_ml = _mlstm(p, ml_conv_w[l], ml_conv_b[l][None], wq_bd.astype(BF16), wk_bd.T.astype(BF16), gb,
                      ml_out_g[l][None], ml_skip[l][None], n_ctx)

        xs = _merge((y_mla, y_gla, y_ml, y_df), p, br_w[l].astype(BF16), w_out[l].astype(BF16), xs, gates,
                    n_ctx, last)
    return xs
```

```python
import functools
import math

import jax
import jax.numpy as jnp
from jax import lax
from jax.experimental import pallas as pl
from jax.experimental.pallas import tpu as pltpu

F32 = jnp.float32
BF16 = jnp.bfloat16

D_MODEL = 1024
GRID_W = 64
EPS = 1e-6
ROPE_BASE = 10000.0
ROT_DIM = 64
CHUNK = 64
N_BRANCH = 4
BRANCH_WIDTH = 512
HEADS = 4
HEAD_V = BRANCH_WIDTH // HEADS
MLA_NOPE = 128
MLA_Q_LORA = 384
MLA_KV_LORA = 256
GLA_DK = 64
GLA_GATE_RANK = 16
GLA_TAU = 16.0
ML_DQK = 64
DF_DQK = 64

IN_SPLITS = (
    ('mla_cq', 384), ('mla_ckv', 256), ('mla_kr', 64), ('mla_z', 512),
    ('gla_q', 256), ('gla_k', 256), ('gla_v', 512), ('gla_af', 16), ('gla_ab', 16), ('gla_z', 512),
    ('ml_x', 512), ('ml_v', 512), ('ml_o', 512), ('ml_if', 16), ('ml_z', 512),
    ('df_q', 512), ('df_k', 512), ('df_v', 512), ('df_z', 512),
    ('merge', 4096),
)

P_MERGE = 0
P_DF_KV = 4096
P_DF_Q = 5120
P_DF_Z = 5632
P_ML = 6144
P_GLA_VZ = 8192
P_MLA_Z = 9216
P_GLA_QK = 9728
P_MLA_A = 10240
P_MLA_CKV = 10752
P_GLA_G = 11008
P_ML_IF = 11136
P_WIDTH = 11264

ROW_TILE = 256
IN_PROJ_COLS = 1024
MERGE_ROWS = 768
ML_CHUNK = 256
ATT_ROWS = 512
VMEM_LIMIT = 56 * 1024 * 1024
EXP_CLAMP = 80.0
GLA_SUB = 16


def _dot(a, b):
    return jnp.dot(a, b, preferred_element_type=F32)


def _dot_nt(a, b):
    return lax.dot_general(a, b, (((1,), (1,)), ((), ())), preferred_element_type=F32)


def _dot_tn(a, b):
    return lax.dot_general(a, b, (((0,), (0,)), ((), ())), preferred_element_type=F32)


def _sigmoid(x):
    return 1.0 / (1.0 + jnp.exp(-x))


def _silu(x):
    return x * _sigmoid(x)


def _log_sigmoid(x):
    return jnp.minimum(x, 0.0) - jnp.log(1.0 + jnp.exp(-jnp.abs(x)))


def _rms(x, n):
    return x * lax.rsqrt(jnp.sum(x * x, axis=-1, keepdims=True) * (1.0 / n) + EPS)


def _split(x):
    hi = x.astype(BF16)
    return hi, (x - hi.astype(F32)).astype(BF16)


def _group_sum(x, group):
    k = x.shape[-1]
    row = lax.broadcasted_iota(jnp.int32, (k, 128), 0)
    col = lax.broadcasted_iota(jnp.int32, (k, 128), 1)
    sel = jnp.ones((k, 128), BF16) if group is None else jnp.where(row // group == col // group, 1.0, 0.0).astype(BF16)
    hi, lo = _split(x)
    return _dot(hi, sel) + _dot(lo, sel)


def _rms_mxu(x, n, group=None):
    inv = lax.rsqrt(_group_sum(x * x, group) * (1.0 / n) + EPS)
    return x * (inv if x.shape[-1] == 128 else jnp.tile(inv, (1, x.shape[-1] // 128)))


def _rot_matrix():
    src = lax.broadcasted_iota(jnp.int32, (128, 128), 0)
    dst = lax.broadcasted_iota(jnp.int32, (128, 128), 1)
    even = (dst // 16) % 2 == 0
    return jnp.where(even & (src == dst + 16), -1.0, jnp.where(~even & (src == dst - 16), 1.0, 0.0)).astype(BF16)


def _rope_mxu(x, cos, sin, rot):
    hi, lo = _split(x)
    return x * cos + (_dot(hi, rot) + _dot(lo, rot)) * sin


def _ones_col(rows):
    return jnp.where(lax.broadcasted_iota(jnp.int32, (rows, HEAD_V), 1) == 0, 1.0, 0.0).astype(BF16)


def _softmax_pv(s, v_ext):
    e = jnp.exp((s - jnp.max(s, axis=-1, keepdims=True)).astype(BF16))
    o = _dot(e, v_ext)
    return o[:, 0:HEAD_V] / o[:, HEAD_V:HEAD_V + 1]


def _params(*sem):
    return pltpu.CompilerParams(dimension_semantics=sem, vmem_limit_bytes=VMEM_LIMIT)


def _mod_kernel(c_ref, w_ref, b_ref, o_ref):
    s = _silu(c_ref[...])
    o_ref[0] = _dot(s.astype(BF16), w_ref[0].astype(BF16)) + b_ref[0]


def _modulation(cc, ada_w, ada_b):
    L, D, _ = ada_w.shape
    R = cc.shape[0]
    return pl.pallas_call(
        _mod_kernel,
        grid=(L, 3),
        in_specs=[pl.BlockSpec((R, D), lambda l, j: (0, 0)),
                  pl.BlockSpec((1, D, D), lambda l, j: (l, 0, j)),
                  pl.BlockSpec((1, 1, D), lambda l, j: (l, 0, j))],
        out_specs=pl.BlockSpec((1, R, D), lambda l, j: (l, 0, j)),
        out_shape=jax.ShapeDtypeStruct((L, R, 3 * D), F32),
        compiler_params=_params("arbitrary", "arbitrary"),
        name="modulation",
    )(cc, ada_w, ada_b.reshape(L, 1, 3 * D))


def _inproj_kernel(x_ref, mod_ref, g_ref, w_ref, o_ref, h_scr, *, n_ctx):
    T = x_ref.shape[1]

    @pl.when(pl.program_id(1) == 0)
    def _():
        g = g_ref[...]
        for r0 in range(0, T, ROW_TILE):
            k = 0 if r0 < n_ctx else 3
            y = _rms(x_ref[0, r0:r0 + ROW_TILE, :], D_MODEL) * g
            h = y * (1.0 + mod_ref[0, k + 1:k + 2, :]) + mod_ref[0, k:k + 1, :]
            h_scr[r0:r0 + ROW_TILE, :] = h.astype(BF16)

    for r0 in range(0, T, ROW_TILE):
        o_ref[0, r0:r0 + ROW_TILE, :] = _dot(h_scr[r0:r0 + ROW_TILE, :], w_ref[...]).astype(BF16)


def _in_projection(xs, mod, norm_g, w, n_ctx):
    B, T, D = xs.shape
    n_col = P_WIDTH // IN_PROJ_COLS
    return pl.pallas_call(
        functools.partial(_inproj_kernel, n_ctx=n_ctx),
        grid=(B, n_col),
        in_specs=[pl.BlockSpec((1, T, D), lambda b, j: (b, 0, 0)),
                  pl.BlockSpec((1, 8, D), lambda b, j: (b, 0, 0)),
                  pl.BlockSpec((1, D), lambda b, j: (0, 0)),
                  pl.BlockSpec((D, IN_PROJ_COLS), lambda b, j: (0, j))],
        out_specs=pl.BlockSpec((1, T, IN_PROJ_COLS), lambda b, j: (b, 0, j)),
        out_shape=jax.ShapeDtypeStruct((B, T, P_WIDTH), BF16),
        scratch_shapes=[pltpu.VMEM((T, D), BF16)],
        compiler_params=_params("arbitrary", "arbitrary"),
        name="in_projection",
    )(xs, mod, norm_g.reshape(1, D), w)


def _latent_rows(T, n_ctx):
    return ATT_ROWS if (T - n_ctx) % ATT_ROWS == 0 else ROW_TILE


def _attend_all(attend, scores0, s0_scr, o_ref, T, n_ctx, with_ctx):
    if with_ctx:
        attend(pl.ds(0, n_ctx), n_ctx, None, None)
    else:
        o_ref[0, 0:n_ctx, :] = jnp.zeros((n_ctx, o_ref.shape[2]), o_ref.dtype)
    rows_per = s0_scr.shape[0]
    n = (T - n_ctx) // rows_per

    def rows_of(i):
        return pl.ds(pl.multiple_of(n_ctx + i * rows_per, ROW_TILE), rows_per)

    s0_scr[...] = scores0(rows_of(0))

    def latent(i, carry):
        attend(rows_of(i), T, s0_scr[...], rows_of(jnp.minimum(i + 1, n - 1)))
        return carry

    lax.fori_loop(0, n, latent, 0)


def _mla_kernel(pa_ref, pc_ref, z_ref, wuq_ref, wukv_ref, cqg_ref, ckvg_ref, qg_ref, kg_ref, rope_ref,
                o_ref, q_scr, k_scr, v_scr, s0_scr, *, n_ctx, with_ctx):
    T = pa_ref.shape[1]
    scale = (MLA_NOPE + ROT_DIM) ** -0.5
    rot = _rot_matrix()

    def prep(i, carry):
        r = pl.multiple_of(i * ROW_TILE, ROW_TILE)
        rows = pl.ds(r, ROW_TILE)
        cos, sin = rope_ref[0, rows, :], rope_ref[1, rows, :]
        pa = pa_ref[0, rows, :].astype(F32)
        cq = (_rms(pa[:, :MLA_Q_LORA], MLA_Q_LORA) * cqg_ref[...]).astype(BF16)
        q = _dot(cq, wuq_ref[...])
        kr = _rms(pa[:, MLA_Q_LORA:], ROT_DIM) * kg_ref[1:2, :]
        kr = _rope_mxu(kr, cos, sin, rot).astype(BF16)
        ckv = (_rms(pc_ref[0, rows, :].astype(F32), MLA_KV_LORA) * ckvg_ref[...]).astype(BF16)
        kv = _dot(ckv, wukv_ref[...])
        for h in range(HEADS):
            qn = _rms(q[:, 256 * h:256 * h + 128], MLA_NOPE) * qg_ref[0:1, :]
            qr = _rms(q[:, 256 * h + 128:256 * h + 256], ROT_DIM) * qg_ref[1:2, :]
            qr = _rope_mxu(qr, cos, sin, rot)
            q_scr[rows, 256 * h:256 * h + 128] = (qn * scale).astype(BF16)
            q_scr[rows, 256 * h + 128:256 * h + 256] = (qr * scale).astype(BF16)
            kn = _rms(kv[:, 128 * h:128 * h + 128], MLA_NOPE) * kg_ref[0:1, :]
            k_scr[rows, 256 * h:256 * h + 128] = kn.astype(BF16)
            k_scr[rows, 256 * h + 128:256 * h + 256] = kr
            v_scr[rows, 256 * h:256 * h + 128] = kv[:, 512 + 128 * h:512 + 128 * h + 128].astype(BF16)
            v_scr[rows, 256 * h + 128:256 * h + 256] = _ones_col(ROW_TILE)
        return carry

    lax.fori_loop(0, T // ROW_TILE, prep, 0)

    def scores(h, rows, nk):
        return _dot_nt(q_scr[rows, 256 * h:256 * h + 256], k_scr[0:nk, 256 * h:256 * h + 256])

    def attend(rows, nk, s_first, next_rows):
        ahead = HEADS if next_rows is None else 1
        pending = [scores(0, rows, nk) if s_first is None else s_first]
        pending += [scores(h, rows, nk) for h in range(1, ahead)]
        for h in range(HEADS):
            s = pending.pop(0)
            if h + ahead < HEADS:
                pending.append(scores(h + ahead, rows, nk))
            elif h + ahead == HEADS and next_rows is not None:
                s0_scr[...] = scores(0, next_rows, nk)
            o = _softmax_pv(s, v_scr[0:nk, 256 * h:256 * h + 256])
            z = z_ref[0, rows, 128 * h:128 * h + 128].astype(F32)
            o_ref[0, rows, 128 * h:128 * h + 128] = (o * _silu(z)).astype(BF16)

    _attend_all(attend, lambda rows: scores(0, rows, T), s0_scr, o_ref, T, n_ctx, with_ctx)


def _mla(p, wuq, wukv, cqg, ckvg, qg, kg, rope, n_ctx, last):
    B, T, _ = p.shape
    full = lambda shape: pl.BlockSpec(shape, lambda b: (0,) * len(shape))
    return pl.pallas_call(
        functools.partial(_mla_kernel, n_ctx=n_ctx, with_ctx=not last),
        grid=(B,),
        in_specs=[pl.BlockSpec((1, T, 512), lambda b: (b, 0, P_MLA_A // 512)),
                  pl.BlockSpec((1, T, 256), lambda b: (b, 0, P_MLA_CKV // 256)),
                  pl.BlockSpec((1, T, 512), lambda b: (b, 0, P_MLA_Z // 512)),
                  full(wuq.shape), full(wukv.shape), full(cqg.shape), full(ckvg.shape),
                  full(qg.shape), full(kg.shape), full(rope.shape)],
        out_specs=pl.BlockSpec((1, T, 512), lambda b: (b, 0, 0)),
        out_shape=jax.ShapeDtypeStruct((B, T, BRANCH_WIDTH), BF16),
        scratch_shapes=[pltpu.VMEM((T, 1024), BF16), pltpu.VMEM((T, 1024), BF16),
                        pltpu.VMEM((T, 1024), BF16), pltpu.VMEM((_latent_rows(T, n_ctx), T), F32)],
        compiler_params=_params("arbitrary"),
        name="mla",
    )(p, p, p, wuq, wukv, cqg, ckvg, qg, kg, rope)


def _diff_kernel(kv_ref, q_ref, z_ref, g_ref, lam_ref, og_ref, rope_ref, o_ref, q0_scr, q1_scr, k_scr, v_scr, s0_scr,
                 *, n_ctx, with_ctx, lam_init):
    T = kv_ref.shape[1]
    scale = DF_DQK ** -0.5
    lo_mask = lax.broadcasted_iota(jnp.int32, (1, 128), 1) < 64
    rot = _rot_matrix()

    def prep(i, carry):
        r = pl.multiple_of(i * ROW_TILE, ROW_TILE)
        rows = pl.ds(r, ROW_TILE)
        cos, sin = rope_ref[0, rows, :], rope_ref[1, rows, :]
        for h in range(HEADS):
            cols = slice(128 * h, 128 * h + 128)
            q = _rms_mxu(q_ref[0, rows, cols].astype(F32), DF_DQK, DF_DQK) * g_ref[0:1, :]
            q = _rope_mxu(q, cos, sin, rot) * scale
            q0_scr[rows, cols] = jnp.where(lo_mask, q, 0.0).astype(BF16)
            q1_scr[rows, cols] = jnp.where(lo_mask, 0.0, q).astype(BF16)
            k = _rms_mxu(kv_ref[0, rows, cols].astype(F32), DF_DQK, DF_DQK) * g_ref[1:2, :]
            k_scr[rows, cols] = _rope_mxu(k, cos, sin, rot).astype(BF16)
            v_scr[rows, 256 * h:256 * h + 128] = kv_ref[0, rows, 512 + 128 * h:512 + 128 * h + 128]
            v_scr[rows, 256 * h + 128:256 * h + 256] = _ones_col(ROW_TILE)
        return carry

    lax.fori_loop(0, T // ROW_TILE, prep, 0)

    lp = lam_ref[...]
    lam = (jnp.exp(jnp.sum(lp[0:1] * lp[1:2], axis=-1, keepdims=True))
           - jnp.exp(jnp.sum(lp[2:3] * lp[3:4], axis=-1, keepdims=True)) + lam_init)
    def scores(u, rows, nk):
        cols = slice(128 * (u // 2), 128 * (u // 2) + 128)
        return _dot_nt((q1_scr if u % 2 else q0_scr)[rows, cols], k_scr[0:nk, cols])

    def attend(rows, nk, s_first, next_rows):
        n_units = 2 * HEADS
        ahead = n_units if next_rows is None else 1
        pending = [scores(0, rows, nk) if s_first is None else s_first]
        pending += [scores(u, rows, nk) for u in range(1, ahead)]
        for h in range(HEADS):
            cols = slice(128 * h, 128 * h + 128)
            v_ext = v_scr[0:nk, 256 * h:256 * h + 256]
            parts = []
            for mp in range(2):
                u = 2 * h + mp
                s = pending.pop(0)
                if u + ahead < n_units:
                    pending.append(scores(u + ahead, rows, nk))
                elif u + ahead == n_units and next_rows is not None:
                    s0_scr[...] = scores(0, next_rows, nk)
                parts.append(_softmax_pv(s, v_ext))
            o = parts[0] - lam * parts[1]
            o = _rms(o, HEAD_V) * og_ref[...] * (1.0 - lam_init)
            z = z_ref[0, rows, cols].astype(F32)
            o_ref[0, rows, cols] = (o * _silu(z)).astype(BF16)

    _attend_all(attend, lambda rows: scores(0, rows, T), s0_scr, o_ref, T, n_ctx, with_ctx)


def _diff(p, qkg, lam_p, og, rope, n_ctx, last, lam_init):
    B, T, _ = p.shape
    full = lambda shape: pl.BlockSpec(shape, lambda b: (0,) * len(shape))
    return pl.pallas_call(
        functools.partial(_diff_kernel, n_ctx=n_ctx, with_ctx=not last, lam_init=lam_init),
        grid=(B,),
        in_specs=[pl.BlockSpec((1, T, 1024), lambda b: (b, 0, P_DF_KV // 1024)),
                  pl.BlockSpec((1, T, 512), lambda b: (b, 0, P_DF_Q // 512)),
                  pl.BlockSpec((1, T, 512), lambda b: (b, 0, P_DF_Z // 512)),
                  full(qkg.shape), full(lam_p.shape), full(og.shape), full(rope.shape)],
        out_specs=pl.BlockSpec((1, T, 512), lambda b: (b, 0, 0)),
        out_shape=jax.ShapeDtypeStruct((B, T, BRANCH_WIDTH), BF16),
        scratch_shapes=[pltpu.VMEM((T, 512), BF16), pltpu.VMEM((T, 512), BF16), pltpu.VMEM((T, 512), BF16),
                        pltpu.VMEM((T, 1024), BF16), pltpu.VMEM((_latent_rows(T, n_ctx), T), F32)],
        compiler_params=_params("arbitrary"),
        name="diff_attn",
    )(p, p, p, qkg, lam_p, og, rope)


def _tile_order(i, n_tiles, rev):
    if not rev:
        return i
    return jnp.where(i == 0, 0, n_tiles - i)


def _chunk_tri(n, rev, chunk=CHUNK):
    r = lax.broadcasted_iota(jnp.int32, (n, n), 0)
    c = lax.broadcasted_iota(jnp.int32, (n, n), 1)
    same = (r // chunk) == (c // chunk)
    tri = (c >= r) if rev else (c <= r)
    return jnp.where(same & tri, 1.0, 0.0).astype(F32)


def _causal(rev, n=CHUNK):
    t = lax.broadcasted_iota(jnp.int32, (n, n), 0)
    s = lax.broadcasted_iota(jnp.int32, (n, n), 1)
    return (s >= t) if rev else (s <= t)


def _cummax_rows(x, rev):
    n = x.shape[0]
    row = lax.broadcasted_iota(jnp.int32, (n, 1), 0)
    s = 1
    while s < n:
        if rev:
            x = jnp.maximum(x, jnp.where(row < n - s, pltpu.roll(x, n - s, 0), -jnp.inf))
        else:
            x = jnp.maximum(x, jnp.where(row >= s, pltpu.roll(x, s, 0), -jnp.inf))
        s *= 2
    return x


def _head_mask(width, group):
    lane = lax.broadcasted_iota(jnp.int32, (1, width), 1) // group
    return [lane == h for h in range(HEADS)]


def _gla_kernel(qk_ref, vz_ref, g_ref, aw_ref, ab_ref, og_ref, o_ref, of_scr, ob_scr, st_scr):
    T = qk_ref.shape[1]
    n_tiles = T // ROW_TILE
    n_chunk = ROW_TILE // CHUNK
    hm = _head_mask(HEADS * GLA_DK, GLA_DK)
    tris = (_chunk_tri(ROW_TILE, False), _chunk_tri(ROW_TILE, True))
    st_scr[...] = jnp.zeros_like(st_scr)

    def load_tile(t, rev):
        rows = pl.ds(pl.multiple_of(t * ROW_TILE, ROW_TILE), ROW_TILE)
        gcol = 256 if rev else 0
        qk = qk_ref[0, rows, :].astype(F32)
        pre = _dot(g_ref[0, rows, :], aw_ref[:, gcol:gcol + 256]) + ab_ref[:, gcol:gcol + 256]
        la = _log_sigmoid(pre) * (1.0 / GLA_TAU)
        hi, lo = _split(la)
        tri = tris[rev].astype(BF16)
        return qk[:, :256] * (GLA_DK ** -0.5), qk[:, 256:], _dot(tri, hi) + _dot(tri, lo)

    def tile(i, carry):
        tiles = (i, _tile_order(i, n_tiles, True))
        steps = []
        for cc in range(n_chunk):
            steps += [(0, cc), (1, n_chunk - 1 - cc)]
        data = [load_tile(tiles[d], bool(d)) for d in (0, 1)]
        n_sub = CHUNK // GLA_SUB
        pre, a_blk, upd, a_msk, intra, o_inter = {}, {}, {}, {}, {}, {}
        st = [st_scr[0], st_scr[1]]

        def prepare(d, cc):
            q, k, b = data[d]
            cr = slice(CHUNK * cc, CHUNK * cc + CHUNK)
            bc, qc, kc = b[cr], q[cr], k[cr]
            b_end = bc[0:1] if d else bc[CHUNK - 1:CHUNK]
            crow = pl.ds(pl.multiple_of(tiles[d] * ROW_TILE + CHUNK * cc, CHUNK), CHUNK)
            q_sub, k_sub = [], []
            for i in range(CHUNK // GLA_SUB):
                sr = slice(GLA_SUB * i, GLA_SUB * i + GLA_SUB)
                first = GLA_SUB * i + (GLA_SUB - 1 if d else 0)
                beta = bc[first:first + 1]
                q_i = qc[sr] * jnp.exp(bc[sr] - beta)
                q_sub.append(jnp.concatenate([jnp.where(hm[h], q_i, 0.0) for h in range(HEADS)],
                                             axis=0).astype(BF16))
                k_sub.append((kc * jnp.exp(jnp.minimum(beta - bc, EXP_CLAMP))).astype(BF16))
            qe = qc * jnp.exp(bc)
            stacked = jnp.concatenate([kc * jnp.exp(b_end - bc), jnp.broadcast_to(jnp.exp(b_end), (8, 256)),
                                       jnp.zeros((CHUNK - 8, 256), F32)], axis=0).T
            pre[d, cc] = dict(
                crow=crow, vc=vz_ref[0, crow, 0:BRANCH_WIDTH], decay=stacked[:, CHUNK:CHUNK + 1],
                qe=[jnp.where(hm[h], qe, 0.0).astype(BF16) for h in range(HEADS)], q_sub=q_sub, k_sub=k_sub,
                k_up_t=stacked[:, 0:CHUNK].astype(BF16))

        def score_dots(d, cc):
            p = pre[d, cc]
            a_blk[d, cc] = [_dot_nt(p['q_sub'][i], p['k_sub'][i]) for i in range(n_sub)]
            upd[d, cc] = jnp.concatenate(
                [_dot(p['k_up_t'][GLA_DK * h:GLA_DK * h + GLA_DK], p['vc'][:, 128 * h:128 * h + 128])
                 for h in range(HEADS)], axis=0)

        def mask_scores(d, cc):
            a_msk[d, cc] = [
                jnp.where(_causal(bool(d)), jnp.concatenate(
                    [a_blk[d, cc][i][GLA_SUB * h:GLA_SUB * h + GLA_SUB] for i in range(n_sub)], axis=0),
                    0.0).astype(BF16) for h in range(HEADS)]

        def value_dots(d, cc):
            vc = pre[d, cc]['vc']
            intra[d, cc] = jnp.concatenate(
                [_dot(a_msk[d, cc][h], vc[:, 128 * h:128 * h + 128]) for h in range(HEADS)], axis=-1)

        def state_dot(d, cc):
            s_bf = st[d].astype(BF16)
            o_inter[d, cc] = jnp.concatenate([_dot(pre[d, cc]['qe'][h], s_bf) for h in range(HEADS)], axis=-1)

        def finish_step(d, cc):
            (ob_scr if d else of_scr)[pre[d, cc]['crow'], :] = o_inter[d, cc] + intra[d, cc]
            st[d] = st[d] * pre[d, cc]['decay'] + upd[d, cc]

        def at(t):
            return [steps[t]] if 0 <= t < len(steps) else []

        for t in range(len(steps) + 4):
            for g in at(t - 1):
                score_dots(*g)
            for g in at(t - 3):
                value_dots(*g)
            for g in at(t - 4):
                state_dot(*g)
            for g in at(t):
                prepare(*g)
            for g in at(t - 2):
                mask_scores(*g)
            for g in at(t - 4):
                finish_step(*g)
        st_scr[0] = st[0]
        st_scr[1] = st[1]
        return carry

    lax.fori_loop(0, n_tiles, tile, 0)

    def finish(i, carry):
        rows = pl.ds(pl.multiple_of(i * ROW_TILE, ROW_TILE), ROW_TILE)
        tot = of_scr[rows, :] + ob_scr[rows, :]
        z = vz_ref[0, rows, BRANCH_WIDTH:].astype(F32)
        for h in range(HEADS):
            cols = slice(128 * h, 128 * h + 128)
            y = _rms(tot[:, cols], HEAD_V) * og_ref[:, cols]
            o_ref[0, rows, cols] = (y * _silu(z[:, cols])).astype(BF16)
        return carry

    lax.fori_loop(0, n_tiles, finish, 0)


def _gla(p, aw, ab, og):
    B, T, _ = p.shape
    full = lambda shape: pl.BlockSpec(shape, lambda b: (0,) * len(shape))
    return pl.pallas_call(
        _gla_kernel,
        grid=(B,),
        in_specs=[pl.BlockSpec((1, T, 512), lambda b: (b, 0, P_GLA_QK // 512)),
                  pl.BlockSpec((1, T, 1024), lambda b: (b, 0, P_GLA_VZ // 1024)),
                  pl.BlockSpec((1, T, 128), lambda b: (b, 0, P_GLA_G // 128)),
                  full(aw.shape), full(ab.shape), full(og.shape)],
        out_specs=pl.BlockSpec((1, T, 512), lambda b: (b, 0, 0)),
        out_shape=jax.ShapeDtypeStruct((B, T, BRANCH_WIDTH), BF16),
        scratch_shapes=[pltpu.VMEM((T, BRANCH_WIDTH), F32), pltpu.VMEM((T, BRANCH_WIDTH), F32),
                        pltpu.VMEM((2, HEADS * GLA_DK, HEAD_V), F32)],
        compiler_params=_params("arbitrary"),
        name="gla_scan",
    )(p, p, p, aw, ab, og)


def _mlstm_kernel(m_ref, if_ref, cw_ref, cb_ref, wq_ref, wkt_ref, gb_ref, og_ref, sk_ref, o_ref,
                  xc_scr, q_scr, kt_scr, at_scr, bt_scr, cm_scr, b_scr, hf_scr, hb_scr, c_scr, *, n_ctx):
    T = m_ref.shape[1]
    n_tiles = T // ROW_TILE
    assert ML_CHUNK == ROW_TILE
    n_chunk = 1
    ctx_tiles = n_ctx // ROW_TILE
    hm = _head_mask(HEADS * ML_DQK, ML_DQK)
    lane128 = lax.broadcasted_iota(jnp.int32, (1, 128), 1)
    is_forget = ((lane128 // HEADS) % 2) == 1
    row_in_tile = lax.broadcasted_iota(jnp.int32, (ROW_TILE, 1), 0)
    tri_f = _chunk_tri(ROW_TILE, False, ML_CHUNK).astype(BF16)
    tri_b = _chunk_tri(ROW_TILE, True, ML_CHUNK).astype(BF16)

    def prep(i, carry):
        r = pl.multiple_of(i * ROW_TILE, ROW_TILE)
        rows = pl.ds(r, ROW_TILE)
        x = m_ref[0, rows, 0:512].astype(F32)
        rp = pl.multiple_of(jnp.maximum(r - 16, 0), 16)
        rn = pl.multiple_of(jnp.minimum(r + ROW_TILE, T - 16), 16)
        prev_row = m_ref[0, pl.ds(rp, 16), 0:512].astype(F32)[15:16]
        next_row = m_ref[0, pl.ds(rn, 16), 0:512].astype(F32)[0:1]
        seg_start = (i == 0) | (i == ctx_tiles)
        seg_end = (i == ctx_tiles - 1) | (i == n_tiles - 1)
        prev_row = jnp.where(seg_start, 0.0, prev_row)
        next_row = jnp.where(seg_end, 0.0, next_row)
        xm = jnp.where(row_in_tile == 0, prev_row, pltpu.roll(x, 1, 0))
        xp = jnp.where(row_in_tile == ROW_TILE - 1, next_row, pltpu.roll(x, ROW_TILE - 1, 0))
        xc = _silu(cw_ref[0:1, :] * xm + cw_ref[1:2, :] * x + cw_ref[2:3, :] * xp + cb_ref[...])
        xc_scr[rows, :] = xc
        xb = xc.astype(BF16)
        q_scr[rows, :] = _dot(xb, wq_ref[...]).astype(BF16)
        g = if_ref[0, rows, :].astype(F32) + gb_ref[...]
        g2 = jnp.where(is_forget, _log_sigmoid(g), g)
        g_hi, g_lo = _split(g2)
        cs = jnp.where(lane128 < 2 * HEADS, _dot(tri_f, g_hi) + _dot(tri_f, g_lo),
                       _dot(tri_b, g_hi) + _dot(tri_b, g_lo))
        b = pltpu.roll(cs, 128 - HEADS, 1)
        a = g2 - b
        cm_scr[rows, :] = jnp.where(lane128 < 2 * HEADS, _cummax_rows(a, False), _cummax_rows(a, True))
        b_scr[rows, :] = b
        kt_scr[i] = (_dot_nt(wkt_ref[...], xb) * (ML_DQK ** -0.5)).astype(BF16)
        at_scr[i] = a.T[0:16, :]
        bt_scr[i] = b.T[0:16, :]
        return carry

    lax.fori_loop(0, n_tiles, prep, 0)

    c_scr[...] = jnp.zeros_like(c_scr)
    ones_col = _ones_col(ML_CHUNK)
    sel_r = lax.broadcasted_iota(jnp.int32, (256, HEADS * 128), 0) % 128
    sel_c = lax.broadcasted_iota(jnp.int32, (256, HEADS * 128), 1) // 128
    sel = [jnp.where(sel_r == 2 * HEADS * d + sel_c, 1.0, 0.0).astype(BF16) for d in (0, 1)]

    def col_bcast(x, d):
        hi, lo = _split(x)
        return _dot(jnp.concatenate([hi, lo], axis=-1), sel[d])

    def chunk_pair(c_f, c_b, m_in):
        units = [(d, h) for d in (0, 1) for h in range(HEADS)]
        cs = (c_f, c_b)
        crow = [pl.ds(pl.multiple_of(c * ML_CHUNK, ML_CHUNK), ML_CHUNK) for c in cs]
        qc = [q_scr[crow[d], :] for d in (0, 1)]
        kt = [kt_scr[cs[d]] for d in (0, 1)]
        vc = [m_ref[0, crow[d], 512:1024] for d in (0, 1)]
        at = [at_scr[cs[d]] for d in (0, 1)]
        bt = [bt_scr[cs[d]] for d in (0, 1)]
        c_bf = [c_scr[d].astype(BF16) for d in (0, 1)]
        causal = [_causal(False, ML_CHUNK), _causal(True, ML_CHUNK)]
        last = [ML_CHUNK - 1, 0]
        qh = {(d, h): jnp.where(hm[h], qc[d], jnp.zeros_like(qc[d])) for d, h in units}
        v_ext = {(d, h): jnp.concatenate([vc[d][:, 128 * h:128 * h + 128], ones_col], axis=-1)
                 for d, h in units}
        s_raw = {u: _dot(qh[u], kt[u[0]]) for u in units}
        q_c = {u: _dot(qh[u], c_bf[u[0]]) for u in units}
        cm_col = [col_bcast(cm_scr[crow[d], :], d) for d in (0, 1)]
        b_col = [col_bcast(b_scr[crow[d], :], d) for d in (0, 1)]
        s_w, m_run, m_last, ktw = {}, {}, {}, {}
        for d, h in units:
            j = 2 * HEADS * d + h
            a_row, m_old = at[d][j:j + 1, :], m_in[HEADS * d + h]
            pm = jnp.where(causal[d], a_row, -jnp.inf)
            m_run[d, h] = jnp.maximum(m_old, cm_col[d][:, 128 * h:128 * h + 128])
            m_wide = jnp.concatenate([m_run[d, h]] * (ML_CHUNK // 128), axis=-1)
            s_w[d, h] = (s_raw[d, h] * jnp.exp(pm - m_wide)).astype(BF16)
            m_last[d, h] = m_run[d, h][last[d]:last[d] + 1, 0:1]
            k_h = kt[d][ML_DQK * h:ML_DQK * h + ML_DQK, :].astype(F32)
            ktw[d, h] = (k_h * jnp.exp(a_row - m_last[d, h])).astype(BF16)
        s_v = {u: _dot(s_w[u], v_ext[u]) for u in units}
        upd = {u: _dot(ktw[u], v_ext[u]) for u in units}
        m_out = []
        for d in (0, 1):
            parts = []
            for h in range(HEADS):
                j = 2 * HEADS * d + h
                hr = slice(ML_DQK * h, ML_DQK * h + ML_DQK)
                b_row, m_old = bt[d][j:j + 1, :], m_in[HEADS * d + h]
                carry_w = jnp.exp(m_old - m_run[d, h])
                num = jnp.concatenate([carry_w, carry_w], axis=-1) * q_c[d, h] + s_v[d, h]
                den = num[:, HEAD_V:HEAD_V + 1]
                floor = jnp.exp(-(b_col[d][:, 128 * h:128 * h + 128] + m_run[d, h]))
                parts.append(num[:, 0:HEAD_V] / jnp.maximum(jnp.abs(den), floor))
                c_scr[d, hr, :] = jnp.exp(m_old - m_last[d, h]) * c_scr[d, hr, :] + upd[d, h]
                m_out.append(b_row[:, last[d]:last[d] + 1] + m_last[d, h])
            (hb_scr if d else hf_scr)[crow[d], :] = jnp.concatenate(parts, axis=-1)
        return m_out

    def tile(i, carry):
        m = list(carry)
        c_f = i * n_chunk
        c_b = _tile_order(i, n_tiles, True) * n_chunk + (n_chunk - 1)
        for cc in range(n_chunk):
            m = chunk_pair(c_f + cc, c_b - cc, m)
        return tuple(m)

    lax.fori_loop(0, n_tiles, tile, tuple(jnp.zeros((1, 1), F32) for _ in range(2 * HEADS)))

    def finish(i, carry):
        rows = pl.ds(pl.multiple_of(i * ROW_TILE, ROW_TILE), ROW_TILE)
        tot = (hf_scr[rows, :] + hb_scr[rows, :]) * _sigmoid(m_ref[0, rows, 1024:1536].astype(F32))
        z = m_ref[0, rows, 1536:2048].astype(F32)
        xc = xc_scr[rows, :]
        for h in range(HEADS):
            cols = slice(128 * h, 128 * h + 128)
            y = _rms(tot[:, cols], HEAD_V) * og_ref[:, cols]
            y = (y + sk_ref[:, cols] * xc[:, cols]) * _silu(z[:, cols])
            o_ref[0, rows, cols] = y.astype(BF16)
        return carry

    lax.fori_loop(0, n_tiles, finish, 0)


def _mlstm(p, cw, cb, wq, wkt, gb, og, sk, n_ctx):
    B, T, _ = p.shape
    full = lambda shape: pl.BlockSpec(shape, lambda b: (0,) * len(shape))
    return pl.pallas_call(
        functools.partial(_mlstm_kernel, n_ctx=n_ctx),
        grid=(B,),
        in_specs=[pl.BlockSpec((1, T, 2048), lambda b: (b, 0, P_ML // 2048)),
                  pl.BlockSpec((1, T, 128), lambda b: (b, 0, P_ML_IF // 128)),
                  full(cw.shape), full(cb.shape), full(wq.shape), full(wkt.shape), full(gb.shape),
                  full(og.shape), full(sk.shape)],
        out_specs=pl.BlockSpec((1, T, 512), lambda b: (b, 0, 0)),
        out_shape=jax.ShapeDtypeStruct((B, T, BRANCH_WIDTH), BF16),
        scratch_shapes=[pltpu.VMEM((T, BRANCH_WIDTH), F32),
                        pltpu.VMEM((T, HEADS * ML_DQK), BF16),
                        pltpu.VMEM((T // ML_CHUNK, HEADS * ML_DQK, ML_CHUNK), BF16),
                        pltpu.VMEM((T // ML_CHUNK, 16, ML_CHUNK), F32),
                        pltpu.VMEM((T // ML_CHUNK, 16, ML_CHUNK), F32),
                        pltpu.VMEM((T, 128), F32),
                        pltpu.VMEM((T, 128), F32),
                        pltpu.VMEM((T, BRANCH_WIDTH), F32),
                        pltpu.VMEM((T, BRANCH_WIDTH), F32),
                        pltpu.VMEM((2, HEADS * ML_DQK, 2 * HEAD_V), F32)],
        compiler_params=_params("arbitrary"),
        name="mlstm_scan",
    )(p, p, cw, cb, wq, wkt, gb, og, sk)


def _merge_kernel(ya_ref, yb_ref, yc_ref, yd_ref, gl_ref, brw_ref, wo_ref, x_ref, gate_ref, o_ref,
                  *, row0, ctx_rows):
    tm = x_ref.shape[1]
    first = pl.program_id(1) * tm + row0
    acc = None
    for i, y_ref in enumerate((ya_ref, yb_ref, yc_ref, yd_ref)):
        u = _dot(y_ref[0], brw_ref[i])
        gsig = _sigmoid(gl_ref[0, :, D_MODEL * i:D_MODEL * (i + 1)].astype(F32))
        acc = gsig * u if acc is None else acc + gsig * u
    out = _dot(acc.astype(BF16), wo_ref[...])
    for r0 in range(0, tm, ROW_TILE):
        rs = slice(r0, r0 + ROW_TILE)
        gate = jnp.where(first + r0 < ctx_rows, gate_ref[0, 0:1, :], gate_ref[0, 1:2, :])
        o_ref[0, rs, :] = x_ref[0, rs, :] + gate * out[rs]


def _merge(ys, p, brw, wo, xs, gates, n_ctx, last):
    B, T, D = xs.shape
    tm = ROW_TILE if last else MERGE_ROWS
    row0 = n_ctx if last else 0
    tile0 = row0 // tm
    nt = (T - row0) // tm
    assert (T - row0) % tm == 0 and row0 % tm == 0 and n_ctx % ROW_TILE == 0
    ymap = lambda b, t: (b, t + tile0, 0)
    return pl.pallas_call(
        functools.partial(_merge_kernel, row0=row0, ctx_rows=n_ctx),
        grid=(B, nt),
        in_specs=[pl.BlockSpec((1, tm, BRANCH_WIDTH), ymap)] * 4 + [
            pl.BlockSpec((1, tm, N_BRANCH * D), lambda b, t: (b, t + tile0, P_MERGE // (N_BRANCH * D))),
            pl.BlockSpec(brw.shape, lambda b, t: (0, 0, 0)),
            pl.BlockSpec(wo.shape, lambda b, t: (0, 0)),
            pl.BlockSpec((1, tm, D), ymap),
            pl.BlockSpec((1, 2, D), lambda b, t: (b, 0, 0))],
        out_specs=pl.BlockSpec((1, tm, D), lambda b, t: (b, t, 0)),
        out_shape=jax.ShapeDtypeStruct((B, nt * tm, D), F32),
        compiler_params=_params("arbitrary", "arbitrary"),
        name="merge",
    )(*ys, p, brw, wo, xs, gates)


def _layout_w_in(w_in):
    offs, off = {}, 0
    for name, w in IN_SPLITS:
        offs[name] = (off, w)
        off += w

    def col(name):
        o, w = offs[name]
        return w_in[..., o:o + w]

    def zeros(n):
        return jnp.zeros(w_in.shape[:-1] + (n,), w_in.dtype)

    parts = [col('merge'), col('df_k'), col('df_v'), col('df_q'), col('df_z'),
             col('ml_x'), col('ml_v'), col('ml_o'), col('ml_z'),
             col('gla_v'), col('gla_z'), col('mla_z'), col('gla_q'), col('gla_k'),
             col('mla_cq'), col('mla_kr'), zeros(64),
             col('mla_ckv'), col('gla_af'), col('gla_ab'), zeros(96), col('ml_if'), zeros(112)]
    out = jnp.concatenate(parts, axis=-1).astype(BF16)
    assert out.shape[-1] == P_WIDTH
    return out


def _rope_tables(rows, n_ctx):
    quarter = ROT_DIM // 4
    inv_freq = ROPE_BASE ** (-jnp.arange(quarter, dtype=F32) / quarter)
    row = jnp.repeat(jnp.arange(rows, dtype=F32), GRID_W)
    col = jnp.tile(jnp.arange(GRID_W, dtype=F32), rows)
    ar = row[:, None] * inv_freq
    ac = col[:, None] * inv_freq
    ang = jnp.concatenate([ar, ar, ac, ac], axis=-1)
    cos = jnp.concatenate([jnp.ones((n_ctx, ROT_DIM), F32), jnp.cos(ang)], axis=0)
    sin = jnp.concatenate([jnp.zeros((n_ctx, ROT_DIM), F32), jnp.sin(ang)], axis=0)
    zero = jnp.zeros_like(cos)
    both = jnp.stack([jnp.tile(cos, (1, 2)), jnp.tile(sin, (1, 2))])
    half = jnp.stack([jnp.concatenate([cos, zero], -1), jnp.concatenate([sin, zero], -1)])
    return half, both


def _pad_lanes(v, n):
    return jnp.concatenate([v, jnp.zeros(v.shape[:-1] + (n - v.shape[-1],), v.dtype)], axis=-1)


def kernel(x, c, ctx, c_ctx, ada_w, ada_b, norm_g, w_in, mla_cq_g, mla_ckv_g, mla_wuq, mla_wukv, mla_q_g,
           mla_k_g, gla_a_w, gla_a_b, gla_out_g, ml_conv_w, ml_conv_b, ml_wq, ml_wk, ml_gate_b, ml_out_g,
           ml_skip, df_qk_g, df_lambda, df_out_g, br_w, w_out):
    B, S, D = x.shape
    n_ctx = ctx.shape[1]
    L = ada_w.shape[0]
    assert D == D_MODEL and n_ctx == ROW_TILE and S % ROW_TILE == 0 and S % GRID_W == 0

    rope_half, rope_both = _rope_tables(S // GRID_W, n_ctx)
    w_in_p = _layout_w_in(w_in)

    n_rows = -(-(B + 1) // 8) * 8
    cc = jnp.concatenate([c, c_ctx[None], jnp.zeros((n_rows - B - 1, D), F32)], axis=0)
    mod_all = _modulation(cc, ada_w, ada_b)

    xs = jnp.concatenate([ctx, x], axis=1)
    for l in range(L):
        last = l == L - 1
        lam_init = 0.8 - 0.6 * math.exp(-0.3 * l)
        m3 = mod_all[l].reshape(n_rows, 3, D)
        lat, cx = m3[:B], jnp.broadcast_to(m3[B][None], (B, 3, D))
        mod = jnp.concatenate([cx, lat, jnp.zeros((B, 2, D), F32)], axis=1)
        gates = jnp.stack([cx[:, 2], lat[:, 2]], axis=1)

        p = _in_projection(xs, mod, norm_g[l], w_in_p[l], n_ctx)

        wq4 = mla_wuq[l].reshape(MLA_Q_LORA, HEADS, MLA_NOPE + ROT_DIM)
        wuq = _pad_lanes(wq4, 256).reshape(MLA_Q_LORA, HEADS * 256).astype(BF16)
        wkv4 = mla_wukv[l].reshape(MLA_KV_LORA, HEADS, MLA_NOPE + HEAD_V)
        wukv = jnp.concatenate([wkv4[..., :MLA_NOPE].reshape(MLA_KV_LORA, -1),
                                wkv4[..., MLA_NOPE:].reshape(MLA_KV_LORA, -1)], axis=-1).astype(BF16)
        qg = jnp.stack([mla_q_g[l, :MLA_NOPE], _pad_lanes(mla_q_g[l, MLA_NOPE:], 128)])
        kg = jnp.stack([mla_k_g[l, :MLA_NOPE], _pad_lanes(mla_k_g[l, MLA_NOPE:], 128)])
        y_mla = _mla(p, wuq, wukv, mla_cq_g[l][None], mla_ckv_g[l][None], qg, kg, rope_half, n_ctx, last)

        qkg = jnp.tile(df_qk_g[l], (1, 2))
        y_df = _diff(p, qkg, df_lambda[l], df_out_g[l][None], rope_both, n_ctx, last, lam_init)

        aw = jnp.zeros((128, 512), F32)
        aw = aw.at[0:16, 0:256].set(gla_a_w[l, 0]).at[16:32, 256:512].set(gla_a_w[l, 1]).astype(BF16)
        y_gla = _gla(p, aw, gla_a_b[l].reshape(1, 512), gla_out_g[l][None])

        wq_bd = jnp.zeros((BRANCH_WIDTH, HEADS * ML_DQK), F32)
        wk_bd = jnp.zeros((BRANCH_WIDTH, HEADS * ML_DQK), F32)
        for h in range(HEADS):
            wq_bd = wq_bd.at[128 * h:128 * h + 128, 64 * h:64 * h + 64].set(ml_wq[l, h])
            wk_bd = wk_bd.at[128 * h:128 * h + 128, 64 * h:64 * h + 64].set(ml_wk[l, h])
        gb = _pad_lanes(ml_gate_b[l].reshape(1, 16), 128)
        y_ml = _mlstm(p, ml_conv_w[l], ml_conv_b[l][None], wq_bd.astype(BF16), wk_bd.T.astype(BF16), gb,
                      ml_out_g[l][None], ml_skip[l][None], n_ctx)

        xs = _merge((y_mla, y_gla, y_ml, y_df), p, br_w[l].astype(BF16), w_out[l].astype(BF16), xs, gates,
                    n_ctx, last)
    return xs
```

```python
import functools
import math

import jax
import jax.numpy as jnp
from jax import lax
from jax.experimental import pallas as pl
from jax.experimental.pallas import tpu as pltpu

F32 = jnp.float32
BF16 = jnp.bfloat16

D_MODEL = 1024
GRID_W = 64
EPS = 1e-6
ROPE_BASE = 10000.0
ROT_DIM = 64
CHUNK = 64
N_BRANCH = 4
BRANCH_WIDTH = 512
HEADS = 4
HEAD_V = BRANCH_WIDTH // HEADS
MLA_NOPE = 128
MLA_Q_LORA = 384
MLA_KV_LORA = 256
GLA_DK = 64
GLA_GATE_RANK = 16
GLA_TAU = 16.0
ML_DQK = 64
DF_DQK = 64

IN_SPLITS = (
    ('mla_cq', 384), ('mla_ckv', 256), ('mla_kr', 64), ('mla_z', 512),
    ('gla_q', 256), ('gla_k', 256), ('gla_v', 512), ('gla_af', 16), ('gla_ab', 16), ('gla_z', 512),
    ('ml_x', 512), ('ml_v', 512), ('ml_o', 512), ('ml_if', 16), ('ml_z', 512),
    ('df_q', 512), ('df_k', 512), ('df_v', 512), ('df_z', 512),
    ('merge', 4096),
)

P_MERGE = 0
P_DF_KV = 4096
P_DF_Q = 5120
P_DF_Z = 5632
P_ML = 6144
P_GLA_VZ = 8192
P_MLA_Z = 9216
P_GLA_QK = 9728
P_MLA_A = 10240
P_MLA_CKV = 10752
P_GLA_G = 11008
P_ML_IF = 11136
P_WIDTH = 11264

ROW_TILE = 256
IN_PROJ_COLS = 1024
MERGE_ROWS = 768
ML_CHUNK = 256
ATT_ROWS = 512
VMEM_LIMIT = 56 * 1024 * 1024
EXP_CLAMP = 80.0
GLA_SUB = 16


def _dot(a, b):
    return jnp.dot(a, b, preferred_element_type=F32)


def _dot_nt(a, b):
    return lax.dot_general(a, b, (((1,), (1,)), ((), ())), preferred_element_type=F32)


def _dot_tn(a, b):
    return lax.dot_general(a, b, (((0,), (0,)), ((), ())), preferred_element_type=F32)


def _sigmoid(x):
    return 1.0 / (1.0 + jnp.exp(-x))


def _silu(x):
    return x * _sigmoid(x)


def _log_sigmoid(x):
    return jnp.minimum(x, 0.0) - jnp.log(1.0 + jnp.exp(-jnp.abs(x)))


def _rms(x, n):
    return x * lax.rsqrt(jnp.sum(x * x, axis=-1, keepdims=True) * (1.0 / n) + EPS)


def _split(x):
    hi = x.astype(BF16)
    return hi, (x - hi.astype(F32)).astype(BF16)


def _group_sum(x, group):
    k = x.shape[-1]
    row = lax.broadcasted_iota(jnp.int32, (k, 128), 0)
    col = lax.broadcasted_iota(jnp.int32, (k, 128), 1)
    sel = jnp.ones((k, 128), BF16) if group is None else jnp.where(row // group == col // group, 1.0, 0.0).astype(BF16)
    hi, lo = _split(x)
    return _dot(hi, sel) + _dot(lo, sel)


def _rms_mxu(x, n, group=None):
    inv = lax.rsqrt(_group_sum(x * x, group) * (1.0 / n) + EPS)
    return x * (inv if x.shape[-1] == 128 else jnp.tile(inv, (1, x.shape[-1] // 128)))


def _rot_matrix():
    src = lax.broadcasted_iota(jnp.int32, (128, 128), 0)
    dst = lax.broadcasted_iota(jnp.int32, (128, 128), 1)
    even = (dst // 16) % 2 == 0
    return jnp.where(even & (src == dst + 16), -1.0, jnp.where(~even & (src == dst - 16), 1.0, 0.0)).astype(BF16)


def _rope_mxu(x, cos, sin, rot):
    hi, lo = _split(x)
    return x * cos + (_dot(hi, rot) + _dot(lo, rot)) * sin


def _ones_col(rows):
    return jnp.where(lax.broadcasted_iota(jnp.int32, (rows, HEAD_V), 1) == 0, 1.0, 0.0).astype(BF16)


def _softmax_pv(s, v_ext):
    e = jnp.exp((s - jnp.max(s, axis=-1, keepdims=True)).astype(BF16))
    o = _dot(e, v_ext)
    return o[:, 0:HEAD_V] / o[:, HEAD_V:HEAD_V + 1]


def _params(*sem):
    return pltpu.CompilerParams(dimension_semantics=sem, vmem_limit_bytes=VMEM_LIMIT)


def _mod_kernel(c_ref, w_ref, b_ref, o_ref):
    s = _silu(c_ref[...])
    o_ref[0] = _dot(s.astype(BF16), w_ref[0].astype(BF16)) + b_ref[0]


def _modulation(cc, ada_w, ada_b):
    L, D, _ = ada_w.shape
    R = cc.shape[0]
    return pl.pallas_call(
        _mod_kernel,
        grid=(L, 3),
        in_specs=[pl.BlockSpec((R, D), lambda l, j: (0, 0)),
                  pl.BlockSpec((1, D, D), lambda l, j: (l, 0, j)),
                  pl.BlockSpec((1, 1, D), lambda l, j: (l, 0, j))],
        out_specs=pl.BlockSpec((1, R, D), lambda l, j: (l, 0, j)),
        out_shape=jax.ShapeDtypeStruct((L, R, 3 * D), F32),
        compiler_params=_params("arbitrary", "arbitrary"),
        name="modulation",
    )(cc, ada_w, ada_b.reshape(L, 1, 3 * D))


def _inproj_kernel(x_ref, mod_ref, g_ref, w_ref, o_ref, h_scr, *, n_ctx):
    T = x_ref.shape[1]

    @pl.when(pl.program_id(1) == 0)
    def _():
        g = g_ref[...]
        for r0 in range(0, T, ROW_TILE):
            k = 0 if r0 < n_ctx else 3
            y = _rms(x_ref[0, r0:r0 + ROW_TILE, :], D_MODEL) * g
            h = y * (1.0 + mod_ref[0, k + 1:k + 2, :]) + mod_ref[0, k:k + 1, :]
            h_scr[r0:r0 + ROW_TILE, :] = h.astype(BF16)

    for r0 in range(0, T, ROW_TILE):
        o_ref[0, r0:r0 + ROW_TILE, :] = _dot(h_scr[r0:r0 + ROW_TILE, :], w_ref[0]).astype(BF16)


def _in_projection(xs, mod, norm_g, w_all, layer, n_ctx):
    B, T, D = xs.shape
    n_col = P_WIDTH // IN_PROJ_COLS
    return pl.pallas_call(
        functools.partial(_inproj_kernel, n_ctx=n_ctx),
        grid=(B, n_col),
        in_specs=[pl.BlockSpec((1, T, D), lambda b, j: (b, 0, 0)),
                  pl.BlockSpec((1, 8, D), lambda b, j: (b, 0, 0)),
                  pl.BlockSpec((1, D), lambda b, j: (0, 0)),
                  pl.BlockSpec((1, D, IN_PROJ_COLS), lambda b, j: (layer, 0, j))],
        out_specs=pl.BlockSpec((1, T, IN_PROJ_COLS), lambda b, j: (b, 0, j)),
        out_shape=jax.ShapeDtypeStruct((B, T, P_WIDTH), BF16),
        scratch_shapes=[pltpu.VMEM((T, D), BF16)],
        compiler_params=_params("arbitrary", "arbitrary"),
        name="in_projection",
    )(xs, mod, norm_g.reshape(1, D), w_all)


def _latent_rows(T, n_ctx):
    return ATT_ROWS if (T - n_ctx) % ATT_ROWS == 0 else ROW_TILE


def _attend_all(attend, scores0, s0_scr, o_ref, T, n_ctx, with_ctx):
    if with_ctx:
        attend(pl.ds(0, n_ctx), n_ctx, None, None)
    else:
        o_ref[0, 0:n_ctx, :] = jnp.zeros((n_ctx, o_ref.shape[2]), o_ref.dtype)
    rows_per = s0_scr.shape[0]
    n = (T - n_ctx) // rows_per

    def rows_of(i):
        return pl.ds(pl.multiple_of(n_ctx + i * rows_per, ROW_TILE), rows_per)

    s0_scr[...] = scores0(rows_of(0))

    def latent(i, carry):
        attend(rows_of(i), T, s0_scr[...], rows_of(jnp.minimum(i + 1, n - 1)))
        return carry

    lax.fori_loop(0, n, latent, 0)


def _mla_kernel(pa_ref, pc_ref, z_ref, wuq_ref, wukv_ref, cqg_ref, ckvg_ref, qg_ref, kg_ref, rope_ref,
                o_ref, q_scr, k_scr, v_scr, s0_scr, *, n_ctx, with_ctx):
    T = pa_ref.shape[1]
    scale = (MLA_NOPE + ROT_DIM) ** -0.5
    rot = _rot_matrix()

    def prep(i, carry):
        r = pl.multiple_of(i * ROW_TILE, ROW_TILE)
        rows = pl.ds(r, ROW_TILE)
        cos, sin = rope_ref[0, rows, :], rope_ref[1, rows, :]
        pa = pa_ref[0, rows, :].astype(F32)
        cq = (_rms(pa[:, :MLA_Q_LORA], MLA_Q_LORA) * cqg_ref[...]).astype(BF16)
        q = _dot(cq, wuq_ref[...])
        kr = _rms(pa[:, MLA_Q_LORA:], ROT_DIM) * kg_ref[1:2, :]
        kr = _rope_mxu(kr, cos, sin, rot).astype(BF16)
        ckv = (_rms(pc_ref[0, rows, :].astype(F32), MLA_KV_LORA) * ckvg_ref[...]).astype(BF16)
        kv = _dot(ckv, wukv_ref[...])
        for h in range(HEADS):
            qn = _rms(q[:, 256 * h:256 * h + 128], MLA_NOPE) * qg_ref[0:1, :]
            qr = _rms(q[:, 256 * h + 128:256 * h + 256], ROT_DIM) * qg_ref[1:2, :]
            qr = _rope_mxu(qr, cos, sin, rot)
            q_scr[rows, 256 * h:256 * h + 128] = (qn * scale).astype(BF16)
            q_scr[rows, 256 * h + 128:256 * h + 256] = (qr * scale).astype(BF16)
            kn = _rms(kv[:, 128 * h:128 * h + 128], MLA_NOPE) * kg_ref[0:1, :]
            k_scr[rows, 256 * h:256 * h + 128] = kn.astype(BF16)
            k_scr[rows, 256 * h + 128:256 * h + 256] = kr
            v_scr[rows, 256 * h:256 * h + 128] = kv[:, 512 + 128 * h:512 + 128 * h + 128].astype(BF16)
            v_scr[rows, 256 * h + 128:256 * h + 256] = _ones_col(ROW_TILE)
        return carry

    lax.fori_loop(0, T // ROW_TILE, prep, 0)

    def scores(h, rows, nk):
        return _dot_nt(q_scr[rows, 256 * h:256 * h + 256], k_scr[0:nk, 256 * h:256 * h + 256])

    def attend(rows, nk, s_first, next_rows):
        ahead = HEADS if next_rows is None else 1
        pending = [scores(0, rows, nk) if s_first is None else s_first]
        pending += [scores(h, rows, nk) for h in range(1, ahead)]
        for h in range(HEADS):
            s = pending.pop(0)
            if h + ahead < HEADS:
                pending.append(scores(h + ahead, rows, nk))
            elif h + ahead == HEADS and next_rows is not None:
                s0_scr[...] = scores(0, next_rows, nk)
            o = _softmax_pv(s, v_scr[0:nk, 256 * h:256 * h + 256])
            z = z_ref[0, rows, 128 * h:128 * h + 128].astype(F32)
            o_ref[0, rows, 128 * h:128 * h + 128] = (o * _silu(z)).astype(BF16)

    _attend_all(attend, lambda rows: scores(0, rows, T), s0_scr, o_ref, T, n_ctx, with_ctx)


def _mla(p, wuq, wukv, cqg, ckvg, qg, kg, rope, n_ctx, last):
    B, T, _ = p.shape
    full = lambda shape: pl.BlockSpec(shape, lambda b: (0,) * len(shape))
    return pl.pallas_call(
        functools.partial(_mla_kernel, n_ctx=n_ctx, with_ctx=not last),
        grid=(B,),
        in_specs=[pl.BlockSpec((1, T, 512), lambda b: (b, 0, P_MLA_A // 512)),
                  pl.BlockSpec((1, T, 256), lambda b: (b, 0, P_MLA_CKV // 256)),
                  pl.BlockSpec((1, T, 512), lambda b: (b, 0, P_MLA_Z // 512)),
                  full(wuq.shape), full(wukv.shape), full(cqg.shape), full(ckvg.shape),
                  full(qg.shape), full(kg.shape), full(rope.shape)],
        out_specs=pl.BlockSpec((1, T, 512), lambda b: (b, 0, 0)),
        out_shape=jax.ShapeDtypeStruct((B, T, BRANCH_WIDTH), BF16),
        scratch_shapes=[pltpu.VMEM((T, 1024), BF16), pltpu.VMEM((T, 1024), BF16),
                        pltpu.VMEM((T, 1024), BF16), pltpu.VMEM((_latent_rows(T, n_ctx), T), F32)],
        compiler_params=_params("arbitrary"),
        name="mla",
    )(p, p, p, wuq, wukv, cqg, ckvg, qg, kg, rope)


def _diff_kernel(kv_ref, q_ref, z_ref, g_ref, lam_ref, og_ref, rope_ref, o_ref, q0_scr, q1_scr, k_scr, v_scr, s0_scr,
                 *, n_ctx, with_ctx, lam_init):
    T = kv_ref.shape[1]
    scale = DF_DQK ** -0.5
    lo_mask = lax.broadcasted_iota(jnp.int32, (1, 128), 1) < 64
    rot = _rot_matrix()

    def prep(i, carry):
        r = pl.multiple_of(i * ROW_TILE, ROW_TILE)
        rows = pl.ds(r, ROW_TILE)
        cos, sin = rope_ref[0, rows, :], rope_ref[1, rows, :]
        for h in range(HEADS):
            cols = slice(128 * h, 128 * h + 128)
            q = _rms_mxu(q_ref[0, rows, cols].astype(F32), DF_DQK, DF_DQK) * g_ref[0:1, :]
            q = _rope_mxu(q, cos, sin, rot) * scale
            q0_scr[rows, cols] = jnp.where(lo_mask, q, 0.0).astype(BF16)
            q1_scr[rows, cols] = jnp.where(lo_mask, 0.0, q).astype(BF16)
            k = _rms_mxu(kv_ref[0, rows, cols].astype(F32), DF_DQK, DF_DQK) * g_ref[1:2, :]
            k_scr[rows, cols] = _rope_mxu(k, cos, sin, rot).astype(BF16)
            v_scr[rows, 256 * h:256 * h + 128] = kv_ref[0, rows, 512 + 128 * h:512 + 128 * h + 128]
            v_scr[rows, 256 * h + 128:256 * h + 256] = _ones_col(ROW_TILE)
        return carry

    lax.fori_loop(0, T // ROW_TILE, prep, 0)

    lp = lam_ref[...]
    lam = (jnp.exp(jnp.sum(lp[0:1] * lp[1:2], axis=-1, keepdims=True))
           - jnp.exp(jnp.sum(lp[2:3] * lp[3:4], axis=-1, keepdims=True)) + lam_init)
    def scores(u, rows, nk):
        cols = slice(128 * (u // 2), 128 * (u // 2) + 128)
        return _dot_nt((q1_scr if u % 2 else q0_scr)[rows, cols], k_scr[0:nk, cols])

    def attend(rows, nk, s_first, next_rows):
        n_units = 2 * HEADS
        ahead = n_units if next_rows is None else 1
        pending = [scores(0, rows, nk) if s_first is None else s_first]
        pending += [scores(u, rows, nk) for u in range(1, ahead)]
        for h in range(HEADS):
            cols = slice(128 * h, 128 * h + 128)
            v_ext = v_scr[0:nk, 256 * h:256 * h + 256]
            parts = []
            for mp in range(2):
                u = 2 * h + mp
                s = pending.pop(0)
                if u + ahead < n_units:
                    pending.append(scores(u + ahead, rows, nk))
                elif u + ahead == n_units and next_rows is not None:
                    s0_scr[...] = scores(0, next_rows, nk)
                parts.append(_softmax_pv(s, v_ext))
            o = parts[0] - lam * parts[1]
            o = _rms(o, HEAD_V) * og_ref[...] * (1.0 - lam_init)
            z = z_ref[0, rows, cols].astype(F32)
            o_ref[0, rows, cols] = (o * _silu(z)).astype(BF16)

    _attend_all(attend, lambda rows: scores(0, rows, T), s0_scr, o_ref, T, n_ctx, with_ctx)


def _diff(p, qkg, lam_p, og, rope, n_ctx, last, lam_init):
    B, T, _ = p.shape
    full = lambda shape: pl.BlockSpec(shape, lambda b: (0,) * len(shape))
    return pl.pallas_call(
        functools.partial(_diff_kernel, n_ctx=n_ctx, with_ctx=not last, lam_init=lam_init),
        grid=(B,),
        in_specs=[pl.BlockSpec((1, T, 1024), lambda b: (b, 0, P_DF_KV // 1024)),
                  pl.BlockSpec((1, T, 512), lambda b: (b, 0, P_DF_Q // 512)),
                  pl.BlockSpec((1, T, 512), lambda b: (b, 0, P_DF_Z // 512)),
                  full(qkg.shape), full(lam_p.shape), full(og.shape), full(rope.shape)],
        out_specs=pl.BlockSpec((1, T, 512), lambda b: (b, 0, 0)),
        out_shape=jax.ShapeDtypeStruct((B, T, BRANCH_WIDTH), BF16),
        scratch_shapes=[pltpu.VMEM((T, 512), BF16), pltpu.VMEM((T, 512), BF16), pltpu.VMEM((T, 512), BF16),
                        pltpu.VMEM((T, 1024), BF16), pltpu.VMEM((_latent_rows(T, n_ctx), T), F32)],
        compiler_params=_params("arbitrary"),
        name="diff_attn",
    )(p, p, p, qkg, lam_p, og, rope)


def _tile_order(i, n_tiles, rev):
    if not rev:
        return i
    return jnp.where(i == 0, 0, n_tiles - i)


def _chunk_tri(n, rev, chunk=CHUNK):
    r = lax.broadcasted_iota(jnp.int32, (n, n), 0)
    c = lax.broadcasted_iota(jnp.int32, (n, n), 1)
    same = (r // chunk) == (c // chunk)
    tri = (c >= r) if rev else (c <= r)
    return jnp.where(same & tri, 1.0, 0.0).astype(F32)


def _causal(rev, n=CHUNK):
    t = lax.broadcasted_iota(jnp.int32, (n, n), 0)
    s = lax.broadcasted_iota(jnp.int32, (n, n), 1)
    return (s >= t) if rev else (s <= t)


def _cummax_rows(x, rev):
    n = x.shape[0]
    row = lax.broadcasted_iota(jnp.int32, (n, 1), 0)
    s = 1
    while s < n:
        if rev:
            x = jnp.maximum(x, jnp.where(row < n - s, pltpu.roll(x, n - s, 0), -jnp.inf))
        else:
            x = jnp.maximum(x, jnp.where(row >= s, pltpu.roll(x, s, 0), -jnp.inf))
        s *= 2
    return x


def _head_mask(width, group):
    lane = lax.broadcasted_iota(jnp.int32, (1, width), 1) // group
    return [lane == h for h in range(HEADS)]


def _gla_kernel(qk_ref, vz_ref, g_ref, aw_ref, ab_ref, og_ref, o_ref, of_scr, ob_scr, st_scr):
    T = qk_ref.shape[1]
    n_tiles = T // ROW_TILE
    n_chunk = ROW_TILE // CHUNK
    hm = _head_mask(HEADS * GLA_DK, GLA_DK)
    tris = (_chunk_tri(ROW_TILE, False), _chunk_tri(ROW_TILE, True))
    st_scr[...] = jnp.zeros_like(st_scr)

    def load_tile(t, rev):
        rows = pl.ds(pl.multiple_of(t * ROW_TILE, ROW_TILE), ROW_TILE)
        gcol = 256 if rev else 0
        qk = qk_ref[0, rows, :].astype(F32)
        pre = _dot(g_ref[0, rows, :], aw_ref[:, gcol:gcol + 256]) + ab_ref[:, gcol:gcol + 256]
        la = _log_sigmoid(pre) * (1.0 / GLA_TAU)
        hi, lo = _split(la)
        tri = tris[rev].astype(BF16)
        return qk[:, :256] * (GLA_DK ** -0.5), qk[:, 256:], _dot(tri, hi) + _dot(tri, lo)

    def tile(i, carry):
        tiles = (i, _tile_order(i, n_tiles, True))
        steps = []
        for cc in range(n_chunk):
            steps += [(0, cc), (1, n_chunk - 1 - cc)]
        data = [load_tile(tiles[d], bool(d)) for d in (0, 1)]
        n_sub = CHUNK // GLA_SUB
        pre, a_blk, upd, a_msk, intra, o_inter = {}, {}, {}, {}, {}, {}
        st = [st_scr[0], st_scr[1]]

        def prepare(d, cc):
            q, k, b = data[d]
            cr = slice(CHUNK * cc, CHUNK * cc + CHUNK)
            bc, qc, kc = b[cr], q[cr], k[cr]
            b_end = bc[0:1] if d else bc[CHUNK - 1:CHUNK]
            crow = pl.ds(pl.multiple_of(tiles[d] * ROW_TILE + CHUNK * cc, CHUNK), CHUNK)
            q_sub, k_sub = [], []
            for i in range(CHUNK // GLA_SUB):
                sr = slice(GLA_SUB * i, GLA_SUB * i + GLA_SUB)
                first = GLA_SUB * i + (GLA_SUB - 1 if d else 0)
                beta = bc[first:first + 1]
                q_i = qc[sr] * jnp.exp(bc[sr] - beta)
                q_sub.append(jnp.concatenate([jnp.where(hm[h], q_i, 0.0) for h in range(HEADS)],
                                             axis=0).astype(BF16))
                k_sub.append((kc * jnp.exp(jnp.minimum(beta - bc, EXP_CLAMP))).astype(BF16))
            qe = qc * jnp.exp(bc)
            stacked = jnp.concatenate([kc * jnp.exp(b_end - bc), jnp.broadcast_to(jnp.exp(b_end), (8, 256)),
                                       jnp.zeros((CHUNK - 8, 256), F32)], axis=0).T
            pre[d, cc] = dict(
                crow=crow, vc=vz_ref[0, crow, 0:BRANCH_WIDTH], decay=stacked[:, CHUNK:CHUNK + 1],
                qe=[jnp.where(hm[h], qe, 0.0).astype(BF16) for h in range(HEADS)], q_sub=q_sub, k_sub=k_sub,
                k_up_t=stacked[:, 0:CHUNK].astype(BF16))

        def score_dots(d, cc):
            p = pre[d, cc]
            a_blk[d, cc] = [_dot_nt(p['q_sub'][i], p['k_sub'][i]) for i in range(n_sub)]
            upd[d, cc] = jnp.concatenate(
                [_dot(p['k_up_t'][GLA_DK * h:GLA_DK * h + GLA_DK], p['vc'][:, 128 * h:128 * h + 128])
                 for h in range(HEADS)], axis=0)

        def mask_scores(d, cc):
            a_msk[d, cc] = [
                jnp.where(_causal(bool(d)), jnp.concatenate(
                    [a_blk[d, cc][i][GLA_SUB * h:GLA_SUB * h + GLA_SUB] for i in range(n_sub)], axis=0),
                    0.0).astype(BF16) for h in range(HEADS)]

        def value_dots(d, cc):
            vc = pre[d, cc]['vc']
            intra[d, cc] = jnp.concatenate(
                [_dot(a_msk[d, cc][h], vc[:, 128 * h:128 * h + 128]) for h in range(HEADS)], axis=-1)

        def state_dot(d, cc):
            s_bf = st[d].astype(BF16)
            o_inter[d, cc] = jnp.concatenate([_dot(pre[d, cc]['qe'][h], s_bf) for h in range(HEADS)], axis=-1)

        def finish_step(d, cc):
            (ob_scr if d else of_scr)[pre[d, cc]['crow'], :] = o_inter[d, cc] + intra[d, cc]
            st[d] = st[d] * pre[d, cc]['decay'] + upd[d, cc]

        def at(t):
            return [steps[t]] if 0 <= t < len(steps) else []

        for t in range(len(steps) + 4):
            for g in at(t - 1):
                score_dots(*g)
            for g in at(t - 3):
                value_dots(*g)
            for g in at(t - 4):
                state_dot(*g)
            for g in at(t):
                prepare(*g)
            for g in at(t - 2):
                mask_scores(*g)
            for g in at(t - 4):
                finish_step(*g)
        st_scr[0] = st[0]
        st_scr[1] = st[1]
        return carry

    lax.fori_loop(0, n_tiles, tile, 0)

    def finish(i, carry):
        rows = pl.ds(pl.multiple_of(i * ROW_TILE, ROW_TILE), ROW_TILE)
        tot = of_scr[rows, :] + ob_scr[rows, :]
        z = vz_ref[0, rows, BRANCH_WIDTH:].astype(F32)
        for h in range(HEADS):
            cols = slice(128 * h, 128 * h + 128)
            y = _rms(tot[:, cols], HEAD_V) * og_ref[:, cols]
            o_ref[0, rows, cols] = (y * _silu(z[:, cols])).astype(BF16)
        return carry

    lax.fori_loop(0, n_tiles, finish, 0)


def _gla(p, aw, ab, og):
    B, T, _ = p.shape
    full = lambda shape: pl.BlockSpec(shape, lambda b: (0,) * len(shape))
    return pl.pallas_call(
        _gla_kernel,
        grid=(B,),
        in_specs=[pl.BlockSpec((1, T, 512), lambda b: (b, 0, P_GLA_QK // 512)),
                  pl.BlockSpec((1, T, 1024), lambda b: (b, 0, P_GLA_VZ // 1024)),
                  pl.BlockSpec((1, T, 128), lambda b: (b, 0, P_GLA_G // 128)),
                  full(aw.shape), full(ab.shape), full(og.shape)],
        out_specs=pl.BlockSpec((1, T, 512), lambda b: (b, 0, 0)),
        out_shape=jax.ShapeDtypeStruct((B, T, BRANCH_WIDTH), BF16),
        scratch_shapes=[pltpu.VMEM((T, BRANCH_WIDTH), F32), pltpu.VMEM((T, BRANCH_WIDTH), F32),
                        pltpu.VMEM((2, HEADS * GLA_DK, HEAD_V), F32)],
        compiler_params=_params("arbitrary"),
        name="gla_scan",
    )(p, p, p, aw, ab, og)


def _mlstm_kernel(m_ref, if_ref, cw_ref, cb_ref, wq_ref, wkt_ref, gb_ref, og_ref, sk_ref, o_ref,
                  xc_scr, q_scr, kt_scr, at_scr, bt_scr, cm_scr, b_scr, hf_scr, hb_scr, c_scr, *, n_ctx):
    T = m_ref.shape[1]
    n_tiles = T // ROW_TILE
    assert ML_CHUNK == ROW_TILE
    n_chunk = 1
    ctx_tiles = n_ctx // ROW_TILE
    hm = _head_mask(HEADS * ML_DQK, ML_DQK)
    lane128 = lax.broadcasted_iota(jnp.int32, (1, 128), 1)
    is_forget = ((lane128 // HEADS) % 2) == 1
    row_in_tile = lax.broadcasted_iota(jnp.int32, (ROW_TILE, 1), 0)
    tri_f = _chunk_tri(ROW_TILE, False, ML_CHUNK).astype(BF16)
    tri_b = _chunk_tri(ROW_TILE, True, ML_CHUNK).astype(BF16)

    def prep(i, carry):
        r = pl.multiple_of(i * ROW_TILE, ROW_TILE)
        rows = pl.ds(r, ROW_TILE)
        x = m_ref[0, rows, 0:512].astype(F32)
        rp = pl.multiple_of(jnp.maximum(r - 16, 0), 16)
        rn = pl.multiple_of(jnp.minimum(r + ROW_TILE, T - 16), 16)
        prev_row = m_ref[0, pl.ds(rp, 16), 0:512].astype(F32)[15:16]
        next_row = m_ref[0, pl.ds(rn, 16), 0:512].astype(F32)[0:1]
        seg_start = (i == 0) | (i == ctx_tiles)
        seg_end = (i == ctx_tiles - 1) | (i == n_tiles - 1)
        prev_row = jnp.where(seg_start, 0.0, prev_row)
        next_row = jnp.where(seg_end, 0.0, next_row)
        xm = jnp.where(row_in_tile == 0, prev_row, pltpu.roll(x, 1, 0))
        xp = jnp.where(row_in_tile == ROW_TILE - 1, next_row, pltpu.roll(x, ROW_TILE - 1, 0))
        xc = _silu(cw_ref[0:1, :] * xm + cw_ref[1:2, :] * x + cw_ref[2:3, :] * xp + cb_ref[...])
        xc_scr[rows, :] = xc
        xb = xc.astype(BF16)
        q_scr[rows, :] = _dot(xb, wq_ref[...]).astype(BF16)
        g = if_ref[0, rows, :].astype(F32) + gb_ref[...]
        g2 = jnp.where(is_forget, _log_sigmoid(g), g)
        g_hi, g_lo = _split(g2)
        cs = jnp.where(lane128 < 2 * HEADS, _dot(tri_f, g_hi) + _dot(tri_f, g_lo),
                       _dot(tri_b, g_hi) + _dot(tri_b, g_lo))
        b = pltpu.roll(cs, 128 - HEADS, 1)
        a = g2 - b
        cm_scr[rows, :] = jnp.where(lane128 < 2 * HEADS, _cummax_rows(a, False), _cummax_rows(a, True))
        b_scr[rows, :] = b
        kt_scr[i] = (_dot_nt(wkt_ref[...], xb) * (ML_DQK ** -0.5)).astype(BF16)
        at_scr[i] = a.T[0:16, :]
        bt_scr[i] = b.T[0:16, :]
        return carry

    lax.fori_loop(0, n_tiles, prep, 0)

    c_scr[...] = jnp.zeros_like(c_scr)
    ones_col = _ones_col(ML_CHUNK)
    sel_r = lax.broadcasted_iota(jnp.int32, (256, HEADS * 128), 0) % 128
    sel_c = lax.broadcasted_iota(jnp.int32, (256, HEADS * 128), 1) // 128
    sel = [jnp.where(sel_r == 2 * HEADS * d + sel_c, 1.0, 0.0).astype(BF16) for d in (0, 1)]

    def col_bcast(x, d):
        hi, lo = _split(x)
        return _dot(jnp.concatenate([hi, lo], axis=-1), sel[d])

    def chunk_pair(c_f, c_b, m_in):
        units = [(d, h) for d in (0, 1) for h in range(HEADS)]
        cs = (c_f, c_b)
        crow = [pl.ds(pl.multiple_of(c * ML_CHUNK, ML_CHUNK), ML_CHUNK) for c in cs]
        qc = [q_scr[crow[d], :] for d in (0, 1)]
        kt = [kt_scr[cs[d]] for d in (0, 1)]
        vc = [m_ref[0, crow[d], 512:1024] for d in (0, 1)]
        at = [at_scr[cs[d]] for d in (0, 1)]
        bt = [bt_scr[cs[d]] for d in (0, 1)]
        c_bf = [c_scr[d].astype(BF16) for d in (0, 1)]
        causal = [_causal(False, ML_CHUNK), _causal(True, ML_CHUNK)]
        last = [ML_CHUNK - 1, 0]
        qh = {(d, h): jnp.where(hm[h], qc[d], jnp.zeros_like(qc[d])) for d, h in units}
        v_ext = {(d, h): jnp.concatenate([vc[d][:, 128 * h:128 * h + 128], ones_col], axis=-1)
                 for d, h in units}
        s_raw = {u: _dot(qh[u], kt[u[0]]) for u in units}
        q_c = {u: _dot(qh[u], c_bf[u[0]]) for u in units}
        cm_col = [col_bcast(cm_scr[crow[d], :], d) for d in (0, 1)]
        b_col = [col_bcast(b_scr[crow[d], :], d) for d in (0, 1)]
        s_w, m_run, m_last, ktw = {}, {}, {}, {}
        for d, h in units:
            j = 2 * HEADS * d + h
            a_row, m_old = at[d][j:j + 1, :], m_in[HEADS * d + h]
            pm = jnp.where(causal[d], a_row, -jnp.inf)
            m_run[d, h] = jnp.maximum(m_old, cm_col[d][:, 128 * h:128 * h + 128])
            m_wide = jnp.concatenate([m_run[d, h]] * (ML_CHUNK // 128), axis=-1)
            s_w[d, h] = (s_raw[d, h] * jnp.exp(pm - m_wide)).astype(BF16)
            m_last[d, h] = m_run[d, h][last[d]:last[d] + 1, 0:1]
            k_h = kt[d][ML_DQK * h:ML_DQK * h + ML_DQK, :].astype(F32)
            ktw[d, h] = (k_h * jnp.exp(a_row - m_last[d, h])).astype(BF16)
        s_v = {u: _dot(s_w[u], v_ext[u]) for u in units}
        upd = {u: _dot(ktw[u], v_ext[u]) for u in units}
        m_out = []
        for d in (0, 1):
            parts = []
            for h in range(HEADS):
                j = 2 * HEADS * d + h
                hr = slice(ML_DQK * h, ML_DQK * h + ML_DQK)
                b_row, m_old = bt[d][j:j + 1, :], m_in[HEADS * d + h]
                carry_w = jnp.exp(m_old - m_run[d, h])
                num = jnp.concatenate([carry_w, carry_w], axis=-1) * q_c[d, h] + s_v[d, h]
                den = num[:, HEAD_V:HEAD_V + 1]
                floor = jnp.exp(-(b_col[d][:, 128 * h:128 * h + 128] + m_run[d, h]))
                parts.append(num[:, 0:HEAD_V] / jnp.maximum(jnp.abs(den), floor))
                c_scr[d, hr, :] = jnp.exp(m_old - m_last[d, h]) * c_scr[d, hr, :] + upd[d, h]
                m_out.append(b_row[:, last[d]:last[d] + 1] + m_last[d, h])
            (hb_scr if d else hf_scr)[crow[d], :] = jnp.concatenate(parts, axis=-1)
        return m_out

    def tile(i, carry):
        m = list(carry)
        c_f = i * n_chunk
        c_b = _tile_order(i, n_tiles, True) * n_chunk + (n_chunk - 1)
        for cc in range(n_chunk):
            m = chunk_pair(c_f + cc, c_b - cc, m)
        return tuple(m)

    lax.fori_loop(0, n_tiles, tile, tuple(jnp.zeros((1, 1), F32) for _ in range(2 * HEADS)))

    def finish(i, carry):
        rows = pl.ds(pl.multiple_of(i * ROW_TILE, ROW_TILE), ROW_TILE)
        tot = (hf_scr[rows, :] + hb_scr[rows, :]) * _sigmoid(m_ref[0, rows, 1024:1536].astype(F32))
        z = m_ref[0, rows, 1536:2048].astype(F32)
        xc = xc_scr[rows, :]
        for h in range(HEADS):
            cols = slice(128 * h, 128 * h + 128)
            y = _rms(tot[:, cols], HEAD_V) * og_ref[:, cols]
            y = (y + sk_ref[:, cols] * xc[:, cols]) * _silu(z[:, cols])
            o_ref[0, rows, cols] = y.astype(BF16)
        return carry

    lax.fori_loop(0, n_tiles, finish, 0)


def _mlstm(p, cw, cb, wq, wkt, gb, og, sk, n_ctx):
    B, T, _ = p.shape
    full = lambda shape: pl.BlockSpec(shape, lambda b: (0,) * len(shape))
    return pl.pallas_call(
        functools.partial(_mlstm_kernel, n_ctx=n_ctx),
        grid=(B,),
        in_specs=[pl.BlockSpec((1, T, 2048), lambda b: (b, 0, P_ML // 2048)),
                  pl.BlockSpec((1, T, 128), lambda b: (b, 0, P_ML_IF // 128)),
                  full(cw.shape), full(cb.shape), full(wq.shape), full(wkt.shape), full(gb.shape),
                  full(og.shape), full(sk.shape)],
        out_specs=pl.BlockSpec((1, T, 512), lambda b: (b, 0, 0)),
        out_shape=jax.ShapeDtypeStruct((B, T, BRANCH_WIDTH), BF16),
        scratch_shapes=[pltpu.VMEM((T, BRANCH_WIDTH), F32),
                        pltpu.VMEM((T, HEADS * ML_DQK), BF16),
                        pltpu.VMEM((T // ML_CHUNK, HEADS * ML_DQK, ML_CHUNK), BF16),
                        pltpu.VMEM((T // ML_CHUNK, 16, ML_CHUNK), F32),
                        pltpu.VMEM((T // ML_CHUNK, 16, ML_CHUNK), F32),
                        pltpu.VMEM((T, 128), F32),
                        pltpu.VMEM((T, 128), F32),
                        pltpu.VMEM((T, BRANCH_WIDTH), F32),
                        pltpu.VMEM((T, BRANCH_WIDTH), F32),
                        pltpu.VMEM((2, HEADS * ML_DQK, 2 * HEAD_V), F32)],
        compiler_params=_params("arbitrary"),
        name="mlstm_scan",
    )(p, p, cw, cb, wq, wkt, gb, og, sk)


def _merge_kernel(ya_ref, yb_ref, yc_ref, yd_ref, gl_ref, brw_ref, wo_ref, x_ref, gate_ref, o_ref,
                  *, row0, ctx_rows):
    tm = x_ref.shape[1]
    first = pl.program_id(1) * tm + row0
    acc = None
    for i, y_ref in enumerate((ya_ref, yb_ref, yc_ref, yd_ref)):
        u = _dot(y_ref[0], brw_ref[i])
        gsig = _sigmoid(gl_ref[0, :, D_MODEL * i:D_MODEL * (i + 1)].astype(F32))
        acc = gsig * u if acc is None else acc + gsig * u
    out = _dot(acc.astype(BF16), wo_ref[...])
    for r0 in range(0, tm, ROW_TILE):
        rs = slice(r0, r0 + ROW_TILE)
        gate = jnp.where(first + r0 < ctx_rows, gate_ref[0, 0:1, :], gate_ref[0, 1:2, :])
        o_ref[0, rs, :] = x_ref[0, rs, :] + gate * out[rs]


def _merge(ys, p, brw, wo, xs, gates, n_ctx, last):
    B, T, D = xs.shape
    tm = ROW_TILE if last else MERGE_ROWS
    row0 = n_ctx if last else 0
    tile0 = row0 // tm
    nt = (T - row0) // tm
    assert (T - row0) % tm == 0 and row0 % tm == 0 and n_ctx % ROW_TILE == 0
    ymap = lambda b, t: (b, t + tile0, 0)
    return pl.pallas_call(
        functools.partial(_merge_kernel, row0=row0, ctx_rows=n_ctx),
        grid=(B, nt),
        in_specs=[pl.BlockSpec((1, tm, BRANCH_WIDTH), ymap)] * 4 + [
            pl.BlockSpec((1, tm, N_BRANCH * D), lambda b, t: (b, t + tile0, P_MERGE // (N_BRANCH * D))),
            pl.BlockSpec(brw.shape, lambda b, t: (0, 0, 0)),
            pl.BlockSpec(wo.shape, lambda b, t: (0, 0)),
            pl.BlockSpec((1, tm, D), ymap),
            pl.BlockSpec((1, 2, D), lambda b, t: (b, 0, 0))],
        out_specs=pl.BlockSpec((1, tm, D), lambda b, t: (b, t, 0)),
        out_shape=jax.ShapeDtypeStruct((B, nt * tm, D), F32),
        compiler_params=_params("arbitrary", "arbitrary"),
        name="merge",
    )(*ys, p, brw, wo, xs, gates)


def _layout_w_in(w_in):
    offs, off = {}, 0
    for name, w in IN_SPLITS:
        offs[name] = (off, w)
        off += w

    w_bf = w_in.astype(BF16)

    def col(name):
        o, w = offs[name]
        return w_bf[..., o:o + w]

    def zeros(n):
        return jnp.zeros(w_in.shape[:-1] + (n,), BF16)

    parts = [col('merge'), col('df_k'), col('df_v'), col('df_q'), col('df_z'),
             col('ml_x'), col('ml_v'), col('ml_o'), col('ml_z'),
             col('gla_v'), col('gla_z'), col('mla_z'), col('gla_q'), col('gla_k'),
             col('mla_cq'), col('mla_kr'), zeros(64),
             col('mla_ckv'), col('gla_af'), col('gla_ab'), zeros(96), col('ml_if'), zeros(112)]
    out = jnp.concatenate(parts, axis=-1)
    assert out.shape[-1] == P_WIDTH
    return out


def _rope_tables(rows, n_ctx):
    quarter = ROT_DIM // 4
    inv_freq = ROPE_BASE ** (-jnp.arange(quarter, dtype=F32) / quarter)
    row = jnp.repeat(jnp.arange(rows, dtype=F32), GRID_W)
    col = jnp.tile(jnp.arange(GRID_W, dtype=F32), rows)
    ar = row[:, None] * inv_freq
    ac = col[:, None] * inv_freq
    ang = jnp.concatenate([ar, ar, ac, ac], axis=-1)
    cos = jnp.concatenate([jnp.ones((n_ctx, ROT_DIM), F32), jnp.cos(ang)], axis=0)
    sin = jnp.concatenate([jnp.zeros((n_ctx, ROT_DIM), F32), jnp.sin(ang)], axis=0)
    zero = jnp.zeros_like(cos)
    both = jnp.stack([jnp.tile(cos, (1, 2)), jnp.tile(sin, (1, 2))])
    half = jnp.stack([jnp.concatenate([cos, zero], -1), jnp.concatenate([sin, zero], -1)])
    return half, both


def _pad_lanes(v, n):
    return jnp.concatenate([v, jnp.zeros(v.shape[:-1] + (n - v.shape[-1],), v.dtype)], axis=-1)


def kernel(x, c, ctx, c_ctx, ada_w, ada_b, norm_g, w_in, mla_cq_g, mla_ckv_g, mla_wuq, mla_wukv, mla_q_g,
           mla_k_g, gla_a_w, gla_a_b, gla_out_g, ml_conv_w, ml_conv_b, ml_wq, ml_wk, ml_gate_b, ml_out_g,
           ml_skip, df_qk_g, df_lambda, df_out_g, br_w, w_out):
    B, S, D = x.shape
    n_ctx = ctx.shape[1]
    L = ada_w.shape[0]
    assert D == D_MODEL and n_ctx == ROW_TILE and S % ROW_TILE == 0 and S % GRID_W == 0

    rope_half, rope_both = _rope_tables(S // GRID_W, n_ctx)
    w_in_p = _layout_w_in(w_in)

    n_rows = -(-(B + 1) // 8) * 8
    cc = jnp.concatenate([c, c_ctx[None], jnp.zeros((n_rows - B - 1, D), F32)], axis=0)
    mod_all = _modulation(cc, ada_w, ada_b)

    xs = jnp.concatenate([ctx, x], axis=1)
    for l in range(L):
        last = l == L - 1
        lam_init = 0.8 - 0.6 * math.exp(-0.3 * l)
        m3 = mod_all[l].reshape(n_rows, 3, D)
        lat, cx = m3[:B], jnp.broadcast_to(m3[B][None], (B, 3, D))
        mod = jnp.concatenate([cx, lat, jnp.zeros((B, 2, D), F32)], axis=1)
        gates = jnp.stack([cx[:, 2], lat[:, 2]], axis=1)

        p = _in_projection(xs, mod, norm_g[l], w_in_p, l, n_ctx)

        wq4 = mla_wuq[l].reshape(MLA_Q_LORA, HEADS, MLA_NOPE + ROT_DIM)
        wuq = _pad_lanes(wq4, 256).reshape(MLA_Q_LORA, HEADS * 256).astype(BF16)
        wkv4 = mla_wukv[l].reshape(MLA_KV_LORA, HEADS, MLA_NOPE + HEAD_V)
        wukv = jnp.concatenate([wkv4[..., :MLA_NOPE].reshape(MLA_KV_LORA, -1),
                                wkv4[..., MLA_NOPE:].reshape(MLA_KV_LORA, -1)], axis=-1).astype(BF16)
        qg = jnp.stack([mla_q_g[l, :MLA_NOPE], _pad_lanes(mla_q_g[l, MLA_NOPE:], 128)])
        kg = jnp.stack([mla_k_g[l, :MLA_NOPE], _pad_lanes(mla_k_g[l, MLA_NOPE:], 128)])
        y_mla = _mla(p, wuq, wukv, mla_cq_g[l][None], mla_ckv_g[l][None], qg, kg, rope_half, n_ctx, last)

        qkg = jnp.tile(df_qk_g[l], (1, 2))
        y_df = _diff(p, qkg, df_lambda[l], df_out_g[l][None], rope_both, n_ctx, last, lam_init)

        aw = jnp.zeros((128, 512), F32)
        aw = aw.at[0:16, 0:256].set(gla_a_w[l, 0]).at[16:32, 256:512].set(gla_a_w[l, 1]).astype(BF16)
        y_gla = _gla(p, aw, gla_a_b[l].reshape(1, 512), gla_out_g[l][None])

        wq_bd = jnp.zeros((BRANCH_WIDTH, HEADS * ML_DQK), F32)
        wk_bd = jnp.zeros((BRANCH_WIDTH, HEADS * ML_DQK), F32)
        for h in range(HEADS):
            wq_bd = wq_bd.at[128 * h:128 * h + 128, 64 * h:64 * h + 64].set(ml_wq[l, h])
            wk_bd = wk_bd.at[128 * h:128 * h + 128, 64 * h:64 * h + 64].set(ml_wk[l, h])
        gb = _pad_lanes(ml_gate_b[l].reshape(1, 16), 128)
        y_ml = _mlstm(p, ml_conv_w[l], ml_conv_b[l][None], wq_bd.astype(BF16), wk_bd.T.astype(BF16), gb,
                      ml_out_g[l][None], ml_skip[l][None], n_ctx)

        xs = _merge((y_mla, y_gla, y_ml, y_df), p, br_w[l].astype(BF16), w_out[l].astype(BF16), xs, gates,
                    n_ctx, last)
    return xs
```

```python
import functools
import math

import jax
import jax.numpy as jnp
from jax import lax
from jax.experimental import pallas as pl
from jax.experimental.pallas import tpu as pltpu

F32 = jnp.float32
BF16 = jnp.bfloat16

D_MODEL = 1024
GRID_W = 64
EPS = 1e-6
ROPE_BASE = 10000.0
ROT_DIM = 64
CHUNK = 64
N_BRANCH = 4
BRANCH_WIDTH = 512
HEADS = 4
HEAD_V = BRANCH_WIDTH // HEADS
MLA_NOPE = 128
MLA_Q_LORA = 384
MLA_KV_LORA = 256
GLA_DK = 64
GLA_GATE_RANK = 16
GLA_TAU = 16.0
ML_DQK = 64
DF_DQK = 64

IN_SPLITS = (
    ('mla_cq', 384), ('mla_ckv', 256), ('mla_kr', 64), ('mla_z', 512),
    ('gla_q', 256), ('gla_k', 256), ('gla_v', 512), ('gla_af', 16), ('gla_ab', 16), ('gla_z', 512),
    ('ml_x', 512), ('ml_v', 512), ('ml_o', 512), ('ml_if', 16), ('ml_z', 512),
    ('df_q', 512), ('df_k', 512), ('df_v', 512), ('df_z', 512),
    ('merge', 4096),
)

P_MERGE = 0
P_DF_KV = 4096
P_DF_Q = 5120
P_DF_Z = 5632
P_ML = 6144
P_GLA_VZ = 8192
P_MLA_Z = 9216
P_GLA_QK = 9728
P_MLA_A = 10240
P_MLA_CKV = 10752
P_GLA_G = 11008
P_ML_IF = 11136
P_WIDTH = 11264

ROW_TILE = 256
IN_PROJ_COLS = 1024
MERGE_ROWS = 768
ML_CHUNK = 256
ATT_ROWS = 512
VMEM_LIMIT = 56 * 1024 * 1024
EXP_CLAMP = 80.0
GLA_SUB = 16


def _dot(a, b):
    return jnp.dot(a, b, preferred_element_type=F32)


def _dot_nt(a, b):
    return lax.dot_general(a, b, (((1,), (1,)), ((), ())), preferred_element_type=F32)


def _dot_tn(a, b):
    return lax.dot_general(a, b, (((0,), (0,)), ((), ())), preferred_element_type=F32)


def _sigmoid(x):
    return 1.0 / (1.0 + jnp.exp(-x))


def _silu(x):
    return x * _sigmoid(x)


def _log_sigmoid(x):
    return jnp.minimum(x, 0.0) - jnp.log(1.0 + jnp.exp(-jnp.abs(x)))


def _rms(x, n):
    return x * lax.rsqrt(jnp.sum(x * x, axis=-1, keepdims=True) * (1.0 / n) + EPS)


def _split(x):
    hi = x.astype(BF16)
    return hi, (x - hi.astype(F32)).astype(BF16)


def _group_sum(x, group):
    k = x.shape[-1]
    row = lax.broadcasted_iota(jnp.int32, (k, 128), 0)
    col = lax.broadcasted_iota(jnp.int32, (k, 128), 1)
    sel = jnp.ones((k, 128), BF16) if group is None else jnp.where(row // group == col // group, 1.0, 0.0).astype(BF16)
    hi, lo = _split(x)
    return _dot(hi, sel) + _dot(lo, sel)


def _rms_mxu(x, n, group=None):
    inv = lax.rsqrt(_group_sum(x * x, group) * (1.0 / n) + EPS)
    return x * (inv if x.shape[-1] == 128 else jnp.tile(inv, (1, x.shape[-1] // 128)))


def _rot_matrix():
    src = lax.broadcasted_iota(jnp.int32, (128, 128), 0)
    dst = lax.broadcasted_iota(jnp.int32, (128, 128), 1)
    even = (dst // 16) % 2 == 0
    return jnp.where(even & (src == dst + 16), -1.0, jnp.where(~even & (src == dst - 16), 1.0, 0.0)).astype(BF16)


def _rope_mxu(x, cos, sin, rot):
    hi, lo = _split(x)
    return x * cos + (_dot(hi, rot) + _dot(lo, rot)) * sin


def _ones_col(rows):
    return jnp.where(lax.broadcasted_iota(jnp.int32, (rows, HEAD_V), 1) == 0, 1.0, 0.0).astype(BF16)


def _softmax_pv(s, v_ext):
    e = jnp.exp((s - jnp.max(s, axis=-1, keepdims=True)).astype(BF16))
    o = _dot(e, v_ext)
    return o[:, 0:HEAD_V] / o[:, HEAD_V:HEAD_V + 1]


def _params(*sem):
    return pltpu.CompilerParams(dimension_semantics=sem, vmem_limit_bytes=VMEM_LIMIT)


def _mod_kernel(c_ref, w_ref, b_ref, o_ref):
    s = _silu(c_ref[...])
    o_ref[0] = _dot(s.astype(BF16), w_ref[0].astype(BF16)) + b_ref[0]


def _modulation(cc, ada_w, ada_b):
    L, D, _ = ada_w.shape
    R = cc.shape[0]
    return pl.pallas_call(
        _mod_kernel,
        grid=(L, 3),
        in_specs=[pl.BlockSpec((R, D), lambda l, j: (0, 0)),
                  pl.BlockSpec((1, D, D), lambda l, j: (l, 0, j)),
                  pl.BlockSpec((1, 1, D), lambda l, j: (l, 0, j))],
        out_specs=pl.BlockSpec((1, R, D), lambda l, j: (l, 0, j)),
        out_shape=jax.ShapeDtypeStruct((L, R, 3 * D), F32),
        compiler_params=_params("arbitrary", "arbitrary"),
        name="modulation",
    )(cc, ada_w, ada_b.reshape(L, 1, 3 * D))


def _inproj_kernel(x_ref, mod_ref, g_ref, w_ref, o_ref, h_scr, *, n_ctx):
    T = x_ref.shape[1]
    first_cols = pl.program_id(1) == 0

    def normalise(r0):
        k = 0 if r0 < n_ctx else 3
        y = _rms(x_ref[0, r0:r0 + ROW_TILE, :], D_MODEL) * g_ref[...]
        h = y * (1.0 + mod_ref[0, k + 1:k + 2, :]) + mod_ref[0, k:k + 1, :]
        h_scr[r0:r0 + ROW_TILE, :] = h.astype(BF16)

    def project(r0):
        o_ref[0, r0:r0 + ROW_TILE, :] = _dot(h_scr[r0:r0 + ROW_TILE, :], w_ref[0]).astype(BF16)

    @pl.when(first_cols)
    def _():
        normalise(0)
        for r0 in range(0, T, ROW_TILE):
            project(r0)
            if r0 + ROW_TILE < T:
                normalise(r0 + ROW_TILE)

    @pl.when(jnp.logical_not(first_cols))
    def _():
        for r0 in range(0, T, ROW_TILE):
            project(r0)


def _in_projection(xs, mod, norm_g, w_all, layer, n_ctx):
    B, T, D = xs.shape
    n_col = P_WIDTH // IN_PROJ_COLS
    return pl.pallas_call(
        functools.partial(_inproj_kernel, n_ctx=n_ctx),
        grid=(B, n_col),
        in_specs=[pl.BlockSpec((1, T, D), lambda b, j: (b, 0, 0)),
                  pl.BlockSpec((1, 8, D), lambda b, j: (b, 0, 0)),
                  pl.BlockSpec((1, D), lambda b, j: (0, 0)),
                  pl.BlockSpec((1, D, IN_PROJ_COLS), lambda b, j: (layer, 0, j))],
        out_specs=pl.BlockSpec((1, T, IN_PROJ_COLS), lambda b, j: (b, 0, j)),
        out_shape=jax.ShapeDtypeStruct((B, T, P_WIDTH), BF16),
        scratch_shapes=[pltpu.VMEM((T, D), BF16)],
        compiler_params=_params("arbitrary", "arbitrary"),
        name="in_projection",
    )(xs, mod, norm_g.reshape(1, D), w_all)


def _latent_rows(T, n_ctx):
    return ATT_ROWS if (T - n_ctx) % ATT_ROWS == 0 else ROW_TILE


def _attend_all(attend, scores0, s0_scr, o_ref, T, n_ctx, with_ctx):
    if with_ctx:
        attend(pl.ds(0, n_ctx), n_ctx, None, None)
    else:
        o_ref[0, 0:n_ctx, :] = jnp.zeros((n_ctx, o_ref.shape[2]), o_ref.dtype)
    rows_per = s0_scr.shape[0]
    n = (T - n_ctx) // rows_per

    def rows_of(i):
        return pl.ds(pl.multiple_of(n_ctx + i * rows_per, ROW_TILE), rows_per)

    s0_scr[...] = scores0(rows_of(0))

    def latent(i, carry):
        attend(rows_of(i), T, s0_scr[...], rows_of(jnp.minimum(i + 1, n - 1)))
        return carry

    lax.fori_loop(0, n, latent, 0)


def _mla_kernel(pa_ref, pc_ref, z_ref, wuq_ref, wukv_ref, cqg_ref, ckvg_ref, qg_ref, kg_ref, rope_ref,
                o_ref, q_scr, k_scr, v_scr, s0_scr, *, n_ctx, with_ctx):
    T = pa_ref.shape[1]
    scale = (MLA_NOPE + ROT_DIM) ** -0.5
    rot = _rot_matrix()

    def prep(i, carry):
        r = pl.multiple_of(i * ROW_TILE, ROW_TILE)
        rows = pl.ds(r, ROW_TILE)
        cos, sin = rope_ref[0, rows, :], rope_ref[1, rows, :]
        pa = pa_ref[0, rows, :].astype(F32)
        cq = (_rms(pa[:, :MLA_Q_LORA], MLA_Q_LORA) * cqg_ref[...]).astype(BF16)
        q = _dot(cq, wuq_ref[...])
        kr = _rms(pa[:, MLA_Q_LORA:], ROT_DIM) * kg_ref[1:2, :]
        kr = _rope_mxu(kr, cos, sin, rot).astype(BF16)
        ckv = (_rms(pc_ref[0, rows, :].astype(F32), MLA_KV_LORA) * ckvg_ref[...]).astype(BF16)
        kv = _dot(ckv, wukv_ref[...])
        for h in range(HEADS):
            qn = _rms(q[:, 256 * h:256 * h + 128], MLA_NOPE) * qg_ref[0:1, :]
            qr = _rms(q[:, 256 * h + 128:256 * h + 256], ROT_DIM) * qg_ref[1:2, :]
            qr = _rope_mxu(qr, cos, sin, rot)
            q_scr[rows, 256 * h:256 * h + 128] = (qn * scale).astype(BF16)
            q_scr[rows, 256 * h + 128:256 * h + 256] = (qr * scale).astype(BF16)
            kn = _rms(kv[:, 128 * h:128 * h + 128], MLA_NOPE) * kg_ref[0:1, :]
            k_scr[rows, 256 * h:256 * h + 128] = kn.astype(BF16)
            k_scr[rows, 256 * h + 128:256 * h + 256] = kr
            v_scr[rows, 256 * h:256 * h + 128] = kv[:, 512 + 128 * h:512 + 128 * h + 128].astype(BF16)
            v_scr[rows, 256 * h + 128:256 * h + 256] = _ones_col(ROW_TILE)
        return carry

    lax.fori_loop(0, T // ROW_TILE, prep, 0)

    def scores(h, rows, nk):
        return _dot_nt(q_scr[rows, 256 * h:256 * h + 256], k_scr[0:nk, 256 * h:256 * h + 256])

    def attend(rows, nk, s_first, next_rows):
        ahead = HEADS if next_rows is None else 1
        pending = [scores(0, rows, nk) if s_first is None else s_first]
        pending += [scores(h, rows, nk) for h in range(1, ahead)]
        for h in range(HEADS):
            s = pending.pop(0)
            if h + ahead < HEADS:
                pending.append(scores(h + ahead, rows, nk))
            elif h + ahead == HEADS and next_rows is not None:
                s0_scr[...] = scores(0, next_rows, nk)
            o = _softmax_pv(s, v_scr[0:nk, 256 * h:256 * h + 256])
            z = z_ref[0, rows, 128 * h:128 * h + 128].astype(F32)
            o_ref[0, rows, 128 * h:128 * h + 128] = (o * _silu(z)).astype(BF16)

    _attend_all(attend, lambda rows: scores(0, rows, T), s0_scr, o_ref, T, n_ctx, with_ctx)


def _mla(p, wuq, wukv, cqg, ckvg, qg, kg, rope, n_ctx, last):
    B, T, _ = p.shape
    full = lambda shape: pl.BlockSpec(shape, lambda b: (0,) * len(shape))
    return pl.pallas_call(
        functools.partial(_mla_kernel, n_ctx=n_ctx, with_ctx=not last),
        grid=(B,),
        in_specs=[pl.BlockSpec((1, T, 512), lambda b: (b, 0, P_MLA_A // 512)),
                  pl.BlockSpec((1, T, 256), lambda b: (b, 0, P_MLA_CKV // 256)),
                  pl.BlockSpec((1, T, 512), lambda b: (b, 0, P_MLA_Z // 512)),
                  full(wuq.shape), full(wukv.shape), full(cqg.shape), full(ckvg.shape),
                  full(qg.shape), full(kg.shape), full(rope.shape)],
        out_specs=pl.BlockSpec((1, T, 512), lambda b: (b, 0, 0)),
        out_shape=jax.ShapeDtypeStruct((B, T, BRANCH_WIDTH), BF16),
        scratch_shapes=[pltpu.VMEM((T, 1024), BF16), pltpu.VMEM((T, 1024), BF16),
                        pltpu.VMEM((T, 1024), BF16), pltpu.VMEM((_latent_rows(T, n_ctx), T), F32)],
        compiler_params=_params("arbitrary"),
        name="mla",
    )(p, p, p, wuq, wukv, cqg, ckvg, qg, kg, rope)


def _diff_kernel(kv_ref, q_ref, z_ref, g_ref, lam_ref, og_ref, rope_ref, o_ref, q0_scr, q1_scr, k_scr, v_scr, s0_scr,
                 *, n_ctx, with_ctx, lam_init):
    T = kv_ref.shape[1]
    scale = DF_DQK ** -0.5
    lo_mask = lax.broadcasted_iota(jnp.int32, (1, 128), 1) < 64
    rot = _rot_matrix()

    def prep(i, carry):
        r = pl.multiple_of(i * ROW_TILE, ROW_TILE)
        rows = pl.ds(r, ROW_TILE)
        cos, sin = rope_ref[0, rows, :], rope_ref[1, rows, :]
        for h in range(HEADS):
            cols = slice(128 * h, 128 * h + 128)
            q = _rms_mxu(q_ref[0, rows, cols].astype(F32), DF_DQK, DF_DQK) * g_ref[0:1, :]
            q = _rope_mxu(q, cos, sin, rot) * scale
            q0_scr[rows, cols] = jnp.where(lo_mask, q, 0.0).astype(BF16)
            q1_scr[rows, cols] = jnp.where(lo_mask, 0.0, q).astype(BF16)
            k = _rms_mxu(kv_ref[0, rows, cols].astype(F32), DF_DQK, DF_DQK) * g_ref[1:2, :]
            k_scr[rows, cols] = _rope_mxu(k, cos, sin, rot).astype(BF16)
            v_scr[rows, 256 * h:256 * h + 128] = kv_ref[0, rows, 512 + 128 * h:512 + 128 * h + 128]
            v_scr[rows, 256 * h + 128:256 * h + 256] = _ones_col(ROW_TILE)
        return carry

    lax.fori_loop(0, T // ROW_TILE, prep, 0)

    lp = lam_ref[...]
    lam = (jnp.exp(jnp.sum(lp[0:1] * lp[1:2], axis=-1, keepdims=True))
           - jnp.exp(jnp.sum(lp[2:3] * lp[3:4], axis=-1, keepdims=True)) + lam_init)
    def scores(u, rows, nk):
        cols = slice(128 * (u // 2), 128 * (u // 2) + 128)
        return _dot_nt((q1_scr if u % 2 else q0_scr)[rows, cols], k_scr[0:nk, cols])

    def attend(rows, nk, s_first, next_rows):
        n_units = 2 * HEADS
        ahead = n_units if next_rows is None else 1
        pending = [scores(0, rows, nk) if s_first is None else s_first]
        pending += [scores(u, rows, nk) for u in range(1, ahead)]
        for h in range(HEADS):
            cols = slice(128 * h, 128 * h + 128)
            v_ext = v_scr[0:nk, 256 * h:256 * h + 256]
            parts = []
            for mp in range(2):
                u = 2 * h + mp
                s = pending.pop(0)
                if u + ahead < n_units:
                    pending.append(scores(u + ahead, rows, nk))
                elif u + ahead == n_units and next_rows is not None:
                    s0_scr[...] = scores(0, next_rows, nk)
                parts.append(_softmax_pv(s, v_ext))
            o = parts[0] - lam * parts[1]
            o = _rms(o, HEAD_V) * og_ref[...] * (1.0 - lam_init)
            z = z_ref[0, rows, cols].astype(F32)
            o_ref[0, rows, cols] = (o * _silu(z)).astype(BF16)

    _attend_all(attend, lambda rows: scores(0, rows, T), s0_scr, o_ref, T, n_ctx, with_ctx)


def _diff(p, qkg, lam_p, og, rope, n_ctx, last, lam_init):
    B, T, _ = p.shape
    full = lambda shape: pl.BlockSpec(shape, lambda b: (0,) * len(shape))
    return pl.pallas_call(
        functools.partial(_diff_kernel, n_ctx=n_ctx, with_ctx=not last, lam_init=lam_init),
        grid=(B,),
        in_specs=[pl.BlockSpec((1, T, 1024), lambda b: (b, 0, P_DF_KV // 1024)),
                  pl.BlockSpec((1, T, 512), lambda b: (b, 0, P_DF_Q // 512)),
                  pl.BlockSpec((1, T, 512), lambda b: (b, 0, P_DF_Z // 512)),
                  full(qkg.shape), full(lam_p.shape), full(og.shape), full(rope.shape)],
        out_specs=pl.BlockSpec((1, T, 512), lambda b: (b, 0, 0)),
        out_shape=jax.ShapeDtypeStruct((B, T, BRANCH_WIDTH), BF16),
        scratch_shapes=[pltpu.VMEM((T, 512), BF16), pltpu.VMEM((T, 512), BF16), pltpu.VMEM((T, 512), BF16),
                        pltpu.VMEM((T, 1024), BF16), pltpu.VMEM((_latent_rows(T, n_ctx), T), F32)],
        compiler_params=_params("arbitrary"),
        name="diff_attn",
    )(p, p, p, qkg, lam_p, og, rope)


def _tile_order(i, n_tiles, rev):
    if not rev:
        return i
    return jnp.where(i == 0, 0, n_tiles - i)


def _chunk_tri(n, rev, chunk=CHUNK):
    r = lax.broadcasted_iota(jnp.int32, (n, n), 0)
    c = lax.broadcasted_iota(jnp.int32, (n, n), 1)
    same = (r // chunk) == (c // chunk)
    tri = (c >= r) if rev else (c <= r)
    return jnp.where(same & tri, 1.0, 0.0).astype(F32)


def _causal(rev, n=CHUNK):
    t = lax.broadcasted_iota(jnp.int32, (n, n), 0)
    s = lax.broadcasted_iota(jnp.int32, (n, n), 1)
    return (s >= t) if rev else (s <= t)


def _cummax_rows(x, rev):
    n = x.shape[0]
    row = lax.broadcasted_iota(jnp.int32, (n, 1), 0)
    s = 1
    while s < n:
        if rev:
            x = jnp.maximum(x, jnp.where(row < n - s, pltpu.roll(x, n - s, 0), -jnp.inf))
        else:
            x = jnp.maximum(x, jnp.where(row >= s, pltpu.roll(x, s, 0), -jnp.inf))
        s *= 2
    return x


def _head_mask(width, group):
    lane = lax.broadcasted_iota(jnp.int32, (1, width), 1) // group
    return [lane == h for h in range(HEADS)]


def _gla_kernel(qk_ref, vz_ref, g_ref, aw_ref, ab_ref, og_ref, o_ref, of_scr, ob_scr, st_scr):
    T = qk_ref.shape[1]
    n_tiles = T // ROW_TILE
    n_chunk = ROW_TILE // CHUNK
    hm = _head_mask(HEADS * GLA_DK, GLA_DK)
    tris = (_chunk_tri(ROW_TILE, False), _chunk_tri(ROW_TILE, True))
    st_scr[...] = jnp.zeros_like(st_scr)

    def load_tile(t, rev):
        rows = pl.ds(pl.multiple_of(t * ROW_TILE, ROW_TILE), ROW_TILE)
        gcol = 256 if rev else 0
        qk = qk_ref[0, rows, :].astype(F32)
        pre = _dot(g_ref[0, rows, :], aw_ref[:, gcol:gcol + 256]) + ab_ref[:, gcol:gcol + 256]
        la = _log_sigmoid(pre) * (1.0 / GLA_TAU)
        hi, lo = _split(la)
        tri = tris[rev].astype(BF16)
        return qk[:, :256] * (GLA_DK ** -0.5), qk[:, 256:], _dot(tri, hi) + _dot(tri, lo)

    def tile(i, carry):
        tiles = (i, _tile_order(i, n_tiles, True))
        steps = []
        for cc in range(n_chunk):
            steps += [(0, cc), (1, n_chunk - 1 - cc)]
        data = [load_tile(tiles[d], bool(d)) for d in (0, 1)]
        n_sub = CHUNK // GLA_SUB
        pre, a_blk, upd, a_msk, intra, o_inter = {}, {}, {}, {}, {}, {}
        st = [st_scr[0], st_scr[1]]

        def prepare(d, cc):
            q, k, b = data[d]
            cr = slice(CHUNK * cc, CHUNK * cc + CHUNK)
            bc, qc, kc = b[cr], q[cr], k[cr]
            b_end = bc[0:1] if d else bc[CHUNK - 1:CHUNK]
            crow = pl.ds(pl.multiple_of(tiles[d] * ROW_TILE + CHUNK * cc, CHUNK), CHUNK)
            q_sub, k_sub = [], []
            for i in range(CHUNK // GLA_SUB):
                sr = slice(GLA_SUB * i, GLA_SUB * i + GLA_SUB)
                first = GLA_SUB * i + (GLA_SUB - 1 if d else 0)
                beta = bc[first:first + 1]
                q_i = qc[sr] * jnp.exp(bc[sr] - beta)
                q_sub.append(jnp.concatenate([jnp.where(hm[h], q_i, 0.0) for h in range(HEADS)],
                                             axis=0).astype(BF16))
                k_sub.append((kc * jnp.exp(jnp.minimum(beta - bc, EXP_CLAMP))).astype(BF16))
            qe = qc * jnp.exp(bc)
            stacked = jnp.concatenate([kc * jnp.exp(b_end - bc), jnp.broadcast_to(jnp.exp(b_end), (8, 256)),
                                       jnp.zeros((CHUNK - 8, 256), F32)], axis=0).T
            pre[d, cc] = dict(
                crow=crow, vc=vz_ref[0, crow, 0:BRANCH_WIDTH], decay=stacked[:, CHUNK:CHUNK + 1],
                qe=[jnp.where(hm[h], qe, 0.0).astype(BF16) for h in range(HEADS)], q_sub=q_sub, k_sub=k_sub,
                k_up_t=stacked[:, 0:CHUNK].astype(BF16))

        def score_dots(d, cc):
            p = pre[d, cc]
            a_blk[d, cc] = [_dot_nt(p['q_sub'][i], p['k_sub'][i]) for i in range(n_sub)]
            upd[d, cc] = jnp.concatenate(
                [_dot(p['k_up_t'][GLA_DK * h:GLA_DK * h + GLA_DK], p['vc'][:, 128 * h:128 * h + 128])
                 for h in range(HEADS)], axis=0)

        def mask_scores(d, cc):
            a_msk[d, cc] = [
                jnp.where(_causal(bool(d)), jnp.concatenate(
                    [a_blk[d, cc][i][GLA_SUB * h:GLA_SUB * h + GLA_SUB] for i in range(n_sub)], axis=0),
                    0.0).astype(BF16) for h in range(HEADS)]

        def value_dots(d, cc):
            vc = pre[d, cc]['vc']
            intra[d, cc] = jnp.concatenate(
                [_dot(a_msk[d, cc][h], vc[:, 128 * h:128 * h + 128]) for h in range(HEADS)], axis=-1)

        def state_dot(d, cc):
            s_bf = st[d].astype(BF16)
            o_inter[d, cc] = jnp.concatenate([_dot(pre[d, cc]['qe'][h], s_bf) for h in range(HEADS)], axis=-1)

        def finish_step(d, cc):
            (ob_scr if d else of_scr)[pre[d, cc]['crow'], :] = o_inter[d, cc] + intra[d, cc]
            st[d] = st[d] * pre[d, cc]['decay'] + upd[d, cc]

        def at(t):
            return [steps[t]] if 0 <= t < len(steps) else []

        for t in range(len(steps) + 4):
            for g in at(t - 1):
                score_dots(*g)
            for g in at(t - 3):
                value_dots(*g)
            for g in at(t - 4):
                state_dot(*g)
            for g in at(t):
                prepare(*g)
            for g in at(t - 2):
                mask_scores(*g)
            for g in at(t - 4):
                finish_step(*g)
        st_scr[0] = st[0]
        st_scr[1] = st[1]
        return carry

    lax.fori_loop(0, n_tiles, tile, 0)

    def finish(i, carry):
        rows = pl.ds(pl.multiple_of(i * ROW_TILE, ROW_TILE), ROW_TILE)
        tot = of_scr[rows, :] + ob_scr[rows, :]
        z = vz_ref[0, rows, BRANCH_WIDTH:].astype(F32)
        for h in range(HEADS):
            cols = slice(128 * h, 128 * h + 128)
            y = _rms(tot[:, cols], HEAD_V) * og_ref[:, cols]
            o_ref[0, rows, cols] = (y * _silu(z[:, cols])).astype(BF16)
        return carry

    lax.fori_loop(0, n_tiles, finish, 0)


def _gla(p, aw, ab, og):
    B, T, _ = p.shape
    full = lambda shape: pl.BlockSpec(shape, lambda b: (0,) * len(shape))
    return pl.pallas_call(
        _gla_kernel,
        grid=(B,),
        in_specs=[pl.BlockSpec((1, T, 512), lambda b: (b, 0, P_GLA_QK // 512)),
                  pl.BlockSpec((1, T, 1024), lambda b: (b, 0, P_GLA_VZ // 1024)),
                  pl.BlockSpec((1, T, 128), lambda b: (b, 0, P_GLA_G // 128)),
                  full(aw.shape), full(ab.shape), full(og.shape)],
        out_specs=pl.BlockSpec((1, T, 512), lambda b: (b, 0, 0)),
        out_shape=jax.ShapeDtypeStruct((B, T, BRANCH_WIDTH), BF16),
        scratch_shapes=[pltpu.VMEM((T, BRANCH_WIDTH), F32), pltpu.VMEM((T, BRANCH_WIDTH), F32),
                        pltpu.VMEM((2, HEADS * GLA_DK, HEAD_V), F32)],
        compiler_params=_params("arbitrary"),
        name="gla_scan",
    )(p, p, p, aw, ab, og)


def _mlstm_kernel(m_ref, if_ref, cw_ref, cb_ref, wq_ref, wkt_ref, gb_ref, og_ref, sk_ref, o_ref,
                  xc_scr, q_scr, kt_scr, at_scr, bt_scr, cm_scr, b_scr, hf_scr, hb_scr, c_scr, *, n_ctx):
    T = m_ref.shape[1]
    n_tiles = T // ROW_TILE
    assert ML_CHUNK == ROW_TILE
    n_chunk = 1
    ctx_tiles = n_ctx // ROW_TILE
    hm = _head_mask(HEADS * ML_DQK, ML_DQK)
    lane128 = lax.broadcasted_iota(jnp.int32, (1, 128), 1)
    is_forget = ((lane128 // HEADS) % 2) == 1
    row_in_tile = lax.broadcasted_iota(jnp.int32, (ROW_TILE, 1), 0)
    tri_f = _chunk_tri(ROW_TILE, False, ML_CHUNK).astype(BF16)
    tri_b = _chunk_tri(ROW_TILE, True, ML_CHUNK).astype(BF16)

    def prep(i, carry):
        r = pl.multiple_of(i * ROW_TILE, ROW_TILE)
        rows = pl.ds(r, ROW_TILE)
        x = m_ref[0, rows, 0:512].astype(F32)
        rp = pl.multiple_of(jnp.maximum(r - 16, 0), 16)
        rn = pl.multiple_of(jnp.minimum(r + ROW_TILE, T - 16), 16)
        prev_row = m_ref[0, pl.ds(rp, 16), 0:512].astype(F32)[15:16]
        next_row = m_ref[0, pl.ds(rn, 16), 0:512].astype(F32)[0:1]
        seg_start = (i == 0) | (i == ctx_tiles)
        seg_end = (i == ctx_tiles - 1) | (i == n_tiles - 1)
        prev_row = jnp.where(seg_start, 0.0, prev_row)
        next_row = jnp.where(seg_end, 0.0, next_row)
        xm = jnp.where(row_in_tile == 0, prev_row, pltpu.roll(x, 1, 0))
        xp = jnp.where(row_in_tile == ROW_TILE - 1, next_row, pltpu.roll(x, ROW_TILE - 1, 0))
        xc = _silu(cw_ref[0:1, :] * xm + cw_ref[1:2, :] * x + cw_ref[2:3, :] * xp + cb_ref[...])
        xc_scr[rows, :] = xc
        xb = xc.astype(BF16)
        q_scr[rows, :] = _dot(xb, wq_ref[...]).astype(BF16)
        g = if_ref[0, rows, :].astype(F32) + gb_ref[...]
        g2 = jnp.where(is_forget, _log_sigmoid(g), g)
        g_hi, g_lo = _split(g2)
        cs = jnp.where(lane128 < 2 * HEADS, _dot(tri_f, g_hi) + _dot(tri_f, g_lo),
                       _dot(tri_b, g_hi) + _dot(tri_b, g_lo))
        b = pltpu.roll(cs, 128 - HEADS, 1)
        a = g2 - b
        cm_scr[rows, :] = jnp.where(lane128 < 2 * HEADS, _cummax_rows(a, False), _cummax_rows(a, True))
        b_scr[rows, :] = b
        kt_scr[i] = (_dot_nt(wkt_ref[...], xb) * (ML_DQK ** -0.5)).astype(BF16)
        at_scr[i] = a.T[0:16, :]
        bt_scr[i] = b.T[0:16, :]
        return carry

    lax.fori_loop(0, n_tiles, prep, 0)

    c_scr[...] = jnp.zeros_like(c_scr)
    ones_col = _ones_col(ML_CHUNK)
    sel_r = lax.broadcasted_iota(jnp.int32, (256, HEADS * 128), 0) % 128
    sel_c = lax.broadcasted_iota(jnp.int32, (256, HEADS * 128), 1) // 128
    sel = [jnp.where(sel_r == 2 * HEADS * d + sel_c, 1.0, 0.0).astype(BF16) for d in (0, 1)]

    def col_bcast(x, d):
        hi, lo = _split(x)
        return _dot(jnp.concatenate([hi, lo], axis=-1), sel[d])

    def chunk_pair(c_f, c_b, m_in):
        units = [(d, h) for d in (0, 1) for h in range(HEADS)]
        cs = (c_f, c_b)
        crow = [pl.ds(pl.multiple_of(c * ML_CHUNK, ML_CHUNK), ML_CHUNK) for c in cs]
        qc = [q_scr[crow[d], :] for d in (0, 1)]
        kt = [kt_scr[cs[d]] for d in (0, 1)]
        vc = [m_ref[0, crow[d], 512:1024] for d in (0, 1)]
        at = [at_scr[cs[d]] for d in (0, 1)]
        bt = [bt_scr[cs[d]] for d in (0, 1)]
        c_bf = [c_scr[d].astype(BF16) for d in (0, 1)]
        causal = [_causal(False, ML_CHUNK), _causal(True, ML_CHUNK)]
        last = [ML_CHUNK - 1, 0]
        qh = {(d, h): jnp.where(hm[h], qc[d], jnp.zeros_like(qc[d])) for d, h in units}
        v_ext = {(d, h): jnp.concatenate([vc[d][:, 128 * h:128 * h + 128], ones_col], axis=-1)
                 for d, h in units}
        s_raw = {u: _dot(qh[u], kt[u[0]]) for u in units}
        q_c = {u: _dot(qh[u], c_bf[u[0]]) for u in units}
        cm_col = [col_bcast(cm_scr[crow[d], :], d) for d in (0, 1)]
        b_col = [col_bcast(b_scr[crow[d], :], d) for d in (0, 1)]
        s_w, m_run, m_last, ktw = {}, {}, {}, {}
        for d, h in units:
            j = 2 * HEADS * d + h
            a_row, m_old = at[d][j:j + 1, :], m_in[HEADS * d + h]
            pm = jnp.where(causal[d], a_row, -jnp.inf)
            m_run[d, h] = jnp.maximum(m_old, cm_col[d][:, 128 * h:128 * h + 128])
            m_wide = jnp.concatenate([m_run[d, h]] * (ML_CHUNK // 128), axis=-1)
            s_w[d, h] = (s_raw[d, h] * jnp.exp(pm - m_wide)).astype(BF16)
            m_last[d, h] = m_run[d, h][last[d]:last[d] + 1, 0:1]
            k_h = kt[d][ML_DQK * h:ML_DQK * h + ML_DQK, :].astype(F32)
            ktw[d, h] = (k_h * jnp.exp(a_row - m_last[d, h])).astype(BF16)
        s_v = {u: _dot(s_w[u], v_ext[u]) for u in units}
        upd = {u: _dot(ktw[u], v_ext[u]) for u in units}
        m_out = []
        for d in (0, 1):
            parts = []
            for h in range(HEADS):
                j = 2 * HEADS * d + h
                hr = slice(ML_DQK * h, ML_DQK * h + ML_DQK)
                b_row, m_old = bt[d][j:j + 1, :], m_in[HEADS * d + h]
                carry_w = jnp.exp(m_old - m_run[d, h])
                num = jnp.concatenate([carry_w, carry_w], axis=-1) * q_c[d, h] + s_v[d, h]
                den = num[:, HEAD_V:HEAD_V + 1]
                floor = jnp.exp(-(b_col[d][:, 128 * h:128 * h + 128] + m_run[d, h]))
                parts.append(num[:, 0:HEAD_V] / jnp.maximum(jnp.abs(den), floor))
                c_scr[d, hr, :] = jnp.exp(m_old - m_last[d, h]) * c_scr[d, hr, :] + upd[d, h]
                m_out.append(b_row[:, last[d]:last[d] + 1] + m_last[d, h])
            (hb_scr if d else hf_scr)[crow[d], :] = jnp.concatenate(parts, axis=-1)
        return m_out

    def tile(i, carry):
        m = list(carry)
        c_f = i * n_chunk
        c_b = _tile_order(i, n_tiles, True) * n_chunk + (n_chunk - 1)
        for cc in range(n_chunk):
            m = chunk_pair(c_f + cc, c_b - cc, m)
        return tuple(m)

    lax.fori_loop(0, n_tiles, tile, tuple(jnp.zeros((1, 1), F32) for _ in range(2 * HEADS)))

    def finish(i, carry):
        rows = pl.ds(pl.multiple_of(i * ROW_TILE, ROW_TILE), ROW_TILE)
        tot = (hf_scr[rows, :] + hb_scr[rows, :]) * _sigmoid(m_ref[0, rows, 1024:1536].astype(F32))
        z = m_ref[0, rows, 1536:2048].astype(F32)
        xc = xc_scr[rows, :]
        for h in range(HEADS):
            cols = slice(128 * h, 128 * h + 128)
            y = _rms(tot[:, cols], HEAD_V) * og_ref[:, cols]
            y = (y + sk_ref[:, cols] * xc[:, cols]) * _silu(z[:, cols])
            o_ref[0, rows, cols] = y.astype(BF16)
        return carry

    lax.fori_loop(0, n_tiles, finish, 0)


def _mlstm(p, cw, cb, wq, wkt, gb, og, sk, n_ctx):
    B, T, _ = p.shape
    full = lambda shape: pl.BlockSpec(shape, lambda b: (0,) * len(shape))
    return pl.pallas_call(
        functools.partial(_mlstm_kernel, n_ctx=n_ctx),
        grid=(B,),
        in_specs=[pl.BlockSpec((1, T, 2048), lambda b: (b, 0, P_ML // 2048)),
                  pl.BlockSpec((1, T, 128), lambda b: (b, 0, P_ML_IF // 128)),
                  full(cw.shape), full(cb.shape), full(wq.shape), full(wkt.shape), full(gb.shape),
                  full(og.shape), full(sk.shape)],
        out_specs=pl.BlockSpec((1, T, 512), lambda b: (b, 0, 0)),
        out_shape=jax.ShapeDtypeStruct((B, T, BRANCH_WIDTH), BF16),
        scratch_shapes=[pltpu.VMEM((T, BRANCH_WIDTH), F32),
                        pltpu.VMEM((T, HEADS * ML_DQK), BF16),
                        pltpu.VMEM((T // ML_CHUNK, HEADS * ML_DQK, ML_CHUNK), BF16),
                        pltpu.VMEM((T // ML_CHUNK, 16, ML_CHUNK), F32),
                        pltpu.VMEM((T // ML_CHUNK, 16, ML_CHUNK), F32),
                        pltpu.VMEM((T, 128), F32),
                        pltpu.VMEM((T, 128), F32),
                        pltpu.VMEM((T, BRANCH_WIDTH), F32),
                        pltpu.VMEM((T, BRANCH_WIDTH), F32),
                        pltpu.VMEM((2, HEADS * ML_DQK, 2 * HEAD_V), F32)],
        compiler_params=_params("arbitrary"),
        name="mlstm_scan",
    )(p, p, cw, cb, wq, wkt, gb, og, sk)


def _merge_kernel(ya_ref, yb_ref, yc_ref, yd_ref, gl_ref, brw_ref, wo_ref, x_ref, gate_ref, o_ref,
                  *, row0, ctx_rows):
    tm = x_ref.shape[1]
    first = pl.program_id(1) * tm + row0
    acc = None
    for i, y_ref in enumerate((ya_ref, yb_ref, yc_ref, yd_ref)):
        u = _dot(y_ref[0], brw_ref[i])
        gsig = _sigmoid(gl_ref[0, :, D_MODEL * i:D_MODEL * (i + 1)].astype(F32))
        acc = gsig * u if acc is None else acc + gsig * u
    out = _dot(acc.astype(BF16), wo_ref[...])
    for r0 in range(0, tm, ROW_TILE):
        rs = slice(r0, r0 + ROW_TILE)
        gate = jnp.where(first + r0 < ctx_rows, gate_ref[0, 0:1, :], gate_ref[0, 1:2, :])
        o_ref[0, rs, :] = x_ref[0, rs, :] + gate * out[rs]


def _merge(ys, p, brw, wo, xs, gates, n_ctx, last):
    B, T, D = xs.shape
    tm = ROW_TILE if last else MERGE_ROWS
    row0 = n_ctx if last else 0
    tile0 = row0 // tm
    nt = (T - row0) // tm
    assert (T - row0) % tm == 0 and row0 % tm == 0 and n_ctx % ROW_TILE == 0
    ymap = lambda b, t: (b, t + tile0, 0)
    return pl.pallas_call(
        functools.partial(_merge_kernel, row0=row0, ctx_rows=n_ctx),
        grid=(B, nt),
        in_specs=[pl.BlockSpec((1, tm, BRANCH_WIDTH), ymap)] * 4 + [
            pl.BlockSpec((1, tm, N_BRANCH * D), lambda b, t: (b, t + tile0, P_MERGE // (N_BRANCH * D))),
            pl.BlockSpec(brw.shape, lambda b, t: (0, 0, 0)),
            pl.BlockSpec(wo.shape, lambda b, t: (0, 0)),
            pl.BlockSpec((1, tm, D), ymap),
            pl.BlockSpec((1, 2, D), lambda b, t: (b, 0, 0))],
        out_specs=pl.BlockSpec((1, tm, D), lambda b, t: (b, t, 0)),
        out_shape=jax.ShapeDtypeStruct((B, nt * tm, D), F32),
        compiler_params=_params("arbitrary", "arbitrary"),
        name="merge",
    )(*ys, p, brw, wo, xs, gates)


def _layout_w_in(w_in):
    offs, off = {}, 0
    for name, w in IN_SPLITS:
        offs[name] = (off, w)
        off += w

    w_bf = w_in.astype(BF16)

    def col(name):
        o, w = offs[name]
        return w_bf[..., o:o + w]

    def zeros(n):
        return jnp.zeros(w_in.shape[:-1] + (n,), BF16)

    parts = [col('merge'), col('df_k'), col('df_v'), col('df_q'), col('df_z'),
             col('ml_x'), col('ml_v'), col('ml_o'), col('ml_z'),
             col('gla_v'), col('gla_z'), col('mla_z'), col('gla_q'), col('gla_k'),
             col('mla_cq'), col('mla_kr'), zeros(64),
             col('mla_ckv'), col('gla_af'), col('gla_ab'), zeros(96), col('ml_if'), zeros(112)]
    out = jnp.concatenate(parts, axis=-1)
    assert out.shape[-1] == P_WIDTH
    return out


def _rope_tables(rows, n_ctx):
    quarter = ROT_DIM // 4
    inv_freq = ROPE_BASE ** (-jnp.arange(quarter, dtype=F32) / quarter)
    row = jnp.repeat(jnp.arange(rows, dtype=F32), GRID_W)
    col = jnp.tile(jnp.arange(GRID_W, dtype=F32), rows)
    ar = row[:, None] * inv_freq
    ac = col[:, None] * inv_freq
    ang = jnp.concatenate([ar, ar, ac, ac], axis=-1)
    cos = jnp.concatenate([jnp.ones((n_ctx, ROT_DIM), F32), jnp.cos(ang)], axis=0)
    sin = jnp.concatenate([jnp.zeros((n_ctx, ROT_DIM), F32), jnp.sin(ang)], axis=0)
    zero = jnp.zeros_like(cos)
    both = jnp.stack([jnp.tile(cos, (1, 2)), jnp.tile(sin, (1, 2))])
    half = jnp.stack([jnp.concatenate([cos, zero], -1), jnp.concatenate([sin, zero], -1)])
    return half, both


def _pad_lanes(v, n):
    return jnp.concatenate([v, jnp.zeros(v.shape[:-1] + (n - v.shape[-1],), v.dtype)], axis=-1)


def kernel(x, c, ctx, c_ctx, ada_w, ada_b, norm_g, w_in, mla_cq_g, mla_ckv_g, mla_wuq, mla_wukv, mla_q_g,
           mla_k_g, gla_a_w, gla_a_b, gla_out_g, ml_conv_w, ml_conv_b, ml_wq, ml_wk, ml_gate_b, ml_out_g,
           ml_skip, df_qk_g, df_lambda, df_out_g, br_w, w_out):
    B, S, D = x.shape
    n_ctx = ctx.shape[1]
    L = ada_w.shape[0]
    assert D == D_MODEL and n_ctx == ROW_TILE and S % ROW_TILE == 0 and S % GRID_W == 0

    rope_half, rope_both = _rope_tables(S // GRID_W, n_ctx)
    w_in_p = _layout_w_in(w_in)

    n_rows = -(-(B + 1) // 8) * 8
    cc = jnp.concatenate([c, c_ctx[None], jnp.zeros((n_rows - B - 1, D), F32)], axis=0)
    mod_all = _modulation(cc, ada_w, ada_b)

    xs = jnp.concatenate([ctx, x], axis=1)
    for l in range(L):
        last = l == L - 1
        lam_init = 0.8 - 0.6 * math.exp(-0.3 * l)
        m3 = mod_all[l].reshape(n_rows, 3, D)
        lat, cx = m3[:B], jnp.broadcast_to(m3[B][None], (B, 3, D))
        mod = jnp.concatenate([cx, lat, jnp.zeros((B, 2, D), F32)], axis=1)
        gates = jnp.stack([cx[:, 2], lat[:, 2]], axis=1)

        p = _in_projection(xs, mod, norm_g[l], w_in_p, l, n_ctx)

        wq4 = mla_wuq[l].reshape(MLA_Q_LORA, HEADS, MLA_NOPE + ROT_DIM)
        wuq = _pad_lanes(wq4, 256).reshape(MLA_Q_LORA, HEADS * 256).astype(BF16)
        wkv4 = mla_wukv[l].reshape(MLA_KV_LORA, HEADS, MLA_NOPE + HEAD_V)
        wukv = jnp.concatenate([wkv4[..., :MLA_NOPE].reshape(MLA_KV_LORA, -1),
                                wkv4[..., MLA_NOPE:].reshape(MLA_KV_LORA, -1)], axis=-1).astype(BF16)
        qg = jnp.stack([mla_q_g[l, :MLA_NOPE], _pad_lanes(mla_q_g[l, MLA_NOPE:], 128)])
        kg = jnp.stack([mla_k_g[l, :MLA_NOPE], _pad_lanes(mla_k_g[l, MLA_NOPE:], 128)])
        y_mla = _mla(p, wuq, wukv, mla_cq_g[l][None], mla_ckv_g[l][None], qg, kg, rope_half, n_ctx, last)

        qkg = jnp.tile(df_qk_g[l], (1, 2))
        y_df = _diff(p, qkg, df_lambda[l], df_out_g[l][None], rope_both, n_ctx, last, lam_init)

        aw = jnp.zeros((128, 512), F32)
        aw = aw.at[0:16, 0:256].set(gla_a_w[l, 0]).at[16:32, 256:512].set(gla_a_w[l, 1]).astype(BF16)
        y_gla = _gla(p, aw, gla_a_b[l].reshape(1, 512), gla_out_g[l][None])

        wq_bd = jnp.zeros((BRANCH_WIDTH, HEADS * ML_DQK), F32)
        wk_bd = jnp.zeros((BRANCH_WIDTH, HEADS * ML_DQK), F32)
        for h in range(HEADS):
            wq_bd = wq_bd.at[128 * h:128 * h + 128, 64 * h:64 * h + 64].set(ml_wq[l, h])
            wk_bd = wk_bd.at[128 * h:128 * h + 128, 64 * h:64 * h + 64].set(ml_wk[l, h])
        gb = _pad_lanes(ml_gate_b[l].reshape(1, 16), 128)
        y_ml = _mlstm(p, ml_conv_w[l], ml_conv_b[l][None], wq_bd.astype(BF16), wk_bd.T.astype(BF16), gb,
                      ml_out_g[l][None], ml_skip[l][None], n_ctx)

        xs = _merge((y_mla, y_gla, y_ml, y_df), p, br_w[l].astype(BF16), w_out[l].astype(BF16), xs, gates,
                    n_ctx, last)
    return xs
```

```python
import functools
import math

import jax
import jax.numpy as jnp
from jax import lax
from jax.experimental import pallas as pl
from jax.experimental.pallas import tpu as pltpu

F32 = jnp.float32
BF16 = jnp.bfloat16

D_MODEL = 1024
GRID_W = 64
EPS = 1e-6
ROPE_BASE = 10000.0
ROT_DIM = 64
CHUNK = 64
N_BRANCH = 4
BRANCH_WIDTH = 512
HEADS = 4
HEAD_V = BRANCH_WIDTH // HEADS
MLA_NOPE = 128
MLA_Q_LORA = 384
MLA_KV_LORA = 256
GLA_DK = 64
GLA_TAU = 16.0
ML_DQK = 64
DF_DQK = 64

IN_SPLITS = (
    ('mla_cq', 384), ('mla_ckv', 256), ('mla_kr', 64), ('mla_z', 512),
    ('gla_q', 256), ('gla_k', 256), ('gla_v', 512), ('gla_af', 16), ('gla_ab', 16), ('gla_z', 512),
    ('ml_x', 512), ('ml_v', 512), ('ml_o', 512), ('ml_if', 16), ('ml_z', 512),
    ('df_q', 512), ('df_k', 512), ('df_v', 512), ('df_z', 512),
    ('merge', 4096),
)

P_MERGE = 0
P_DF_KV = 4096
P_DF_Q = 5120
P_DF_Z = 5632
P_ML = 6144
P_GLA_VZ = 8192
P_MLA_Z = 9216
P_GLA_QK = 9728
P_MLA_A = 10240
P_MLA_CKV = 10752
P_GLA_G = 11008
P_ML_IF = 11136
P_WIDTH = 11264

ROW_TILE = 256
IN_PROJ_COLS = 1024
MERGE_ROWS = 768
ML_CHUNK = 256
ATT_ROWS = 512
VMEM_LIMIT = 56 * 1024 * 1024
EXP_CLAMP = 80.0
GLA_SUB = 16


def _dot(a, b):
    return jnp.dot(a, b, preferred_element_type=F32)


def _dot_nt(a, b):
    return lax.dot_general(a, b, (((1,), (1,)), ((), ())), preferred_element_type=F32)


def _sigmoid(x):
    return 1.0 / (1.0 + jnp.exp(-x))


def _silu(x):
    return x * _sigmoid(x)


def _log_sigmoid(x):
    return jnp.minimum(x, 0.0) - jnp.log(1.0 + jnp.exp(-jnp.abs(x)))


def _rms(x, n):
    return x * lax.rsqrt(jnp.sum(x * x, axis=-1, keepdims=True) * (1.0 / n) + EPS)


def _split(x):
    hi = x.astype(BF16)
    return hi, (x - hi.astype(F32)).astype(BF16)


def _group_sum(x, group):
    row = lax.broadcasted_iota(jnp.int32, (128, 128), 0)
    col = lax.broadcasted_iota(jnp.int32, (128, 128), 1)
    sel = jnp.where(row // group == col // group, 1.0, 0.0).astype(BF16)
    hi, lo = _split(x)
    return _dot(hi, sel) + _dot(lo, sel)


def _rms_mxu(x, group):
    return x * lax.rsqrt(_group_sum(x * x, group) * (1.0 / group) + EPS)


def _rot_matrix():
    src = lax.broadcasted_iota(jnp.int32, (128, 128), 0)
    dst = lax.broadcasted_iota(jnp.int32, (128, 128), 1)
    even = (dst // 16) % 2 == 0
    return jnp.where(even & (src == dst + 16), -1.0, jnp.where(~even & (src == dst - 16), 1.0, 0.0)).astype(BF16)


def _rope_mxu(x, cos, sin, rot):
    hi, lo = _split(x)
    return x * cos + (_dot(hi, rot) + _dot(lo, rot)) * sin


def _ones_col(rows):
    return jnp.where(lax.broadcasted_iota(jnp.int32, (rows, HEAD_V), 1) == 0, 1.0, 0.0).astype(BF16)


def _softmax_pv(s, v_ext):
    e = jnp.exp((s - jnp.max(s, axis=-1, keepdims=True)).astype(BF16))
    o = _dot(e, v_ext)
    return o[:, 0:HEAD_V] / o[:, HEAD_V:HEAD_V + 1]


def _params(*sem):
    return pltpu.CompilerParams(dimension_semantics=sem, vmem_limit_bytes=VMEM_LIMIT)


def _mod_kernel(c_ref, w_ref, b_ref, o_ref):
    s = _silu(c_ref[...])
    o_ref[0] = _dot(s.astype(BF16), w_ref[0].astype(BF16)) + b_ref[0]


def _modulation(cc, ada_w, ada_b):
    L, D, _ = ada_w.shape
    R = cc.shape[0]
    return pl.pallas_call(
        _mod_kernel,
        grid=(L, 3),
        in_specs=[pl.BlockSpec((R, D), lambda l, j: (0, 0)),
                  pl.BlockSpec((1, D, D), lambda l, j: (l, 0, j)),
                  pl.BlockSpec((1, 1, D), lambda l, j: (l, 0, j))],
        out_specs=pl.BlockSpec((1, R, D), lambda l, j: (l, 0, j)),
        out_shape=jax.ShapeDtypeStruct((L, R, 3 * D), F32),
        compiler_params=_params("arbitrary", "arbitrary"),
        name="modulation",
    )(cc, ada_w, ada_b.reshape(L, 1, 3 * D))


def _inproj_kernel(x_ref, mod_ref, g_ref, w_ref, o_ref, h_scr, *, n_ctx):
    T = x_ref.shape[1]
    first_cols = pl.program_id(1) == 0

    def normalise(r0):
        k = 0 if r0 < n_ctx else 3
        y = _rms(x_ref[0, r0:r0 + ROW_TILE, :], D_MODEL) * g_ref[...]
        h = y * (1.0 + mod_ref[0, k + 1:k + 2, :]) + mod_ref[0, k:k + 1, :]
        h_scr[r0:r0 + ROW_TILE, :] = h.astype(BF16)

    def project(r0):
        o_ref[0, r0:r0 + ROW_TILE, :] = _dot(h_scr[r0:r0 + ROW_TILE, :], w_ref[0]).astype(BF16)

    @pl.when(first_cols)
    def _():
        normalise(0)
        for r0 in range(0, T, ROW_TILE):
            project(r0)
            if r0 + ROW_TILE < T:
                normalise(r0 + ROW_TILE)

    @pl.when(jnp.logical_not(first_cols))
    def _():
        for r0 in range(0, T, ROW_TILE):
            project(r0)


def _in_projection(xs, mod, norm_g, w_all, layer, n_ctx):
    B, T, D = xs.shape
    n_col = P_WIDTH // IN_PROJ_COLS
    return pl.pallas_call(
        functools.partial(_inproj_kernel, n_ctx=n_ctx),
        grid=(B, n_col),
        in_specs=[pl.BlockSpec((1, T, D), lambda b, j: (b, 0, 0)),
                  pl.BlockSpec((1, 8, D), lambda b, j: (b, 0, 0)),
                  pl.BlockSpec((1, D), lambda b, j: (0, 0)),
                  pl.BlockSpec((1, D, IN_PROJ_COLS), lambda b, j: (layer, 0, j))],
        out_specs=pl.BlockSpec((1, T, IN_PROJ_COLS), lambda b, j: (b, 0, j)),
        out_shape=jax.ShapeDtypeStruct((B, T, P_WIDTH), BF16),
        scratch_shapes=[pltpu.VMEM((T, D), BF16)],
        compiler_params=_params("arbitrary", "arbitrary"),
        name="in_projection",
    )(xs, mod, norm_g.reshape(1, D), w_all)


def _latent_rows(T, n_ctx):
    return ATT_ROWS if (T - n_ctx) % ATT_ROWS == 0 else ROW_TILE


def _attend_all(attend, scores0, s0_scr, o_ref, T, n_ctx, with_ctx):
    if with_ctx:
        attend(pl.ds(0, n_ctx), n_ctx, None, None)
    else:
        o_ref[0, 0:n_ctx, :] = jnp.zeros((n_ctx, o_ref.shape[2]), o_ref.dtype)
    rows_per = s0_scr.shape[0]
    n = (T - n_ctx) // rows_per

    def rows_of(i):
        return pl.ds(pl.multiple_of(n_ctx + i * rows_per, ROW_TILE), rows_per)

    s0_scr[...] = scores0(rows_of(0))

    def latent(i, carry):
        attend(rows_of(i), T, s0_scr[...], rows_of(jnp.minimum(i + 1, n - 1)))
        return carry

    lax.fori_loop(0, n, latent, 0)


def _mla_kernel(pa_ref, pc_ref, z_ref, wuq_ref, wukv_ref, cqg_ref, ckvg_ref, qg_ref, kg_ref, rope_ref,
                o_ref, q_scr, k_scr, v_scr, s0_scr, *, n_ctx, with_ctx):
    T = pa_ref.shape[1]
    scale = (MLA_NOPE + ROT_DIM) ** -0.5
    rot = _rot_matrix()

    def prep(i, carry):
        r = pl.multiple_of(i * ROW_TILE, ROW_TILE)
        rows = pl.ds(r, ROW_TILE)
        cos, sin = rope_ref[0, rows, :], rope_ref[1, rows, :]
        pa = pa_ref[0, rows, :].astype(F32)
        cq = (_rms(pa[:, :MLA_Q_LORA], MLA_Q_LORA) * cqg_ref[...]).astype(BF16)
        q = _dot(cq, wuq_ref[...])
        kr = _rms(pa[:, MLA_Q_LORA:], ROT_DIM) * kg_ref[1:2, :]
        kr = _rope_mxu(kr, cos, sin, rot).astype(BF16)
        ckv = (_rms(pc_ref[0, rows, :].astype(F32), MLA_KV_LORA) * ckvg_ref[...]).astype(BF16)
        kv = _dot(ckv, wukv_ref[...])
        for h in range(HEADS):
            qn = _rms(q[:, 256 * h:256 * h + 128], MLA_NOPE) * qg_ref[0:1, :]
            qr = _rms(q[:, 256 * h + 128:256 * h + 256], ROT_DIM) * qg_ref[1:2, :]
            qr = _rope_mxu(qr, cos, sin, rot)
            q_scr[rows, 256 * h:256 * h + 128] = (qn * scale).astype(BF16)
            q_scr[rows, 256 * h + 128:256 * h + 256] = (qr * scale).astype(BF16)
            kn = _rms(kv[:, 128 * h:128 * h + 128], MLA_NOPE) * kg_ref[0:1, :]
            k_scr[rows, 256 * h:256 * h + 128] = kn.astype(BF16)
            k_scr[rows, 256 * h + 128:256 * h + 256] = kr
            v_scr[rows, 256 * h:256 * h + 128] = kv[:, 512 + 128 * h:512 + 128 * h + 128].astype(BF16)
            v_scr[rows, 256 * h + 128:256 * h + 256] = _ones_col(ROW_TILE)
        return carry

    lax.fori_loop(0, T // ROW_TILE, prep, 0)

    def scores(h, rows, nk):
        return _dot_nt(q_scr[rows, 256 * h:256 * h + 256], k_scr[0:nk, 256 * h:256 * h + 256])

    def attend(rows, nk, s_first, next_rows):
        ahead = HEADS if next_rows is None else 1
        pending = [scores(0, rows, nk) if s_first is None else s_first]
        pending += [scores(h, rows, nk) for h in range(1, ahead)]
        for h in range(HEADS):
            s = pending.pop(0)
            if h + ahead < HEADS:
                pending.append(scores(h + ahead, rows, nk))
            elif h + ahead == HEADS and next_rows is not None:
                s0_scr[...] = scores(0, next_rows, nk)
            o = _softmax_pv(s, v_scr[0:nk, 256 * h:256 * h + 256])
            z = z_ref[0, rows, 128 * h:128 * h + 128].astype(F32)
            o_ref[0, rows, 128 * h:128 * h + 128] = (o * _silu(z)).astype(BF16)

    _attend_all(attend, lambda rows: scores(0, rows, T), s0_scr, o_ref, T, n_ctx, with_ctx)


def _mla(p, wuq, wukv, cqg, ckvg, qg, kg, rope, n_ctx, last):
    B, T, _ = p.shape
    full = lambda shape: pl.BlockSpec(shape, lambda b: (0,) * len(shape))
    return pl.pallas_call(
        functools.partial(_mla_kernel, n_ctx=n_ctx, with_ctx=not last),
        grid=(B,),
        in_specs=[pl.BlockSpec((1, T, 512), lambda b: (b, 0, P_MLA_A // 512)),
                  pl.BlockSpec((1, T, 256), lambda b: (b, 0, P_MLA_CKV // 256)),
                  pl.BlockSpec((1, T, 512), lambda b: (b, 0, P_MLA_Z // 512)),
                  full(wuq.shape), full(wukv.shape), full(cqg.shape), full(ckvg.shape),
                  full(qg.shape), full(kg.shape), full(rope.shape)],
        out_specs=pl.BlockSpec((1, T, 512), lambda b: (b, 0, 0)),
        out_shape=jax.ShapeDtypeStruct((B, T, BRANCH_WIDTH), BF16),
        scratch_shapes=[pltpu.VMEM((T, 1024), BF16), pltpu.VMEM((T, 1024), BF16),
                        pltpu.VMEM((T, 1024), BF16), pltpu.VMEM((_latent_rows(T, n_ctx), T), F32)],
        compiler_params=_params("arbitrary"),
        name="mla",
    )(p, p, p, wuq, wukv, cqg, ckvg, qg, kg, rope)


def _diff_kernel(kv_ref, q_ref, z_ref, g_ref, lam_ref, og_ref, rope_ref, o_ref, q0_scr, q1_scr, k_scr, v_scr, s0_scr,
                 *, n_ctx, with_ctx, lam_init):
    T = kv_ref.shape[1]
    scale = DF_DQK ** -0.5
    lo_mask = lax.broadcasted_iota(jnp.int32, (1, 128), 1) < 64
    rot = _rot_matrix()

    def prep(i, carry):
        r = pl.multiple_of(i * ROW_TILE, ROW_TILE)
        rows = pl.ds(r, ROW_TILE)
        cos, sin = rope_ref[0, rows, :], rope_ref[1, rows, :]
        for h in range(HEADS):
            cols = slice(128 * h, 128 * h + 128)
            q = _rms_mxu(q_ref[0, rows, cols].astype(F32), DF_DQK) * g_ref[0:1, :]
            q = _rope_mxu(q, cos, sin, rot) * scale
            q0_scr[rows, cols] = jnp.where(lo_mask, q, 0.0).astype(BF16)
            q1_scr[rows, cols] = jnp.where(lo_mask, 0.0, q).astype(BF16)
            k = _rms_mxu(kv_ref[0, rows, cols].astype(F32), DF_DQK) * g_ref[1:2, :]
            k_scr[rows, cols] = _rope_mxu(k, cos, sin, rot).astype(BF16)
            v_scr[rows, 256 * h:256 * h + 128] = kv_ref[0, rows, 512 + 128 * h:512 + 128 * h + 128]
            v_scr[rows, 256 * h + 128:256 * h + 256] = _ones_col(ROW_TILE)
        return carry

    lax.fori_loop(0, T // ROW_TILE, prep, 0)

    lp = lam_ref[...]
    lam = (jnp.exp(jnp.sum(lp[0:1] * lp[1:2], axis=-1, keepdims=True))
           - jnp.exp(jnp.sum(lp[2:3] * lp[3:4], axis=-1, keepdims=True)) + lam_init)
    def scores(u, rows, nk):
        cols = slice(128 * (u // 2), 128 * (u // 2) + 128)
        return _dot_nt((q1_scr if u % 2 else q0_scr)[rows, cols], k_scr[0:nk, cols])

    def attend(rows, nk, s_first, next_rows):
        n_units = 2 * HEADS
        ahead = n_units if next_rows is None else 1
        pending = [scores(0, rows, nk) if s_first is None else s_first]
        pending += [scores(u, rows, nk) for u in range(1, ahead)]
        for h in range(HEADS):
            cols = slice(128 * h, 128 * h + 128)
            v_ext = v_scr[0:nk, 256 * h:256 * h + 256]
            parts = []
            for mp in range(2):
                u = 2 * h + mp
                s = pending.pop(0)
                if u + ahead < n_units:
                    pending.append(scores(u + ahead, rows, nk))
                elif u + ahead == n_units and next_rows is not None:
                    s0_scr[...] = scores(0, next_rows, nk)
                parts.append(_softmax_pv(s, v_ext))
            o = parts[0] - lam * parts[1]
            o = _rms(o, HEAD_V) * og_ref[...] * (1.0 - lam_init)
            z = z_ref[0, rows, cols].astype(F32)
            o_ref[0, rows, cols] = (o * _silu(z)).astype(BF16)

    _attend_all(attend, lambda rows: scores(0, rows, T), s0_scr, o_ref, T, n_ctx, with_ctx)


def _diff(p, qkg, lam_p, og, rope, n_ctx, last, lam_init):
    B, T, _ = p.shape
    full = lambda shape: pl.BlockSpec(shape, lambda b: (0,) * len(shape))
    return pl.pallas_call(
        functools.partial(_diff_kernel, n_ctx=n_ctx, with_ctx=not last, lam_init=lam_init),
        grid=(B,),
        in_specs=[pl.BlockSpec((1, T, 1024), lambda b: (b, 0, P_DF_KV // 1024)),
                  pl.BlockSpec((1, T, 512), lambda b: (b, 0, P_DF_Q // 512)),
                  pl.BlockSpec((1, T, 512), lambda b: (b, 0, P_DF_Z // 512)),
                  full(qkg.shape), full(lam_p.shape), full(og.shape), full(rope.shape)],
        out_specs=pl.BlockSpec((1, T, 512), lambda b: (b, 0, 0)),
        out_shape=jax.ShapeDtypeStruct((B, T, BRANCH_WIDTH), BF16),
        scratch_shapes=[pltpu.VMEM((T, 512), BF16), pltpu.VMEM((T, 512), BF16), pltpu.VMEM((T, 512), BF16),
                        pltpu.VMEM((T, 1024), BF16), pltpu.VMEM((_latent_rows(T, n_ctx), T), F32)],
        compiler_params=_params("arbitrary"),
        name="diff_attn",
    )(p, p, p, qkg, lam_p, og, rope)


def _tile_order(i, n_tiles, rev):
    if not rev:
        return i
    return jnp.where(i == 0, 0, n_tiles - i)


def _chunk_tri(n, rev, chunk=CHUNK):
    r = lax.broadcasted_iota(jnp.int32, (n, n), 0)
    c = lax.broadcasted_iota(jnp.int32, (n, n), 1)
    same = (r // chunk) == (c // chunk)
    tri = (c >= r) if rev else (c <= r)
    return jnp.where(same & tri, 1.0, 0.0).astype(F32)


def _causal(rev, n=CHUNK):
    t = lax.broadcasted_iota(jnp.int32, (n, n), 0)
    s = lax.broadcasted_iota(jnp.int32, (n, n), 1)
    return (s >= t) if rev else (s <= t)


def _cummax_rows(x, rev):
    n = x.shape[0]
    row = lax.broadcasted_iota(jnp.int32, (n, 1), 0)
    s = 1
    while s < n:
        if rev:
            x = jnp.maximum(x, jnp.where(row < n - s, pltpu.roll(x, n - s, 0), -jnp.inf))
        else:
            x = jnp.maximum(x, jnp.where(row >= s, pltpu.roll(x, s, 0), -jnp.inf))
        s *= 2
    return x


def _head_mask(width, group):
    lane = lax.broadcasted_iota(jnp.int32, (1, width), 1) // group
    return [lane == h for h in range(HEADS)]


def _gla_kernel(qk_ref, vz_ref, g_ref, aw_ref, ab_ref, og_ref, o_ref, of_scr, ob_scr, st_scr):
    T = qk_ref.shape[1]
    n_tiles = T // ROW_TILE
    n_chunk = ROW_TILE // CHUNK
    hm = _head_mask(HEADS * GLA_DK, GLA_DK)
    tris = (_chunk_tri(ROW_TILE, False), _chunk_tri(ROW_TILE, True))
    st_scr[...] = jnp.zeros_like(st_scr)

    def load_tile(t, rev):
        rows = pl.ds(pl.multiple_of(t * ROW_TILE, ROW_TILE), ROW_TILE)
        gcol = 256 if rev else 0
        qk = qk_ref[0, rows, :].astype(F32)
        pre = _dot(g_ref[0, rows, :], aw_ref[:, gcol:gcol + 256]) + ab_ref[:, gcol:gcol + 256]
        la = _log_sigmoid(pre) * (1.0 / GLA_TAU)
        hi, lo = _split(la)
        tri = tris[rev].astype(BF16)
        return qk[:, :256] * (GLA_DK ** -0.5), qk[:, 256:], _dot(tri, hi) + _dot(tri, lo)

    def tile(i, carry):
        tiles = (i, _tile_order(i, n_tiles, True))
        steps = []
        for cc in range(n_chunk):
            steps += [(0, cc), (1, n_chunk - 1 - cc)]
        data = [load_tile(tiles[d], bool(d)) for d in (0, 1)]
        n_sub = CHUNK // GLA_SUB
        pre, a_blk, upd, a_msk, intra, o_inter = {}, {}, {}, {}, {}, {}
        st = [st_scr[0], st_scr[1]]

        def prepare(d, cc):
            q, k, b = data[d]
            cr = slice(CHUNK * cc, CHUNK * cc + CHUNK)
            bc, qc, kc = b[cr], q[cr], k[cr]
            b_end = bc[0:1] if d else bc[CHUNK - 1:CHUNK]
            crow = pl.ds(pl.multiple_of(tiles[d] * ROW_TILE + CHUNK * cc, CHUNK), CHUNK)
            q_sub, k_sub = [], []
            for i in range(CHUNK // GLA_SUB):
                sr = slice(GLA_SUB * i, GLA_SUB * i + GLA_SUB)
                first = GLA_SUB * i + (GLA_SUB - 1 if d else 0)
                beta = bc[first:first + 1]
                q_i = qc[sr] * jnp.exp(bc[sr] - beta)
                q_sub.append(jnp.concatenate([jnp.where(hm[h], q_i, 0.0) for h in range(HEADS)],
                                             axis=0).astype(BF16))
                k_sub.append((kc * jnp.exp(jnp.minimum(beta - bc, EXP_CLAMP))).astype(BF16))
            qe = qc * jnp.exp(bc)
            stacked = jnp.concatenate([kc * jnp.exp(b_end - bc), jnp.broadcast_to(jnp.exp(b_end), (8, 256)),
                                       jnp.zeros((CHUNK - 8, 256), F32)], axis=0).T
            pre[d, cc] = dict(
                crow=crow, vc=vz_ref[0, crow, 0:BRANCH_WIDTH], decay=stacked[:, CHUNK:CHUNK + 1],
                qe=[jnp.where(hm[h], qe, 0.0).astype(BF16) for h in range(HEADS)], q_sub=q_sub, k_sub=k_sub,
                k_up_t=stacked[:, 0:CHUNK].astype(BF16))

        def score_dots(d, cc):
            p = pre[d, cc]
            a_blk[d, cc] = [_dot_nt(p['q_sub'][i], p['k_sub'][i]) for i in range(n_sub)]
            upd[d, cc] = jnp.concatenate(
                [_dot(p['k_up_t'][GLA_DK * h:GLA_DK * h + GLA_DK], p['vc'][:, 128 * h:128 * h + 128])
                 for h in range(HEADS)], axis=0)

        def mask_scores(d, cc):
            a_msk[d, cc] = [
                jnp.where(_causal(bool(d)), jnp.concatenate(
                    [a_blk[d, cc][i][GLA_SUB * h:GLA_SUB * h + GLA_SUB] for i in range(n_sub)], axis=0),
                    0.0).astype(BF16) for h in range(HEADS)]

        def value_dots(d, cc):
            vc = pre[d, cc]['vc']
            intra[d, cc] = jnp.concatenate(
                [_dot(a_msk[d, cc][h], vc[:, 128 * h:128 * h + 128]) for h in range(HEADS)], axis=-1)

        def state_dot(d, cc):
            s_bf = st[d].astype(BF16)
            o_inter[d, cc] = jnp.concatenate([_dot(pre[d, cc]['qe'][h], s_bf) for h in range(HEADS)], axis=-1)

        def finish_step(d, cc):
            (ob_scr if d else of_scr)[pre[d, cc]['crow'], :] = o_inter[d, cc] + intra[d, cc]
            st[d] = st[d] * pre[d, cc]['decay'] + upd[d, cc]

        def at(t):
            return [steps[t]] if 0 <= t < len(steps) else []

        for t in range(len(steps) + 4):
            for g in at(t - 1):
                score_dots(*g)
            for g in at(t - 3):
                value_dots(*g)
            for g in at(t - 4):
                state_dot(*g)
            for g in at(t):
                prepare(*g)
            for g in at(t - 2):
                mask_scores(*g)
            for g in at(t - 4):
                finish_step(*g)
        st_scr[0] = st[0]
        st_scr[1] = st[1]
        return carry

    lax.fori_loop(0, n_tiles, tile, 0)

    def finish(i, carry):
        rows = pl.ds(pl.multiple_of(i * ROW_TILE, ROW_TILE), ROW_TILE)
        tot = of_scr[rows, :] + ob_scr[rows, :]
        z = vz_ref[0, rows, BRANCH_WIDTH:].astype(F32)
        for h in range(HEADS):
            cols = slice(128 * h, 128 * h + 128)
            y = _rms(tot[:, cols], HEAD_V) * og_ref[:, cols]
            o_ref[0, rows, cols] = (y * _silu(z[:, cols])).astype(BF16)
        return carry

    lax.fori_loop(0, n_tiles, finish, 0)


def _gla(p, aw, ab, og):
    B, T, _ = p.shape
    full = lambda shape: pl.BlockSpec(shape, lambda b: (0,) * len(shape))
    return pl.pallas_call(
        _gla_kernel,
        grid=(B,),
        in_specs=[pl.BlockSpec((1, T, 512), lambda b: (b, 0, P_GLA_QK // 512)),
                  pl.BlockSpec((1, T, 1024), lambda b: (b, 0, P_GLA_VZ // 1024)),
                  pl.BlockSpec((1, T, 128), lambda b: (b, 0, P_GLA_G // 128)),
                  full(aw.shape), full(ab.shape), full(og.shape)],
        out_specs=pl.BlockSpec((1, T, 512), lambda b: (b, 0, 0)),
        out_shape=jax.ShapeDtypeStruct((B, T, BRANCH_WIDTH), BF16),
        scratch_shapes=[pltpu.VMEM((T, BRANCH_WIDTH), F32), pltpu.VMEM((T, BRANCH_WIDTH), F32),
                        pltpu.VMEM((2, HEADS * GLA_DK, HEAD_V), F32)],
        compiler_params=_params("arbitrary"),
        name="gla_scan",
    )(p, p, p, aw, ab, og)


def _mlstm_kernel(m_ref, if_ref, cw_ref, cb_ref, wq_ref, wkt_ref, gb_ref, og_ref, sk_ref, o_ref,
                  xc_scr, q_scr, kt_scr, at_scr, bt_scr, cm_scr, b_scr, hf_scr, hb_scr, c_scr, *, n_ctx):
    T = m_ref.shape[1]
    n_tiles = T // ROW_TILE
    assert ML_CHUNK == ROW_TILE
    ctx_tiles = n_ctx // ROW_TILE
    hm = _head_mask(HEADS * ML_DQK, ML_DQK)
    lane128 = lax.broadcasted_iota(jnp.int32, (1, 128), 1)
    is_forget = ((lane128 // HEADS) % 2) == 1
    row_in_tile = lax.broadcasted_iota(jnp.int32, (ROW_TILE, 1), 0)
    tri_f = _chunk_tri(ROW_TILE, False, ML_CHUNK).astype(BF16)
    tri_b = _chunk_tri(ROW_TILE, True, ML_CHUNK).astype(BF16)

    def prep(i, carry):
        r = pl.multiple_of(i * ROW_TILE, ROW_TILE)
        rows = pl.ds(r, ROW_TILE)
        x = m_ref[0, rows, 0:512].astype(F32)
        rp = pl.multiple_of(jnp.maximum(r - 16, 0), 16)
        rn = pl.multiple_of(jnp.minimum(r + ROW_TILE, T - 16), 16)
        prev_row = m_ref[0, pl.ds(rp, 16), 0:512].astype(F32)[15:16]
        next_row = m_ref[0, pl.ds(rn, 16), 0:512].astype(F32)[0:1]
        seg_start = (i == 0) | (i == ctx_tiles)
        seg_end = (i == ctx_tiles - 1) | (i == n_tiles - 1)
        prev_row = jnp.where(seg_start, 0.0, prev_row)
        next_row = jnp.where(seg_end, 0.0, next_row)
        xm = jnp.where(row_in_tile == 0, prev_row, pltpu.roll(x, 1, 0))
        xp = jnp.where(row_in_tile == ROW_TILE - 1, next_row, pltpu.roll(x, ROW_TILE - 1, 0))
        xc = _silu(cw_ref[0:1, :] * xm + cw_ref[1:2, :] * x + cw_ref[2:3, :] * xp + cb_ref[...])
        xc_scr[rows, :] = xc
        xb = xc.astype(BF16)
        q_scr[rows, :] = _dot(xb, wq_ref[...]).astype(BF16)
        g = if_ref[0, rows, :].astype(F32) + gb_ref[...]
        g2 = jnp.where(is_forget, _log_sigmoid(g), g)
        g_hi, g_lo = _split(g2)
        cs = jnp.where(lane128 < 2 * HEADS, _dot(tri_f, g_hi) + _dot(tri_f, g_lo),
                       _dot(tri_b, g_hi) + _dot(tri_b, g_lo))
        b = pltpu.roll(cs, 128 - HEADS, 1)
        a = g2 - b
        cm_scr[rows, :] = jnp.where(lane128 < 2 * HEADS, _cummax_rows(a, False), _cummax_rows(a, True))
        b_scr[rows, :] = b
        kt_scr[i] = (_dot_nt(wkt_ref[...], xb) * (ML_DQK ** -0.5)).astype(BF16)
        at_scr[i] = a.T[0:16, :]
        bt_scr[i] = b.T[0:16, :]
        return carry

    lax.fori_loop(0, n_tiles, prep, 0)

    c_scr[...] = jnp.zeros_like(c_scr)
    ones_col = _ones_col(ML_CHUNK)
    sel_r = lax.broadcasted_iota(jnp.int32, (256, HEADS * 128), 0) % 128
    sel_c = lax.broadcasted_iota(jnp.int32, (256, HEADS * 128), 1) // 128
    sel = [jnp.where(sel_r == 2 * HEADS * d + sel_c, 1.0, 0.0).astype(BF16) for d in (0, 1)]

    def col_bcast(x, d):
        hi, lo = _split(x)
        return _dot(jnp.concatenate([hi, lo], axis=-1), sel[d])

    def chunk_pair(c_f, c_b, m_in):
        units = [(d, h) for d in (0, 1) for h in range(HEADS)]
        cs = (c_f, c_b)
        crow = [pl.ds(pl.multiple_of(c * ML_CHUNK, ML_CHUNK), ML_CHUNK) for c in cs]
        qc = [q_scr[crow[d], :] for d in (0, 1)]
        kt = [kt_scr[cs[d]] for d in (0, 1)]
        vc = [m_ref[0, crow[d], 512:1024] for d in (0, 1)]
        at = [at_scr[cs[d]] for d in (0, 1)]
        bt = [bt_scr[cs[d]] for d in (0, 1)]
        c_bf = [c_scr[d].astype(BF16) for d in (0, 1)]
        causal = [_causal(False, ML_CHUNK), _causal(True, ML_CHUNK)]
        last = [ML_CHUNK - 1, 0]
        qh = {(d, h): jnp.where(hm[h], qc[d], jnp.zeros_like(qc[d])) for d, h in units}
        v_ext = {(d, h): jnp.concatenate([vc[d][:, 128 * h:128 * h + 128], ones_col], axis=-1)
                 for d, h in units}
        s_raw = {u: _dot(qh[u], kt[u[0]]) for u in units}
        q_c = {u: _dot(qh[u], c_bf[u[0]]) for u in units}
        cm_col = [col_bcast(cm_scr[crow[d], :], d) for d in (0, 1)]
        b_col = [col_bcast(b_scr[crow[d], :], d) for d in (0, 1)]
        s_w, m_run, m_last, ktw = {}, {}, {}, {}
        for d, h in units:
            j = 2 * HEADS * d + h
            a_row, m_old = at[d][j:j + 1, :], m_in[HEADS * d + h]
            pm = jnp.where(causal[d], a_row, -jnp.inf)
            m_run[d, h] = jnp.maximum(m_old, cm_col[d][:, 128 * h:128 * h + 128])
            m_wide = jnp.concatenate([m_run[d, h]] * (ML_CHUNK // 128), axis=-1)
            s_w[d, h] = (s_raw[d, h] * jnp.exp(pm - m_wide)).astype(BF16)
            m_last[d, h] = m_run[d, h][last[d]:last[d] + 1, 0:1]
            k_h = kt[d][ML_DQK * h:ML_DQK * h + ML_DQK, :].astype(F32)
            ktw[d, h] = (k_h * jnp.exp(a_row - m_last[d, h])).astype(BF16)
        s_v = {u: _dot(s_w[u], v_ext[u]) for u in units}
        upd = {u: _dot(ktw[u], v_ext[u]) for u in units}
        m_out = []
        for d in (0, 1):
            parts = []
            for h in range(HEADS):
                j = 2 * HEADS * d + h
                hr = slice(ML_DQK * h, ML_DQK * h + ML_DQK)
                b_row, m_old = bt[d][j:j + 1, :], m_in[HEADS * d + h]
                carry_w = jnp.exp(m_old - m_run[d, h])
                num = jnp.concatenate([carry_w, carry_w], axis=-1) * q_c[d, h] + s_v[d, h]
                den = num[:, HEAD_V:HEAD_V + 1]
                floor = jnp.exp(-(b_col[d][:, 128 * h:128 * h + 128] + m_run[d, h]))
                parts.append(num[:, 0:HEAD_V] / jnp.maximum(jnp.abs(den), floor))
                c_scr[d, hr, :] = jnp.exp(m_old - m_last[d, h]) * c_scr[d, hr, :] + upd[d, h]
                m_out.append(b_row[:, last[d]:last[d] + 1] + m_last[d, h])
            (hb_scr if d else hf_scr)[crow[d], :] = jnp.concatenate(parts, axis=-1)
        return m_out

    def tile(i, carry):
        return tuple(chunk_pair(i, _tile_order(i, n_tiles, True), list(carry)))

    lax.fori_loop(0, n_tiles, tile, tuple(jnp.zeros((1, 1), F32) for _ in range(2 * HEADS)))

    def finish(i, carry):
        rows = pl.ds(pl.multiple_of(i * ROW_TILE, ROW_TILE), ROW_TILE)
        tot = (hf_scr[rows, :] + hb_scr[rows, :]) * _sigmoid(m_ref[0, rows, 1024:1536].astype(F32))
        z = m_ref[0, rows, 1536:2048].astype(F32)
        xc = xc_scr[rows, :]
        for h in range(HEADS):
            cols = slice(128 * h, 128 * h + 128)
            y = _rms(tot[:, cols], HEAD_V) * og_ref[:, cols]
            y = (y + sk_ref[:, cols] * xc[:, cols]) * _silu(z[:, cols])
            o_ref[0, rows, cols] = y.astype(BF16)
        return carry

    lax.fori_loop(0, n_tiles, finish, 0)


def _mlstm(p, cw, cb, wq, wkt, gb, og, sk, n_ctx):
    B, T, _ = p.shape
    full = lambda shape: pl.BlockSpec(shape, lambda b: (0,) * len(shape))
    return pl.pallas_call(
        functools.partial(_mlstm_kernel, n_ctx=n_ctx),
        grid=(B,),
        in_specs=[pl.BlockSpec((1, T, 2048), lambda b: (b, 0, P_ML // 2048)),
                  pl.BlockSpec((1, T, 128), lambda b: (b, 0, P_ML_IF // 128)),
                  full(cw.shape), full(cb.shape), full(wq.shape), full(wkt.shape), full(gb.shape),
                  full(og.shape), full(sk.shape)],
        out_specs=pl.BlockSpec((1, T, 512), lambda b: (b, 0, 0)),
        out_shape=jax.ShapeDtypeStruct((B, T, BRANCH_WIDTH), BF16),
        scratch_shapes=[pltpu.VMEM((T, BRANCH_WIDTH), F32),
                        pltpu.VMEM((T, HEADS * ML_DQK), BF16),
                        pltpu.VMEM((T // ML_CHUNK, HEADS * ML_DQK, ML_CHUNK), BF16),
                        pltpu.VMEM((T // ML_CHUNK, 16, ML_CHUNK), F32),
                        pltpu.VMEM((T // ML_CHUNK, 16, ML_CHUNK), F32),
                        pltpu.VMEM((T, 128), F32),
                        pltpu.VMEM((T, 128), F32),
                        pltpu.VMEM((T, BRANCH_WIDTH), F32),
                        pltpu.VMEM((T, BRANCH_WIDTH), F32),
                        pltpu.VMEM((2, HEADS * ML_DQK, 2 * HEAD_V), F32)],
        compiler_params=_params("arbitrary"),
        name="mlstm_scan",
    )(p, p, cw, cb, wq, wkt, gb, og, sk)


def _merge_kernel(ya_ref, yb_ref, yc_ref, yd_ref, gl_ref, brw_ref, wo_ref, x_ref, gate_ref, o_ref,
                  *, row0, ctx_rows):
    tm = x_ref.shape[1]
    first = pl.program_id(1) * tm + row0
    acc = None
    for i, y_ref in enumerate((ya_ref, yb_ref, yc_ref, yd_ref)):
        u = _dot(y_ref[0], brw_ref[i])
        gsig = _sigmoid(gl_ref[0, :, D_MODEL * i:D_MODEL * (i + 1)].astype(F32))
        acc = gsig * u if acc is None else acc + gsig * u
    out = _dot(acc.astype(BF16), wo_ref[...])
    for r0 in range(0, tm, ROW_TILE):
        rs = slice(r0, r0 + ROW_TILE)
        gate = jnp.where(first + r0 < ctx_rows, gate_ref[0, 0:1, :], gate_ref[0, 1:2, :])
        o_ref[0, rs, :] = x_ref[0, rs, :] + gate * out[rs]


def _merge(ys, p, brw, wo, xs, gates, n_ctx, last):
    B, T, D = xs.shape
    tm = ROW_TILE if last else MERGE_ROWS
    row0 = n_ctx if last else 0
    tile0 = row0 // tm
    nt = (T - row0) // tm
    assert (T - row0) % tm == 0 and row0 % tm == 0 and n_ctx % ROW_TILE == 0
    ymap = lambda b, t: (b, t + tile0, 0)
    return pl.pallas_call(
        functools.partial(_merge_kernel, row0=row0, ctx_rows=n_ctx),
        grid=(B, nt),
        in_specs=[pl.BlockSpec((1, tm, BRANCH_WIDTH), ymap)] * 4 + [
            pl.BlockSpec((1, tm, N_BRANCH * D), lambda b, t: (b, t + tile0, P_MERGE // (N_BRANCH * D))),
            pl.BlockSpec(brw.shape, lambda b, t: (0, 0, 0)),
            pl.BlockSpec(wo.shape, lambda b, t: (0, 0)),
            pl.BlockSpec((1, tm, D), ymap),
            pl.BlockSpec((1, 2, D), lambda b, t: (b, 0, 0))],
        out_specs=pl.BlockSpec((1, tm, D), lambda b, t: (b, t, 0)),
        out_shape=jax.ShapeDtypeStruct((B, nt * tm, D), F32),
        compiler_params=_params("arbitrary", "arbitrary"),
        name="merge",
    )(*ys, p, brw, wo, xs, gates)


def _layout_w_in(w_in):
    offs, off = {}, 0
    for name, w in IN_SPLITS:
        offs[name] = (off, w)
        off += w

    w_bf = w_in.astype(BF16)

    def col(name):
        o, w = offs[name]
        return w_bf[..., o:o + w]

    def zeros(n):
        return jnp.zeros(w_in.shape[:-1] + (n,), BF16)

    parts = [col('merge'), col('df_k'), col('df_v'), col('df_q'), col('df_z'),
             col('ml_x'), col('ml_v'), col('ml_o'), col('ml_z'),
             col('gla_v'), col('gla_z'), col('mla_z'), col('gla_q'), col('gla_k'),
             col('mla_cq'), col('mla_kr'), zeros(64),
             col('mla_ckv'), col('gla_af'), col('gla_ab'), zeros(96), col('ml_if'), zeros(112)]
    out = jnp.concatenate(parts, axis=-1)
    assert out.shape[-1] == P_WIDTH
    return out


def _rope_tables(rows, n_ctx):
    quarter = ROT_DIM // 4
    inv_freq = ROPE_BASE ** (-jnp.arange(quarter, dtype=F32) / quarter)
    row = jnp.repeat(jnp.arange(rows, dtype=F32), GRID_W)
    col = jnp.tile(jnp.arange(GRID_W, dtype=F32), rows)
    ar = row[:, None] * inv_freq
    ac = col[:, None] * inv_freq
    ang = jnp.concatenate([ar, ar, ac, ac], axis=-1)
    cos = jnp.concatenate([jnp.ones((n_ctx, ROT_DIM), F32), jnp.cos(ang)], axis=0)
    sin = jnp.concatenate([jnp.zeros((n_ctx, ROT_DIM), F32), jnp.sin(ang)], axis=0)
    zero = jnp.zeros_like(cos)
    both = jnp.stack([jnp.tile(cos, (1, 2)), jnp.tile(sin, (1, 2))])
    half = jnp.stack([jnp.concatenate([cos, zero], -1), jnp.concatenate([sin, zero], -1)])
    return half, both


def _pad_lanes(v, n):
    return jnp.concatenate([v, jnp.zeros(v.shape[:-1] + (n - v.shape[-1],), v.dtype)], axis=-1)


def kernel(x, c, ctx, c_ctx, ada_w, ada_b, norm_g, w_in, mla_cq_g, mla_ckv_g, mla_wuq, mla_wukv, mla_q_g,
           mla_k_g, gla_a_w, gla_a_b, gla_out_g, ml_conv_w, ml_conv_b, ml_wq, ml_wk, ml_gate_b, ml_out_g,
           ml_skip, df_qk_g, df_lambda, df_out_g, br_w, w_out):
    B, S, D = x.shape
    n_ctx = ctx.shape[1]
    L = ada_w.shape[0]
    assert D == D_MODEL and n_ctx == ROW_TILE and S % ROW_TILE == 0 and S % GRID_W == 0

    rope_half, rope_both = _rope_tables(S // GRID_W, n_ctx)
    w_in_p = _layout_w_in(w_in)

    n_rows = -(-(B + 1) // 8) * 8
    cc = jnp.concatenate([c, c_ctx[None], jnp.zeros((n_rows - B - 1, D), F32)], axis=0)
    mod_all = _modulation(cc, ada_w, ada_b)

    xs = jnp.concatenate([ctx, x], axis=1)
    for l in range(L):
        last = l == L - 1
        lam_init = 0.8 - 0.6 * math.exp(-0.3 * l)
        m3 = mod_all[l].reshape(n_rows, 3, D)
        lat, cx = m3[:B], jnp.broadcast_to(m3[B][None], (B, 3, D))
        mod = jnp.concatenate([cx, lat, jnp.zeros((B, 2, D), F32)], axis=1)
        gates = jnp.stack([cx[:, 2], lat[:, 2]], axis=1)

        p = _in_projection(xs, mod, norm_g[l], w_in_p, l, n_ctx)

        wq4 = mla_wuq[l].reshape(MLA_Q_LORA, HEADS, MLA_NOPE + ROT_DIM)
        wuq = _pad_lanes(wq4, 256).reshape(MLA_Q_LORA, HEADS * 256).astype(BF16)
        wkv4 = mla_wukv[l].reshape(MLA_KV_LORA, HEADS, MLA_NOPE + HEAD_V)
        wukv = jnp.concatenate([wkv4[..., :MLA_NOPE].reshape(MLA_KV_LORA, -1),
                                wkv4[..., MLA_NOPE:].reshape(MLA_KV_LORA, -1)], axis=-1).astype(BF16)
        qg = jnp.stack([mla_q_g[l, :MLA_NOPE], _pad_lanes(mla_q_g[l, MLA_NOPE:], 128)])
        kg = jnp.stack([mla_k_g[l, :MLA_NOPE], _pad_lanes(mla_k_g[l, MLA_NOPE:], 128)])
        y_mla = _mla(p, wuq, wukv, mla_cq_g[l][None], mla_ckv_g[l][None], qg, kg, rope_half, n_ctx, last)

        qkg = jnp.tile(df_qk_g[l], (1, 2))
        y_df = _diff(p, qkg, df_lambda[l], df_out_g[l][None], rope_both, n_ctx, last, lam_init)

        aw = jnp.zeros((128, 512), F32)
        aw = aw.at[0:16, 0:256].set(gla_a_w[l, 0]).at[16:32, 256:512].set(gla_a_w[l, 1]).astype(BF16)
        y_gla = _gla(p, aw, gla_a_b[l].reshape(1, 512), gla_out_g[l][None])

        wq_bd = jnp.zeros((BRANCH_WIDTH, HEADS * ML_DQK), F32)
        wk_bd = jnp.zeros((BRANCH_WIDTH, HEADS * ML_DQK), F32)
        for h in range(HEADS):
            wq_bd = wq_bd.at[128 * h:128 * h + 128, 64 * h:64 * h + 64].set(ml_wq[l, h])
            wk_bd = wk_bd.at[128 * h:128 * h + 128, 64 * h:64 * h + 64].set(ml_wk[l, h])
        gb = _pad_lanes(ml_gate_b[l].reshape(1, 16), 128)
        y_ml = _mlstm(p, ml_conv_w[l], ml_conv_b[l][None], wq_bd.astype(BF16), wk_bd.T.astype(BF16), gb,
                      ml_out_g[l][None], ml_skip[l][None], n_ctx)

        xs = _merge((y_mla, y_gla, y_ml, y_df), p, br_w[l].astype(BF16), w_out[l].astype(BF16), xs, gates,
                    n_ctx, last)
    return xs
```

```python
import functools
import math

import jax
import jax.numpy as jnp
from jax import lax
from jax.experimental import pallas as pl
from jax.experimental.pallas import tpu as pltpu

F32 = jnp.float32
BF16 = jnp.bfloat16

D_MODEL = 1024
GRID_W = 64
EPS = 1e-6
ROPE_BASE = 10000.0
ROT_DIM = 64
CHUNK = 64
N_BRANCH = 4
BRANCH_WIDTH = 512
HEADS = 4
HEAD_V = BRANCH_WIDTH // HEADS
MLA_NOPE = 128
MLA_Q_LORA = 384
MLA_KV_LORA = 256
GLA_DK = 64
GLA_TAU = 16.0
ML_DQK = 64
DF_DQK = 64

IN_SPLITS = (
    ('mla_cq', 384), ('mla_ckv', 256), ('mla_kr', 64), ('mla_z', 512),
    ('gla_q', 256), ('gla_k', 256), ('gla_v', 512), ('gla_af', 16), ('gla_ab', 16), ('gla_z', 512),
    ('ml_x', 512), ('ml_v', 512), ('ml_o', 512), ('ml_if', 16), ('ml_z', 512),
    ('df_q', 512), ('df_k', 512), ('df_v', 512), ('df_z', 512),
    ('merge', 4096),
)

P_MERGE = 0
P_DF_KV = 4096
P_DF_Q = 5120
P_DF_Z = 5632
P_ML = 6144
P_GLA_VZ = 8192
P_MLA_Z = 9216
P_GLA_QK = 9728
P_MLA_A = 10240
P_MLA_CKV = 10752
P_GLA_G = 11008
P_ML_IF = 11136
P_WIDTH = 11264

ROW_TILE = 256
IN_PROJ_COLS = 1024
MERGE_ROWS = 768
ML_CHUNK = 256
ATT_ROWS = 512
VMEM_LIMIT = 56 * 1024 * 1024
EXP_CLAMP = 80.0
GLA_SUB = 16


def _dot(a, b):
    return jnp.dot(a, b, preferred_element_type=F32)


def _dot_nt(a, b):
    return lax.dot_general(a, b, (((1,), (1,)), ((), ())), preferred_element_type=F32)


def _sigmoid(x):
    return 1.0 / (1.0 + jnp.exp(-x))


def _silu(x):
    return x * _sigmoid(x)


def _log_sigmoid(x):
    return jnp.minimum(x, 0.0) - jnp.log(1.0 + jnp.exp(-jnp.abs(x)))


def _rms(x, n):
    return x * lax.rsqrt(jnp.sum(x * x, axis=-1, keepdims=True) * (1.0 / n) + EPS)


def _split(x):
    hi = x.astype(BF16)
    return hi, (x - hi.astype(F32)).astype(BF16)


def _group_sum(x, group):
    row = lax.broadcasted_iota(jnp.int32, (128, 128), 0)
    col = lax.broadcasted_iota(jnp.int32, (128, 128), 1)
    sel = jnp.where(row // group == col // group, 1.0, 0.0).astype(BF16)
    hi, lo = _split(x)
    return _dot(hi, sel) + _dot(lo, sel)


def _rms_mxu(x, group):
    return x * lax.rsqrt(_group_sum(x * x, group) * (1.0 / group) + EPS)


def _rot_matrix():
    src = lax.broadcasted_iota(jnp.int32, (128, 128), 0)
    dst = lax.broadcasted_iota(jnp.int32, (128, 128), 1)
    even = (dst // 16) % 2 == 0
    return jnp.where(even & (src == dst + 16), -1.0, jnp.where(~even & (src == dst - 16), 1.0, 0.0)).astype(BF16)


def _rope_mxu(x, cos, sin, rot):
    hi, lo = _split(x)
    return x * cos + (_dot(hi, rot) + _dot(lo, rot)) * sin


def _ones_col(rows):
    return jnp.where(lax.broadcasted_iota(jnp.int32, (rows, HEAD_V), 1) == 0, 1.0, 0.0).astype(BF16)


def _softmax_pv(s, v_ext):
    e = jnp.exp((s - jnp.max(s, axis=-1, keepdims=True)).astype(BF16))
    o = _dot(e, v_ext)
    return o[:, 0:HEAD_V] / o[:, HEAD_V:HEAD_V + 1]


def _params(*sem):
    return pltpu.CompilerParams(dimension_semantics=sem, vmem_limit_bytes=VMEM_LIMIT)


def _mod_kernel(c_ref, w_ref, b_ref, o_ref):
    s = _silu(c_ref[...])
    o_ref[0] = _dot(s.astype(BF16), w_ref[0].astype(BF16)) + b_ref[0]


def _modulation(cc, ada_w, ada_b):
    L, D, _ = ada_w.shape
    R = cc.shape[0]
    return pl.pallas_call(
        _mod_kernel,
        grid=(L, 3),
        in_specs=[pl.BlockSpec((R, D), lambda l, j: (0, 0)),
                  pl.BlockSpec((1, D, D), lambda l, j: (l, 0, j)),
                  pl.BlockSpec((1, 1, D), lambda l, j: (l, 0, j))],
        out_specs=pl.BlockSpec((1, R, D), lambda l, j: (l, 0, j)),
        out_shape=jax.ShapeDtypeStruct((L, R, 3 * D), F32),
        compiler_params=_params("arbitrary", "arbitrary"),
        name="modulation",
    )(cc, ada_w, ada_b.reshape(L, 1, 3 * D))


def _inproj_kernel(x_ref, mod_ref, g_ref, w_ref, o_ref, h_scr, *, n_ctx):
    T = x_ref.shape[1]
    first_cols = pl.program_id(1) == 0

    def normalise(r0):
        k = 0 if r0 < n_ctx else 3
        y = _rms(x_ref[0, r0:r0 + ROW_TILE, :], D_MODEL) * g_ref[...]
        h = y * (1.0 + mod_ref[0, k + 1:k + 2, :]) + mod_ref[0, k:k + 1, :]
        h_scr[r0:r0 + ROW_TILE, :] = h.astype(BF16)

    def project(r0):
        o_ref[0, r0:r0 + ROW_TILE, :] = _dot(h_scr[r0:r0 + ROW_TILE, :], w_ref[0]).astype(BF16)

    @pl.when(first_cols)
    def _():
        normalise(0)
        for r0 in range(0, T, ROW_TILE):
            project(r0)
            if r0 + ROW_TILE < T:
                normalise(r0 + ROW_TILE)

    @pl.when(jnp.logical_not(first_cols))
    def _():
        for r0 in range(0, T, ROW_TILE):
            project(r0)


def _in_projection(xs, mod, norm_g, w_all, layer, n_ctx):
    B, T, D = xs.shape
    n_col = P_WIDTH // IN_PROJ_COLS
    return pl.pallas_call(
        functools.partial(_inproj_kernel, n_ctx=n_ctx),
        grid=(B, n_col),
        in_specs=[pl.BlockSpec((1, T, D), lambda b, j: (b, 0, 0)),
                  pl.BlockSpec((1, 8, D), lambda b, j: (b, 0, 0)),
                  pl.BlockSpec((1, D), lambda b, j: (0, 0)),
                  pl.BlockSpec((1, D, IN_PROJ_COLS), lambda b, j: (layer, 0, j))],
        out_specs=pl.BlockSpec((1, T, IN_PROJ_COLS), lambda b, j: (b, 0, j)),
        out_shape=jax.ShapeDtypeStruct((B, T, P_WIDTH), BF16),
        scratch_shapes=[pltpu.VMEM((T, D), BF16)],
        compiler_params=_params("arbitrary", "arbitrary"),
        name="in_projection",
    )(xs, mod, norm_g.reshape(1, D), w_all)


def _latent_rows(T, n_ctx):
    return ATT_ROWS if (T - n_ctx) % ATT_ROWS == 0 else ROW_TILE


def _attend_all(attend, scores0, s0_scr, o_ref, T, n_ctx, with_ctx):
    if with_ctx:
        attend(pl.ds(0, n_ctx), n_ctx, None, None)
    else:
        o_ref[0, 0:n_ctx, :] = jnp.zeros((n_ctx, o_ref.shape[2]), o_ref.dtype)
    rows_per = s0_scr.shape[0]
    n = (T - n_ctx) // rows_per

    def rows_of(i):
        return pl.ds(pl.multiple_of(n_ctx + i * rows_per, ROW_TILE), rows_per)

    s0_scr[...] = scores0(rows_of(0))

    def latent(i, carry):
        attend(rows_of(i), T, s0_scr[...], rows_of(jnp.minimum(i + 1, n - 1)))
        return carry

    lax.fori_loop(0, n, latent, 0)


def _mla_kernel(pa_ref, pc_ref, z_ref, wuq_ref, wukv_ref, cqg_ref, ckvg_ref, qg_ref, kg_ref, rope_ref,
                o_ref, q_scr, k_scr, v_scr, s0_scr, *, n_ctx, with_ctx):
    T = pa_ref.shape[1]
    scale = (MLA_NOPE + ROT_DIM) ** -0.5
    rot = _rot_matrix()

    def prep(i, carry):
        r = pl.multiple_of(i * ROW_TILE, ROW_TILE)
        rows = pl.ds(r, ROW_TILE)
        cos, sin = rope_ref[0, rows, :], rope_ref[1, rows, :]
        pa = pa_ref[0, rows, :].astype(F32)
        cq = (_rms(pa[:, :MLA_Q_LORA], MLA_Q_LORA) * cqg_ref[...]).astype(BF16)
        q = _dot(cq, wuq_ref[...])
        kr = _rms(pa[:, MLA_Q_LORA:], ROT_DIM) * kg_ref[1:2, :]
        kr = _rope_mxu(kr, cos, sin, rot).astype(BF16)
        ckv = (_rms(pc_ref[0, rows, :].astype(F32), MLA_KV_LORA) * ckvg_ref[...]).astype(BF16)
        kv = _dot(ckv, wukv_ref[...])
        for h in range(HEADS):
            qn = _rms(q[:, 256 * h:256 * h + 128], MLA_NOPE) * qg_ref[0:1, :]
            qr = _rms(q[:, 256 * h + 128:256 * h + 256], ROT_DIM) * qg_ref[1:2, :]
            qr = _rope_mxu(qr, cos, sin, rot)
            q_scr[rows, 256 * h:256 * h + 128] = (qn * scale).astype(BF16)
            q_scr[rows, 256 * h + 128:256 * h + 256] = (qr * scale).astype(BF16)
            kn = _rms(kv[:, 128 * h:128 * h + 128], MLA_NOPE) * kg_ref[0:1, :]
            k_scr[rows, 256 * h:256 * h + 128] = kn.astype(BF16)
            k_scr[rows, 256 * h + 128:256 * h + 256] = kr
            v_scr[rows, 256 * h:256 * h + 128] = kv[:, 512 + 128 * h:512 + 128 * h + 128].astype(BF16)
            v_scr[rows, 256 * h + 128:256 * h + 256] = _ones_col(ROW_TILE)
        return carry

    lax.fori_loop(0, T // ROW_TILE, prep, 0)

    def scores(h, rows, nk):
        return _dot_nt(q_scr[rows, 256 * h:256 * h + 256], k_scr[0:nk, 256 * h:256 * h + 256])

    def attend(rows, nk, s_first, next_rows):
        ahead = HEADS if next_rows is None else 1
        pending = [scores(0, rows, nk) if s_first is None else s_first]
        pending += [scores(h, rows, nk) for h in range(1, ahead)]
        for h in range(HEADS):
            s = pending.pop(0)
            if h + ahead < HEADS:
                pending.append(scores(h + ahead, rows, nk))
            elif h + ahead == HEADS and next_rows is not None:
                s0_scr[...] = scores(0, next_rows, nk)
            o = _softmax_pv(s, v_scr[0:nk, 256 * h:256 * h + 256])
            z = z_ref[0, rows, 128 * h:128 * h + 128].astype(F32)
            o_ref[0, rows, 128 * h:128 * h + 128] = (o * _silu(z)).astype(BF16)

    _attend_all(attend, lambda rows: scores(0, rows, T), s0_scr, o_ref, T, n_ctx, with_ctx)


def _mla(p, wuq, wukv, cqg, ckvg, qg, kg, rope, n_ctx, last):
    B, T, _ = p.shape
    full = lambda shape: pl.BlockSpec(shape, lambda b: (0,) * len(shape))
    return pl.pallas_call(
        functools.partial(_mla_kernel, n_ctx=n_ctx, with_ctx=not last),
        grid=(B,),
        in_specs=[pl.BlockSpec((1, T, 512), lambda b: (b, 0, P_MLA_A // 512)),
                  pl.BlockSpec((1, T, 256), lambda b: (b, 0, P_MLA_CKV // 256)),
                  pl.BlockSpec((1, T, 512), lambda b: (b, 0, P_MLA_Z // 512)),
                  full(wuq.shape), full(wukv.shape), full(cqg.shape), full(ckvg.shape),
                  full(qg.shape), full(kg.shape), full(rope.shape)],
        out_specs=pl.BlockSpec((1, T, 512), lambda b: (b, 0, 0)),
        out_shape=jax.ShapeDtypeStruct((B, T, BRANCH_WIDTH), BF16),
        scratch_shapes=[pltpu.VMEM((T, 1024), BF16), pltpu.VMEM((T, 1024), BF16),
                        pltpu.VMEM((T, 1024), BF16), pltpu.VMEM((_latent_rows(T, n_ctx), T), F32)],
        compiler_params=_params("arbitrary"),
        name="mla",
    )(p, p, p, wuq, wukv, cqg, ckvg, qg, kg, rope)


def _diff_kernel(kv_ref, q_ref, z_ref, g_ref, lam_ref, og_ref, rope_ref, o_ref, q0_scr, q1_scr, k_scr, v_scr, s0_scr,
                 *, n_ctx, with_ctx, lam_init):
    T = kv_ref.shape[1]
    scale = DF_DQK ** -0.5
    lo_mask = lax.broadcasted_iota(jnp.int32, (1, 128), 1) < 64
    rot = _rot_matrix()

    def prep(i, carry):
        r = pl.multiple_of(i * ROW_TILE, ROW_TILE)
        rows = pl.ds(r, ROW_TILE)
        cos, sin = rope_ref[0, rows, :], rope_ref[1, rows, :]
        for h in range(HEADS):
            cols = slice(128 * h, 128 * h + 128)
            q = _rms_mxu(q_ref[0, rows, cols].astype(F32), DF_DQK) * g_ref[0:1, :]
            q = _rope_mxu(q, cos, sin, rot) * scale
            q0_scr[rows, cols] = jnp.where(lo_mask, q, 0.0).astype(BF16)
            q1_scr[rows, cols] = jnp.where(lo_mask, 0.0, q).astype(BF16)
            k = _rms_mxu(kv_ref[0, rows, cols].astype(F32), DF_DQK) * g_ref[1:2, :]
            k_scr[rows, cols] = _rope_mxu(k, cos, sin, rot).astype(BF16)
            v_scr[rows, 256 * h:256 * h + 128] = kv_ref[0, rows, 512 + 128 * h:512 + 128 * h + 128]
            v_scr[rows, 256 * h + 128:256 * h + 256] = _ones_col(ROW_TILE)
        return carry

    lax.fori_loop(0, T // ROW_TILE, prep, 0)

    lp = lam_ref[...]
    lam = (jnp.exp(jnp.sum(lp[0:1] * lp[1:2], axis=-1, keepdims=True))
           - jnp.exp(jnp.sum(lp[2:3] * lp[3:4], axis=-1, keepdims=True)) + lam_init)
    def scores(u, rows, nk):
        cols = slice(128 * (u // 2), 128 * (u // 2) + 128)
        return _dot_nt((q1_scr if u % 2 else q0_scr)[rows, cols], k_scr[0:nk, cols])

    def attend(rows, nk, s_first, next_rows):
        n_units = 2 * HEADS
        ahead = n_units if next_rows is None else 1
        pending = [scores(0, rows, nk) if s_first is None else s_first]
        pending += [scores(u, rows, nk) for u in range(1, ahead)]
        for h in range(HEADS):
            cols = slice(128 * h, 128 * h + 128)
            v_ext = v_scr[0:nk, 256 * h:256 * h + 256]
            parts = []
            for mp in range(2):
                u = 2 * h + mp
                s = pending.pop(0)
                if u + ahead < n_units:
                    pending.append(scores(u + ahead, rows, nk))
                elif u + ahead == n_units and next_rows is not None:
                    s0_scr[...] = scores(0, next_rows, nk)
                parts.append(_softmax_pv(s, v_ext))
            o = parts[0] - lam * parts[1]
            o = _rms(o, HEAD_V) * og_ref[...] * (1.0 - lam_init)
            z = z_ref[0, rows, cols].astype(F32)
            o_ref[0, rows, cols] = (o * _silu(z)).astype(BF16)

    _attend_all(attend, lambda rows: scores(0, rows, T), s0_scr, o_ref, T, n_ctx, with_ctx)


def _diff(p, qkg, lam_p, og, rope, n_ctx, last, lam_init):
    B, T, _ = p.shape
    full = lambda shape: pl.BlockSpec(shape, lambda b: (0,) * len(shape))
    return pl.pallas_call(
        functools.partial(_diff_kernel, n_ctx=n_ctx, with_ctx=not last, lam_init=lam_init),
        grid=(B,),
        in_specs=[pl.BlockSpec((1, T, 1024), lambda b: (b, 0, P_DF_KV // 1024)),
                  pl.BlockSpec((1, T, 512), lambda b: (b, 0, P_DF_Q // 512)),
                  pl.BlockSpec((1, T, 512), lambda b: (b, 0, P_DF_Z // 512)),
                  full(qkg.shape), full(lam_p.shape), full(og.shape), full(rope.shape)],
        out_specs=pl.BlockSpec((1, T, 512), lambda b: (b, 0, 0)),
        out_shape=jax.ShapeDtypeStruct((B, T, BRANCH_WIDTH), BF16),
        scratch_shapes=[pltpu.VMEM((T, 512), BF16), pltpu.VMEM((T, 512), BF16), pltpu.VMEM((T, 512), BF16),
                        pltpu.VMEM((T, 1024), BF16), pltpu.VMEM((_latent_rows(T, n_ctx), T), F32)],
        compiler_params=_params("arbitrary"),
        name="diff_attn",
    )(p, p, p, qkg, lam_p, og, rope)


def _tile_order(i, n_tiles, rev):
    if not rev:
        return i
    return jnp.where(i == 0, 0, n_tiles - i)


def _chunk_tri(n, rev, chunk=CHUNK):
    r = lax.broadcasted_iota(jnp.int32, (n, n), 0)
    c = lax.broadcasted_iota(jnp.int32, (n, n), 1)
    same = (r // chunk) == (c // chunk)
    tri = (c >= r) if rev else (c <= r)
    return jnp.where(same & tri, 1.0, 0.0).astype(F32)


def _causal(rev, n=CHUNK):
    t = lax.broadcasted_iota(jnp.int32, (n, n), 0)
    s = lax.broadcasted_iota(jnp.int32, (n, n), 1)
    return (s >= t) if rev else (s <= t)


def _cummax_rows(x, rev):
    n = x.shape[0]
    row = lax.broadcasted_iota(jnp.int32, (n, 1), 0)
    s = 1
    while s < n:
        if rev:
            x = jnp.maximum(x, jnp.where(row < n - s, pltpu.roll(x, n - s, 0), -jnp.inf))
        else:
            x = jnp.maximum(x, jnp.where(row >= s, pltpu.roll(x, s, 0), -jnp.inf))
        s *= 2
    return x


def _head_mask(width, group):
    lane = lax.broadcasted_iota(jnp.int32, (1, width), 1) // group
    return [lane == h for h in range(HEADS)]


def _gla_kernel(qk_ref, vz_ref, g_ref, aw_ref, ab_ref, og_ref, o_ref, of_scr, ob_scr, st_scr):
    T = qk_ref.shape[1]
    n_tiles = T // ROW_TILE
    n_chunk = ROW_TILE // CHUNK
    hm = _head_mask(HEADS * GLA_DK, GLA_DK)
    tris = (_chunk_tri(ROW_TILE, False), _chunk_tri(ROW_TILE, True))
    st_scr[...] = jnp.zeros_like(st_scr)

    def load_tile(t, rev):
        rows = pl.ds(pl.multiple_of(t * ROW_TILE, ROW_TILE), ROW_TILE)
        gcol = 256 if rev else 0
        qk = qk_ref[0, rows, :].astype(F32)
        pre = _dot(g_ref[0, rows, :], aw_ref[:, gcol:gcol + 256]) + ab_ref[:, gcol:gcol + 256]
        la = _log_sigmoid(pre) * (1.0 / GLA_TAU)
        hi, lo = _split(la)
        tri = tris[rev].astype(BF16)
        return qk[:, :256] * (GLA_DK ** -0.5), qk[:, 256:], _dot(tri, hi) + _dot(tri, lo)

    def tile(i, carry):
        tiles = (i, _tile_order(i, n_tiles, True))
        steps = []
        for cc in range(n_chunk):
            steps += [(0, cc), (1, n_chunk - 1 - cc)]
        data = [load_tile(tiles[d], bool(d)) for d in (0, 1)]
        n_sub = CHUNK // GLA_SUB
        pre, a_blk, upd, a_msk, intra, o_inter = {}, {}, {}, {}, {}, {}
        st = [st_scr[0], st_scr[1]]

        def prepare(d, cc):
            q, k, b = data[d]
            cr = slice(CHUNK * cc, CHUNK * cc + CHUNK)
            bc, qc, kc = b[cr], q[cr], k[cr]
            b_end = bc[0:1] if d else bc[CHUNK - 1:CHUNK]
            crow = pl.ds(pl.multiple_of(tiles[d] * ROW_TILE + CHUNK * cc, CHUNK), CHUNK)
            q_sub, k_sub = [], []
            for i in range(CHUNK // GLA_SUB):
                sr = slice(GLA_SUB * i, GLA_SUB * i + GLA_SUB)
                first = GLA_SUB * i + (GLA_SUB - 1 if d else 0)
                beta = bc[first:first + 1]
                q_i = qc[sr] * jnp.exp(bc[sr] - beta)
                q_sub.append(jnp.concatenate([jnp.where(hm[h], q_i, 0.0) for h in range(HEADS)],
                                             axis=0).astype(BF16))
                k_sub.append((kc * jnp.exp(jnp.minimum(beta - bc, EXP_CLAMP))).astype(BF16))
            qe = qc * jnp.exp(bc)
            stacked = jnp.concatenate([kc * jnp.exp(b_end - bc), jnp.broadcast_to(jnp.exp(b_end), (8, 256)),
                                       jnp.zeros((CHUNK - 8, 256), F32)], axis=0).T
            pre[d, cc] = dict(
                crow=crow, vc=vz_ref[0, crow, 0:BRANCH_WIDTH], decay=stacked[:, CHUNK:CHUNK + 1],
                qe=[jnp.where(hm[h], qe, 0.0).astype(BF16) for h in range(HEADS)], q_sub=q_sub, k_sub=k_sub,
                k_up_t=stacked[:, 0:CHUNK].astype(BF16))

        def score_dots(d, cc):
            p = pre[d, cc]
            a_blk[d, cc] = [_dot_nt(p['q_sub'][i], p['k_sub'][i]) for i in range(n_sub)]
            upd[d, cc] = jnp.concatenate(
                [_dot(p['k_up_t'][GLA_DK * h:GLA_DK * h + GLA_DK], p['vc'][:, 128 * h:128 * h + 128])
                 for h in range(HEADS)], axis=0)

        def mask_scores(d, cc):
            a_msk[d, cc] = [
                jnp.where(_causal(bool(d)), jnp.concatenate(
                    [a_blk[d, cc][i][GLA_SUB * h:GLA_SUB * h + GLA_SUB] for i in range(n_sub)], axis=0),
                    0.0).astype(BF16) for h in range(HEADS)]

        def value_dots(d, cc):
            vc = pre[d, cc]['vc']
            intra[d, cc] = jnp.concatenate(
                [_dot(a_msk[d, cc][h], vc[:, 128 * h:128 * h + 128]) for h in range(HEADS)], axis=-1)

        def state_dot(d, cc):
            s_bf = st[d].astype(BF16)
            o_inter[d, cc] = jnp.concatenate([_dot(pre[d, cc]['qe'][h], s_bf) for h in range(HEADS)], axis=-1)

        def finish_step(d, cc):
            (ob_scr if d else of_scr)[pre[d, cc]['crow'], :] = o_inter[d, cc] + intra[d, cc]
            st[d] = st[d] * pre[d, cc]['decay'] + upd[d, cc]

        def at(t):
            return [steps[t]] if 0 <= t < len(steps) else []

        for t in range(len(steps) + 4):
            for g in at(t - 1):
                score_dots(*g)
            for g in at(t - 3):
                value_dots(*g)
            for g in at(t - 4):
                state_dot(*g)
            for g in at(t):
                prepare(*g)
            for g in at(t - 2):
                mask_scores(*g)
            for g in at(t - 4):
                finish_step(*g)
        st_scr[0] = st[0]
        st_scr[1] = st[1]
        return carry

    lax.fori_loop(0, n_tiles, tile, 0)

    def finish(i, carry):
        rows = pl.ds(pl.multiple_of(i * ROW_TILE, ROW_TILE), ROW_TILE)
        tot = of_scr[rows, :] + ob_scr[rows, :]
        z = vz_ref[0, rows, BRANCH_WIDTH:].astype(F32)
        for h in range(HEADS):
            cols = slice(128 * h, 128 * h + 128)
            y = _rms(tot[:, cols], HEAD_V) * og_ref[:, cols]
            o_ref[0, rows, cols] = (y * _silu(z[:, cols])).astype(BF16)
        return carry

    lax.fori_loop(0, n_tiles, finish, 0)


def _gla(p, aw, ab, og):
    B, T, _ = p.shape
    full = lambda shape: pl.BlockSpec(shape, lambda b: (0,) * len(shape))
    return pl.pallas_call(
        _gla_kernel,
        grid=(B,),
        in_specs=[pl.BlockSpec((1, T, 512), lambda b: (b, 0, P_GLA_QK // 512)),
                  pl.BlockSpec((1, T, 1024), lambda b: (b, 0, P_GLA_VZ // 1024)),
                  pl.BlockSpec((1, T, 128), lambda b: (b, 0, P_GLA_G // 128)),
                  full(aw.shape), full(ab.shape), full(og.shape)],
        out_specs=pl.BlockSpec((1, T, 512), lambda b: (b, 0, 0)),
        out_shape=jax.ShapeDtypeStruct((B, T, BRANCH_WIDTH), BF16),
        scratch_shapes=[pltpu.VMEM((T, BRANCH_WIDTH), F32), pltpu.VMEM((T, BRANCH_WIDTH), F32),
                        pltpu.VMEM((2, HEADS * GLA_DK, HEAD_V), F32)],
        compiler_params=_params("arbitrary"),
        name="gla_scan",
    )(p, p, p, aw, ab, og)


def _mlstm_kernel(m_ref, if_ref, cw_ref, cb_ref, wq_ref, wkt_ref, gb_ref, og_ref, sk_ref, o_ref,
                  xc_scr, q_scr, kt_scr, at_scr, bt_scr, cm_scr, b_scr, hf_scr, hb_scr, c_scr, *, n_ctx):
    T = m_ref.shape[1]
    n_tiles = T // ROW_TILE
    assert ML_CHUNK == ROW_TILE
    ctx_tiles = n_ctx // ROW_TILE
    hm = _head_mask(HEADS * ML_DQK, ML_DQK)
    lane128 = lax.broadcasted_iota(jnp.int32, (1, 128), 1)
    is_forget = ((lane128 // HEADS) % 2) == 1
    row_in_tile = lax.broadcasted_iota(jnp.int32, (ROW_TILE, 1), 0)
    tri_f = _chunk_tri(ROW_TILE, False, ML_CHUNK).astype(BF16)
    tri_b = _chunk_tri(ROW_TILE, True, ML_CHUNK).astype(BF16)

    def prep_parts(i):
        r = pl.multiple_of(i * ROW_TILE, ROW_TILE)
        rows = pl.ds(r, ROW_TILE)
        kept = {}

        def conv_and_q():
            x = m_ref[0, rows, 0:512].astype(F32)
            rp = pl.multiple_of(jnp.maximum(r - 16, 0), 16)
            rn = pl.multiple_of(jnp.minimum(r + ROW_TILE, T - 16), 16)
            prev_row = m_ref[0, pl.ds(rp, 16), 0:512].astype(F32)[15:16]
            next_row = m_ref[0, pl.ds(rn, 16), 0:512].astype(F32)[0:1]
            seg_start = (i == 0) | (i == ctx_tiles)
            seg_end = (i == ctx_tiles - 1) | (i == n_tiles - 1)
            prev_row = jnp.where(seg_start, 0.0, prev_row)
            next_row = jnp.where(seg_end, 0.0, next_row)
            xm = jnp.where(row_in_tile == 0, prev_row, pltpu.roll(x, 1, 0))
            xp = jnp.where(row_in_tile == ROW_TILE - 1, next_row, pltpu.roll(x, ROW_TILE - 1, 0))
            xc = _silu(cw_ref[0:1, :] * xm + cw_ref[1:2, :] * x + cw_ref[2:3, :] * xp + cb_ref[...])
            xc_scr[rows, :] = xc
            kept['xb'] = xc.astype(BF16)
            q_scr[rows, :] = _dot(kept['xb'], wq_ref[...]).astype(BF16)

        def gate_terms():
            g = if_ref[0, rows, :].astype(F32) + gb_ref[...]
            g2 = jnp.where(is_forget, _log_sigmoid(g), g)
            g_hi, g_lo = _split(g2)
            cs = jnp.where(lane128 < 2 * HEADS, _dot(tri_f, g_hi) + _dot(tri_f, g_lo),
                           _dot(tri_b, g_hi) + _dot(tri_b, g_lo))
            b = pltpu.roll(cs, 128 - HEADS, 1)
            a = g2 - b
            kept['a'], kept['b'] = a, b
            cm_scr[rows, :] = jnp.where(lane128 < 2 * HEADS, _cummax_rows(a, False), _cummax_rows(a, True))
            b_scr[rows, :] = b

        def keys_and_rows():
            kt_scr[i] = (_dot_nt(wkt_ref[...], kept['xb']) * (ML_DQK ** -0.5)).astype(BF16)
            at_scr[i] = kept['a'].T[0:16, :]
            bt_scr[i] = kept['b'].T[0:16, :]

        return conv_and_q, gate_terms, keys_and_rows

    c_scr[...] = jnp.zeros_like(c_scr)
    ones_col = _ones_col(ML_CHUNK)
    sel_r = lax.broadcasted_iota(jnp.int32, (256, HEADS * 128), 0) % 128
    sel_c = lax.broadcasted_iota(jnp.int32, (256, HEADS * 128), 1) // 128
    sel = [jnp.where(sel_r == 2 * HEADS * d + sel_c, 1.0, 0.0).astype(BF16) for d in (0, 1)]

    def col_bcast(x, d):
        hi, lo = _split(x)
        return _dot(jnp.concatenate([hi, lo], axis=-1), sel[d])

    def chunk_step(d, c, m_in, fill):
        crow = pl.ds(pl.multiple_of(c * ML_CHUNK, ML_CHUNK), ML_CHUNK)
        qc, kt, vc = q_scr[crow, :], kt_scr[c], m_ref[0, crow, 512:1024]
        at, bt = at_scr[c], bt_scr[c]
        c_bf = c_scr[d].astype(BF16)
        causal = _causal(bool(d), ML_CHUNK)
        last = 0 if d else ML_CHUNK - 1
        heads = range(HEADS)
        qh = [jnp.where(hm[h], qc, jnp.zeros_like(qc)) for h in heads]
        v_ext = [jnp.concatenate([vc[:, 128 * h:128 * h + 128], ones_col], axis=-1) for h in heads]
        s_raw = [_dot(qh[h], kt) for h in heads]
        q_c = [_dot(qh[h], c_bf) for h in heads]
        cm_col = col_bcast(cm_scr[crow, :], d)
        b_col = col_bcast(b_scr[crow, :], d)
        if fill[0]:
            fill[0]()
        s_w, m_run, m_last, ktw = [], [], [], []
        for h in heads:
            a_row = at[2 * HEADS * d + h:2 * HEADS * d + h + 1, :]
            pm = jnp.where(causal, a_row, -jnp.inf)
            m_run.append(jnp.maximum(m_in[h], cm_col[:, 128 * h:128 * h + 128]))
            m_wide = jnp.concatenate([m_run[h]] * (ML_CHUNK // 128), axis=-1)
            s_w.append((s_raw[h] * jnp.exp(pm - m_wide)).astype(BF16))
            m_last.append(m_run[h][last:last + 1, 0:1])
            k_h = kt[ML_DQK * h:ML_DQK * h + ML_DQK, :].astype(F32)
            ktw.append((k_h * jnp.exp(a_row - m_last[h])).astype(BF16))
        s_v = [_dot(s_w[h], v_ext[h]) for h in heads]
        upd = [_dot(ktw[h], v_ext[h]) for h in heads]
        if fill[1]:
            fill[1]()
        parts, m_out = [], []
        for h in heads:
            j = 2 * HEADS * d + h
            hr = slice(ML_DQK * h, ML_DQK * h + ML_DQK)
            carry_w = jnp.exp(m_in[h] - m_run[h])
            num = jnp.concatenate([carry_w, carry_w], axis=-1) * q_c[h] + s_v[h]
            den = num[:, HEAD_V:HEAD_V + 1]
            floor = jnp.exp(-(b_col[:, 128 * h:128 * h + 128] + m_run[h]))
            parts.append(num[:, 0:HEAD_V] / jnp.maximum(jnp.abs(den), floor))
            c_scr[d, hr, :] = jnp.exp(m_in[h] - m_last[h]) * c_scr[d, hr, :] + upd[h]
            m_out.append(bt[j:j + 1, last:last + 1] + m_last[h])
        if fill[2]:
            fill[2]()
        return jnp.concatenate(parts, axis=-1), tuple(m_out)

    def finish(t, h_b):
        rows = pl.ds(pl.multiple_of(t * ROW_TILE, ROW_TILE), ROW_TILE)
        tot = (hf_scr[rows, :] + h_b) * _sigmoid(m_ref[0, rows, 1024:1536].astype(F32))
        z = m_ref[0, rows, 1536:2048].astype(F32)
        xc = xc_scr[rows, :]
        for h in range(HEADS):
            cols = slice(128 * h, 128 * h + 128)
            y = _rms(tot[:, cols], HEAD_V) * og_ref[:, cols]
            y = (y + sk_ref[:, cols] * xc[:, cols]) * _silu(z[:, cols])
            o_ref[0, rows, cols] = y.astype(BF16)

    def forward(i, m):
        h_f, m = chunk_step(0, i, m, prep_parts(jnp.minimum(i + 1, n_tiles - 1)))
        hf_scr[pl.ds(pl.multiple_of(i * ROW_TILE, ROW_TILE), ROW_TILE), :] = h_f
        return m

    def backward(i, m):
        t = _tile_order(i, n_tiles, True)
        t_prev = _tile_order(jnp.maximum(i - 1, 0), n_tiles, True)
        h_b, m = chunk_step(1, t, m, (None, lambda: finish(t_prev, hb_scr[...]), None))
        hb_scr[...] = h_b
        return m

    m0 = tuple(jnp.zeros((1, 1), F32) for _ in range(HEADS))
    for part in prep_parts(0):
        part()
    lax.fori_loop(0, n_tiles, forward, m0)
    hb_scr[...] = jnp.zeros_like(hb_scr)
    lax.fori_loop(0, n_tiles, backward, m0)
    finish(_tile_order(n_tiles - 1, n_tiles, True), hb_scr[...])


def _mlstm(p, cw, cb, wq, wkt, gb, og, sk, n_ctx):
    B, T, _ = p.shape
    full = lambda shape: pl.BlockSpec(shape, lambda b: (0,) * len(shape))
    return pl.pallas_call(
        functools.partial(_mlstm_kernel, n_ctx=n_ctx),
        grid=(B,),
        in_specs=[pl.BlockSpec((1, T, 2048), lambda b: (b, 0, P_ML // 2048)),
                  pl.BlockSpec((1, T, 128), lambda b: (b, 0, P_ML_IF // 128)),
                  full(cw.shape), full(cb.shape), full(wq.shape), full(wkt.shape), full(gb.shape),
                  full(og.shape), full(sk.shape)],
        out_specs=pl.BlockSpec((1, T, 512), lambda b: (b, 0, 0)),
        out_shape=jax.ShapeDtypeStruct((B, T, BRANCH_WIDTH), BF16),
        scratch_shapes=[pltpu.VMEM((T, BRANCH_WIDTH), F32),
                        pltpu.VMEM((T, HEADS * ML_DQK), BF16),
                        pltpu.VMEM((T // ML_CHUNK, HEADS * ML_DQK, ML_CHUNK), BF16),
                        pltpu.VMEM((T // ML_CHUNK, 16, ML_CHUNK), F32),
                        pltpu.VMEM((T // ML_CHUNK, 16, ML_CHUNK), F32),
                        pltpu.VMEM((T, 128), F32),
                        pltpu.VMEM((T, 128), F32),
                        pltpu.VMEM((T, BRANCH_WIDTH), F32),
                        pltpu.VMEM((ROW_TILE, BRANCH_WIDTH), F32),
                        pltpu.VMEM((2, HEADS * ML_DQK, 2 * HEAD_V), F32)],
        compiler_params=_params("arbitrary"),
        name="mlstm_scan",
    )(p, p, cw, cb, wq, wkt, gb, og, sk)


def _merge_kernel(ya_ref, yb_ref, yc_ref, yd_ref, gl_ref, brw_ref, wo_ref, x_ref, gate_ref, o_ref,
                  *, row0, ctx_rows):
    tm = x_ref.shape[1]
    first = pl.program_id(1) * tm + row0
    acc = None
    for i, y_ref in enumerate((ya_ref, yb_ref, yc_ref, yd_ref)):
        u = _dot(y_ref[0], brw_ref[i])
        gsig = _sigmoid(gl_ref[0, :, D_MODEL * i:D_MODEL * (i + 1)].astype(F32))
        acc = gsig * u if acc is None else acc + gsig * u
    out = _dot(acc.astype(BF16), wo_ref[...])
    for r0 in range(0, tm, ROW_TILE):
        rs = slice(r0, r0 + ROW_TILE)
        gate = jnp.where(first + r0 < ctx_rows, gate_ref[0, 0:1, :], gate_ref[0, 1:2, :])
        o_ref[0, rs, :] = x_ref[0, rs, :] + gate * out[rs]


def _merge(ys, p, brw, wo, xs, gates, n_ctx, last):
    B, T, D = xs.shape
    tm = ROW_TILE if last else MERGE_ROWS
    row0 = n_ctx if last else 0
    tile0 = row0 // tm
    nt = (T - row0) // tm
    assert (T - row0) % tm == 0 and row0 % tm == 0 and n_ctx % ROW_TILE == 0
    ymap = lambda b, t: (b, t + tile0, 0)
    return pl.pallas_call(
        functools.partial(_merge_kernel, row0=row0, ctx_rows=n_ctx),
        grid=(B, nt),
        in_specs=[pl.BlockSpec((1, tm, BRANCH_WIDTH), ymap)] * 4 + [
            pl.BlockSpec((1, tm, N_BRANCH * D), lambda b, t: (b, t + tile0, P_MERGE // (N_BRANCH * D))),
            pl.BlockSpec(brw.shape, lambda b, t: (0, 0, 0)),
            pl.BlockSpec(wo.shape, lambda b, t: (0, 0)),
            pl.BlockSpec((1, tm, D), ymap),
            pl.BlockSpec((1, 2, D), lambda b, t: (b, 0, 0))],
        out_specs=pl.BlockSpec((1, tm, D), lambda b, t: (b, t, 0)),
        out_shape=jax.ShapeDtypeStruct((B, nt * tm, D), F32),
        compiler_params=_params("arbitrary", "arbitrary"),
        name="merge",
    )(*ys, p, brw, wo, xs, gates)


def _layout_w_in(w_in):
    offs, off = {}, 0
    for name, w in IN_SPLITS:
        offs[name] = (off, w)
        off += w

    w_bf = w_in.astype(BF16)

    def col(name):
        o, w = offs[name]
        return w_bf[..., o:o + w]

    def zeros(n):
        return jnp.zeros(w_in.shape[:-1] + (n,), BF16)

    parts = [col('merge'), col('df_k'), col('df_v'), col('df_q'), col('df_z'),
             col('ml_x'), col('ml_v'), col('ml_o'), col('ml_z'),
             col('gla_v'), col('gla_z'), col('mla_z'), col('gla_q'), col('gla_k'),
             col('mla_cq'), col('mla_kr'), zeros(64),
             col('mla_ckv'), col('gla_af'), col('gla_ab'), zeros(96), col('ml_if'), zeros(112)]
    out = jnp.concatenate(parts, axis=-1)
    assert out.shape[-1] == P_WIDTH
    return out


def _rope_tables(rows, n_ctx):
    quarter = ROT_DIM // 4
    inv_freq = ROPE_BASE ** (-jnp.arange(quarter, dtype=F32) / quarter)
    row = jnp.repeat(jnp.arange(rows, dtype=F32), GRID_W)
    col = jnp.tile(jnp.arange(GRID_W, dtype=F32), rows)
    ar = row[:, None] * inv_freq
    ac = col[:, None] * inv_freq
    ang = jnp.concatenate([ar, ar, ac, ac], axis=-1)
    cos = jnp.concatenate([jnp.ones((n_ctx, ROT_DIM), F32), jnp.cos(ang)], axis=0)
    sin = jnp.concatenate([jnp.zeros((n_ctx, ROT_DIM), F32), jnp.sin(ang)], axis=0)
    zero = jnp.zeros_like(cos)
    both = jnp.stack([jnp.tile(cos, (1, 2)), jnp.tile(sin, (1, 2))])
    half = jnp.stack([jnp.concatenate([cos, zero], -1), jnp.concatenate([sin, zero], -1)])
    return half, both


def _pad_lanes(v, n):
    return jnp.concatenate([v, jnp.zeros(v.shape[:-1] + (n - v.shape[-1],), v.dtype)], axis=-1)


def kernel(x, c, ctx, c_ctx, ada_w, ada_b, norm_g, w_in, mla_cq_g, mla_ckv_g, mla_wuq, mla_wukv, mla_q_g,
           mla_k_g, gla_a_w, gla_a_b, gla_out_g, ml_conv_w, ml_conv_b, ml_wq, ml_wk, ml_gate_b, ml_out_g,
           ml_skip, df_qk_g, df_lambda, df_out_g, br_w, w_out):
    B, S, D = x.shape
    n_ctx = ctx.shape[1]
    L = ada_w.shape[0]
    assert D == D_MODEL and n_ctx == ROW_TILE and S % ROW_TILE == 0 and S % GRID_W == 0

    rope_half, rope_both = _rope_tables(S // GRID_W, n_ctx)
    w_in_p = _layout_w_in(w_in)

    n_rows = -(-(B + 1) // 8) * 8
    cc = jnp.concatenate([c, c_ctx[None], jnp.zeros((n_rows - B - 1, D), F32)], axis=0)
    mod_all = _modulation(cc, ada_w, ada_b)

    xs = jnp.concatenate([ctx, x], axis=1)
    for l in range(L):
        last = l == L - 1
        lam_init = 0.8 - 0.6 * math.exp(-0.3 * l)
        m3 = mod_all[l].reshape(n_rows, 3, D)
        lat, cx = m3[:B], jnp.broadcast_to(m3[B][None], (B, 3, D))
        mod = jnp.concatenate([cx, lat, jnp.zeros((B, 2, D), F32)], axis=1)
        gates = jnp.stack([cx[:, 2], lat[:, 2]], axis=1)

        p = _in_projection(xs, mod, norm_g[l], w_in_p, l, n_ctx)

        wq4 = mla_wuq[l].reshape(MLA_Q_LORA, HEADS, MLA_NOPE + ROT_DIM)
        wuq = _pad_lanes(wq4, 256).reshape(MLA_Q_LORA, HEADS * 256).astype(BF16)
        wkv4 = mla_wukv[l].reshape(MLA_KV_LORA, HEADS, MLA_NOPE + HEAD_V)
        wukv = jnp.concatenate([wkv4[..., :MLA_NOPE].reshape(MLA_KV_LORA, -1),
                                wkv4[..., MLA_NOPE:].reshape(MLA_KV_LORA, -1)], axis=-1).astype(BF16)
        qg = jnp.stack([mla_q_g[l, :MLA_NOPE], _pad_lanes(mla_q_g[l, MLA_NOPE:], 128)])
        kg = jnp.stack([mla_k_g[l, :MLA_NOPE], _pad_lanes(mla_k_g[l, MLA_NOPE:], 128)])
        y_mla = _mla(p, wuq, wukv, mla_cq_g[l][None], mla_ckv_g[l][None], qg, kg, rope_half, n_ctx, last)

        qkg = jnp.tile(df_qk_g[l], (1, 2))
        y_df = _diff(p, qkg, df_lambda[l], df_out_g[l][None], rope_both, n_ctx, last, lam_init)

        aw = jnp.zeros((128, 512), F32)
        aw = aw.at[0:16, 0:256].set(gla_a_w[l, 0]).at[16:32, 256:512].set(gla_a_w[l, 1]).astype(BF16)
        y_gla = _gla(p, aw, gla_a_b[l].reshape(1, 512), gla_out_g[l][None])

        wq_bd = jnp.zeros((BRANCH_WIDTH, HEADS * ML_DQK), F32)
        wk_bd = jnp.zeros((BRANCH_WIDTH, HEADS * ML_DQK), F32)
        for h in range(HEADS):
            wq_bd = wq_bd.at[128 * h:128 * h + 128, 64 * h:64 * h + 64].set(ml_wq[l, h])
            wk_bd = wk_bd.at[128 * h:128 * h + 128, 64 * h:64 * h + 64].set(ml_wk[l, h])
        gb = _pad_lanes(ml_gate_b[l].reshape(1, 16), 128)
        y_ml = _mlstm(p, ml_conv_w[l], ml_conv_b[l][None], wq_bd.astype(BF16), wk_bd.T.astype(BF16), gb,
                      ml_out_g[l][None], ml_skip[l][None], n_ctx)

        xs = _merge((y_mla, y_gla, y_ml, y_df), p, br_w[l].astype(BF16), w_out[l].astype(BF16), xs, gates,
                    n_ctx, last)
    return xs
```

```python
import functools
import math

import jax
import jax.numpy as jnp
from jax import lax
from jax.experimental import pallas as pl
from jax.experimental.pallas import tpu as pltpu

F32 = jnp.float32
BF16 = jnp.bfloat16

D_MODEL = 1024
GRID_W = 64
EPS = 1e-6
ROPE_BASE = 10000.0
ROT_DIM = 64
CHUNK = 64
N_BRANCH = 4
BRANCH_WIDTH = 512
HEADS = 4
HEAD_V = BRANCH_WIDTH // HEADS
MLA_NOPE = 128
MLA_Q_LORA = 384
MLA_KV_LORA = 256
GLA_DK = 64
GLA_TAU = 16.0
ML_DQK = 64
DF_DQK = 64

IN_SPLITS = (
    ('mla_cq', 384), ('mla_ckv', 256), ('mla_kr', 64), ('mla_z', 512),
    ('gla_q', 256), ('gla_k', 256), ('gla_v', 512), ('gla_af', 16), ('gla_ab', 16), ('gla_z', 512),
    ('ml_x', 512), ('ml_v', 512), ('ml_o', 512), ('ml_if', 16), ('ml_z', 512),
    ('df_q', 512), ('df_k', 512), ('df_v', 512), ('df_z', 512),
    ('merge', 4096),
)

P_MERGE = 0
P_DF_KV = 4096
P_DF_Q = 5120
P_DF_Z = 5632
P_ML = 6144
P_GLA_VZ = 8192
P_MLA_Z = 9216
P_GLA_QK = 9728
P_MLA_A = 10240
P_MLA_CKV = 10752
P_GLA_G = 11008
P_ML_IF = 11136
P_WIDTH = 11264

ROW_TILE = 256
IN_PROJ_COLS = 2816
MERGE_ROWS = 768
ML_CHUNK = 256
ATT_ROWS = 512
VMEM_LIMIT = 56 * 1024 * 1024
EXP_CLAMP = 80.0
GLA_SUB = 16


def _dot(a, b):
    return jnp.dot(a, b, preferred_element_type=F32)


def _dot_nt(a, b):
    return lax.dot_general(a, b, (((1,), (1,)), ((), ())), preferred_element_type=F32)


def _sigmoid(x):
    return 1.0 / (1.0 + jnp.exp(-x))


def _silu(x):
    return x * _sigmoid(x)


def _log_sigmoid(x):
    return jnp.minimum(x, 0.0) - jnp.log(1.0 + jnp.exp(-jnp.abs(x)))


def _rms(x, n):
    return x * lax.rsqrt(jnp.sum(x * x, axis=-1, keepdims=True) * (1.0 / n) + EPS)


def _split(x):
    hi = x.astype(BF16)
    return hi, (x - hi.astype(F32)).astype(BF16)


def _group_sum(x, group):
    row = lax.broadcasted_iota(jnp.int32, (128, 128), 0)
    col = lax.broadcasted_iota(jnp.int32, (128, 128), 1)
    sel = jnp.where(row // group == col // group, 1.0, 0.0).astype(BF16)
    hi, lo = _split(x)
    return _dot(hi, sel) + _dot(lo, sel)


def _rms_mxu(x, group):
    return x * lax.rsqrt(_group_sum(x * x, group) * (1.0 / group) + EPS)


def _rot_matrix():
    src = lax.broadcasted_iota(jnp.int32, (128, 128), 0)
    dst = lax.broadcasted_iota(jnp.int32, (128, 128), 1)
    even = (dst // 16) % 2 == 0
    return jnp.where(even & (src == dst + 16), -1.0, jnp.where(~even & (src == dst - 16), 1.0, 0.0)).astype(BF16)


def _rope_mxu(x, cos, sin, rot):
    hi, lo = _split(x)
    return x * cos + (_dot(hi, rot) + _dot(lo, rot)) * sin


def _ones_col(rows):
    return jnp.where(lax.broadcasted_iota(jnp.int32, (rows, HEAD_V), 1) == 0, 1.0, 0.0).astype(BF16)


def _softmax_pv(s, v_ext):
    e = jnp.exp((s - jnp.max(s, axis=-1, keepdims=True)).astype(BF16))
    o = _dot(e, v_ext)
    return o[:, 0:HEAD_V] / o[:, HEAD_V:HEAD_V + 1]


def _params(*sem):
    return pltpu.CompilerParams(dimension_semantics=sem, vmem_limit_bytes=VMEM_LIMIT)


def _mod_kernel(c_ref, w_ref, b_ref, o_ref):
    s = _silu(c_ref[...])
    o_ref[0] = _dot(s.astype(BF16), w_ref[0].astype(BF16)) + b_ref[0]


def _modulation(cc, ada_w, ada_b):
    L, D, _ = ada_w.shape
    R = cc.shape[0]
    return pl.pallas_call(
        _mod_kernel,
        grid=(L, 3),
        in_specs=[pl.BlockSpec((R, D), lambda l, j: (0, 0)),
                  pl.BlockSpec((1, D, D), lambda l, j: (l, 0, j)),
                  pl.BlockSpec((1, 1, D), lambda l, j: (l, 0, j))],
        out_specs=pl.BlockSpec((1, R, D), lambda l, j: (l, 0, j)),
        out_shape=jax.ShapeDtypeStruct((L, R, 3 * D), F32),
        compiler_params=_params("arbitrary", "arbitrary"),
        name="modulation",
    )(cc, ada_w, ada_b.reshape(L, 1, 3 * D))


def _modulated_norm(x, g, mod_ref, is_ctx):
    shift = jnp.where(is_ctx, mod_ref[0, 0:1, :], mod_ref[0, 3:4, :])
    scale = jnp.where(is_ctx, mod_ref[0, 1:2, :], mod_ref[0, 4:5, :])
    return (_rms(x, D_MODEL) * g * (1.0 + scale) + shift).astype(BF16)


def _norm_kernel(x_ref, mod_ref, g_ref, h_ref, *, ctx_rows):
    tm = x_ref.shape[1]
    first = pl.program_id(1) * tm
    for r0 in range(0, tm, ROW_TILE):
        rs = slice(r0, r0 + ROW_TILE)
        h_ref[0, rs, :] = _modulated_norm(x_ref[0, rs, :], g_ref[...], mod_ref, first + r0 < ctx_rows)


def _first_norm(xs, mod, norm_g, n_ctx):
    B, T, D = xs.shape
    return pl.pallas_call(
        functools.partial(_norm_kernel, ctx_rows=n_ctx),
        grid=(B, T // MERGE_ROWS),
        in_specs=[pl.BlockSpec((1, MERGE_ROWS, D), lambda b, t: (b, t, 0)),
                  pl.BlockSpec((1, 8, D), lambda b, t: (b, 0, 0)),
                  pl.BlockSpec((1, D), lambda b, t: (0, 0))],
        out_specs=pl.BlockSpec((1, MERGE_ROWS, D), lambda b, t: (b, t, 0)),
        out_shape=jax.ShapeDtypeStruct((B, T, D), BF16),
        compiler_params=_params("arbitrary", "arbitrary"),
        name="first_norm",
    )(xs, mod, norm_g.reshape(1, D))


def _inproj_kernel(h_ref, w_ref, o_ref):
    for r0 in range(0, h_ref.shape[1], ROW_TILE):
        o_ref[0, r0:r0 + ROW_TILE, :] = _dot(h_ref[0, r0:r0 + ROW_TILE, :], w_ref[0]).astype(BF16)


def _in_projection(h, w_all, layer):
    B, T, D = h.shape
    return pl.pallas_call(
        _inproj_kernel,
        grid=(B, P_WIDTH // IN_PROJ_COLS),
        in_specs=[pl.BlockSpec((1, T, D), lambda b, j: (b, 0, 0)),
                  pl.BlockSpec((1, D, IN_PROJ_COLS), lambda b, j: (layer, 0, j))],
        out_specs=pl.BlockSpec((1, T, IN_PROJ_COLS), lambda b, j: (b, 0, j)),
        out_shape=jax.ShapeDtypeStruct((B, T, P_WIDTH), BF16),
        compiler_params=_params("arbitrary", "arbitrary"),
        name="in_projection",
    )(h, w_all)


def _latent_rows(T, n_ctx):
    return ATT_ROWS if (T - n_ctx) % ATT_ROWS == 0 else ROW_TILE


def _attend_all(attend, scores0, s0_scr, o_ref, T, n_ctx, with_ctx):
    if with_ctx:
        attend(pl.ds(0, n_ctx), n_ctx, None, None)
    else:
        o_ref[0, 0:n_ctx, :] = jnp.zeros((n_ctx, o_ref.shape[2]), o_ref.dtype)
    rows_per = s0_scr.shape[0]
    n = (T - n_ctx) // rows_per

    def rows_of(i):
        return pl.ds(pl.multiple_of(n_ctx + i * rows_per, ROW_TILE), rows_per)

    s0_scr[...] = scores0(rows_of(0))

    def latent(i, carry):
        attend(rows_of(i), T, s0_scr[...], rows_of(jnp.minimum(i + 1, n - 1)))
        return carry

    lax.fori_loop(0, n, latent, 0)


def _mla_kernel(pa_ref, pc_ref, z_ref, wuq_ref, wukv_ref, cqg_ref, ckvg_ref, qg_ref, kg_ref, rope_ref,
                o_ref, q_scr, k_scr, v_scr, s0_scr, *, n_ctx, with_ctx):
    T = pa_ref.shape[1]
    scale = (MLA_NOPE + ROT_DIM) ** -0.5
    rot = _rot_matrix()

    def prep(i, carry):
        r = pl.multiple_of(i * ROW_TILE, ROW_TILE)
        rows = pl.ds(r, ROW_TILE)
        cos, sin = rope_ref[0, rows, :], rope_ref[1, rows, :]
        pa = pa_ref[0, rows, :].astype(F32)
        cq = (_rms(pa[:, :MLA_Q_LORA], MLA_Q_LORA) * cqg_ref[...]).astype(BF16)
        q = _dot(cq, wuq_ref[...])
        kr = _rms(pa[:, MLA_Q_LORA:], ROT_DIM) * kg_ref[1:2, :]
        kr = _rope_mxu(kr, cos, sin, rot).astype(BF16)
        ckv = (_rms(pc_ref[0, rows, :].astype(F32), MLA_KV_LORA) * ckvg_ref[...]).astype(BF16)
        kv = _dot(ckv, wukv_ref[...])
        for h in range(HEADS):
            qn = _rms(q[:, 256 * h:256 * h + 128], MLA_NOPE) * qg_ref[0:1, :]
            qr = _rms(q[:, 256 * h + 128:256 * h + 256], ROT_DIM) * qg_ref[1:2, :]
            qr = _rope_mxu(qr, cos, sin, rot)
            q_scr[rows, 256 * h:256 * h + 128] = (qn * scale).astype(BF16)
            q_scr[rows, 256 * h + 128:256 * h + 256] = (qr * scale).astype(BF16)
            kn = _rms(kv[:, 128 * h:128 * h + 128], MLA_NOPE) * kg_ref[0:1, :]
            k_scr[rows, 256 * h:256 * h + 128] = kn.astype(BF16)
            k_scr[rows, 256 * h + 128:256 * h + 256] = kr
            v_scr[rows, 256 * h:256 * h + 128] = kv[:, 512 + 128 * h:512 + 128 * h + 128].astype(BF16)
            v_scr[rows, 256 * h + 128:256 * h + 256] = _ones_col(ROW_TILE)
        return carry

    lax.fori_loop(0, T // ROW_TILE, prep, 0)

    def scores(h, rows, nk):
        return _dot_nt(q_scr[rows, 256 * h:256 * h + 256], k_scr[0:nk, 256 * h:256 * h + 256])

    def attend(rows, nk, s_first, next_rows):
        ahead = HEADS if next_rows is None else 1
        pending = [scores(0, rows, nk) if s_first is None else s_first]
        pending += [scores(h, rows, nk) for h in range(1, ahead)]
        for h in range(HEADS):
            s = pending.pop(0)
            if h + ahead < HEADS:
                pending.append(scores(h + ahead, rows, nk))
            elif h + ahead == HEADS and next_rows is not None:
                s0_scr[...] = scores(0, next_rows, nk)
            o = _softmax_pv(s, v_scr[0:nk, 256 * h:256 * h + 256])
            z = z_ref[0, rows, 128 * h:128 * h + 128].astype(F32)
            o_ref[0, rows, 128 * h:128 * h + 128] = (o * _silu(z)).astype(BF16)

    _attend_all(attend, lambda rows: scores(0, rows, T), s0_scr, o_ref, T, n_ctx, with_ctx)


def _mla(p, wuq, wukv, cqg, ckvg, qg, kg, rope, n_ctx, last):
    B, T, _ = p.shape
    full = lambda shape: pl.BlockSpec(shape, lambda b: (0,) * len(shape))
    return pl.pallas_call(
        functools.partial(_mla_kernel, n_ctx=n_ctx, with_ctx=not last),
        grid=(B,),
        in_specs=[pl.BlockSpec((1, T, 512), lambda b: (b, 0, P_MLA_A // 512)),
                  pl.BlockSpec((1, T, 256), lambda b: (b, 0, P_MLA_CKV // 256)),
                  pl.BlockSpec((1, T, 512), lambda b: (b, 0, P_MLA_Z // 512)),
                  full(wuq.shape), full(wukv.shape), full(cqg.shape), full(ckvg.shape),
                  full(qg.shape), full(kg.shape), full(rope.shape)],
        out_specs=pl.BlockSpec((1, T, 512), lambda b: (b, 0, 0)),
        out_shape=jax.ShapeDtypeStruct((B, T, BRANCH_WIDTH), BF16),
        scratch_shapes=[pltpu.VMEM((T, 1024), BF16), pltpu.VMEM((T, 1024), BF16),
                        pltpu.VMEM((T, 1024), BF16), pltpu.VMEM((_latent_rows(T, n_ctx), T), F32)],
        compiler_params=_params("arbitrary"),
        name="mla",
    )(p, p, p, wuq, wukv, cqg, ckvg, qg, kg, rope)


def _diff_kernel(kv_ref, q_ref, z_ref, g_ref, lam_ref, og_ref, rope_ref, o_ref, q0_scr, q1_scr, k_scr, v_scr, s0_scr,
                 *, n_ctx, with_ctx, lam_init):
    T = kv_ref.shape[1]
    scale = DF_DQK ** -0.5
    lo_mask = lax.broadcasted_iota(jnp.int32, (1, 128), 1) < 64
    rot = _rot_matrix()

    def prep(i, carry):
        r = pl.multiple_of(i * ROW_TILE, ROW_TILE)
        rows = pl.ds(r, ROW_TILE)
        cos, sin = rope_ref[0, rows, :], rope_ref[1, rows, :]
        for h in range(HEADS):
            cols = slice(128 * h, 128 * h + 128)
            q = _rms_mxu(q_ref[0, rows, cols].astype(F32), DF_DQK) * g_ref[0:1, :]
            q = _rope_mxu(q, cos, sin, rot) * scale
            q0_scr[rows, cols] = jnp.where(lo_mask, q, 0.0).astype(BF16)
            q1_scr[rows, cols] = jnp.where(lo_mask, 0.0, q).astype(BF16)
            k = _rms_mxu(kv_ref[0, rows, cols].astype(F32), DF_DQK) * g_ref[1:2, :]
            k_scr[rows, cols] = _rope_mxu(k, cos, sin, rot).astype(BF16)
            v_scr[rows, 256 * h:256 * h + 128] = kv_ref[0, rows, 512 + 128 * h:512 + 128 * h + 128]
            v_scr[rows, 256 * h + 128:256 * h + 256] = _ones_col(ROW_TILE)
        return carry

    lax.fori_loop(0, T // ROW_TILE, prep, 0)

    lp = lam_ref[...]
    lam = (jnp.exp(jnp.sum(lp[0:1] * lp[1:2], axis=-1, keepdims=True))
           - jnp.exp(jnp.sum(lp[2:3] * lp[3:4], axis=-1, keepdims=True)) + lam_init)
    def scores(u, rows, nk):
        cols = slice(128 * (u // 2), 128 * (u // 2) + 128)
        return _dot_nt((q1_scr if u % 2 else q0_scr)[rows, cols], k_scr[0:nk, cols])

    def attend(rows, nk, s_first, next_rows):
        n_units = 2 * HEADS
        ahead = n_units if next_rows is None else 1
        pending = [scores(0, rows, nk) if s_first is None else s_first]
        pending += [scores(u, rows, nk) for u in range(1, ahead)]
        for h in range(HEADS):
            cols = slice(128 * h, 128 * h + 128)
            v_ext = v_scr[0:nk, 256 * h:256 * h + 256]
            parts = []
            for mp in range(2):
                u = 2 * h + mp
                s = pending.pop(0)
                if u + ahead < n_units:
                    pending.append(scores(u + ahead, rows, nk))
                elif u + ahead == n_units and next_rows is not None:
                    s0_scr[...] = scores(0, next_rows, nk)
                parts.append(_softmax_pv(s, v_ext))
            o = parts[0] - lam * parts[1]
            o = _rms(o, HEAD_V) * og_ref[...] * (1.0 - lam_init)
            z = z_ref[0, rows, cols].astype(F32)
            o_ref[0, rows, cols] = (o * _silu(z)).astype(BF16)

    _attend_all(attend, lambda rows: scores(0, rows, T), s0_scr, o_ref, T, n_ctx, with_ctx)


def _diff(p, qkg, lam_p, og, rope, n_ctx, last, lam_init):
    B, T, _ = p.shape
    full = lambda shape: pl.BlockSpec(shape, lambda b: (0,) * len(shape))
    return pl.pallas_call(
        functools.partial(_diff_kernel, n_ctx=n_ctx, with_ctx=not last, lam_init=lam_init),
        grid=(B,),
        in_specs=[pl.BlockSpec((1, T, 1024), lambda b: (b, 0, P_DF_KV // 1024)),
                  pl.BlockSpec((1, T, 512), lambda b: (b, 0, P_DF_Q // 512)),
                  pl.BlockSpec((1, T, 512), lambda b: (b, 0, P_DF_Z // 512)),
                  full(qkg.shape), full(lam_p.shape), full(og.shape), full(rope.shape)],
        out_specs=pl.BlockSpec((1, T, 512), lambda b: (b, 0, 0)),
        out_shape=jax.ShapeDtypeStruct((B, T, BRANCH_WIDTH), BF16),
        scratch_shapes=[pltpu.VMEM((T, 512), BF16), pltpu.VMEM((T, 512), BF16), pltpu.VMEM((T, 512), BF16),
                        pltpu.VMEM((T, 1024), BF16), pltpu.VMEM((_latent_rows(T, n_ctx), T), F32)],
        compiler_params=_params("arbitrary"),
        name="diff_attn",
    )(p, p, p, qkg, lam_p, og, rope)


def _tile_order(i, n_tiles, rev):
    if not rev:
        return i
    return jnp.where(i == 0, 0, n_tiles - i)


def _chunk_tri(n, rev, chunk=CHUNK):
    r = lax.broadcasted_iota(jnp.int32, (n, n), 0)
    c = lax.broadcasted_iota(jnp.int32, (n, n), 1)
    same = (r // chunk) == (c // chunk)
    tri = (c >= r) if rev else (c <= r)
    return jnp.where(same & tri, 1.0, 0.0).astype(F32)


def _causal(rev, n=CHUNK):
    t = lax.broadcasted_iota(jnp.int32, (n, n), 0)
    s = lax.broadcasted_iota(jnp.int32, (n, n), 1)
    return (s >= t) if rev else (s <= t)


def _cummax_rows(x, rev):
    n = x.shape[0]
    row = lax.broadcasted_iota(jnp.int32, (n, 1), 0)
    s = 1
    while s < n:
        if rev:
            x = jnp.maximum(x, jnp.where(row < n - s, pltpu.roll(x, n - s, 0), -jnp.inf))
        else:
            x = jnp.maximum(x, jnp.where(row >= s, pltpu.roll(x, s, 0), -jnp.inf))
        s *= 2
    return x


def _head_mask(width, group):
    lane = lax.broadcasted_iota(jnp.int32, (1, width), 1) // group
    return [lane == h for h in range(HEADS)]


def _gla_kernel(qk_ref, vz_ref, g_ref, aw_ref, ab_ref, og_ref, o_ref, of_scr, ob_scr, st_scr):
    T = qk_ref.shape[1]
    n_tiles = T // ROW_TILE
    n_chunk = ROW_TILE // CHUNK
    hm = _head_mask(HEADS * GLA_DK, GLA_DK)
    tris = (_chunk_tri(ROW_TILE, False), _chunk_tri(ROW_TILE, True))
    st_scr[...] = jnp.zeros_like(st_scr)

    def load_tile(t, rev):
        rows = pl.ds(pl.multiple_of(t * ROW_TILE, ROW_TILE), ROW_TILE)
        gcol = 256 if rev else 0
        qk = qk_ref[0, rows, :].astype(F32)
        pre = _dot(g_ref[0, rows, :], aw_ref[:, gcol:gcol + 256]) + ab_ref[:, gcol:gcol + 256]
        la = _log_sigmoid(pre) * (1.0 / GLA_TAU)
        hi, lo = _split(la)
        tri = tris[rev].astype(BF16)
        return qk[:, :256] * (GLA_DK ** -0.5), qk[:, 256:], _dot(tri, hi) + _dot(tri, lo)

    def tile(i, carry):
        tiles = (i, _tile_order(i, n_tiles, True))
        steps = []
        for cc in range(n_chunk):
            steps += [(0, cc), (1, n_chunk - 1 - cc)]
        data = [load_tile(tiles[d], bool(d)) for d in (0, 1)]
        n_sub = CHUNK // GLA_SUB
        pre, a_blk, upd, a_msk, intra, o_inter = {}, {}, {}, {}, {}, {}
        st = [st_scr[0], st_scr[1]]

        def prepare(d, cc):
            q, k, b = data[d]
            cr = slice(CHUNK * cc, CHUNK * cc + CHUNK)
            bc, qc, kc = b[cr], q[cr], k[cr]
            b_end = bc[0:1] if d else bc[CHUNK - 1:CHUNK]
            crow = pl.ds(pl.multiple_of(tiles[d] * ROW_TILE + CHUNK * cc, CHUNK), CHUNK)
            q_sub, k_sub = [], []
            for i in range(CHUNK // GLA_SUB):
                sr = slice(GLA_SUB * i, GLA_SUB * i + GLA_SUB)
                first = GLA_SUB * i + (GLA_SUB - 1 if d else 0)
                beta = bc[first:first + 1]
                q_i = qc[sr] * jnp.exp(bc[sr] - beta)
                q_sub.append(jnp.concatenate([jnp.where(hm[h], q_i, 0.0) for h in range(HEADS)],
                                             axis=0).astype(BF16))
                k_sub.append((kc * jnp.exp(jnp.minimum(beta - bc, EXP_CLAMP))).astype(BF16))
            qe = qc * jnp.exp(bc)
            stacked = jnp.concatenate([kc * jnp.exp(b_end - bc), jnp.broadcast_to(jnp.exp(b_end), (8, 256)),
                                       jnp.zeros((CHUNK - 8, 256), F32)], axis=0).T
            pre[d, cc] = dict(
                crow=crow, vc=vz_ref[0, crow, 0:BRANCH_WIDTH], decay=stacked[:, CHUNK:CHUNK + 1],
                qe=[jnp.where(hm[h], qe, 0.0).astype(BF16) for h in range(HEADS)], q_sub=q_sub, k_sub=k_sub,
                k_up_t=stacked[:, 0:CHUNK].astype(BF16))

        def score_dots(d, cc):
            p = pre[d, cc]
            a_blk[d, cc] = [_dot_nt(p['q_sub'][i], p['k_sub'][i]) for i in range(n_sub)]
            upd[d, cc] = jnp.concatenate(
                [_dot(p['k_up_t'][GLA_DK * h:GLA_DK * h + GLA_DK], p['vc'][:, 128 * h:128 * h + 128])
                 for h in range(HEADS)], axis=0)

        def mask_scores(d, cc):
            a_msk[d, cc] = [
                jnp.where(_causal(bool(d)), jnp.concatenate(
                    [a_blk[d, cc][i][GLA_SUB * h:GLA_SUB * h + GLA_SUB] for i in range(n_sub)], axis=0),
                    0.0).astype(BF16) for h in range(HEADS)]

        def value_dots(d, cc):
            vc = pre[d, cc]['vc']
            intra[d, cc] = jnp.concatenate(
                [_dot(a_msk[d, cc][h], vc[:, 128 * h:128 * h + 128]) for h in range(HEADS)], axis=-1)

        def state_dot(d, cc):
            s_bf = st[d].astype(BF16)
            o_inter[d, cc] = jnp.concatenate([_dot(pre[d, cc]['qe'][h], s_bf) for h in range(HEADS)], axis=-1)

        def finish_step(d, cc):
            (ob_scr if d else of_scr)[pre[d, cc]['crow'], :] = o_inter[d, cc] + intra[d, cc]
            st[d] = st[d] * pre[d, cc]['decay'] + upd[d, cc]

        def at(t):
            return [steps[t]] if 0 <= t < len(steps) else []

        for t in range(len(steps) + 4):
            for g in at(t - 1):
                score_dots(*g)
            for g in at(t - 3):
                value_dots(*g)
            for g in at(t - 4):
                state_dot(*g)
            for g in at(t):
                prepare(*g)
            for g in at(t - 2):
                mask_scores(*g)
            for g in at(t - 4):
                finish_step(*g)
        st_scr[0] = st[0]
        st_scr[1] = st[1]
        return carry

    lax.fori_loop(0, n_tiles, tile, 0)

    def finish(i, carry):
        rows = pl.ds(pl.multiple_of(i * ROW_TILE, ROW_TILE), ROW_TILE)
        tot = of_scr[rows, :] + ob_scr[rows, :]
        z = vz_ref[0, rows, BRANCH_WIDTH:].astype(F32)
        for h in range(HEADS):
            cols = slice(128 * h, 128 * h + 128)
            y = _rms(tot[:, cols], HEAD_V) * og_ref[:, cols]
            o_ref[0, rows, cols] = (y * _silu(z[:, cols])).astype(BF16)
        return carry

    lax.fori_loop(0, n_tiles, finish, 0)


def _gla(p, aw, ab, og):
    B, T, _ = p.shape
    full = lambda shape: pl.BlockSpec(shape, lambda b: (0,) * len(shape))
    return pl.pallas_call(
        _gla_kernel,
        grid=(B,),
        in_specs=[pl.BlockSpec((1, T, 512), lambda b: (b, 0, P_GLA_QK // 512)),
                  pl.BlockSpec((1, T, 1024), lambda b: (b, 0, P_GLA_VZ // 1024)),
                  pl.BlockSpec((1, T, 128), lambda b: (b, 0, P_GLA_G // 128)),
                  full(aw.shape), full(ab.shape), full(og.shape)],
        out_specs=pl.BlockSpec((1, T, 512), lambda b: (b, 0, 0)),
        out_shape=jax.ShapeDtypeStruct((B, T, BRANCH_WIDTH), BF16),
        scratch_shapes=[pltpu.VMEM((T, BRANCH_WIDTH), F32), pltpu.VMEM((T, BRANCH_WIDTH), F32),
                        pltpu.VMEM((2, HEADS * GLA_DK, HEAD_V), F32)],
        compiler_params=_params("arbitrary"),
        name="gla_scan",
    )(p, p, p, aw, ab, og)


def _mlstm_kernel(m_ref, if_ref, cw_ref, cb_ref, wq_ref, wkt_ref, gb_ref, og_ref, sk_ref, o_ref,
                  xc_scr, q_scr, kt_scr, at_scr, bt_scr, cm_scr, b_scr, hf_scr, hb_scr, c_scr, *, n_ctx):
    T = m_ref.shape[1]
    n_tiles = T // ROW_TILE
    assert ML_CHUNK == ROW_TILE
    ctx_tiles = n_ctx // ROW_TILE
    hm = _head_mask(HEADS * ML_DQK, ML_DQK)
    lane128 = lax.broadcasted_iota(jnp.int32, (1, 128), 1)
    is_forget = ((lane128 // HEADS) % 2) == 1
    row_in_tile = lax.broadcasted_iota(jnp.int32, (ROW_TILE, 1), 0)
    tri_f = _chunk_tri(ROW_TILE, False, ML_CHUNK).astype(BF16)
    tri_b = _chunk_tri(ROW_TILE, True, ML_CHUNK).astype(BF16)

    def prep_parts(i):
        r = pl.multiple_of(i * ROW_TILE, ROW_TILE)
        rows = pl.ds(r, ROW_TILE)
        kept = {}

        def conv_and_q():
            x = m_ref[0, rows, 0:512].astype(F32)
            rp = pl.multiple_of(jnp.maximum(r - 16, 0), 16)
            rn = pl.multiple_of(jnp.minimum(r + ROW_TILE, T - 16), 16)
            prev_row = m_ref[0, pl.ds(rp, 16), 0:512].astype(F32)[15:16]
            next_row = m_ref[0, pl.ds(rn, 16), 0:512].astype(F32)[0:1]
            seg_start = (i == 0) | (i == ctx_tiles)
            seg_end = (i == ctx_tiles - 1) | (i == n_tiles - 1)
            prev_row = jnp.where(seg_start, 0.0, prev_row)
            next_row = jnp.where(seg_end, 0.0, next_row)
            xm = jnp.where(row_in_tile == 0, prev_row, pltpu.roll(x, 1, 0))
            xp = jnp.where(row_in_tile == ROW_TILE - 1, next_row, pltpu.roll(x, ROW_TILE - 1, 0))
            xc = _silu(cw_ref[0:1, :] * xm + cw_ref[1:2, :] * x + cw_ref[2:3, :] * xp + cb_ref[...])
            xc_scr[rows, :] = xc
            kept['xb'] = xc.astype(BF16)
            q_scr[rows, :] = _dot(kept['xb'], wq_ref[...]).astype(BF16)

        def gate_terms():
            g = if_ref[0, rows, :].astype(F32) + gb_ref[...]
            g2 = jnp.where(is_forget, _log_sigmoid(g), g)
            g_hi, g_lo = _split(g2)
            cs = jnp.where(lane128 < 2 * HEADS, _dot(tri_f, g_hi) + _dot(tri_f, g_lo),
                           _dot(tri_b, g_hi) + _dot(tri_b, g_lo))
            b = pltpu.roll(cs, 128 - HEADS, 1)
            a = g2 - b
            kept['a'], kept['b'] = a, b
            cm_scr[rows, :] = jnp.where(lane128 < 2 * HEADS, _cummax_rows(a, False), _cummax_rows(a, True))
            b_scr[rows, :] = b

        def keys_and_rows():
            kt_scr[i] = (_dot_nt(wkt_ref[...], kept['xb']) * (ML_DQK ** -0.5)).astype(BF16)
            at_scr[i] = kept['a'].T[0:16, :]
            bt_scr[i] = kept['b'].T[0:16, :]

        return conv_and_q, gate_terms, keys_and_rows

    c_scr[...] = jnp.zeros_like(c_scr)
    ones_col = _ones_col(ML_CHUNK)
    sel_r = lax.broadcasted_iota(jnp.int32, (256, HEADS * 128), 0) % 128
    sel_c = lax.broadcasted_iota(jnp.int32, (256, HEADS * 128), 1) // 128
    sel = [jnp.where(sel_r == 2 * HEADS * d + sel_c, 1.0, 0.0).astype(BF16) for d in (0, 1)]

    def col_bcast(x, d):
        hi, lo = _split(x)
        return _dot(jnp.concatenate([hi, lo], axis=-1), sel[d])

    def chunk_step(d, c, m_in, fill):
        crow = pl.ds(pl.multiple_of(c * ML_CHUNK, ML_CHUNK), ML_CHUNK)
        qc, kt, vc = q_scr[crow, :], kt_scr[c], m_ref[0, crow, 512:1024]
        at, bt = at_scr[c], bt_scr[c]
        c_bf = c_scr[d].astype(BF16)
        causal = _causal(bool(d), ML_CHUNK)
        last = 0 if d else ML_CHUNK - 1
        heads = range(HEADS)
        qh = [jnp.where(hm[h], qc, jnp.zeros_like(qc)) for h in heads]
        v_ext = [jnp.concatenate([vc[:, 128 * h:128 * h + 128], ones_col], axis=-1) for h in heads]
        s_raw = [_dot(qh[h], kt) for h in heads]
        q_c = [_dot(qh[h], c_bf) for h in heads]
        cm_col = col_bcast(cm_scr[crow, :], d)
        b_col = col_bcast(b_scr[crow, :], d)
        if fill[0]:
            fill[0]()
        s_w, m_run, m_last, ktw = [], [], [], []
        for h in heads:
            a_row = at[2 * HEADS * d + h:2 * HEADS * d + h + 1, :]
            pm = jnp.where(causal, a_row, -jnp.inf)
            m_run.append(jnp.maximum(m_in[h], cm_col[:, 128 * h:128 * h + 128]))
            m_wide = jnp.concatenate([m_run[h]] * (ML_CHUNK // 128), axis=-1)
            s_w.append((s_raw[h] * jnp.exp(pm - m_wide)).astype(BF16))
            m_last.append(m_run[h][last:last + 1, 0:1])
            k_h = kt[ML_DQK * h:ML_DQK * h + ML_DQK, :].astype(F32)
            ktw.append((k_h * jnp.exp(a_row - m_last[h])).astype(BF16))
        s_v = [_dot(s_w[h], v_ext[h]) for h in heads]
        upd = [_dot(ktw[h], v_ext[h]) for h in heads]
        if fill[1]:
            fill[1]()
        parts, m_out = [], []
        for h in heads:
            j = 2 * HEADS * d + h
            hr = slice(ML_DQK * h, ML_DQK * h + ML_DQK)
            carry_w = jnp.exp(m_in[h] - m_run[h])
            num = jnp.concatenate([carry_w, carry_w], axis=-1) * q_c[h] + s_v[h]
            den = num[:, HEAD_V:HEAD_V + 1]
            floor = jnp.exp(-(b_col[:, 128 * h:128 * h + 128] + m_run[h]))
            parts.append(num[:, 0:HEAD_V] / jnp.maximum(jnp.abs(den), floor))
            c_scr[d, hr, :] = jnp.exp(m_in[h] - m_last[h]) * c_scr[d, hr, :] + upd[h]
            m_out.append(bt[j:j + 1, last:last + 1] + m_last[h])
        if fill[2]:
            fill[2]()
        return jnp.concatenate(parts, axis=-1), tuple(m_out)

    def finish(t, h_b):
        rows = pl.ds(pl.multiple_of(t * ROW_TILE, ROW_TILE), ROW_TILE)
        tot = (hf_scr[rows, :] + h_b) * _sigmoid(m_ref[0, rows, 1024:1536].astype(F32))
        z = m_ref[0, rows, 1536:2048].astype(F32)
        xc = xc_scr[rows, :]
        for h in range(HEADS):
            cols = slice(128 * h, 128 * h + 128)
            y = _rms(tot[:, cols], HEAD_V) * og_ref[:, cols]
            y = (y + sk_ref[:, cols] * xc[:, cols]) * _silu(z[:, cols])
            o_ref[0, rows, cols] = y.astype(BF16)

    def forward(i, m):
        h_f, m = chunk_step(0, i, m, prep_parts(jnp.minimum(i + 1, n_tiles - 1)))
        hf_scr[pl.ds(pl.multiple_of(i * ROW_TILE, ROW_TILE), ROW_TILE), :] = h_f
        return m

    def backward(i, m):
        t = _tile_order(i, n_tiles, True)
        t_prev = _tile_order(jnp.maximum(i - 1, 0), n_tiles, True)
        h_b, m = chunk_step(1, t, m, (None, lambda: finish(t_prev, hb_scr[...]), None))
        hb_scr[...] = h_b
        return m

    m0 = tuple(jnp.zeros((1, 1), F32) for _ in range(HEADS))
    for part in prep_parts(0):
        part()
    lax.fori_loop(0, n_tiles, forward, m0)
    hb_scr[...] = jnp.zeros_like(hb_scr)
    lax.fori_loop(0, n_tiles, backward, m0)
    finish(_tile_order(n_tiles - 1, n_tiles, True), hb_scr[...])


def _mlstm(p, cw, cb, wq, wkt, gb, og, sk, n_ctx):
    B, T, _ = p.shape
    full = lambda shape: pl.BlockSpec(shape, lambda b: (0,) * len(shape))
    return pl.pallas_call(
        functools.partial(_mlstm_kernel, n_ctx=n_ctx),
        grid=(B,),
        in_specs=[pl.BlockSpec((1, T, 2048), lambda b: (b, 0, P_ML // 2048)),
                  pl.BlockSpec((1, T, 128), lambda b: (b, 0, P_ML_IF // 128)),
                  full(cw.shape), full(cb.shape), full(wq.shape), full(wkt.shape), full(gb.shape),
                  full(og.shape), full(sk.shape)],
        out_specs=pl.BlockSpec((1, T, 512), lambda b: (b, 0, 0)),
        out_shape=jax.ShapeDtypeStruct((B, T, BRANCH_WIDTH), BF16),
        scratch_shapes=[pltpu.VMEM((T, BRANCH_WIDTH), F32),
                        pltpu.VMEM((T, HEADS * ML_DQK), BF16),
                        pltpu.VMEM((T // ML_CHUNK, HEADS * ML_DQK, ML_CHUNK), BF16),
                        pltpu.VMEM((T // ML_CHUNK, 16, ML_CHUNK), F32),
                        pltpu.VMEM((T // ML_CHUNK, 16, ML_CHUNK), F32),
                        pltpu.VMEM((T, 128), F32),
                        pltpu.VMEM((T, 128), F32),
                        pltpu.VMEM((T, BRANCH_WIDTH), F32),
                        pltpu.VMEM((ROW_TILE, BRANCH_WIDTH), F32),
                        pltpu.VMEM((2, HEADS * ML_DQK, 2 * HEAD_V), F32)],
        compiler_params=_params("arbitrary"),
        name="mlstm_scan",
    )(p, p, cw, cb, wq, wkt, gb, og, sk)


def _merge_kernel(ya_ref, yb_ref, yc_ref, yd_ref, gl_ref, brw_ref, wo_ref, x_ref, gate_ref, modn_ref, gn_ref,
                  o_ref, *h_ref, row0, ctx_rows):
    tm = x_ref.shape[1]
    first = pl.program_id(1) * tm + row0
    acc = None
    for i, y_ref in enumerate((ya_ref, yb_ref, yc_ref, yd_ref)):
        u = _dot(y_ref[0], brw_ref[i])
        gsig = _sigmoid(gl_ref[0, :, D_MODEL * i:D_MODEL * (i + 1)].astype(F32))
        acc = gsig * u if acc is None else acc + gsig * u
    out = _dot(acc.astype(BF16), wo_ref[...])
    for r0 in range(0, tm, ROW_TILE):
        rs = slice(r0, r0 + ROW_TILE)
        is_ctx = first + r0 < ctx_rows
        gate = jnp.where(is_ctx, gate_ref[0, 0:1, :], gate_ref[0, 1:2, :])
        x_new = x_ref[0, rs, :] + gate * out[rs]
        o_ref[0, rs, :] = x_new
        if h_ref:
            h_ref[0][0, rs, :] = _modulated_norm(x_new, gn_ref[...], modn_ref, is_ctx)


def _merge(ys, p, brw, wo, xs, gates, mod_next, g_next, n_ctx, last):
    B, T, D = xs.shape
    tm = ROW_TILE if last else MERGE_ROWS
    row0 = n_ctx if last else 0
    tile0 = row0 // tm
    nt = (T - row0) // tm
    assert (T - row0) % tm == 0 and row0 % tm == 0 and n_ctx % ROW_TILE == 0
    ymap = lambda b, t: (b, t + tile0, 0)
    out_spec = pl.BlockSpec((1, tm, D), lambda b, t: (b, t, 0))
    out_shape = jax.ShapeDtypeStruct((B, nt * tm, D), F32)
    res = pl.pallas_call(
        functools.partial(_merge_kernel, row0=row0, ctx_rows=n_ctx),
        grid=(B, nt),
        in_specs=[pl.BlockSpec((1, tm, BRANCH_WIDTH), ymap)] * 4 + [
            pl.BlockSpec((1, tm, N_BRANCH * D), lambda b, t: (b, t + tile0, P_MERGE // (N_BRANCH * D))),
            pl.BlockSpec(brw.shape, lambda b, t: (0, 0, 0)),
            pl.BlockSpec(wo.shape, lambda b, t: (0, 0)),
            pl.BlockSpec((1, tm, D), ymap),
            pl.BlockSpec((1, 2, D), lambda b, t: (b, 0, 0)),
            pl.BlockSpec((1, 8, D), lambda b, t: (b, 0, 0)),
            pl.BlockSpec((1, D), lambda b, t: (0, 0))],
        out_specs=out_spec if last else [out_spec, out_spec],
        out_shape=out_shape if last else [out_shape, jax.ShapeDtypeStruct((B, nt * tm, D), BF16)],
        compiler_params=_params("arbitrary", "arbitrary"),
        name="merge",
    )(*ys, p, brw, wo, xs, gates, mod_next, g_next.reshape(1, D))
    return (res, None) if last else res


def _layout_w_in(w_in):
    offs, off = {}, 0
    for name, w in IN_SPLITS:
        offs[name] = (off, w)
        off += w

    w_bf = w_in.astype(BF16)

    def col(name):
        o, w = offs[name]
        return w_bf[..., o:o + w]

    def zeros(n):
        return jnp.zeros(w_in.shape[:-1] + (n,), BF16)

    parts = [col('merge'), col('df_k'), col('df_v'), col('df_q'), col('df_z'),
             col('ml_x'), col('ml_v'), col('ml_o'), col('ml_z'),
             col('gla_v'), col('gla_z'), col('mla_z'), col('gla_q'), col('gla_k'),
             col('mla_cq'), col('mla_kr'), zeros(64),
             col('mla_ckv'), col('gla_af'), col('gla_ab'), zeros(96), col('ml_if'), zeros(112)]
    out = jnp.concatenate(parts, axis=-1)
    assert out.shape[-1] == P_WIDTH
    return out


def _rope_tables(rows, n_ctx):
    quarter = ROT_DIM // 4
    inv_freq = ROPE_BASE ** (-jnp.arange(quarter, dtype=F32) / quarter)
    row = jnp.repeat(jnp.arange(rows, dtype=F32), GRID_W)
    col = jnp.tile(jnp.arange(GRID_W, dtype=F32), rows)
    ar = row[:, None] * inv_freq
    ac = col[:, None] * inv_freq
    ang = jnp.concatenate([ar, ar, ac, ac], axis=-1)
    cos = jnp.concatenate([jnp.ones((n_ctx, ROT_DIM), F32), jnp.cos(ang)], axis=0)
    sin = jnp.concatenate([jnp.zeros((n_ctx, ROT_DIM), F32), jnp.sin(ang)], axis=0)
    zero = jnp.zeros_like(cos)
    both = jnp.stack([jnp.tile(cos, (1, 2)), jnp.tile(sin, (1, 2))])
    half = jnp.stack([jnp.concatenate([cos, zero], -1), jnp.concatenate([sin, zero], -1)])
    return half, both


def _pad_lanes(v, n):
    return jnp.concatenate([v, jnp.zeros(v.shape[:-1] + (n - v.shape[-1],), v.dtype)], axis=-1)


def kernel(x, c, ctx, c_ctx, ada_w, ada_b, norm_g, w_in, mla_cq_g, mla_ckv_g, mla_wuq, mla_wukv, mla_q_g,
           mla_k_g, gla_a_w, gla_a_b, gla_out_g, ml_conv_w, ml_conv_b, ml_wq, ml_wk, ml_gate_b, ml_out_g,
           ml_skip, df_qk_g, df_lambda, df_out_g, br_w, w_out):
    B, S, D = x.shape
    n_ctx = ctx.shape[1]
    L = ada_w.shape[0]
    assert D == D_MODEL and n_ctx == ROW_TILE and S % ROW_TILE == 0 and S % GRID_W == 0

    rope_half, rope_both = _rope_tables(S // GRID_W, n_ctx)
    w_in_p = _layout_w_in(w_in)

    n_rows = -(-(B + 1) // 8) * 8
    cc = jnp.concatenate([c, c_ctx[None], jnp.zeros((n_rows - B - 1, D), F32)], axis=0)
    mod_all = _modulation(cc, ada_w, ada_b)

    xs = jnp.concatenate([ctx, x], axis=1)
    mods = []
    for l in range(L):
        m3 = mod_all[l].reshape(n_rows, 3, D)
        lat, cx = m3[:B], jnp.broadcast_to(m3[B][None], (B, 3, D))
        mods.append(jnp.concatenate([cx, lat, jnp.zeros((B, 2, D), F32)], axis=1))
    h = _first_norm(xs, mods[0], norm_g[0], n_ctx)
    for l in range(L):
        last = l == L - 1
        lam_init = 0.8 - 0.6 * math.exp(-0.3 * l)
        gates = jnp.stack([mods[l][:, 2], mods[l][:, 5]], axis=1)

        p = _in_projection(h, w_in_p, l)

        wq4 = mla_wuq[l].reshape(MLA_Q_LORA, HEADS, MLA_NOPE + ROT_DIM)
        wuq = _pad_lanes(wq4, 256).reshape(MLA_Q_LORA, HEADS * 256).astype(BF16)
        wkv4 = mla_wukv[l].reshape(MLA_KV_LORA, HEADS, MLA_NOPE + HEAD_V)
        wukv = jnp.concatenate([wkv4[..., :MLA_NOPE].reshape(MLA_KV_LORA, -1),
                                wkv4[..., MLA_NOPE:].reshape(MLA_KV_LORA, -1)], axis=-1).astype(BF16)
        qg = jnp.stack([mla_q_g[l, :MLA_NOPE], _pad_lanes(mla_q_g[l, MLA_NOPE:], 128)])
        kg = jnp.stack([mla_k_g[l, :MLA_NOPE], _pad_lanes(mla_k_g[l, MLA_NOPE:], 128)])
        y_mla = _mla(p, wuq, wukv, mla_cq_g[l][None], mla_ckv_g[l][None], qg, kg, rope_half, n_ctx, last)

        qkg = jnp.tile(df_qk_g[l], (1, 2))
        y_df = _diff(p, qkg, df_lambda[l], df_out_g[l][None], rope_both, n_ctx, last, lam_init)

        aw = jnp.zeros((128, 512), F32)
        aw = aw.at[0:16, 0:256].set(gla_a_w[l, 0]).at[16:32, 256:512].set(gla_a_w[l, 1]).astype(BF16)
        y_gla = _gla(p, aw, gla_a_b[l].reshape(1, 512), gla_out_g[l][None])

        wq_bd = jnp.zeros((BRANCH_WIDTH, HEADS * ML_DQK), F32)
        wk_bd = jnp.zeros((BRANCH_WIDTH, HEADS * ML_DQK), F32)
        for h in range(HEADS):
            wq_bd = wq_bd.at[128 * h:128 * h + 128, 64 * h:64 * h + 64].set(ml_wq[l, h])
            wk_bd = wk_bd.at[128 * h:128 * h + 128, 64 * h:64 * h + 64].set(ml_wk[l, h])
        gb = _pad_lanes(ml_gate_b[l].reshape(1, 16), 128)
        y_ml = _mlstm(p, ml_conv_w[l], ml_conv_b[l][None], wq_bd.astype(BF16), wk_bd.T.astype(BF16), gb,
                      ml_out_g[l][None], ml_skip[l][None], n_ctx)

        nxt = l if last else l + 1
        xs, h = _merge((y_mla, y_gla, y_ml, y_df), p, br_w[l].astype(BF16), w_out[l].astype(BF16), xs, gates,
                       mods[nxt], norm_g[nxt], n_ctx, last)
    return xs
```

```python
import functools
import math

import jax
import jax.numpy as jnp
from jax import lax
from jax.experimental import pallas as pl
from jax.experimental.pallas import tpu as pltpu

F32 = jnp.float32
BF16 = jnp.bfloat16

D_MODEL = 1024
GRID_W = 64
EPS = 1e-6
ROPE_BASE = 10000.0
ROT_DIM = 64
CHUNK = 64
N_BRANCH = 4
BRANCH_WIDTH = 512
HEADS = 4
HEAD_V = BRANCH_WIDTH // HEADS
MLA_NOPE = 128
MLA_Q_LORA = 384
MLA_KV_LORA = 256
GLA_DK = 64
GLA_TAU = 16.0
ML_DQK = 64
DF_DQK = 64

IN_SPLITS = (
    ('mla_cq', 384), ('mla_ckv', 256), ('mla_kr', 64), ('mla_z', 512),
    ('gla_q', 256), ('gla_k', 256), ('gla_v', 512), ('gla_af', 16), ('gla_ab', 16), ('gla_z', 512),
    ('ml_x', 512), ('ml_v', 512), ('ml_o', 512), ('ml_if', 16), ('ml_z', 512),
    ('df_q', 512), ('df_k', 512), ('df_v', 512), ('df_z', 512),
    ('merge', 4096),
)

P_MERGE = 0
P_DF_KV = 4096
P_DF_Q = 5120
P_DF_Z = 5632
P_ML = 6144
P_GLA_VZ = 8192
P_MLA_Z = 9216
P_GLA_QK = 9728
P_MLA_A = 10240
P_MLA_CKV = 10752
P_GLA_G = 11008
P_ML_IF = 11136
P_WIDTH = 11264

ROW_TILE = 256
IN_PROJ_COLS = 2816
MERGE_ROWS = 768
ML_CHUNK = 256
ATT_ROWS = 512
VMEM_LIMIT = 56 * 1024 * 1024
EXP_CLAMP = 80.0
GLA_SUB = 16


def _dot(a, b):
    return jnp.dot(a, b, preferred_element_type=F32)


def _dot_nt(a, b):
    return lax.dot_general(a, b, (((1,), (1,)), ((), ())), preferred_element_type=F32)


def _sigmoid(x):
    return 1.0 / (1.0 + jnp.exp(-x))


def _silu(x):
    return x * _sigmoid(x)


def _log_sigmoid(x):
    return jnp.minimum(x, 0.0) - jnp.log(1.0 + jnp.exp(-jnp.abs(x)))


def _rms(x, n):
    return x * lax.rsqrt(jnp.sum(x * x, axis=-1, keepdims=True) * (1.0 / n) + EPS)


def _split(x):
    hi = x.astype(BF16)
    return hi, (x - hi.astype(F32)).astype(BF16)


def _group_sum(x, group):
    row = lax.broadcasted_iota(jnp.int32, (128, 128), 0)
    col = lax.broadcasted_iota(jnp.int32, (128, 128), 1)
    sel = jnp.where(row // group == col // group, 1.0, 0.0).astype(BF16)
    hi, lo = _split(x)
    return _dot(hi, sel) + _dot(lo, sel)


def _rms_mxu(x, group):
    return x * lax.rsqrt(_group_sum(x * x, group) * (1.0 / group) + EPS)


def _rot_matrix():
    src = lax.broadcasted_iota(jnp.int32, (128, 128), 0)
    dst = lax.broadcasted_iota(jnp.int32, (128, 128), 1)
    even = (dst // 16) % 2 == 0
    return jnp.where(even & (src == dst + 16), -1.0, jnp.where(~even & (src == dst - 16), 1.0, 0.0)).astype(BF16)


def _rope_mxu(x, cos, sin, rot):
    hi, lo = _split(x)
    return x * cos + (_dot(hi, rot) + _dot(lo, rot)) * sin


def _ones_col(rows):
    return jnp.where(lax.broadcasted_iota(jnp.int32, (rows, HEAD_V), 1) == 0, 1.0, 0.0).astype(BF16)


def _softmax_pv(s, v_ext):
    e = jnp.exp((s - jnp.max(s, axis=-1, keepdims=True)).astype(BF16))
    o = _dot(e, v_ext)
    return o[:, 0:HEAD_V] / o[:, HEAD_V:HEAD_V + 1]


def _params(*sem):
    return pltpu.CompilerParams(dimension_semantics=sem, vmem_limit_bytes=VMEM_LIMIT)


def _mod_kernel(c_ref, w_ref, b_ref, o_ref):
    s = _silu(c_ref[...])
    o_ref[0] = _dot(s.astype(BF16), w_ref[0].astype(BF16)) + b_ref[0]


def _modulation(cc, ada_w, ada_b):
    L, D, _ = ada_w.shape
    R = cc.shape[0]
    return pl.pallas_call(
        _mod_kernel,
        grid=(L, 3),
        in_specs=[pl.BlockSpec((R, D), lambda l, j: (0, 0)),
                  pl.BlockSpec((1, D, D), lambda l, j: (l, 0, j)),
                  pl.BlockSpec((1, 1, D), lambda l, j: (l, 0, j))],
        out_specs=pl.BlockSpec((1, R, D), lambda l, j: (l, 0, j)),
        out_shape=jax.ShapeDtypeStruct((L, R, 3 * D), F32),
        compiler_params=_params("arbitrary", "arbitrary"),
        name="modulation",
    )(cc, ada_w, ada_b.reshape(L, 1, 3 * D))


def _modulated_norm(x, g, mod_ref, is_ctx):
    shift = jnp.where(is_ctx, mod_ref[0, 0:1, :], mod_ref[0, 3:4, :])
    scale = jnp.where(is_ctx, mod_ref[0, 1:2, :], mod_ref[0, 4:5, :])
    return (_rms(x, D_MODEL) * g * (1.0 + scale) + shift).astype(BF16)


def _norm_kernel(x_ref, mod_ref, g_ref, h_ref, *, ctx_rows):
    tm = x_ref.shape[1]
    first = pl.program_id(1) * tm
    for r0 in range(0, tm, ROW_TILE):
        rs = slice(r0, r0 + ROW_TILE)
        h_ref[0, rs, :] = _modulated_norm(x_ref[0, rs, :], g_ref[...], mod_ref, first + r0 < ctx_rows)


def _first_norm(xs, mod, norm_g, n_ctx):
    B, T, D = xs.shape
    return pl.pallas_call(
        functools.partial(_norm_kernel, ctx_rows=n_ctx),
        grid=(B, T // MERGE_ROWS),
        in_specs=[pl.BlockSpec((1, MERGE_ROWS, D), lambda b, t: (b, t, 0)),
                  pl.BlockSpec((1, 8, D), lambda b, t: (b, 0, 0)),
                  pl.BlockSpec((1, D), lambda b, t: (0, 0))],
        out_specs=pl.BlockSpec((1, MERGE_ROWS, D), lambda b, t: (b, t, 0)),
        out_shape=jax.ShapeDtypeStruct((B, T, D), BF16),
        compiler_params=_params("arbitrary", "arbitrary"),
        name="first_norm",
    )(xs, mod, norm_g.reshape(1, D))


def _inproj_kernel(h_ref, w_ref, o_ref):
    for r0 in range(0, h_ref.shape[1], ROW_TILE):
        o_ref[0, r0:r0 + ROW_TILE, :] = _dot(h_ref[0, r0:r0 + ROW_TILE, :], w_ref[0]).astype(BF16)


def _in_projection(h, w_all, layer):
    B, T, D = h.shape
    return pl.pallas_call(
        _inproj_kernel,
        grid=(B, P_WIDTH // IN_PROJ_COLS),
        in_specs=[pl.BlockSpec((1, T, D), lambda b, j: (b, 0, 0)),
                  pl.BlockSpec((1, D, IN_PROJ_COLS), lambda b, j: (layer, 0, j))],
        out_specs=pl.BlockSpec((1, T, IN_PROJ_COLS), lambda b, j: (b, 0, j)),
        out_shape=jax.ShapeDtypeStruct((B, T, P_WIDTH), BF16),
        compiler_params=_params("arbitrary", "arbitrary"),
        name="in_projection",
    )(h, w_all)


def _latent_rows(T, n_ctx):
    return ATT_ROWS if (T - n_ctx) % ATT_ROWS == 0 else ROW_TILE


def _attend_all(attend, scores0, s0_scr, o_ref, T, n_ctx, with_ctx):
    if with_ctx:
        attend(pl.ds(0, n_ctx), n_ctx, None, None)
    else:
        o_ref[0, 0:n_ctx, :] = jnp.zeros((n_ctx, o_ref.shape[2]), o_ref.dtype)
    rows_per = s0_scr.shape[0]
    n = (T - n_ctx) // rows_per

    def rows_of(i):
        return pl.ds(pl.multiple_of(n_ctx + i * rows_per, ROW_TILE), rows_per)

    s0_scr[...] = scores0(rows_of(0))

    def latent(i, carry):
        attend(rows_of(i), T, s0_scr[...], rows_of(jnp.minimum(i + 1, n - 1)))
        return carry

    lax.fori_loop(0, n, latent, 0)


def _mla_kernel(pa_ref, pc_ref, z_ref, wuq_ref, wukv_ref, cqg_ref, ckvg_ref, qg_ref, kg_ref, rope_ref,
                o_ref, q_scr, k_scr, v_scr, s0_scr, *, n_ctx, with_ctx):
    T = pa_ref.shape[1]
    scale = (MLA_NOPE + ROT_DIM) ** -0.5
    rot = _rot_matrix()

    def prep(i, carry):
        r = pl.multiple_of(i * ROW_TILE, ROW_TILE)
        rows = pl.ds(r, ROW_TILE)
        cos, sin = rope_ref[0, rows, :], rope_ref[1, rows, :]
        pa = pa_ref[0, rows, :].astype(F32)
        cq = (_rms(pa[:, :MLA_Q_LORA], MLA_Q_LORA) * cqg_ref[...]).astype(BF16)
        q = _dot(cq, wuq_ref[...])
        kr = _rms(pa[:, MLA_Q_LORA:], ROT_DIM) * kg_ref[1:2, :]
        kr = _rope_mxu(kr, cos, sin, rot).astype(BF16)
        ckv = (_rms(pc_ref[0, rows, :].astype(F32), MLA_KV_LORA) * ckvg_ref[...]).astype(BF16)
        kv = _dot(ckv, wukv_ref[...])
        for h in range(HEADS):
            qn = _rms(q[:, 256 * h:256 * h + 128], MLA_NOPE) * qg_ref[0:1, :]
            qr = _rms(q[:, 256 * h + 128:256 * h + 256], ROT_DIM) * qg_ref[1:2, :]
            qr = _rope_mxu(qr, cos, sin, rot)
            q_scr[rows, 256 * h:256 * h + 128] = (qn * scale).astype(BF16)
            q_scr[rows, 256 * h + 128:256 * h + 256] = (qr * scale).astype(BF16)
            kn = _rms(kv[:, 128 * h:128 * h + 128], MLA_NOPE) * kg_ref[0:1, :]
            k_scr[rows, 256 * h:256 * h + 128] = kn.astype(BF16)
            k_scr[rows, 256 * h + 128:256 * h + 256] = kr
            v_scr[rows, 256 * h:256 * h + 128] = kv[:, 512 + 128 * h:512 + 128 * h + 128].astype(BF16)
            v_scr[rows, 256 * h + 128:256 * h + 256] = _ones_col(ROW_TILE)
        return carry

    lax.fori_loop(0, T // ROW_TILE, prep, 0)

    def scores(h, rows, nk):
        return _dot_nt(q_scr[rows, 256 * h:256 * h + 256], k_scr[0:nk, 256 * h:256 * h + 256])

    def attend(rows, nk, s_first, next_rows):
        ahead = HEADS if next_rows is None else 1
        pending = [scores(0, rows, nk) if s_first is None else s_first]
        pending += [scores(h, rows, nk) for h in range(1, ahead)]
        for h in range(HEADS):
            s = pending.pop(0)
            if h + ahead < HEADS:
                pending.append(scores(h + ahead, rows, nk))
            elif h + ahead == HEADS and next_rows is not None:
                s0_scr[...] = scores(0, next_rows, nk)
            o = _softmax_pv(s, v_scr[0:nk, 256 * h:256 * h + 256])
            z = z_ref[0, rows, 128 * h:128 * h + 128].astype(F32)
            o_ref[0, rows, 128 * h:128 * h + 128] = (o * _silu(z)).astype(BF16)

    _attend_all(attend, lambda rows: scores(0, rows, T), s0_scr, o_ref, T, n_ctx, with_ctx)


def _mla(p, wuq, wukv, cqg, ckvg, qg, kg, rope, n_ctx, last):
    B, T, _ = p.shape
    full = lambda shape: pl.BlockSpec(shape, lambda b: (0,) * len(shape))
    return pl.pallas_call(
        functools.partial(_mla_kernel, n_ctx=n_ctx, with_ctx=not last),
        grid=(B,),
        in_specs=[pl.BlockSpec((1, T, 512), lambda b: (b, 0, P_MLA_A // 512)),
                  pl.BlockSpec((1, T, 256), lambda b: (b, 0, P_MLA_CKV // 256)),
                  pl.BlockSpec((1, T, 512), lambda b: (b, 0, P_MLA_Z // 512)),
                  full(wuq.shape), full(wukv.shape), full(cqg.shape), full(ckvg.shape),
                  full(qg.shape), full(kg.shape), full(rope.shape)],
        out_specs=pl.BlockSpec((1, T, 512), lambda b: (b, 0, 0)),
        out_shape=jax.ShapeDtypeStruct((B, T, BRANCH_WIDTH), BF16),
        scratch_shapes=[pltpu.VMEM((T, 1024), BF16), pltpu.VMEM((T, 1024), BF16),
                        pltpu.VMEM((T, 1024), BF16), pltpu.VMEM((_latent_rows(T, n_ctx), T), F32)],
        compiler_params=_params("arbitrary"),
        name="mla",
    )(p, p, p, wuq, wukv, cqg, ckvg, qg, kg, rope)


def _diff_kernel(kv_ref, q_ref, z_ref, g_ref, lam_ref, og_ref, rope_ref, o_ref, q0_scr, q1_scr, k_scr, v_scr, s0_scr,
                 *, n_ctx, with_ctx, lam_init):
    T = kv_ref.shape[1]
    scale = DF_DQK ** -0.5
    lo_mask = lax.broadcasted_iota(jnp.int32, (1, 128), 1) < 64
    rot = _rot_matrix()

    def prep(i, carry):
        r = pl.multiple_of(i * ROW_TILE, ROW_TILE)
        rows = pl.ds(r, ROW_TILE)
        cos, sin = rope_ref[0, rows, :], rope_ref[1, rows, :]
        for h in range(HEADS):
            cols = slice(128 * h, 128 * h + 128)
            q = _rms_mxu(q_ref[0, rows, cols].astype(F32), DF_DQK) * g_ref[0:1, :]
            q = _rope_mxu(q, cos, sin, rot) * scale
            q0_scr[rows, cols] = jnp.where(lo_mask, q, 0.0).astype(BF16)
            q1_scr[rows, cols] = jnp.where(lo_mask, 0.0, q).astype(BF16)
            k = _rms_mxu(kv_ref[0, rows, cols].astype(F32), DF_DQK) * g_ref[1:2, :]
            k_scr[rows, cols] = _rope_mxu(k, cos, sin, rot).astype(BF16)
            v_scr[rows, 256 * h:256 * h + 128] = kv_ref[0, rows, 512 + 128 * h:512 + 128 * h + 128]
            v_scr[rows, 256 * h + 128:256 * h + 256] = _ones_col(ROW_TILE)
        return carry

    lax.fori_loop(0, T // ROW_TILE, prep, 0)

    lp = lam_ref[...]
    lam = (jnp.exp(jnp.sum(lp[0:1] * lp[1:2], axis=-1, keepdims=True))
           - jnp.exp(jnp.sum(lp[2:3] * lp[3:4], axis=-1, keepdims=True)) + lam_init)
    def scores(u, rows, nk):
        cols = slice(128 * (u // 2), 128 * (u // 2) + 128)
        return _dot_nt((q1_scr if u % 2 else q0_scr)[rows, cols], k_scr[0:nk, cols])

    def attend(rows, nk, s_first, next_rows):
        n_units = 2 * HEADS
        ahead = n_units if next_rows is None else 1
        pending = [scores(0, rows, nk) if s_first is None else s_first]
        pending += [scores(u, rows, nk) for u in range(1, ahead)]
        for h in range(HEADS):
            cols = slice(128 * h, 128 * h + 128)
            v_ext = v_scr[0:nk, 256 * h:256 * h + 256]
            parts = []
            for mp in range(2):
                u = 2 * h + mp
                s = pending.pop(0)
                if u + ahead < n_units:
                    pending.append(scores(u + ahead, rows, nk))
                elif u + ahead == n_units and next_rows is not None:
                    s0_scr[...] = scores(0, next_rows, nk)
                parts.append(_softmax_pv(s, v_ext))
            o = parts[0] - lam * parts[1]
            o = _rms(o, HEAD_V) * og_ref[...] * (1.0 - lam_init)
            z = z_ref[0, rows, cols].astype(F32)
            o_ref[0, rows, cols] = (o * _silu(z)).astype(BF16)

    _attend_all(attend, lambda rows: scores(0, rows, T), s0_scr, o_ref, T, n_ctx, with_ctx)


def _diff(p, qkg, lam_p, og, rope, n_ctx, last, lam_init):
    B, T, _ = p.shape
    full = lambda shape: pl.BlockSpec(shape, lambda b: (0,) * len(shape))
    return pl.pallas_call(
        functools.partial(_diff_kernel, n_ctx=n_ctx, with_ctx=not last, lam_init=lam_init),
        grid=(B,),
        in_specs=[pl.BlockSpec((1, T, 1024), lambda b: (b, 0, P_DF_KV // 1024)),
                  pl.BlockSpec((1, T, 512), lambda b: (b, 0, P_DF_Q // 512)),
                  pl.BlockSpec((1, T, 512), lambda b: (b, 0, P_DF_Z // 512)),
                  full(qkg.shape), full(lam_p.shape), full(og.shape), full(rope.shape)],
        out_specs=pl.BlockSpec((1, T, 512), lambda b: (b, 0, 0)),
        out_shape=jax.ShapeDtypeStruct((B, T, BRANCH_WIDTH), BF16),
        scratch_shapes=[pltpu.VMEM((T, 512), BF16), pltpu.VMEM((T, 512), BF16), pltpu.VMEM((T, 512), BF16),
                        pltpu.VMEM((T, 1024), BF16), pltpu.VMEM((_latent_rows(T, n_ctx), T), F32)],
        compiler_params=_params("arbitrary"),
        name="diff_attn",
    )(p, p, p, qkg, lam_p, og, rope)


def _tile_order(i, n_tiles, rev):
    if not rev:
        return i
    return jnp.where(i == 0, 0, n_tiles - i)


def _chunk_tri(n, rev, chunk=CHUNK):
    r = lax.broadcasted_iota(jnp.int32, (n, n), 0)
    c = lax.broadcasted_iota(jnp.int32, (n, n), 1)
    same = (r // chunk) == (c // chunk)
    tri = (c >= r) if rev else (c <= r)
    return jnp.where(same & tri, 1.0, 0.0).astype(F32)


def _causal(rev, n=CHUNK):
    t = lax.broadcasted_iota(jnp.int32, (n, n), 0)
    s = lax.broadcasted_iota(jnp.int32, (n, n), 1)
    return (s >= t) if rev else (s <= t)


def _cummax_rows(x, rev):
    n = x.shape[0]
    row = lax.broadcasted_iota(jnp.int32, (n, 1), 0)
    s = 1
    while s < n:
        if rev:
            x = jnp.maximum(x, jnp.where(row < n - s, pltpu.roll(x, n - s, 0), -jnp.inf))
        else:
            x = jnp.maximum(x, jnp.where(row >= s, pltpu.roll(x, s, 0), -jnp.inf))
        s *= 2
    return x


def _head_mask(width, group):
    lane = lax.broadcasted_iota(jnp.int32, (1, width), 1) // group
    return [lane == h for h in range(HEADS)]


def _gla_kernel(qk_ref, vz_ref, g_ref, aw_ref, ab_ref, og_ref, o_ref, of_scr, ob_scr, st_scr):
    T = qk_ref.shape[1]
    n_tiles = T // ROW_TILE
    n_chunk = ROW_TILE // CHUNK
    hm = _head_mask(HEADS * GLA_DK, GLA_DK)
    tris = (_chunk_tri(ROW_TILE, False), _chunk_tri(ROW_TILE, True))
    st_scr[...] = jnp.zeros_like(st_scr)

    def load_tile(t, rev):
        rows = pl.ds(pl.multiple_of(t * ROW_TILE, ROW_TILE), ROW_TILE)
        gcol = 256 if rev else 0
        qk = qk_ref[0, rows, :].astype(F32)
        pre = _dot(g_ref[0, rows, :], aw_ref[:, gcol:gcol + 256]) + ab_ref[:, gcol:gcol + 256]
        la = _log_sigmoid(pre) * (1.0 / GLA_TAU)
        hi, lo = _split(la)
        tri = tris[rev].astype(BF16)
        return qk[:, :256] * (GLA_DK ** -0.5), qk[:, 256:], _dot(tri, hi) + _dot(tri, lo)

    def tile(i, carry):
        tiles = (i, _tile_order(i, n_tiles, True))
        steps = []
        for cc in range(n_chunk):
            steps += [(0, cc), (1, n_chunk - 1 - cc)]
        data = [load_tile(tiles[d], bool(d)) for d in (0, 1)]
        n_sub = CHUNK // GLA_SUB
        pre, a_blk, upd, a_msk, intra, o_inter = {}, {}, {}, {}, {}, {}
        st = [st_scr[0], st_scr[1]]

        def prepare(d, cc):
            q, k, b = data[d]
            cr = slice(CHUNK * cc, CHUNK * cc + CHUNK)
            bc, qc, kc = b[cr], q[cr], k[cr]
            b_end = bc[0:1] if d else bc[CHUNK - 1:CHUNK]
            crow = pl.ds(pl.multiple_of(tiles[d] * ROW_TILE + CHUNK * cc, CHUNK), CHUNK)
            q_sub, k_sub = [], []
            for i in range(CHUNK // GLA_SUB):
                sr = slice(GLA_SUB * i, GLA_SUB * i + GLA_SUB)
                first = GLA_SUB * i + (GLA_SUB - 1 if d else 0)
                beta = bc[first:first + 1]
                q_i = qc[sr] * jnp.exp(bc[sr] - beta)
                q_sub.append(jnp.concatenate([jnp.where(hm[h], q_i, 0.0) for h in range(HEADS)],
                                             axis=0).astype(BF16))
                k_sub.append((kc * jnp.exp(jnp.minimum(beta - bc, EXP_CLAMP))).astype(BF16))
            qe = qc * jnp.exp(bc)
            stacked = jnp.concatenate([kc * jnp.exp(b_end - bc), jnp.broadcast_to(jnp.exp(b_end), (8, 256)),
                                       jnp.zeros((CHUNK - 8, 256), F32)], axis=0).T
            pre[d, cc] = dict(
                crow=crow, vc=vz_ref[0, crow, 0:BRANCH_WIDTH], decay=stacked[:, CHUNK:CHUNK + 1],
                qe=[jnp.where(hm[h], qe, 0.0).astype(BF16) for h in range(HEADS)], q_sub=q_sub, k_sub=k_sub,
                k_up_t=stacked[:, 0:CHUNK].astype(BF16))

        def score_dots(d, cc):
            p = pre[d, cc]
            a_blk[d, cc] = [_dot_nt(p['q_sub'][i], p['k_sub'][i]) for i in range(n_sub)]
            upd[d, cc] = jnp.concatenate(
                [_dot(p['k_up_t'][GLA_DK * h:GLA_DK * h + GLA_DK], p['vc'][:, 128 * h:128 * h + 128])
                 for h in range(HEADS)], axis=0)

        def mask_scores(d, cc):
            a_msk[d, cc] = [
                jnp.where(_causal(bool(d)), jnp.concatenate(
                    [a_blk[d, cc][i][GLA_SUB * h:GLA_SUB * h + GLA_SUB] for i in range(n_sub)], axis=0),
                    0.0).astype(BF16) for h in range(HEADS)]

        def value_dots(d, cc):
            vc = pre[d, cc]['vc']
            intra[d, cc] = jnp.concatenate(
                [_dot(a_msk[d, cc][h], vc[:, 128 * h:128 * h + 128]) for h in range(HEADS)], axis=-1)

        def state_dot(d, cc):
            s_bf = st[d].astype(BF16)
            o_inter[d, cc] = jnp.concatenate([_dot(pre[d, cc]['qe'][h], s_bf) for h in range(HEADS)], axis=-1)

        def finish_step(d, cc):
            (ob_scr if d else of_scr)[pre[d, cc]['crow'], :] = o_inter[d, cc] + intra[d, cc]
            st[d] = st[d] * pre[d, cc]['decay'] + upd[d, cc]

        def at(t):
            return [steps[t]] if 0 <= t < len(steps) else []

        for t in range(len(steps) + 4):
            for g in at(t - 1):
                score_dots(*g)
            for g in at(t - 3):
                value_dots(*g)
            for g in at(t - 4):
                state_dot(*g)
            for g in at(t):
                prepare(*g)
            for g in at(t - 2):
                mask_scores(*g)
            for g in at(t - 4):
                finish_step(*g)
        st_scr[0] = st[0]
        st_scr[1] = st[1]
        return carry

    lax.fori_loop(0, n_tiles, tile, 0)

    def finish(i, carry):
        rows = pl.ds(pl.multiple_of(i * ROW_TILE, ROW_TILE), ROW_TILE)
        tot = of_scr[rows, :] + ob_scr[rows, :]
        z = vz_ref[0, rows, BRANCH_WIDTH:].astype(F32)
        for h in range(HEADS):
            cols = slice(128 * h, 128 * h + 128)
            y = _rms(tot[:, cols], HEAD_V) * og_ref[:, cols]
            o_ref[0, rows, cols] = (y * _silu(z[:, cols])).astype(BF16)
        return carry

    lax.fori_loop(0, n_tiles, finish, 0)


def _gla(p, aw, ab, og):
    B, T, _ = p.shape
    full = lambda shape: pl.BlockSpec(shape, lambda b: (0,) * len(shape))
    return pl.pallas_call(
        _gla_kernel,
        grid=(B,),
        in_specs=[pl.BlockSpec((1, T, 512), lambda b: (b, 0, P_GLA_QK // 512)),
                  pl.BlockSpec((1, T, 1024), lambda b: (b, 0, P_GLA_VZ // 1024)),
                  pl.BlockSpec((1, T, 128), lambda b: (b, 0, P_GLA_G // 128)),
                  full(aw.shape), full(ab.shape), full(og.shape)],
        out_specs=pl.BlockSpec((1, T, 512), lambda b: (b, 0, 0)),
        out_shape=jax.ShapeDtypeStruct((B, T, BRANCH_WIDTH), BF16),
        scratch_shapes=[pltpu.VMEM((T, BRANCH_WIDTH), F32), pltpu.VMEM((T, BRANCH_WIDTH), F32),
                        pltpu.VMEM((2, HEADS * GLA_DK, HEAD_V), F32)],
        compiler_params=_params("arbitrary"),
        name="gla_scan",
    )(p, p, p, aw, ab, og)


def _mlstm_kernel(m_ref, if_ref, cw_ref, cb_ref, wq_ref, wkt_ref, gb_ref, og_ref, sk_ref, o_ref,
                  xc_scr, q_scr, kt_scr, at_scr, bt_scr, cm_scr, b_scr, hf_scr, hb_scr, c_scr, *, n_ctx):
    T = m_ref.shape[1]
    n_tiles = T // ROW_TILE
    assert ML_CHUNK == ROW_TILE
    ctx_tiles = n_ctx // ROW_TILE
    hm = _head_mask(HEADS * ML_DQK, ML_DQK)
    lane128 = lax.broadcasted_iota(jnp.int32, (1, 128), 1)
    is_forget = ((lane128 // HEADS) % 2) == 1
    row_in_tile = lax.broadcasted_iota(jnp.int32, (ROW_TILE, 1), 0)
    tri_f = _chunk_tri(ROW_TILE, False, ML_CHUNK).astype(BF16)
    tri_b = _chunk_tri(ROW_TILE, True, ML_CHUNK).astype(BF16)

    def prep_parts(i):
        r = pl.multiple_of(i * ROW_TILE, ROW_TILE)
        rows = pl.ds(r, ROW_TILE)
        kept = {}

        def conv_and_q():
            x = m_ref[0, rows, 0:512].astype(F32)
            rp = pl.multiple_of(jnp.maximum(r - 16, 0), 16)
            rn = pl.multiple_of(jnp.minimum(r + ROW_TILE, T - 16), 16)
            prev_row = m_ref[0, pl.ds(rp, 16), 0:512].astype(F32)[15:16]
            next_row = m_ref[0, pl.ds(rn, 16), 0:512].astype(F32)[0:1]
            seg_start = (i == 0) | (i == ctx_tiles)
            seg_end = (i == ctx_tiles - 1) | (i == n_tiles - 1)
            prev_row = jnp.where(seg_start, 0.0, prev_row)
            next_row = jnp.where(seg_end, 0.0, next_row)
            xm = jnp.where(row_in_tile == 0, prev_row, pltpu.roll(x, 1, 0))
            xp = jnp.where(row_in_tile == ROW_TILE - 1, next_row, pltpu.roll(x, ROW_TILE - 1, 0))
            xc = _silu(cw_ref[0:1, :] * xm + cw_ref[1:2, :] * x + cw_ref[2:3, :] * xp + cb_ref[...])
            xc_scr[rows, :] = xc
            kept['xb'] = xc.astype(BF16)
            q_scr[rows, :] = _dot(kept['xb'], wq_ref[...]).astype(BF16)

        def gate_terms():
            g = if_ref[0, rows, :].astype(F32) + gb_ref[...]
            g2 = jnp.where(is_forget, _log_sigmoid(g), g)
            g_hi, g_lo = _split(g2)
            cs = jnp.where(lane128 < 2 * HEADS, _dot(tri_f, g_hi) + _dot(tri_f, g_lo),
                           _dot(tri_b, g_hi) + _dot(tri_b, g_lo))
            b = pltpu.roll(cs, 128 - HEADS, 1)
            a = g2 - b
            kept['a'], kept['b'] = a, b
            cm_scr[rows, :] = jnp.where(lane128 < 2 * HEADS, _cummax_rows(a, False), _cummax_rows(a, True))
            b_scr[rows, :] = b

        def keys_and_rows():
            kt_scr[i] = (_dot_nt(wkt_ref[...], kept['xb']) * (ML_DQK ** -0.5)).astype(BF16)
            at_scr[i] = kept['a'].T[0:16, :]
            bt_scr[i] = kept['b'].T[0:16, :]

        return conv_and_q, gate_terms, keys_and_rows

    c_scr[...] = jnp.zeros_like(c_scr)
    ones_col = _ones_col(ML_CHUNK)
    sel_r = lax.broadcasted_iota(jnp.int32, (256, HEADS * 128), 0) % 128
    sel_c = lax.broadcasted_iota(jnp.int32, (256, HEADS * 128), 1) // 128
    sel = [jnp.where(sel_r == 2 * HEADS * d + sel_c, 1.0, 0.0).astype(BF16) for d in (0, 1)]

    def col_bcast(x, d):
        hi, lo = _split(x)
        return _dot(jnp.concatenate([hi, lo], axis=-1), sel[d])

    def chunk_step(d, c, m_in, fill):
        crow = pl.ds(pl.multiple_of(c * ML_CHUNK, ML_CHUNK), ML_CHUNK)
        qc, kt, vc = q_scr[crow, :], kt_scr[c], m_ref[0, crow, 512:1024]
        at, bt = at_scr[c], bt_scr[c]
        c_bf = c_scr[d].astype(BF16)
        causal = _causal(bool(d), ML_CHUNK)
        last = 0 if d else ML_CHUNK - 1
        heads = range(HEADS)
        qh = [jnp.where(hm[h], qc, jnp.zeros_like(qc)) for h in heads]
        v_ext = [jnp.concatenate([vc[:, 128 * h:128 * h + 128], ones_col], axis=-1) for h in heads]
        s_raw = [_dot(qh[h], kt) for h in heads]
        q_c = [_dot(qh[h], c_bf) for h in heads]
        cm_col = col_bcast(cm_scr[crow, :], d)
        b_col = col_bcast(b_scr[crow, :], d)
        if fill[0]:
            fill[0]()
        s_w, m_run, m_last, ktw = [], [], [], []
        for h in heads:
            a_row = at[2 * HEADS * d + h:2 * HEADS * d + h + 1, :]
            pm = jnp.where(causal, a_row, -jnp.inf)
            m_run.append(jnp.maximum(m_in[h], cm_col[:, 128 * h:128 * h + 128]))
            m_wide = jnp.concatenate([m_run[h]] * (ML_CHUNK // 128), axis=-1)
            s_w.append((s_raw[h] * jnp.exp(pm - m_wide)).astype(BF16))
            m_last.append(m_run[h][last:last + 1, 0:1])
            k_h = kt[ML_DQK * h:ML_DQK * h + ML_DQK, :].astype(F32)
            ktw.append((k_h * jnp.exp(a_row - m_last[h])).astype(BF16))
        s_v = [_dot(s_w[h], v_ext[h]) for h in heads]
        upd = [_dot(ktw[h], v_ext[h]) for h in heads]
        if fill[1]:
            fill[1]()
        parts, m_out = [], []
        for h in heads:
            j = 2 * HEADS * d + h
            hr = slice(ML_DQK * h, ML_DQK * h + ML_DQK)
            carry_w = jnp.exp(m_in[h] - m_run[h])
            num = jnp.concatenate([carry_w, carry_w], axis=-1) * q_c[h] + s_v[h]
            den = num[:, HEAD_V:HEAD_V + 1]
            floor = jnp.exp(-(b_col[:, 128 * h:128 * h + 128] + m_run[h]))
            parts.append(num[:, 0:HEAD_V] / jnp.maximum(jnp.abs(den), floor))
            c_scr[d, hr, :] = jnp.exp(m_in[h] - m_last[h]) * c_scr[d, hr, :] + upd[h]
            m_out.append(bt[j:j + 1, last:last + 1] + m_last[h])
        if fill[2]:
            fill[2]()
        return jnp.concatenate(parts, axis=-1), tuple(m_out)

    def finish(t, h_b):
        rows = pl.ds(pl.multiple_of(t * ROW_TILE, ROW_TILE), ROW_TILE)
        tot = (hf_scr[rows, :] + h_b) * _sigmoid(m_ref[0, rows, 1024:1536].astype(F32))
        z = m_ref[0, rows, 1536:2048].astype(F32)
        xc = xc_scr[rows, :]
        for h in range(HEADS):
            cols = slice(128 * h, 128 * h + 128)
            y = _rms(tot[:, cols], HEAD_V) * og_ref[:, cols]
            y = (y + sk_ref[:, cols] * xc[:, cols]) * _silu(z[:, cols])
            o_ref[0, rows, cols] = y.astype(BF16)

    def forward(i, m):
        h_f, m = chunk_step(0, i, m, prep_parts(jnp.minimum(i + 1, n_tiles - 1)))
        hf_scr[pl.ds(pl.multiple_of(i * ROW_TILE, ROW_TILE), ROW_TILE), :] = h_f
        return m

    def backward(i, m):
        t = _tile_order(i, n_tiles, True)
        t_prev = _tile_order(jnp.maximum(i - 1, 0), n_tiles, True)
        h_b, m = chunk_step(1, t, m, (None, lambda: finish(t_prev, hb_scr[...]), None))
        hb_scr[...] = h_b
        return m

    m0 = tuple(jnp.zeros((1, 1), F32) for _ in range(HEADS))
    for part in prep_parts(0):
        part()
    lax.fori_loop(0, n_tiles, forward, m0)
    hb_scr[...] = jnp.zeros_like(hb_scr)
    lax.fori_loop(0, n_tiles, backward, m0)
    finish(_tile_order(n_tiles - 1, n_tiles, True), hb_scr[...])


def _mlstm(p, cw, cb, wq, wkt, gb, og, sk, n_ctx):
    B, T, _ = p.shape
    full = lambda shape: pl.BlockSpec(shape, lambda b: (0,) * len(shape))
    return pl.pallas_call(
        functools.partial(_mlstm_kernel, n_ctx=n_ctx),
        grid=(B,),
        in_specs=[pl.BlockSpec((1, T, 2048), lambda b: (b, 0, P_ML // 2048)),
                  pl.BlockSpec((1, T, 128), lambda b: (b, 0, P_ML_IF // 128)),
                  full(cw.shape), full(cb.shape), full(wq.shape), full(wkt.shape), full(gb.shape),
                  full(og.shape), full(sk.shape)],
        out_specs=pl.BlockSpec((1, T, 512), lambda b: (b, 0, 0)),
        out_shape=jax.ShapeDtypeStruct((B, T, BRANCH_WIDTH), BF16),
        scratch_shapes=[pltpu.VMEM((T, BRANCH_WIDTH), F32),
                        pltpu.VMEM((T, HEADS * ML_DQK), BF16),
                        pltpu.VMEM((T // ML_CHUNK, HEADS * ML_DQK, ML_CHUNK), BF16),
                        pltpu.VMEM((T // ML_CHUNK, 16, ML_CHUNK), F32),
                        pltpu.VMEM((T // ML_CHUNK, 16, ML_CHUNK), F32),
                        pltpu.VMEM((T, 128), F32),
                        pltpu.VMEM((T, 128), F32),
                        pltpu.VMEM((T, BRANCH_WIDTH), F32),
                        pltpu.VMEM((ROW_TILE, BRANCH_WIDTH), F32),
                        pltpu.VMEM((2, HEADS * ML_DQK, 2 * HEAD_V), F32)],
        compiler_params=_params("arbitrary"),
        name="mlstm_scan",
    )(p, p, cw, cb, wq, wkt, gb, og, sk)


def _merge_kernel(ya_ref, yb_ref, yc_ref, yd_ref, gl_ref, brw_ref, wo_ref, x_ref, gate_ref, modn_ref, gn_ref,
                  o_ref, *h_ref, row0, ctx_rows):
    tm = x_ref.shape[1]
    first = pl.program_id(1) * tm + row0
    acc = None
    for i, y_ref in enumerate((ya_ref, yb_ref, yc_ref, yd_ref)):
        u = _dot(y_ref[0], brw_ref[i])
        gsig = _sigmoid(gl_ref[0, :, D_MODEL * i:D_MODEL * (i + 1)].astype(F32))
        acc = gsig * u if acc is None else acc + gsig * u
    acc = acc.astype(BF16)
    starts = list(range(0, tm, ROW_TILE))
    outs = {starts[0]: _dot(acc[0:ROW_TILE], wo_ref[...])}
    for k, r0 in enumerate(starts):
        if k + 1 < len(starts):
            nxt = starts[k + 1]
            outs[nxt] = _dot(acc[nxt:nxt + ROW_TILE], wo_ref[...])
        rs = slice(r0, r0 + ROW_TILE)
        is_ctx = first + r0 < ctx_rows
        gate = jnp.where(is_ctx, gate_ref[0, 0:1, :], gate_ref[0, 1:2, :])
        x_new = x_ref[0, rs, :] + gate * outs[r0]
        o_ref[0, rs, :] = x_new
        if h_ref:
            h_ref[0][0, rs, :] = _modulated_norm(x_new, gn_ref[...], modn_ref, is_ctx)


def _merge(ys, p, brw, wo, xs, gates, mod_next, g_next, n_ctx, last):
    B, T, D = xs.shape
    tm = ROW_TILE if last else MERGE_ROWS
    row0 = n_ctx if last else 0
    tile0 = row0 // tm
    nt = (T - row0) // tm
    assert (T - row0) % tm == 0 and row0 % tm == 0 and n_ctx % ROW_TILE == 0
    ymap = lambda b, t: (b, t + tile0, 0)
    out_spec = pl.BlockSpec((1, tm, D), lambda b, t: (b, t, 0))
    out_shape = jax.ShapeDtypeStruct((B, nt * tm, D), F32)
    res = pl.pallas_call(
        functools.partial(_merge_kernel, row0=row0, ctx_rows=n_ctx),
        grid=(B, nt),
        in_specs=[pl.BlockSpec((1, tm, BRANCH_WIDTH), ymap)] * 4 + [
            pl.BlockSpec((1, tm, N_BRANCH * D), lambda b, t: (b, t + tile0, P_MERGE // (N_BRANCH * D))),
            pl.BlockSpec(brw.shape, lambda b, t: (0, 0, 0)),
            pl.BlockSpec(wo.shape, lambda b, t: (0, 0)),
            pl.BlockSpec((1, tm, D), ymap),
            pl.BlockSpec((1, 2, D), lambda b, t: (b, 0, 0)),
            pl.BlockSpec((1, 8, D), lambda b, t: (b, 0, 0)),
            pl.BlockSpec((1, D), lambda b, t: (0, 0))],
        out_specs=out_spec if last else [out_spec, out_spec],
        out_shape=out_shape if last else [out_shape, jax.ShapeDtypeStruct((B, nt * tm, D), BF16)],
        compiler_params=_params("arbitrary", "arbitrary"),
        name="merge",
    )(*ys, p, brw, wo, xs, gates, mod_next, g_next.reshape(1, D))
    return (res, None) if last else res


def _layout_w_in(w_in):
    offs, off = {}, 0
    for name, w in IN_SPLITS:
        offs[name] = (off, w)
        off += w

    w_bf = w_in.astype(BF16)

    def col(name):
        o, w = offs[name]
        return w_bf[..., o:o + w]

    def zeros(n):
        return jnp.zeros(w_in.shape[:-1] + (n,), BF16)

    parts = [col('merge'), col('df_k'), col('df_v'), col('df_q'), col('df_z'),
             col('ml_x'), col('ml_v'), col('ml_o'), col('ml_z'),
             col('gla_v'), col('gla_z'), col('mla_z'), col('gla_q'), col('gla_k'),
             col('mla_cq'), col('mla_kr'), zeros(64),
             col('mla_ckv'), col('gla_af'), col('gla_ab'), zeros(96), col('ml_if'), zeros(112)]
    out = jnp.concatenate(parts, axis=-1)
    assert out.shape[-1] == P_WIDTH
    return out


def _rope_tables(rows, n_ctx):
    quarter = ROT_DIM // 4
    inv_freq = ROPE_BASE ** (-jnp.arange(quarter, dtype=F32) / quarter)
    row = jnp.repeat(jnp.arange(rows, dtype=F32), GRID_W)
    col = jnp.tile(jnp.arange(GRID_W, dtype=F32), rows)
    ar = row[:, None] * inv_freq
    ac = col[:, None] * inv_freq
    ang = jnp.concatenate([ar, ar, ac, ac], axis=-1)
    cos = jnp.concatenate([jnp.ones((n_ctx, ROT_DIM), F32), jnp.cos(ang)], axis=0)
    sin = jnp.concatenate([jnp.zeros((n_ctx, ROT_DIM), F32), jnp.sin(ang)], axis=0)
    zero = jnp.zeros_like(cos)
    both = jnp.stack([jnp.tile(cos, (1, 2)), jnp.tile(sin, (1, 2))])
    half = jnp.stack([jnp.concatenate([cos, zero], -1), jnp.concatenate([sin, zero], -1)])
    return half, both


def _pad_lanes(v, n):
    return jnp.concatenate([v, jnp.zeros(v.shape[:-1] + (n - v.shape[-1],), v.dtype)], axis=-1)


def kernel(x, c, ctx, c_ctx, ada_w, ada_b, norm_g, w_in, mla_cq_g, mla_ckv_g, mla_wuq, mla_wukv, mla_q_g,
           mla_k_g, gla_a_w, gla_a_b, gla_out_g, ml_conv_w, ml_conv_b, ml_wq, ml_wk, ml_gate_b, ml_out_g,
           ml_skip, df_qk_g, df_lambda, df_out_g, br_w, w_out):
    B, S, D = x.shape
    n_ctx = ctx.shape[1]
    L = ada_w.shape[0]
    assert D == D_MODEL and n_ctx == ROW_TILE and S % ROW_TILE == 0 and S % GRID_W == 0

    rope_half, rope_both = _rope_tables(S // GRID_W, n_ctx)
    w_in_p = _layout_w_in(w_in)

    n_rows = -(-(B + 1) // 8) * 8
    cc = jnp.concatenate([c, c_ctx[None], jnp.zeros((n_rows - B - 1, D), F32)], axis=0)
    mod_all = _modulation(cc, ada_w, ada_b)

    xs = jnp.concatenate([ctx, x], axis=1)
    mods = []
    for l in range(L):
        m3 = mod_all[l].reshape(n_rows, 3, D)
        lat, cx = m3[:B], jnp.broadcast_to(m3[B][None], (B, 3, D))
        mods.append(jnp.concatenate([cx, lat, jnp.zeros((B, 2, D), F32)], axis=1))
    h = _first_norm(xs, mods[0], norm_g[0], n_ctx)
    for l in range(L):
        last = l == L - 1
        lam_init = 0.8 - 0.6 * math.exp(-0.3 * l)
        gates = jnp.stack([mods[l][:, 2], mods[l][:, 5]], axis=1)

        p = _in_projection(h, w_in_p, l)

        wq4 = mla_wuq[l].reshape(MLA_Q_LORA, HEADS, MLA_NOPE + ROT_DIM)
        wuq = _pad_lanes(wq4, 256).reshape(MLA_Q_LORA, HEADS * 256).astype(BF16)
        wkv4 = mla_wukv[l].reshape(MLA_KV_LORA, HEADS, MLA_NOPE + HEAD_V)
        wukv = jnp.concatenate([wkv4[..., :MLA_NOPE].reshape(MLA_KV_LORA, -1),
                                wkv4[..., MLA_NOPE:].reshape(MLA_KV_LORA, -1)], axis=-1).astype(BF16)
        qg = jnp.stack([mla_q_g[l, :MLA_NOPE], _pad_lanes(mla_q_g[l, MLA_NOPE:], 128)])
        kg = jnp.stack([mla_k_g[l, :MLA_NOPE], _pad_lanes(mla_k_g[l, MLA_NOPE:], 128)])
        y_mla = _mla(p, wuq, wukv, mla_cq_g[l][None], mla_ckv_g[l][None], qg, kg, rope_half, n_ctx, last)

        qkg = jnp.tile(df_qk_g[l], (1, 2))
        y_df = _diff(p, qkg, df_lambda[l], df_out_g[l][None], rope_both, n_ctx, last, lam_init)

        aw = jnp.zeros((128, 512), F32)
        aw = aw.at[0:16, 0:256].set(gla_a_w[l, 0]).at[16:32, 256:512].set(gla_a_w[l, 1]).astype(BF16)
        y_gla = _gla(p, aw, gla_a_b[l].reshape(1, 512), gla_out_g[l][None])

        wq_bd = jnp.zeros((BRANCH_WIDTH, HEADS * ML_DQK), F32)
        wk_bd = jnp.zeros((BRANCH_WIDTH, HEADS * ML_DQK), F32)
        for h in range(HEADS):
            wq_bd = wq_bd.at[128 * h:128 * h + 128, 64 * h:64 * h + 64].set(ml_wq[l, h])
            wk_bd = wk_bd.at[128 * h:128 * h + 128, 64 * h:64 * h + 64].set(ml_wk[l, h])
        gb = _pad_lanes(ml_gate_b[l].reshape(1, 16), 128)
        y_ml = _mlstm(p, ml_conv_w[l], ml_conv_b[l][None], wq_bd.astype(BF16), wk_bd.T.astype(BF16), gb,
                      ml_out_g[l][None], ml_skip[l][None], n_ctx)

        nxt = l if last else l + 1
        xs, h = _merge((y_mla, y_gla, y_ml, y_df), p, br_w[l].astype(BF16), w_out[l].astype(BF16), xs, gates,
                       mods[nxt], norm_g[nxt], n_ctx, last)
    return xs
```
